```python
import math
import jax, jax.numpy as jnp
from jax import lax
import numpy as np

D_MODEL = 2048
BATCH = 1
SEQ = 8192
DEPTH = 1

HEAD_DIM = 128
N_HEADS = D_MODEL // HEAD_DIM
MOBA_HEADS = N_HEADS // 2
NSA_HEADS = N_HEADS - MOBA_HEADS
NSA_KV_HEADS = 2
NSA_GROUP = NSA_HEADS // NSA_KV_HEADS
MOBA_BLOCK = 256
MOBA_TOPK = 3
CMP_LEN = 32
CMP_STRIDE = 16
SLC_LEN = 64
SLC_TOPK = 16
WINDOW = 512
N_BUCKETS = 32
MAX_DISTANCE = 128
D_FF = 4 * D_MODEL
D_PLE = 256
Q_CHUNK = 128
LN_EPS = 1e-5
DN_ALPHA = (2 * DEPTH) ** 0.25
DN_BETA = (8 * DEPTH) ** -0.25
NEG_INF = -1e30
TINY = 1e-30
FORCE_SCORE = 1e4

MOBA_W = MOBA_HEADS * HEAD_DIM
NSA_Q_W = NSA_HEADS * HEAD_DIM
NSA_KV_W = NSA_KV_HEADS * HEAD_DIM
GATE_W = NSA_HEADS * 3
SPLIT_SIZES = [MOBA_W, MOBA_W, MOBA_W, NSA_Q_W] + [NSA_KV_W] * 6 + [GATE_W]
PROJ_W = sum(SPLIT_SIZES)

kernel_name = "hymba_moba_nsa_deepnorm_hybrid"


def layer_norm(x, g, b):
    xf = x.astype(jnp.float32)
    mu = jnp.mean(xf, axis=-1, keepdims=True)
    var = jnp.mean(jnp.square(xf - mu), axis=-1, keepdims=True)
    y = (xf - mu) * lax.rsqrt(var + LN_EPS)
    return (y * g.astype(jnp.float32) + b.astype(jnp.float32)).astype(x.dtype)


def t5_bucket(dist):
    n = jnp.maximum(dist, 0)
    max_exact = N_BUCKETS // 2
    large = max_exact + (jnp.log(jnp.maximum(n, 1).astype(jnp.float32) / max_exact)
                         / math.log(MAX_DISTANCE / max_exact) * (N_BUCKETS - max_exact)).astype(jnp.int32)
    large = jnp.minimum(large, N_BUCKETS - 1)
    return jnp.where(n < max_exact, n, large)


def masked_softmax(logits, mask):
    logits = jnp.where(mask, logits.astype(jnp.float32), NEG_INF)
    m = jnp.max(logits, axis=-1, keepdims=True)
    e = jnp.where(mask, jnp.exp(logits - m), 0.0)
    return e / jnp.maximum(jnp.sum(e, axis=-1, keepdims=True), TINY)


def compress(kv, pe, w1, w2):
    S = kv.shape[1]
    n_cmp = (S - CMP_LEN) // CMP_STRIDE + 1
    idx = jnp.arange(n_cmp)[:, None] * CMP_STRIDE + jnp.arange(CMP_LEN)[None, :]
    blocks = kv[:, idx] + pe[:, None, :]
    h = jax.nn.gelu(jnp.einsum('bnlgd,lde->bnge', blocks, w1))
    return jnp.einsum('bnge,ef->bngf', h, w2)


def hybrid_mixer(x, w_in, cmp_pe_k, cmp_w1_k, cmp_w2_k, cmp_pe_v, cmp_w1_v, cmp_w2_v, tab_flat, w_out):
    B, S, _ = x.shape
    dt = x.dtype
    scale = HEAD_DIM ** -0.5
    z = x @ w_in
    offs = [int(o) for o in np.cumsum(SPLIT_SIZES)[:-1]]
    mq, mk, mv, nq, ck, cv, sk, sv, wk, wv, gl = jnp.split(z, offs, axis=-1)
    Hm, Hn, G, R, Dh = MOBA_HEADS, NSA_HEADS, NSA_KV_HEADS, NSA_GROUP, HEAD_DIM
    mq = mq.reshape(B, S, Hm, Dh)
    mk = mk.reshape(B, S, Hm, Dh)
    mv = mv.reshape(B, S, Hm, Dh)
    nq = nq.reshape(B, S, G, R, Dh)
    ck, cv, sk, sv, wk, wv = [a.reshape(B, S, G, Dh) for a in (ck, cv, sk, sv, wk, wv)]
    gates = jax.nn.sigmoid(gl).reshape(B, S, G, R, 3)

    L = MOBA_BLOCK
    nb = -(-S // L)
    pad = nb * L - S
    mk_p = jnp.pad(mk, ((0, 0), (0, pad), (0, 0), (0, 0)))
    mv_p = jnp.pad(mv, ((0, 0), (0, pad), (0, 0), (0, 0)))
    kb = mk_p.reshape(B, nb, L, Hm, Dh).transpose(0, 3, 1, 2, 4)
    vb = mv_p.reshape(B, nb, L, Hm, Dh).transpose(0, 3, 1, 2, 4)
    kmean = jnp.mean(kb, axis=3)
    k_moba = min(MOBA_TOPK, nb)
    sel_ok_moba_rank = jnp.arange(k_moba)

    kc = compress(ck, cmp_pe_k, cmp_w1_k, cmp_w2_k)
    vc = compress(cv, cmp_pe_v, cmp_w1_v, cmp_w2_v)
    n_cmp = kc.shape[1]
    cmp_start = jnp.arange(n_cmp) * CMP_STRIDE
    cmp_end = cmp_start + CMP_LEN - 1
    n_slc = S // SLC_LEN
    k_slc = min(SLC_TOPK, n_slc)
    slc_start = jnp.arange(n_slc) * SLC_LEN
    overlap = ((cmp_start[:, None] < slc_start[None, :] + SLC_LEN)
               & (cmp_start[:, None] + CMP_LEN > slc_start[None, :])).astype(jnp.float32)
    skb = sk.reshape(B, n_slc, SLC_LEN, G, Dh).transpose(0, 3, 1, 2, 4)
    svb = sv.reshape(B, n_slc, SLC_LEN, G, Dh).transpose(0, 3, 1, 2, 4)
    wk_p = jnp.pad(wk, ((0, 0), (WINDOW, 0), (0, 0), (0, 0)))
    wv_p = jnp.pad(wv, ((0, 0), (WINDOW, 0), (0, 0), (0, 0)))

    hm = jnp.arange(Hm)
    hn = MOBA_HEADS + jnp.arange(Hn).reshape(G, R)
    bi = jnp.arange(B)[:, None, None, None]

    def chunk(c):
        q0 = c * Q_CHUNK
        t = q0 + jnp.arange(Q_CHUNK)
        qm = lax.dynamic_slice_in_dim(mq, q0, Q_CHUNK, 1)
        blk = q0 // L
        gate_s = jnp.einsum('bchd,bhnd->bchn', qm, kmean).astype(jnp.float32)
        gate_s = jnp.where(jnp.arange(nb) < blk, gate_s, NEG_INF)
        _, sel = lax.top_k(gate_s, k_moba)
        sel_ok = jnp.repeat(sel_ok_moba_rank < blk, L)
        hi = hm[None, None, :, None]
        kg = kb[bi, hi, sel]
        vg = vb[bi, hi, sel]
        kL = k_moba * L
        lg_sel = jnp.einsum('bchd,bchksd->bchks', qm, kg).reshape(B, Q_CHUNK, Hm, kL)
        pos_sel = (sel[..., None] * L + jnp.arange(L)).reshape(B, Q_CHUNK, Hm, kL)
        bias_sel = tab_flat[hm[None, None, :, None] * N_BUCKETS
                            + t5_bucket(t[None, :, None, None] - pos_sel)]
        k_own = lax.dynamic_slice_in_dim(mk_p, blk * L, L, 1)
        v_own = lax.dynamic_slice_in_dim(mv_p, blk * L, L, 1)
        lg_own = jnp.einsum('bchd,bshd->bchs', qm, k_own)
        pos_own = blk * L + jnp.arange(L)
        dist_own = t[:, None] - pos_own[None, :]
        bias_own = tab_flat[hm[None, :, None] * N_BUCKETS + t5_bucket(dist_own)[:, None, :]]
        logits_m = jnp.concatenate([lg_sel.astype(jnp.float32) * scale + bias_sel,
                                    lg_own.astype(jnp.float32) * scale + bias_own[None]], axis=-1)
        mask_m = jnp.concatenate([
            jnp.broadcast_to(sel_ok, (B, Q_CHUNK, Hm, kL)),
            jnp.broadcast_to((dist_own >= 0)[None, :, None, :], (B, Q_CHUNK, Hm, L))], axis=-1)
        prob_m = masked_softmax(logits_m, mask_m).astype(dt)
        o_m = (jnp.einsum('bchks,bchksd->bchd', prob_m[..., :kL].reshape(B, Q_CHUNK, Hm, k_moba, L), vg)
               + jnp.einsum('bchs,bshd->bchd', prob_m[..., kL:], v_own))
        qn = lax.dynamic_slice_in_dim(nq, q0, Q_CHUNK, 1)
        lg_c = jnp.einsum('bcgrd,bngd->bcgrn', qn, kc).astype(jnp.float32) * scale
        dist_c = t[:, None] - cmp_end[None, :]
        bias_c = tab_flat[hn[None, :, :, None] * N_BUCKETS + t5_bucket(dist_c)[:, None, None, :]]
        p_c = masked_softmax(lg_c + bias_c[None], (dist_c >= 0)[None, :, None, None, :])
        o_c = jnp.einsum('bcgrn,bngd->bcgrd', p_c.astype(dt), vc)
        imp = jnp.einsum('bcgrn,nj->bcgj', p_c, overlap)
        jj = jnp.arange(n_slc)
        cur = t // SLC_LEN
        allowed = (slc_start[None, :] <= t[:, None])[None, :, None, :]
        forced = ((jj[None, :] == 0) | (jj[None, :] == cur[:, None])
                  | (jj[None, :] == cur[:, None] - 1))[None, :, None, :]
        score = jnp.where(allowed, jnp.where(forced, FORCE_SCORE, imp), -1.0)
        top_v, sel_n = lax.top_k(score, k_slc)
        gi = jnp.arange(G)[None, None, :, None]
        ksel = skb[bi, gi, sel_n]
        vsel = svb[bi, gi, sel_n]
        kLs = k_slc * SLC_LEN
        lg_s = jnp.einsum('bcgrd,bcgksd->bcgrks', qn, ksel).reshape(B, Q_CHUNK, G, R, kLs)
        pos_s = (sel_n[..., None] * SLC_LEN + jnp.arange(SLC_LEN)).reshape(B, Q_CHUNK, G, kLs)
        dist_s = t[None, :, None, None] - pos_s
        mask_s = (jnp.repeat(top_v >= 0, SLC_LEN, axis=-1) & (dist_s >= 0))[:, :, :, None, :]
        bias_s = tab_flat[hn[None, None, :, :, None] * N_BUCKETS + t5_bucket(dist_s)[:, :, :, None, :]]
        p_s = masked_softmax(lg_s.astype(jnp.float32) * scale + bias_s, mask_s)
        o_s = jnp.einsum('bcgrks,bcgksd->bcgrd',
                         p_s.reshape(B, Q_CHUNK, G, R, k_slc, SLC_LEN).astype(dt), vsel)
        kw = lax.dynamic_slice_in_dim(wk_p, q0, Q_CHUNK + WINDOW, 1)
        vw = lax.dynamic_slice_in_dim(wv_p, q0, Q_CHUNK + WINDOW, 1)
        pos_w = q0 - WINDOW + jnp.arange(Q_CHUNK + WINDOW)
        dist_w = t[:, None] - pos_w[None, :]
        mask_w = (dist_w >= 0) & (dist_w < WINDOW) & (pos_w[None, :] >= 0)
        bias_w = tab_flat[hn[None, :, :, None] * N_BUCKETS + t5_bucket(dist_w)[:, None, None, :]]
        lg_w = jnp.einsum('bcgrd,bsgd->bcgrs', qn, kw).astype(jnp.float32) * scale + bias_w[None]
        p_w = masked_softmax(lg_w, mask_w[None, :, None, None, :])
        o_w = jnp.einsum('bcgrs,bsgd->bcgrd', p_w.astype(dt), vw)
        g = lax.dynamic_slice_in_dim(gates, q0, Q_CHUNK, 1)
        o_n = g[..., 0:1] * o_c + g[..., 1:2] * o_s + g[..., 2:3] * o_w
        return jnp.concatenate([o_m.reshape(B, Q_CHUNK, Hm * Dh),
                                o_n.reshape(B, Q_CHUNK, Hn * Dh)], axis=-1)

    outs = lax.map(chunk, jnp.arange(S // Q_CHUNK))
    o = outs.transpose(1, 0, 2, 3).reshape(B, S, Hm * Dh + Hn * Dh)
    return o @ w_out


def setup_inputs(seed: int = 0) -> dict:
    key = jax.random.key(seed)
    ks = jax.random.split(key, 20)
    nrm = lambda k, shape, s: jax.random.normal(k, shape, jnp.float32) * s
    col_scale = np.ones((PROJ_W,), np.float32)
    offs = np.concatenate([[0], np.cumsum(SPLIT_SIZES)])
    for vi in (2, 5, 7, 9):
        col_scale[offs[vi]:offs[vi + 1]] = DN_BETA
    return {
        "x": nrm(ks[0], (BATCH, SEQ, D_MODEL), 1.0),
        "p": nrm(ks[1], (DEPTH, BATCH, SEQ, D_PLE), 1.0),
        "w_in": nrm(ks[2], (DEPTH, D_MODEL, PROJ_W), D_MODEL ** -0.5) * jnp.asarray(col_scale),
        "cmp_pe_k": nrm(ks[3], (DEPTH, CMP_LEN, HEAD_DIM), 0.1),
        "cmp_w1_k": nrm(ks[4], (DEPTH, CMP_LEN, HEAD_DIM, HEAD_DIM), (CMP_LEN * HEAD_DIM) ** -0.5),
        "cmp_w2_k": nrm(ks[5], (DEPTH, HEAD_DIM, HEAD_DIM), HEAD_DIM ** -0.5),
        "cmp_pe_v": nrm(ks[6], (DEPTH, CMP_LEN, HEAD_DIM), 0.1),
        "cmp_w1_v": nrm(ks[7], (DEPTH, CMP_LEN, HEAD_DIM, HEAD_DIM), (CMP_LEN * HEAD_DIM) ** -0.5),
        "cmp_w2_v": nrm(ks[8], (DEPTH, HEAD_DIM, HEAD_DIM), HEAD_DIM ** -0.5),
        "rel_bias": nrm(ks[9], (N_BUCKETS, N_HEADS), 0.5),
        "w_out": nrm(ks[10], (DEPTH, D_MODEL, D_MODEL), D_MODEL ** -0.5 * DN_BETA),
        "ln1_g": 1.0 + nrm(ks[11], (DEPTH, D_MODEL), 0.01),
        "ln1_b": nrm(ks[12], (DEPTH, D_MODEL), 0.01),
        "w_ff1": nrm(ks[13], (DEPTH, D_MODEL, D_FF), D_MODEL ** -0.5),
        "w_ff2": nrm(ks[14], (DEPTH, D_FF, D_MODEL), D_FF ** -0.5 * DN_BETA),
        "w_ple": nrm(ks[15], (DEPTH, D_PLE, D_MODEL), D_PLE ** -0.5 * DN_BETA),
        "w_ple_gate": nrm(ks[16], (DEPTH, D_MODEL, D_MODEL), D_MODEL ** -0.5),
        "ln2_g": 1.0 + nrm(ks[17], (DEPTH, D_MODEL), 0.01),
        "ln2_b": nrm(ks[18], (DEPTH, D_MODEL), 0.01),
    }


def reference(x, p, w_in, cmp_pe_k, cmp_w1_k, cmp_w2_k, cmp_pe_v, cmp_w1_v, cmp_w2_v, rel_bias,
              w_out, ln1_g, ln1_b, w_ff1, w_ff2, w_ple, w_ple_gate, ln2_g, ln2_b):
    tab_flat = rel_bias.T.reshape(-1)
    h = x
    for i in range(DEPTH):
        a = hybrid_mixer(h, w_in[i], cmp_pe_k[i], cmp_w1_k[i], cmp_w2_k[i],
                         cmp_pe_v[i], cmp_w1_v[i], cmp_w2_v[i], tab_flat, w_out[i])
        h = layer_norm(DN_ALPHA * h + a, ln1_g[i], ln1_b[i])
        f = jnp.square(jax.nn.relu(h @ w_ff1[i])) @ w_ff2[i]
        e = jax.nn.sigmoid(h @ w_ple_gate[i]) * (p[i] @ w_ple[i])
        h = layer_norm(DN_ALPHA * h + f + e, ln2_g[i], ln2_b[i])
    return h
```

```python
import functools
import math

import numpy as np
import jax
import jax.numpy as jnp
from jax import lax
from jax.experimental import pallas as pl
from jax.experimental.pallas import tpu as pltpu

D_MODEL = 2048
SEQ = 8192
HEAD_DIM = 128
N_HEADS = 16
MOBA_HEADS = 8
NSA_HEADS = 8
NSA_KV_HEADS = 2
NSA_GROUP = 4
MOBA_BLOCK = 256
MOBA_TOPK = 3
CMP_LEN = 32
CMP_STRIDE = 16
SLC_LEN = 64
SLC_TOPK = 16
WINDOW = 512
N_BUCKETS = 32
MAX_DISTANCE = 128
D_FF = 4 * D_MODEL
D_PLE = 256
LN_EPS = 1e-5
DN_ALPHA = 2.0 ** 0.25
NEG = -1e30
TINY = 1e-30
FORCE_SCORE = 1e4

N_CMP = (SEQ - CMP_LEN) // CMP_STRIDE + 1
N_SLC = SEQ // SLC_LEN
TQ = 256
N_QT = SEQ // TQ
CMP_PAD = 16
CMP_ROWS = CMP_PAD + 512
CMP_WIN = 32
SLC_PER_TILE = TQ // SLC_LEN
COL_MQ, COL_MK, COL_MV, COL_NQ = 0, 8, 16, 24
COL_CK, COL_CV, COL_SK, COL_SV, COL_WK, COL_WV = 32, 34, 36, 38, 40, 42
N_COLBLK = 44
GATE_COL0 = N_COLBLK * HEAD_DIM

VMEM_LIMIT = 48 * 1024 * 1024

BF16 = jnp.bfloat16
F32 = jnp.float32
_NT = (((1,), (1,)), ((), ()))


def _cparams(*sem):
    return pltpu.CompilerParams(dimension_semantics=sem, vmem_limit_bytes=VMEM_LIMIT)


def _bucket_table():
    n = np.arange(1024, dtype=np.int32)
    max_exact = N_BUCKETS // 2
    ratio = np.maximum(n, 1).astype(np.float32) / np.float32(max_exact)
    large = max_exact + (np.log(ratio).astype(np.float32) / np.float32(math.log(MAX_DISTANCE / max_exact))
                         * np.float32(N_BUCKETS - max_exact)).astype(np.int32)
    large = np.minimum(large, N_BUCKETS - 1)
    return np.where(n < max_exact, n, large).astype(np.int32)


def _bias_tiles(rel_bias):
    bucket = _bucket_table()
    tab = rel_bias.T.astype(F32)
    tvec = tab - tab[:, N_BUCKETS - 1:]
    a = np.arange(TQ)[:, None]
    b = np.arange(TQ)[None, :]
    d0 = a - b
    near0 = jnp.where(jnp.asarray(d0 >= 0)[None], tvec[:, bucket[np.maximum(d0, 0)]], NEG)
    near1 = tvec[:, bucket[TQ + d0]]
    near = jnp.stack([near0, near1])
    win2 = jnp.asarray(np.tile(np.where(a < b, 0.0, NEG).astype(np.float32), (NSA_GROUP, 1)))
    w = np.arange(CMP_WIN)[None, :]
    dc = a - CMP_STRIDE * w + (TQ - CMP_LEN + 1)
    cm = jnp.where(jnp.asarray(dc >= 0)[None], tvec[MOBA_HEADS:, bucket[np.maximum(dc, 0)]], NEG)
    cm0 = jnp.where(jnp.asarray(w >= CMP_PAD)[None], cm, NEG)
    cmpb = jnp.stack([cm0, cm])
    return near, win2, cmpb


def _inproj_kernel(x_ref, w_ref, cs_ref, o_ref, xb_ref):
    @pl.when(pl.program_id(1) == 0)
    def _():
        xb_ref[...] = x_ref[...].astype(BF16)

    acc = jnp.dot(xb_ref[...], w_ref[...], preferred_element_type=F32)
    o_ref[...] = (acc * cs_ref[...]).astype(o_ref.dtype)


def _inproj(x, w, colscale, tm=1024, tn=512):
    m, k = x.shape
    n = w.shape[1]
    return pl.pallas_call(
        _inproj_kernel,
        grid=(m // tm, n // tn),
        in_specs=[pl.BlockSpec((tm, k), lambda i, j: (i, 0)),
                  pl.BlockSpec((k, tn), lambda i, j: (0, j)),
                  pl.BlockSpec((1, tn), lambda i, j: (0, j))],
        out_specs=pl.BlockSpec((tm, tn), lambda i, j: (i, j)),
        out_shape=jax.ShapeDtypeStruct((m, n), BF16),
        scratch_shapes=[pltpu.VMEM((tm, k), BF16)],
        compiler_params=_cparams("parallel", "arbitrary"),
        name="inproj",
    )(x, w, colscale)


def _gates_kernel(x_ref, w_ref, o_ref):
    z = jnp.dot(x_ref[...].astype(BF16), w_ref[...], preferred_element_type=F32)
    o_ref[...] = jax.nn.sigmoid(z)


def _gates(x, wg, tm=1024):
    m, k = x.shape
    return pl.pallas_call(
        _gates_kernel,
        grid=(m // tm,),
        in_specs=[pl.BlockSpec((tm, k), lambda i: (i, 0)),
                  pl.BlockSpec((k, HEAD_DIM), lambda i: (0, 0))],
        out_specs=pl.BlockSpec((tm, HEAD_DIM), lambda i: (i, 0)),
        out_shape=jax.ShapeDtypeStruct((m, HEAD_DIM), F32),
        compiler_params=_cparams("parallel"),
        name="gates",
    )(x, wg)


def _gelu_tanh(x):
    return 0.5 * x * (1.0 + jnp.tanh(math.sqrt(2.0 / math.pi) * (x + 0.044715 * (x * x * x))))


def _compress_kernel(x_ref, pe_ref, w1_ref, w2_ref, o_ref):
    x = x_ref[0].astype(F32)
    lo = jnp.dot((x + pe_ref[0, 0]).astype(BF16), w1_ref[0, 0], preferred_element_type=F32)
    hi = jnp.dot((x + pe_ref[0, 1]).astype(BF16), w1_ref[0, 1], preferred_element_type=F32)
    y = lo + pltpu.roll(hi, 511, 0)
    h = _gelu_tanh(y).astype(BF16)
    out = jnp.dot(h, w2_ref[0], preferred_element_type=F32)
    row = lax.broadcasted_iota(jnp.int32, out.shape, 0)
    out = jnp.where(row < N_CMP, out, 0.0)
    o_ref[0, :CMP_PAD, :] = jnp.zeros((CMP_PAD, HEAD_DIM), o_ref.dtype)
    o_ref[0, CMP_PAD:, :] = out.astype(o_ref.dtype)


def _compress(xr, pe, w1, w2):
    return pl.pallas_call(
        _compress_kernel,
        grid=(4,),
        in_specs=[pl.BlockSpec((1, 512, 2048), lambda i: (i, 0, 0)),
                  pl.BlockSpec((1, 2, 1, 2048), lambda i: (i // 2, 0, 0, 0)),
                  pl.BlockSpec((1, 2, 2048, HEAD_DIM), lambda i: (i // 2, 0, 0, 0)),
                  pl.BlockSpec((1, HEAD_DIM, HEAD_DIM), lambda i: (i // 2, 0, 0))],
        out_specs=pl.BlockSpec((1, CMP_ROWS, HEAD_DIM), lambda i: (i, 0, 0)),
        out_shape=jax.ShapeDtypeStruct((4, CMP_ROWS, HEAD_DIM), BF16),
        compiler_params=_cparams("parallel"),
        name="compress",
    )(xr, pe, w1, w2)


def _flash_init(m_ref, l_ref, acc_ref):
    m_ref[...] = jnp.full(m_ref.shape, NEG, F32)
    l_ref[...] = jnp.zeros(l_ref.shape, F32)
    acc_ref[...] = jnp.zeros(acc_ref.shape, F32)


def _flash_step(q, k, v, bias, m_ref, l_ref, acc_ref):
    s = lax.dot_general(q, k, _NT, preferred_element_type=F32)
    if bias is not None:
        s = s + bias
    m_old = m_ref[...]
    m_new = jnp.maximum(m_old, jnp.max(s, axis=1, keepdims=True))
    alpha = jnp.exp(m_old - m_new)
    p = jnp.exp(s - m_new)
    l_ref[...] = alpha * l_ref[...] + jnp.sum(p, axis=1, keepdims=True)
    acc_ref[...] = alpha * acc_ref[...] + jnp.dot(p.astype(BF16), v, preferred_element_type=F32)
    m_ref[...] = m_new


def _split3(x):
    hi = x.astype(BF16)
    r1 = x - hi.astype(F32)
    mid = r1.astype(BF16)
    lo = (r1 - mid.astype(F32)).astype(BF16)
    return hi, mid, lo


def _topk_bias(score, k):
    lane = lax.broadcasted_iota(jnp.int32, score.shape, 1).astype(F32)
    selb = jnp.full(score.shape, NEG, F32)
    s = score
    for _ in range(k):
        mx = jnp.max(s, axis=1, keepdims=True)
        idx = jnp.min(jnp.where(s == mx, lane, 1e9), axis=1, keepdims=True)
        hit = lane == idx
        selb = jnp.where(hit, 0.0, selb)
        s = jnp.where(hit, -3e38, s)
    return selb


def _moba_kernel(q_ref, k_ref, v_ref, d0_ref, d1_ref, o_ref, kmean_ref, m_ref, l_ref, acc_ref):
    i = pl.program_id(1)

    @pl.when(i == 0)
    def _():
        blk = lax.broadcasted_iota(jnp.int32, (HEAD_DIM, SEQ), 0)
        pos = lax.broadcasted_iota(jnp.int32, (HEAD_DIM, SEQ), 1)
        ind = jnp.where(lax.shift_right_logical(pos, 8) == blk, 1.0 / MOBA_BLOCK, 0.0).astype(BF16)
        kmean_ref[...] = jnp.dot(ind, k_ref[...], preferred_element_type=F32)

    q = q_ref[...]
    gate = jnp.zeros((TQ, HEAD_DIM), F32)
    for part in _split3(kmean_ref[...]):
        gate = gate + lax.dot_general(q, part, _NT, preferred_element_type=F32)
    lane = lax.broadcasted_iota(jnp.int32, (TQ, HEAD_DIM), 1)
    past = lane < i
    selb = _topk_bias(jnp.where(past, gate, NEG), MOBA_TOPK)
    selb = jnp.where(past, selb, NEG)
    q_aug = jnp.concatenate([q, selb.astype(BF16)], axis=1)
    zeros_oh = jnp.zeros((TQ, HEAD_DIM), BF16)

    _flash_init(m_ref, l_ref, acc_ref)
    own = pl.multiple_of(i * TQ, TQ)
    _flash_step(q_aug, jnp.concatenate([k_ref[pl.ds(own, TQ), :], zeros_oh], axis=1),
                v_ref[pl.ds(own, TQ), :], d0_ref[0, 0], m_ref, l_ref, acc_ref)

    def past_step(j, bias):
        off = pl.multiple_of(j * TQ, TQ)
        onehot = jnp.where(lane == j, 1.0, 0.0).astype(BF16)
        _flash_step(q_aug, jnp.concatenate([k_ref[pl.ds(off, TQ), :], onehot], axis=1),
                    v_ref[pl.ds(off, TQ), :], bias, m_ref, l_ref, acc_ref)

    @pl.when(i >= 1)
    def _():
        past_step(i - 1, d1_ref[0, 0])

    def body(j, c):
        past_step(j, None)
        return c

    lax.fori_loop(0, i - 1, body, 0)
    o_ref[...] = (acc_ref[...] / l_ref[...]).astype(o_ref.dtype)


def _moba(zb, near):
    return pl.pallas_call(
        _moba_kernel,
        grid=(MOBA_HEADS, N_QT),
        in_specs=[pl.BlockSpec((TQ, HEAD_DIM), lambda h, i: (i, COL_MQ + h)),
                  pl.BlockSpec((SEQ, HEAD_DIM), lambda h, i: (0, COL_MK + h)),
                  pl.BlockSpec((SEQ, HEAD_DIM), lambda h, i: (0, COL_MV + h)),
                  pl.BlockSpec((1, 1, TQ, TQ), lambda h, i: (0, h, 0, 0)),
                  pl.BlockSpec((1, 1, TQ, TQ), lambda h, i: (1, h, 0, 0))],
        out_specs=pl.BlockSpec((TQ, HEAD_DIM), lambda h, i: (i, h)),
        out_shape=jax.ShapeDtypeStruct((SEQ, MOBA_HEADS * HEAD_DIM), BF16),
        scratch_shapes=[pltpu.VMEM((HEAD_DIM, HEAD_DIM), F32),
                        pltpu.VMEM((TQ, 1), F32), pltpu.VMEM((TQ, 1), F32),
                        pltpu.VMEM((TQ, HEAD_DIM), F32)],
        compiler_params=_cparams("parallel", "arbitrary"),
        name="moba",
    )(zb, zb, zb, near, near)


def _cmp_kernel(q_ref, kc_ref, vc_ref, ov_ref, cb_ref, oc_ref, sel_ref):
    i = pl.program_id(1)
    woff = pl.multiple_of(i * CMP_STRIDE, CMP_STRIDE)
    k_far = kc_ref[0, CMP_PAD:, :]
    v_far = vc_ref[0, CMP_PAD:, :]
    k_win = kc_ref[0, pl.ds(woff, CMP_WIN), :]
    v_win = vc_ref[0, pl.ds(woff, CMP_WIN), :]
    n_far = lax.broadcasted_iota(jnp.int32, (TQ, 512), 1)
    far_ok = n_far < i * CMP_STRIDE - CMP_PAD
    p_far_sum = jnp.zeros((TQ, 512), F32)
    p_win_sum = jnp.zeros((TQ, CMP_WIN), F32)
    for r in range(NSA_GROUP):
        q = q_ref[:, r * HEAD_DIM:(r + 1) * HEAD_DIM]
        s_far = jnp.where(far_ok, lax.dot_general(q, k_far, _NT, preferred_element_type=F32), NEG)
        bias_w = cb_ref[0, r]
        s_win = lax.dot_general(q, k_win, _NT, preferred_element_type=F32) + bias_w
        win_ok = bias_w > 0.5 * NEG
        mx = jnp.maximum(jnp.max(s_far, axis=1, keepdims=True), jnp.max(s_win, axis=1, keepdims=True))
        e_far = jnp.where(far_ok, jnp.exp(s_far - mx), 0.0)
        e_win = jnp.where(win_ok, jnp.exp(s_win - mx), 0.0)
        den = jnp.sum(e_far, axis=1, keepdims=True) + jnp.sum(e_win, axis=1, keepdims=True)
        inv = 1.0 / jnp.maximum(den, TINY)
        p_far = e_far * inv
        p_win = e_win * inv
        o = (jnp.dot(p_far.astype(BF16), v_far, preferred_element_type=F32)
             + jnp.dot(p_win.astype(BF16), v_win, preferred_element_type=F32))
        oc_ref[:, r * HEAD_DIM:(r + 1) * HEAD_DIM] = o
        p_far_sum = p_far_sum + p_far
        p_win_sum = p_win_sum + p_win

    ov_far = ov_ref[CMP_PAD:, :]
    ov_win = ov_ref[pl.ds(woff, CMP_WIN), :]
    imp = jnp.zeros((TQ, N_SLC), F32)
    for part in _split3(p_far_sum):
        imp = imp + jnp.dot(part, ov_far, preferred_element_type=F32)
    for part in _split3(p_win_sum):
        imp = imp + jnp.dot(part, ov_win, preferred_element_type=F32)

    jj = lax.broadcasted_iota(jnp.int32, (TQ, N_SLC), 1)
    a = lax.broadcasted_iota(jnp.int32, (TQ, N_SLC), 0)
    cur = i * SLC_PER_TILE + lax.shift_right_logical(a, 6)
    allowed = jj <= cur
    forced = (jj == 0) | (jj == cur) | (jj == cur - 1)
    score = jnp.where(allowed, jnp.where(forced, FORCE_SCORE, imp), -1.0)
    selb = _topk_bias(score, SLC_TOPK)
    sel_ref[0] = jnp.where(allowed, selb, NEG).astype(sel_ref.dtype)


def _cmp_select(zb, kvc, ov, cmpb):
    return pl.pallas_call(
        _cmp_kernel,
        grid=(NSA_KV_HEADS, N_QT),
        in_specs=[pl.BlockSpec((TQ, NSA_GROUP * HEAD_DIM), lambda g, i: (i, COL_NQ // NSA_GROUP + g)),
                  pl.BlockSpec((1, CMP_ROWS, HEAD_DIM), lambda g, i: (g, 0, 0)),
                  pl.BlockSpec((1, CMP_ROWS, HEAD_DIM), lambda g, i: (2 + g, 0, 0)),
                  pl.BlockSpec((CMP_ROWS, N_SLC), lambda g, i: (0, 0)),
                  pl.BlockSpec((1, NSA_GROUP, TQ, CMP_WIN), lambda g, i: (jnp.minimum(i, 1), g, 0, 0))],
        out_specs=[pl.BlockSpec((TQ, NSA_GROUP * HEAD_DIM), lambda g, i: (i, g)),
                   pl.BlockSpec((1, TQ, N_SLC), lambda g, i: (g, i, 0))],
        out_shape=[jax.ShapeDtypeStruct((SEQ, NSA_HEADS * HEAD_DIM), F32),
                   jax.ShapeDtypeStruct((NSA_KV_HEADS, SEQ, N_SLC), BF16)],
        compiler_params=_cparams("parallel", "arbitrary"),
        name="cmp_select",
    )(zb, kvc, kvc, ov, cmpb)


def _stack_heads(q_ref, extra):
    rows = []
    for r in range(NSA_GROUP):
        q = q_ref[:, r * HEAD_DIM:(r + 1) * HEAD_DIM]
        rows.append(q if extra is None else jnp.concatenate([q, extra], axis=1))
    return jnp.concatenate(rows, axis=0)


def _store_heads(o_ref, acc_ref, l_ref):
    o = acc_ref[...] / l_ref[...]
    for r in range(NSA_GROUP):
        o_ref[:, r * HEAD_DIM:(r + 1) * HEAD_DIM] = o[r * TQ:(r + 1) * TQ].astype(o_ref.dtype)


def _slc_kernel(q_ref, sel_ref, k_ref, v_ref, d0_ref, d1_ref, o_ref, qa_ref, m_ref, l_ref, acc_ref):
    i = pl.program_id(1)
    qa_ref[...] = _stack_heads(q_ref, sel_ref[0])
    row_blk = lax.shift_right_logical(lax.broadcasted_iota(jnp.int32, (TQ, HEAD_DIM), 0), 6)
    lane = lax.broadcasted_iota(jnp.int32, (TQ, HEAD_DIM), 1)

    def step(j, bias):
        off = pl.multiple_of(j * TQ, TQ)
        onehot = jnp.where(lane == j * SLC_PER_TILE + row_blk, 1.0, 0.0).astype(BF16)
        _flash_step(qa_ref[...], jnp.concatenate([k_ref[pl.ds(off, TQ), :], onehot], axis=1),
                    v_ref[pl.ds(off, TQ), :], bias, m_ref, l_ref, acc_ref)

    _flash_init(m_ref, l_ref, acc_ref)
    step(i, d0_ref[0].reshape(NSA_GROUP * TQ, TQ))

    @pl.when(i >= 1)
    def _():
        step(i - 1, d1_ref[0].reshape(NSA_GROUP * TQ, TQ))

    def body(j, c):
        step(j, None)
        return c

    lax.fori_loop(0, i - 1, body, 0)
    _store_heads(o_ref, acc_ref, l_ref)


def _slc(zb, selb, near):
    rows = NSA_GROUP * TQ
    return pl.pallas_call(
        _slc_kernel,
        grid=(NSA_KV_HEADS, N_QT),
        in_specs=[pl.BlockSpec((TQ, NSA_GROUP * HEAD_DIM), lambda g, i: (i, COL_NQ // NSA_GROUP + g)),
                  pl.BlockSpec((1, TQ, N_SLC), lambda g, i: (g, i, 0)),
                  pl.BlockSpec((SEQ, HEAD_DIM), lambda g, i: (0, COL_SK + g)),
                  pl.BlockSpec((SEQ, HEAD_DIM), lambda g, i: (0, COL_SV + g)),
                  pl.BlockSpec((1, NSA_GROUP, TQ, TQ), lambda g, i: (0, 2 + g, 0, 0)),
                  pl.BlockSpec((1, NSA_GROUP, TQ, TQ), lambda g, i: (1, 2 + g, 0, 0))],
        out_specs=pl.BlockSpec((TQ, NSA_GROUP * HEAD_DIM), lambda g, i: (i, g)),
        out_shape=jax.ShapeDtypeStruct((SEQ, NSA_HEADS * HEAD_DIM), F32),
        scratch_shapes=[pltpu.VMEM((rows, 2 * HEAD_DIM), BF16),
                        pltpu.VMEM((rows, 1), F32), pltpu.VMEM((rows, 1), F32),
                        pltpu.VMEM((rows, HEAD_DIM), F32)],
        compiler_params=_cparams("parallel", "arbitrary"),
        name="slc",
    )(zb, selb, zb, zb, near, near)


def _win_kernel(q_ref, k_ref, v_ref, d0_ref, d1_ref, w2_ref, o_ref, qa_ref, m_ref, l_ref, acc_ref):
    i = pl.program_id(1)
    qa_ref[...] = _stack_heads(q_ref, None)

    def step(j, bias):
        off = pl.multiple_of(j * TQ, TQ)
        _flash_step(qa_ref[...], k_ref[pl.ds(off, TQ), :], v_ref[pl.ds(off, TQ), :], bias,
                    m_ref, l_ref, acc_ref)

    _flash_init(m_ref, l_ref, acc_ref)
    step(i, d0_ref[0].reshape(NSA_GROUP * TQ, TQ))

    @pl.when(i >= 1)
    def _():
        step(i - 1, d1_ref[0].reshape(NSA_GROUP * TQ, TQ))

    @pl.when(i >= 2)
    def _():
        step(i - 2, w2_ref[...])

    _store_heads(o_ref, acc_ref, l_ref)


def _win(zb, near, win2):
    rows = NSA_GROUP * TQ
    return pl.pallas_call(
        _win_kernel,
        grid=(NSA_KV_HEADS, N_QT),
        in_specs=[pl.BlockSpec((TQ, NSA_GROUP * HEAD_DIM), lambda g, i: (i, COL_NQ // NSA_GROUP + g)),
                  pl.BlockSpec((SEQ, HEAD_DIM), lambda g, i: (0, COL_WK + g)),
                  pl.BlockSpec((SEQ, HEAD_DIM), lambda g, i: (0, COL_WV + g)),
                  pl.BlockSpec((1, NSA_GROUP, TQ, TQ), lambda g, i: (0, 2 + g, 0, 0)),
                  pl.BlockSpec((1, NSA_GROUP, TQ, TQ), lambda g, i: (1, 2 + g, 0, 0)),
                  pl.BlockSpec((rows, TQ), lambda g, i: (0, 0))],
        out_specs=pl.BlockSpec((TQ, NSA_GROUP * HEAD_DIM), lambda g, i: (i, g)),
        out_shape=jax.ShapeDtypeStruct((SEQ, NSA_HEADS * HEAD_DIM), F32),
        scratch_shapes=[pltpu.VMEM((rows, HEAD_DIM), BF16),
                        pltpu.VMEM((rows, 1), F32), pltpu.VMEM((rows, 1), F32),
                        pltpu.VMEM((rows, HEAD_DIM), F32)],
        compiler_params=_cparams("parallel", "arbitrary"),
        name="win",
    )(zb, zb, zb, near, near, win2)


def _layer_norm(y, g, b):
    mu = jnp.mean(y, axis=-1, keepdims=True)
    yc = y - mu
    var = jnp.mean(yc * yc, axis=-1, keepdims=True)
    return yc * lax.rsqrt(var + LN_EPS) * g + b


def _outproj_kernel(x_ref, om_ref, oc_ref, os_ref, ow_ref, g_ref, e_ref, w_ref, lg_ref, lb_ref,
                    h_ref, hb_ref):
    gate = g_ref[...]
    parts = _split3(gate)
    o_n = jnp.zeros(oc_ref.shape, F32)
    for c, branch in enumerate((oc_ref, os_ref, ow_ref)):
        gexp = jnp.zeros(oc_ref.shape, F32)
        for part in parts:
            gexp = gexp + jnp.dot(part, e_ref[c], preferred_element_type=F32)
        o_n = o_n + gexp * branch[...]
    o = jnp.concatenate([om_ref[...], o_n.astype(BF16)], axis=1)
    a = jnp.dot(o, w_ref[...], preferred_element_type=F32)
    h = _layer_norm(DN_ALPHA * x_ref[...] + a, lg_ref[...], lb_ref[...])
    h_ref[...] = h
    hb_ref[...] = h.astype(BF16)


def _outproj(x, o_m, o_c, o_s, o_w, gates, expand, w_out, ln_g, ln_b, tm=256):
    half = NSA_HEADS * HEAD_DIM
    row = lambda i: (i, 0)
    fixed2 = lambda i: (0, 0)
    return pl.pallas_call(
        _outproj_kernel,
        grid=(SEQ // tm,),
        in_specs=[pl.BlockSpec((tm, D_MODEL), row),
                  pl.BlockSpec((tm, half), row), pl.BlockSpec((tm, half), row),
                  pl.BlockSpec((tm, half), row), pl.BlockSpec((tm, half), row),
                  pl.BlockSpec((tm, HEAD_DIM), row),
                  pl.BlockSpec((3, HEAD_DIM, half), lambda i: (0, 0, 0)),
                  pl.BlockSpec((D_MODEL, D_MODEL), fixed2),
                  pl.BlockSpec((1, D_MODEL), fixed2), pl.BlockSpec((1, D_MODEL), fixed2)],
        out_specs=[pl.BlockSpec((tm, D_MODEL), row), pl.BlockSpec((tm, D_MODEL), row)],
        out_shape=[jax.ShapeDtypeStruct((SEQ, D_MODEL), F32),
                   jax.ShapeDtypeStruct((SEQ, D_MODEL), BF16)],
        compiler_params=_cparams("parallel"),
        name="outproj_ln",
    )(x, o_m, o_c, o_s, o_w, gates, expand, w_out, ln_g, ln_b)


def _ffn_kernel(hb_ref, w1_ref, w2_ref, f_ref):
    @pl.when(pl.program_id(1) == 0)
    def _():
        f_ref[...] = jnp.zeros(f_ref.shape, F32)

    u = jnp.maximum(jnp.dot(hb_ref[...], w1_ref[...], preferred_element_type=F32), 0.0)
    f_ref[...] += jnp.dot((u * u).astype(BF16), w2_ref[...], preferred_element_type=F32)


def _ffn(hb, w1, w2, tm=1024, tf=512):
    return pl.pallas_call(
        _ffn_kernel,
        grid=(SEQ // tm, D_FF // tf),
        in_specs=[pl.BlockSpec((tm, D_MODEL), lambda i, c: (i, 0)),
                  pl.BlockSpec((D_MODEL, tf), lambda i, c: (0, c)),
                  pl.BlockSpec((tf, D_MODEL), lambda i, c: (c, 0))],
        out_specs=pl.BlockSpec((tm, D_MODEL), lambda i, c: (i, 0)),
        out_shape=jax.ShapeDtypeStruct((SEQ, D_MODEL), F32),
        compiler_params=_cparams("parallel", "arbitrary"),
        name="ffn",
    )(hb, w1, w2)


def _final_kernel(h_ref, hb_ref, f_ref, p_ref, wg_ref, wp_ref, lg_ref, lb_ref, o_ref):
    gate = jax.nn.sigmoid(jnp.dot(hb_ref[...], wg_ref[...], preferred_element_type=F32))
    emb = jnp.dot(p_ref[...].astype(BF16), wp_ref[...], preferred_element_type=F32)
    y = DN_ALPHA * h_ref[...] + f_ref[...] + gate * emb
    o_ref[...] = _layer_norm(y, lg_ref[...], lb_ref[...])


def _final(h, hb, f, p, wg, wp, ln_g, ln_b, tm=256):
    row = lambda i: (i, 0)
    fixed2 = lambda i: (0, 0)
    return pl.pallas_call(
        _final_kernel,
        grid=(SEQ // tm,),
        in_specs=[pl.BlockSpec((tm, D_MODEL), row), pl.BlockSpec((tm, D_MODEL), row),
                  pl.BlockSpec((tm, D_MODEL), row), pl.BlockSpec((tm, D_PLE), row),
                  pl.BlockSpec((D_MODEL, D_MODEL), fixed2), pl.BlockSpec((D_PLE, D_MODEL), fixed2),
                  pl.BlockSpec((1, D_MODEL), fixed2), pl.BlockSpec((1, D_MODEL), fixed2)],
        out_specs=pl.BlockSpec((tm, D_MODEL), row),
        out_shape=jax.ShapeDtypeStruct((SEQ, D_MODEL), F32),
        compiler_params=_cparams("parallel"),
        name="final_ln",
    )(h, hb, f, p, wg, wp, ln_g, ln_b)


def _overlap_matrix():
    n = np.arange(512)[:, None]
    j = np.arange(N_SLC)[None, :]
    ov = ((n * CMP_STRIDE < j * SLC_LEN + SLC_LEN) & (n * CMP_STRIDE + CMP_LEN > j * SLC_LEN) & (n < N_CMP))
    return np.concatenate([np.zeros((CMP_PAD, N_SLC), np.float32), ov.astype(np.float32)], axis=0)


def _gate_expand():
    e = np.zeros((3, HEAD_DIM, NSA_HEADS * HEAD_DIM), np.float32)
    for c in range(3):
        for h in range(NSA_HEADS):
            e[c, 3 * h + c, h * HEAD_DIM:(h + 1) * HEAD_DIM] = 1.0
    return e


def _layer(h, p, w_in, cmp_pe_k, cmp_w1_k, cmp_w2_k, cmp_pe_v, cmp_w1_v, cmp_w2_v, tiles, w_out,
           ln1_g, ln1_b, w_ff1, w_ff2, w_ple, w_ple_gate, ln2_g, ln2_b):
    near, win2, cmpb = tiles
    scale = HEAD_DIM ** -0.5
    colscale = np.ones((1, GATE_COL0), np.float32)
    colscale[0, COL_MQ * HEAD_DIM:(COL_MQ + MOBA_HEADS) * HEAD_DIM] = scale
    colscale[0, COL_NQ * HEAD_DIM:(COL_NQ + NSA_HEADS) * HEAD_DIM] = scale
    zb = _inproj(h, w_in[:, :GATE_COL0].astype(BF16), jnp.asarray(colscale))
    w_gl = jnp.pad(w_in[:, GATE_COL0:], ((0, 0), (0, HEAD_DIM - NSA_HEADS * 3))).astype(BF16)
    gates = _gates(h, w_gl)

    ckv = zb[:, COL_CK * HEAD_DIM:(COL_CK + 4) * HEAD_DIM]
    xr = ckv.reshape(SEQ // CMP_STRIDE, CMP_STRIDE, 4, HEAD_DIM).transpose(2, 0, 1, 3)
    xr = xr.reshape(4, SEQ // CMP_STRIDE, CMP_STRIDE * HEAD_DIM)
    pe = jnp.stack([cmp_pe_k, cmp_pe_v]).reshape(2, 2, 1, CMP_STRIDE * HEAD_DIM).astype(F32)
    w1 = jnp.stack([cmp_w1_k, cmp_w1_v]).reshape(2, 2, CMP_STRIDE * HEAD_DIM, HEAD_DIM).astype(BF16)
    w2 = jnp.stack([cmp_w2_k, cmp_w2_v]).astype(BF16)
    kvc = _compress(xr, pe, w1, w2)

    o_m = _moba(zb, near)
    o_c, selb = _cmp_select(zb, kvc, jnp.asarray(_overlap_matrix(), BF16), cmpb)
    o_s = _slc(zb, selb, near)
    o_w = _win(zb, near, win2)

    h1, h1b = _outproj(h, o_m, o_c, o_s, o_w, gates, jnp.asarray(_gate_expand(), BF16),
                       w_out.astype(BF16), ln1_g.reshape(1, -1), ln1_b.reshape(1, -1))
    f = _ffn(h1b, w_ff1.astype(BF16), w_ff2.astype(BF16))
    return _final(h1, h1b, f, p, w_ple_gate.astype(BF16), w_ple.astype(BF16),
                  ln2_g.reshape(1, -1), ln2_b.reshape(1, -1))


def kernel(x, p, w_in, cmp_pe_k, cmp_w1_k, cmp_w2_k, cmp_pe_v, cmp_w1_v, cmp_w2_v, rel_bias, w_out,
           ln1_g, ln1_b, w_ff1, w_ff2, w_ple, w_ple_gate, ln2_g, ln2_b):
    tiles = _bias_tiles(rel_bias)
    h = x[0]
    for i in range(w_in.shape[0]):
        h = _layer(h, p[i, 0], w_in[i], cmp_pe_k[i], cmp_w1_k[i], cmp_w2_k[i], cmp_pe_v[i], cmp_w1_v[i],
                   cmp_w2_v[i], tiles, w_out[i], ln1_g[i], ln1_b[i], w_ff1[i], w_ff2[i], w_ple[i],
                   w_ple_gate[i], ln2_g[i], ln2_b[i])
    return h[None]
```

```python
import functools
import math

import numpy as np
import jax
import jax.numpy as jnp
from jax import lax
from jax.experimental import pallas as pl
from jax.experimental.pallas import tpu as pltpu

D_MODEL = 2048
SEQ = 8192
HEAD_DIM = 128
N_HEADS = 16
MOBA_HEADS = 8
NSA_HEADS = 8
NSA_KV_HEADS = 2
NSA_GROUP = 4
MOBA_BLOCK = 256
MOBA_TOPK = 3
CMP_LEN = 32
CMP_STRIDE = 16
SLC_LEN = 64
SLC_TOPK = 16
WINDOW = 512
N_BUCKETS = 32
MAX_DISTANCE = 128
D_FF = 4 * D_MODEL
D_PLE = 256
LN_EPS = 1e-5
DN_ALPHA = 2.0 ** 0.25
NEG = -1e30
TINY = 1e-30
FORCE_SCORE = 1e4

N_CMP = (SEQ - CMP_LEN) // CMP_STRIDE + 1
N_SLC = SEQ // SLC_LEN
TQ = 256
N_QT = SEQ // TQ
CMP_PAD = 16
CMP_ROWS = CMP_PAD + 512
CMP_WIN = 32
SLC_PER_TILE = TQ // SLC_LEN
COL_MQ, COL_MK, COL_MV, COL_NQ = 0, 8, 16, 24
COL_CK, COL_CV, COL_SK, COL_SV, COL_WK, COL_WV = 32, 34, 36, 38, 40, 42
N_COLBLK = 44
GATE_COL0 = N_COLBLK * HEAD_DIM

VMEM_LIMIT = 48 * 1024 * 1024

BF16 = jnp.bfloat16
F32 = jnp.float32
_NT = (((1,), (1,)), ((), ()))


def _cparams(*sem):
    return pltpu.CompilerParams(dimension_semantics=sem, vmem_limit_bytes=VMEM_LIMIT)


def _bucket_table():
    n = np.arange(1024, dtype=np.int32)
    max_exact = N_BUCKETS // 2
    ratio = np.maximum(n, 1).astype(np.float32) / np.float32(max_exact)
    large = max_exact + (np.log(ratio).astype(np.float32) / np.float32(math.log(MAX_DISTANCE / max_exact))
                         * np.float32(N_BUCKETS - max_exact)).astype(np.int32)
    large = np.minimum(large, N_BUCKETS - 1)
    return np.where(n < max_exact, n, large).astype(np.int32)


def _bias_kernel(tab_ref, near_ref, cmpb_ref, *, steps):
    h = pl.program_id(0)
    far = tab_ref[h, N_BUCKETS - 1]

    def lookup(d):
        val = jnp.full(d.shape, tab_ref[h, 0], F32)
        for start, bucket in steps:
            val = jnp.where(d >= start, tab_ref[h, bucket], val)
        return jnp.where(d >= 0, val - far, NEG)

    a = lax.broadcasted_iota(jnp.int32, (TQ, TQ), 0)
    b = lax.broadcasted_iota(jnp.int32, (TQ, TQ), 1)
    near_ref[0, 0] = lookup(a - b)
    near_ref[1, 0] = lookup(a - b + TQ)
    aw = lax.broadcasted_iota(jnp.int32, (TQ, CMP_WIN), 0)
    w = lax.broadcasted_iota(jnp.int32, (TQ, CMP_WIN), 1)
    cm = lookup(aw - CMP_STRIDE * w + (TQ - CMP_LEN + 1))
    cmpb_ref[0, 0] = jnp.where(w >= CMP_PAD, cm, NEG)
    cmpb_ref[1, 0] = cm


def _bias_tiles(rel_bias):
    bucket = _bucket_table()
    steps = tuple((int(n), int(bucket[n])) for n in range(1, bucket.size) if bucket[n] != bucket[n - 1])
    near, cmpb = pl.pallas_call(
        functools.partial(_bias_kernel, steps=steps),
        grid=(N_HEADS,),
        in_specs=[pl.BlockSpec(memory_space=pltpu.SMEM)],
        out_specs=[pl.BlockSpec((2, 1, TQ, TQ), lambda h: (0, h, 0, 0)),
                   pl.BlockSpec((2, 1, TQ, CMP_WIN), lambda h: (0, h, 0, 0))],
        out_shape=[jax.ShapeDtypeStruct((2, N_HEADS, TQ, TQ), F32),
                   jax.ShapeDtypeStruct((2, N_HEADS, TQ, CMP_WIN), F32)],
        compiler_params=_cparams("parallel"),
        name="bias_tiles",
    )(rel_bias.T.astype(F32))
    a = np.arange(TQ)[:, None]
    b = np.arange(TQ)[None, :]
    win2 = jnp.asarray(np.tile(np.where(a < b, 0.0, NEG).astype(np.float32), (NSA_GROUP, 1)))
    return near, win2, cmpb


def _inproj_kernel(x_ref, w_ref, cs_ref, o_ref, xb_ref):
    @pl.when(pl.program_id(1) == 0)
    def _():
        xb_ref[...] = x_ref[...].astype(BF16)

    acc = jnp.dot(xb_ref[...], w_ref[...], preferred_element_type=F32)
    o_ref[...] = (acc * cs_ref[...]).astype(o_ref.dtype)


def _inproj(x, w, colscale, tm=1024, tn=512):
    m, k = x.shape
    n = w.shape[1]
    return pl.pallas_call(
        _inproj_kernel,
        grid=(m // tm, n // tn),
        in_specs=[pl.BlockSpec((tm, k), lambda i, j: (i, 0)),
                  pl.BlockSpec((k, tn), lambda i, j: (0, j)),
                  pl.BlockSpec((1, tn), lambda i, j: (0, j))],
        out_specs=pl.BlockSpec((tm, tn), lambda i, j: (i, j)),
        out_shape=jax.ShapeDtypeStruct((m, n), BF16),
        scratch_shapes=[pltpu.VMEM((tm, k), BF16)],
        compiler_params=_cparams("parallel", "arbitrary"),
        name="inproj",
    )(x, w, colscale)


def _gates_kernel(x_ref, w_ref, o_ref):
    z = jnp.dot(x_ref[...].astype(BF16), w_ref[...], preferred_element_type=F32)
    o_ref[...] = jax.nn.sigmoid(z)


def _gates(x, wg, tm=1024):
    m, k = x.shape
    return pl.pallas_call(
        _gates_kernel,
        grid=(m // tm,),
        in_specs=[pl.BlockSpec((tm, k), lambda i: (i, 0)),
                  pl.BlockSpec((k, HEAD_DIM), lambda i: (0, 0))],
        out_specs=pl.BlockSpec((tm, HEAD_DIM), lambda i: (i, 0)),
        out_shape=jax.ShapeDtypeStruct((m, HEAD_DIM), F32),
        compiler_params=_cparams("parallel"),
        name="gates",
    )(x, wg)


def _gelu_tanh(x):
    return 0.5 * x * (1.0 + jnp.tanh(math.sqrt(2.0 / math.pi) * (x + 0.044715 * (x * x * x))))


def _compress_kernel(x_ref, pe_ref, w1_ref, w2_ref, o_ref):
    x = x_ref[0].astype(F32)
    lo = jnp.dot((x + pe_ref[0, 0]).astype(BF16), w1_ref[0, 0], preferred_element_type=F32)
    hi = jnp.dot((x + pe_ref[0, 1]).astype(BF16), w1_ref[0, 1], preferred_element_type=F32)
    y = lo + pltpu.roll(hi, 511, 0)
    h = _gelu_tanh(y).astype(BF16)
    out = jnp.dot(h, w2_ref[0], preferred_element_type=F32)
    row = lax.broadcasted_iota(jnp.int32, out.shape, 0)
    out = jnp.where(row < N_CMP, out, 0.0)
    o_ref[0, :CMP_PAD, :] = jnp.zeros((CMP_PAD, HEAD_DIM), o_ref.dtype)
    o_ref[0, CMP_PAD:, :] = out.astype(o_ref.dtype)


def _compress(xr, pe, w1, w2):
    return pl.pallas_call(
        _compress_kernel,
        grid=(4,),
        in_specs=[pl.BlockSpec((1, 512, 2048), lambda i: (i, 0, 0)),
                  pl.BlockSpec((1, 2, 1, 2048), lambda i: (i // 2, 0, 0, 0)),
                  pl.BlockSpec((1, 2, 2048, HEAD_DIM), lambda i: (i // 2, 0, 0, 0)),
                  pl.BlockSpec((1, HEAD_DIM, HEAD_DIM), lambda i: (i // 2, 0, 0))],
        out_specs=pl.BlockSpec((1, CMP_ROWS, HEAD_DIM), lambda i: (i, 0, 0)),
        out_shape=jax.ShapeDtypeStruct((4, CMP_ROWS, HEAD_DIM), BF16),
        compiler_params=_cparams("parallel"),
        name="compress",
    )(xr, pe, w1, w2)


def _flash_init(m_ref, l_ref, acc_ref):
    m_ref[...] = jnp.full(m_ref.shape, NEG, F32)
    l_ref[...] = jnp.zeros(l_ref.shape, F32)
    acc_ref[...] = jnp.zeros(acc_ref.shape, F32)


def _flash_step(q, k, v, bias, m_ref, l_ref, acc_ref):
    s = lax.dot_general(q, k, _NT, preferred_element_type=F32)
    if bias is not None:
        s = s + bias
    m_old = m_ref[...]
    m_new = jnp.maximum(m_old, jnp.max(s, axis=1, keepdims=True))
    alpha = jnp.exp(m_old - m_new)
    p = jnp.exp(s - m_new)
    l_ref[...] = alpha * l_ref[...] + jnp.sum(p, axis=1, keepdims=True)
    acc_ref[...] = alpha * acc_ref[...] + jnp.dot(p.astype(BF16), v, preferred_element_type=F32)
    m_ref[...] = m_new


def _split3(x):
    hi = x.astype(BF16)
    r1 = x - hi.astype(F32)
    mid = r1.astype(BF16)
    lo = (r1 - mid.astype(F32)).astype(BF16)
    return hi, mid, lo


def _topk_bias(score, k):
    lane = lax.broadcasted_iota(jnp.int32, score.shape, 1).astype(F32)
    selb = jnp.full(score.shape, NEG, F32)
    s = score
    for _ in range(k):
        mx = jnp.max(s, axis=1, keepdims=True)
        idx = jnp.min(jnp.where(s == mx, lane, 1e9), axis=1, keepdims=True)
        hit = lane == idx
        selb = jnp.where(hit, 0.0, selb)
        s = jnp.where(hit, -3e38, s)
    return selb


def _moba_kernel(q_ref, k_ref, v_ref, d0_ref, d1_ref, o_ref, kmean_ref, m_ref, l_ref, acc_ref):
    i = pl.program_id(1)

    @pl.when(i == 0)
    def _():
        blk = lax.broadcasted_iota(jnp.int32, (HEAD_DIM, SEQ), 0)
        pos = lax.broadcasted_iota(jnp.int32, (HEAD_DIM, SEQ), 1)
        ind = jnp.where(lax.shift_right_logical(pos, 8) == blk, 1.0 / MOBA_BLOCK, 0.0).astype(BF16)
        kmean_ref[...] = jnp.dot(ind, k_ref[...], preferred_element_type=F32)

    q = q_ref[...]
    gate = jnp.zeros((TQ, HEAD_DIM), F32)
    for part in _split3(kmean_ref[...]):
        gate = gate + lax.dot_general(q, part, _NT, preferred_element_type=F32)
    lane = lax.broadcasted_iota(jnp.int32, (TQ, HEAD_DIM), 1)
    past = lane < i
    selb = _topk_bias(jnp.where(past, gate, NEG), MOBA_TOPK)
    selb = jnp.where(past, selb, NEG)
    q_aug = jnp.concatenate([q, selb.astype(BF16)], axis=1)
    zeros_oh = jnp.zeros((TQ, HEAD_DIM), BF16)

    _flash_init(m_ref, l_ref, acc_ref)
    own = pl.multiple_of(i * TQ, TQ)
    _flash_step(q_aug, jnp.concatenate([k_ref[pl.ds(own, TQ), :], zeros_oh], axis=1),
                v_ref[pl.ds(own, TQ), :], d0_ref[0, 0], m_ref, l_ref, acc_ref)

    def past_step(j, bias):
        off = pl.multiple_of(j * TQ, TQ)
        onehot = jnp.where(lane == j, 1.0, 0.0).astype(BF16)
        _flash_step(q_aug, jnp.concatenate([k_ref[pl.ds(off, TQ), :], onehot], axis=1),
                    v_ref[pl.ds(off, TQ), :], bias, m_ref, l_ref, acc_ref)

    @pl.when(i >= 1)
    def _():
        past_step(i - 1, d1_ref[0, 0])

    def body(j, c):
        past_step(j, None)
        return c

    lax.fori_loop(0, i - 1, body, 0)
    o_ref[...] = (acc_ref[...] / l_ref[...]).astype(o_ref.dtype)


def _moba(zb, near):
    return pl.pallas_call(
        _moba_kernel,
        grid=(MOBA_HEADS, N_QT),
        in_specs=[pl.BlockSpec((TQ, HEAD_DIM), lambda h, i: (i, COL_MQ + h)),
                  pl.BlockSpec((SEQ, HEAD_DIM), lambda h, i: (0, COL_MK + h)),
                  pl.BlockSpec((SEQ, HEAD_DIM), lambda h, i: (0, COL_MV + h)),
                  pl.BlockSpec((1, 1, TQ, TQ), lambda h, i: (0, h, 0, 0)),
                  pl.BlockSpec((1, 1, TQ, TQ), lambda h, i: (1, h, 0, 0))],
        out_specs=pl.BlockSpec((TQ, HEAD_DIM), lambda h, i: (i, h)),
        out_shape=jax.ShapeDtypeStruct((SEQ, MOBA_HEADS * HEAD_DIM), BF16),
        scratch_shapes=[pltpu.VMEM((HEAD_DIM, HEAD_DIM), F32),
                        pltpu.VMEM((TQ, 1), F32), pltpu.VMEM((TQ, 1), F32),
                        pltpu.VMEM((TQ, HEAD_DIM), F32)],
        compiler_params=_cparams("parallel", "arbitrary"),
        name="moba",
    )(zb, zb, zb, near, near)


def _cmp_kernel(q_ref, kc_ref, vc_ref, ov_ref, cb_ref, oc_ref, sel_ref):
    i = pl.program_id(1)
    woff = pl.multiple_of(i * CMP_STRIDE, CMP_STRIDE)
    k_far = kc_ref[0, CMP_PAD:, :]
    v_far = vc_ref[0, CMP_PAD:, :]
    k_win = kc_ref[0, pl.ds(woff, CMP_WIN), :]
    v_win = vc_ref[0, pl.ds(woff, CMP_WIN), :]
    n_far = lax.broadcasted_iota(jnp.int32, (TQ, 512), 1)
    far_ok = n_far < i * CMP_STRIDE - CMP_PAD
    p_far_sum = jnp.zeros((TQ, 512), F32)
    p_win_sum = jnp.zeros((TQ, CMP_WIN), F32)
    for r in range(NSA_GROUP):
        q = q_ref[:, r * HEAD_DIM:(r + 1) * HEAD_DIM]
        s_far = jnp.where(far_ok, lax.dot_general(q, k_far, _NT, preferred_element_type=F32), NEG)
        bias_w = cb_ref[0, r]
        s_win = lax.dot_general(q, k_win, _NT, preferred_element_type=F32) + bias_w
        win_ok = bias_w > 0.5 * NEG
        mx = jnp.maximum(jnp.max(s_far, axis=1, keepdims=True), jnp.max(s_win, axis=1, keepdims=True))
        e_far = jnp.where(far_ok, jnp.exp(s_far - mx), 0.0)
        e_win = jnp.where(win_ok, jnp.exp(s_win - mx), 0.0)
        den = jnp.sum(e_far, axis=1, keepdims=True) + jnp.sum(e_win, axis=1, keepdims=True)
        inv = 1.0 / jnp.maximum(den, TINY)
        p_far = e_far * inv
        p_win = e_win * inv
        o = (jnp.dot(p_far.astype(BF16), v_far, preferred_element_type=F32)
             + jnp.dot(p_win.astype(BF16), v_win, preferred_element_type=F32))
        oc_ref[:, r * HEAD_DIM:(r + 1) * HEAD_DIM] = o
        p_far_sum = p_far_sum + p_far
        p_win_sum = p_win_sum + p_win

    ov_far = ov_ref[CMP_PAD:, :]
    ov_win = ov_ref[pl.ds(woff, CMP_WIN), :]
    imp = jnp.zeros((TQ, N_SLC), F32)
    for part in _split3(p_far_sum):
        imp = imp + jnp.dot(part, ov_far, preferred_element_type=F32)
    for part in _split3(p_win_sum):
        imp = imp + jnp.dot(part, ov_win, preferred_element_type=F32)

    jj = lax.broadcasted_iota(jnp.int32, (TQ, N_SLC), 1)
    a = lax.broadcasted_iota(jnp.int32, (TQ, N_SLC), 0)
    cur = i * SLC_PER_TILE + lax.shift_right_logical(a, 6)
    allowed = jj <= cur
    forced = (jj == 0) | (jj == cur) | (jj == cur - 1)
    score = jnp.where(allowed, jnp.where(forced, FORCE_SCORE, imp), -1.0)
    selb = _topk_bias(score, SLC_TOPK)
    sel_ref[0] = jnp.where(allowed, selb, NEG).astype(sel_ref.dtype)


def _cmp_select(zb, kvc, ov, cmpb):
    return pl.pallas_call(
        _cmp_kernel,
        grid=(NSA_KV_HEADS, N_QT),
        in_specs=[pl.BlockSpec((TQ, NSA_GROUP * HEAD_DIM), lambda g, i: (i, COL_NQ // NSA_GROUP + g)),
                  pl.BlockSpec((1, CMP_ROWS, HEAD_DIM), lambda g, i: (g, 0, 0)),
                  pl.BlockSpec((1, CMP_ROWS, HEAD_DIM), lambda g, i: (2 + g, 0, 0)),
                  pl.BlockSpec((CMP_ROWS, N_SLC), lambda g, i: (0, 0)),
                  pl.BlockSpec((1, NSA_GROUP, TQ, CMP_WIN), lambda g, i: (jnp.minimum(i, 1), 2 + g, 0, 0))],
        out_specs=[pl.BlockSpec((TQ, NSA_GROUP * HEAD_DIM), lambda g, i: (i, g)),
                   pl.BlockSpec((1, TQ, N_SLC), lambda g, i: (g, i, 0))],
        out_shape=[jax.ShapeDtypeStruct((SEQ, NSA_HEADS * HEAD_DIM), F32),
                   jax.ShapeDtypeStruct((NSA_KV_HEADS, SEQ, N_SLC), BF16)],
        compiler_params=_cparams("parallel", "arbitrary"),
        name="cmp_select",
    )(zb, kvc, kvc, ov, cmpb)


def _stack_heads(q_ref, extra):
    rows = []
    for r in range(NSA_GROUP):
        q = q_ref[:, r * HEAD_DIM:(r + 1) * HEAD_DIM]
        rows.append(q if extra is None else jnp.concatenate([q, extra], axis=1))
    return jnp.concatenate(rows, axis=0)


def _store_heads(o_ref, acc_ref, l_ref):
    o = acc_ref[...] / l_ref[...]
    for r in range(NSA_GROUP):
        o_ref[:, r * HEAD_DIM:(r + 1) * HEAD_DIM] = o[r * TQ:(r + 1) * TQ].astype(o_ref.dtype)


def _slc_kernel(q_ref, sel_ref, k_ref, v_ref, d0_ref, d1_ref, o_ref, qa_ref, m_ref, l_ref, acc_ref):
    i = pl.program_id(1)
    qa_ref[...] = _stack_heads(q_ref, sel_ref[0])
    row_blk = lax.shift_right_logical(lax.broadcasted_iota(jnp.int32, (TQ, HEAD_DIM), 0), 6)
    lane = lax.broadcasted_iota(jnp.int32, (TQ, HEAD_DIM), 1)

    def step(j, bias):
        off = pl.multiple_of(j * TQ, TQ)
        onehot = jnp.where(lane == j * SLC_PER_TILE + row_blk, 1.0, 0.0).astype(BF16)
        _flash_step(qa_ref[...], jnp.concatenate([k_ref[pl.ds(off, TQ), :], onehot], axis=1),
                    v_ref[pl.ds(off, TQ), :], bias, m_ref, l_ref, acc_ref)

    _flash_init(m_ref, l_ref, acc_ref)
    step(i, d0_ref[0].reshape(NSA_GROUP * TQ, TQ))

    @pl.when(i >= 1)
    def _():
        step(i - 1, d1_ref[0].reshape(NSA_GROUP * TQ, TQ))

    def body(j, c):
        step(j, None)
        return c

    lax.fori_loop(0, i - 1, body, 0)
    _store_heads(o_ref, acc_ref, l_ref)


def _slc(zb, selb, near):
    rows = NSA_GROUP * TQ
    return pl.pallas_call(
        _slc_kernel,
        grid=(NSA_KV_HEADS, N_QT),
        in_specs=[pl.BlockSpec((TQ, NSA_GROUP * HEAD_DIM), lambda g, i: (i, COL_NQ // NSA_GROUP + g)),
                  pl.BlockSpec((1, TQ, N_SLC), lambda g, i: (g, i, 0)),
                  pl.BlockSpec((SEQ, HEAD_DIM), lambda g, i: (0, COL_SK + g)),
                  pl.BlockSpec((SEQ, HEAD_DIM), lambda g, i: (0, COL_SV + g)),
                  pl.BlockSpec((1, NSA_GROUP, TQ, TQ), lambda g, i: (0, 2 + g, 0, 0)),
                  pl.BlockSpec((1, NSA_GROUP, TQ, TQ), lambda g, i: (1, 2 + g, 0, 0))],
        out_specs=pl.BlockSpec((TQ, NSA_GROUP * HEAD_DIM), lambda g, i: (i, g)),
        out_shape=jax.ShapeDtypeStruct((SEQ, NSA_HEADS * HEAD_DIM), F32),
        scratch_shapes=[pltpu.VMEM((rows, 2 * HEAD_DIM), BF16),
                        pltpu.VMEM((rows, 1), F32), pltpu.VMEM((rows, 1), F32),
                        pltpu.VMEM((rows, HEAD_DIM), F32)],
        compiler_params=_cparams("parallel", "arbitrary"),
        name="slc",
    )(zb, selb, zb, zb, near, near)


def _win_kernel(q_ref, k_ref, v_ref, d0_ref, d1_ref, w2_ref, o_ref, qa_ref, m_ref, l_ref, acc_ref):
    i = pl.program_id(1)
    qa_ref[...] = _stack_heads(q_ref, None)

    def step(j, bias):
        off = pl.multiple_of(j * TQ, TQ)
        _flash_step(qa_ref[...], k_ref[pl.ds(off, TQ), :], v_ref[pl.ds(off, TQ), :], bias,
                    m_ref, l_ref, acc_ref)

    _flash_init(m_ref, l_ref, acc_ref)
    step(i, d0_ref[0].reshape(NSA_GROUP * TQ, TQ))

    @pl.when(i >= 1)
    def _():
        step(i - 1, d1_ref[0].reshape(NSA_GROUP * TQ, TQ))

    @pl.when(i >= 2)
    def _():
        step(i - 2, w2_ref[...])

    _store_heads(o_ref, acc_ref, l_ref)


def _win(zb, near, win2):
    rows = NSA_GROUP * TQ
    return pl.pallas_call(
        _win_kernel,
        grid=(NSA_KV_HEADS, N_QT),
        in_specs=[pl.BlockSpec((TQ, NSA_GROUP * HEAD_DIM), lambda g, i: (i, COL_NQ // NSA_GROUP + g)),
                  pl.BlockSpec((SEQ, HEAD_DIM), lambda g, i: (0, COL_WK + g)),
                  pl.BlockSpec((SEQ, HEAD_DIM), lambda g, i: (0, COL_WV + g)),
                  pl.BlockSpec((1, NSA_GROUP, TQ, TQ), lambda g, i: (0, 2 + g, 0, 0)),
                  pl.BlockSpec((1, NSA_GROUP, TQ, TQ), lambda g, i: (1, 2 + g, 0, 0)),
                  pl.BlockSpec((rows, TQ), lambda g, i: (0, 0))],
        out_specs=pl.BlockSpec((TQ, NSA_GROUP * HEAD_DIM), lambda g, i: (i, g)),
        out_shape=jax.ShapeDtypeStruct((SEQ, NSA_HEADS * HEAD_DIM), F32),
        scratch_shapes=[pltpu.VMEM((rows, HEAD_DIM), BF16),
                        pltpu.VMEM((rows, 1), F32), pltpu.VMEM((rows, 1), F32),
                        pltpu.VMEM((rows, HEAD_DIM), F32)],
        compiler_params=_cparams("parallel", "arbitrary"),
        name="win",
    )(zb, zb, zb, near, near, win2)


def _layer_norm(y, g, b):
    mu = jnp.mean(y, axis=-1, keepdims=True)
    yc = y - mu
    var = jnp.mean(yc * yc, axis=-1, keepdims=True)
    return yc * lax.rsqrt(var + LN_EPS) * g + b


def _outproj_kernel(x_ref, om_ref, oc_ref, os_ref, ow_ref, g_ref, e_ref, w_ref, lg_ref, lb_ref,
                    h_ref, hb_ref):
    gate = g_ref[...]
    parts = _split3(gate)
    o_n = jnp.zeros(oc_ref.shape, F32)
    for c, branch in enumerate((oc_ref, os_ref, ow_ref)):
        gexp = jnp.zeros(oc_ref.shape, F32)
        for part in parts:
            gexp = gexp + jnp.dot(part, e_ref[c], preferred_element_type=F32)
        o_n = o_n + gexp * branch[...]
    o = jnp.concatenate([om_ref[...], o_n.astype(BF16)], axis=1)
    a = jnp.dot(o, w_ref[...], preferred_element_type=F32)
    h = _layer_norm(DN_ALPHA * x_ref[...] + a, lg_ref[...], lb_ref[...])
    h_ref[...] = h
    hb_ref[...] = h.astype(BF16)


def _outproj(x, o_m, o_c, o_s, o_w, gates, expand, w_out, ln_g, ln_b, tm=256):
    half = NSA_HEADS * HEAD_DIM
    row = lambda i: (i, 0)
    fixed2 = lambda i: (0, 0)
    return pl.pallas_call(
        _outproj_kernel,
        grid=(SEQ // tm,),
        in_specs=[pl.BlockSpec((tm, D_MODEL), row),
                  pl.BlockSpec((tm, half), row), pl.BlockSpec((tm, half), row),
                  pl.BlockSpec((tm, half), row), pl.BlockSpec((tm, half), row),
                  pl.BlockSpec((tm, HEAD_DIM), row),
                  pl.BlockSpec((3, HEAD_DIM, half), lambda i: (0, 0, 0)),
                  pl.BlockSpec((D_MODEL, D_MODEL), fixed2),
                  pl.BlockSpec((1, D_MODEL), fixed2), pl.BlockSpec((1, D_MODEL), fixed2)],
        out_specs=[pl.BlockSpec((tm, D_MODEL), row), pl.BlockSpec((tm, D_MODEL), row)],
        out_shape=[jax.ShapeDtypeStruct((SEQ, D_MODEL), F32),
                   jax.ShapeDtypeStruct((SEQ, D_MODEL), BF16)],
        compiler_params=_cparams("parallel"),
        name="outproj_ln",
    )(x, o_m, o_c, o_s, o_w, gates, expand, w_out, ln_g, ln_b)


def _ffn_kernel(hb_ref, w1_ref, w2_ref, f_ref):
    @pl.when(pl.program_id(1) == 0)
    def _():
        f_ref[...] = jnp.zeros(f_ref.shape, F32)

    u = jnp.maximum(jnp.dot(hb_ref[...], w1_ref[...], preferred_element_type=F32), 0.0)
    f_ref[...] += jnp.dot((u * u).astype(BF16), w2_ref[...], preferred_element_type=F32)


def _ffn(hb, w1, w2, tm=1024, tf=512):
    return pl.pallas_call(
        _ffn_kernel,
        grid=(SEQ // tm, D_FF // tf),
        in_specs=[pl.BlockSpec((tm, D_MODEL), lambda i, c: (i, 0)),
                  pl.BlockSpec((D_MODEL, tf), lambda i, c: (0, c)),
                  pl.BlockSpec((tf, D_MODEL), lambda i, c: (c, 0))],
        out_specs=pl.BlockSpec((tm, D_MODEL), lambda i, c: (i, 0)),
        out_shape=jax.ShapeDtypeStruct((SEQ, D_MODEL), F32),
        compiler_params=_cparams("parallel", "arbitrary"),
        name="ffn",
    )(hb, w1, w2)


def _final_kernel(h_ref, hb_ref, f_ref, p_ref, wg_ref, wp_ref, lg_ref, lb_ref, o_ref):
    gate = jax.nn.sigmoid(jnp.dot(hb_ref[...], wg_ref[...], preferred_element_type=F32))
    emb = jnp.dot(p_ref[...].astype(BF16), wp_ref[...], preferred_element_type=F32)
    y = DN_ALPHA * h_ref[...] + f_ref[...] + gate * emb
    o_ref[...] = _layer_norm(y, lg_ref[...], lb_ref[...])


def _final(h, hb, f, p, wg, wp, ln_g, ln_b, tm=256):
    row = lambda i: (i, 0)
    fixed2 = lambda i: (0, 0)
    return pl.pallas_call(
        _final_kernel,
        grid=(SEQ // tm,),
        in_specs=[pl.BlockSpec((tm, D_MODEL), row), pl.BlockSpec((tm, D_MODEL), row),
                  pl.BlockSpec((tm, D_MODEL), row), pl.BlockSpec((tm, D_PLE), row),
                  pl.BlockSpec((D_MODEL, D_MODEL), fixed2), pl.BlockSpec((D_PLE, D_MODEL), fixed2),
                  pl.BlockSpec((1, D_MODEL), fixed2), pl.BlockSpec((1, D_MODEL), fixed2)],
        out_specs=pl.BlockSpec((tm, D_MODEL), row),
        out_shape=jax.ShapeDtypeStruct((SEQ, D_MODEL), F32),
        compiler_params=_cparams("parallel"),
        name="final_ln",
    )(h, hb, f, p, wg, wp, ln_g, ln_b)


def _overlap_matrix():
    n = np.arange(512)[:, None]
    j = np.arange(N_SLC)[None, :]
    ov = ((n * CMP_STRIDE < j * SLC_LEN + SLC_LEN) & (n * CMP_STRIDE + CMP_LEN > j * SLC_LEN) & (n < N_CMP))
    return np.concatenate([np.zeros((CMP_PAD, N_SLC), np.float32), ov.astype(np.float32)], axis=0)


def _gate_expand():
    e = np.zeros((3, HEAD_DIM, NSA_HEADS * HEAD_DIM), np.float32)
    for c in range(3):
        for h in range(NSA_HEADS):
            e[c, 3 * h + c, h * HEAD_DIM:(h + 1) * HEAD_DIM] = 1.0
    return e


def _layer(h, p, w_in, cmp_pe_k, cmp_w1_k, cmp_w2_k, cmp_pe_v, cmp_w1_v, cmp_w2_v, tiles, w_out,
           ln1_g, ln1_b, w_ff1, w_ff2, w_ple, w_ple_gate, ln2_g, ln2_b):
    near, win2, cmpb = tiles
    scale = HEAD_DIM ** -0.5
    colscale = np.ones((1, GATE_COL0), np.float32)
    colscale[0, COL_MQ * HEAD_DIM:(COL_MQ + MOBA_HEADS) * HEAD_DIM] = scale
    colscale[0, COL_NQ * HEAD_DIM:(COL_NQ + NSA_HEADS) * HEAD_DIM] = scale
    zb = _inproj(h, w_in[:, :GATE_COL0].astype(BF16), jnp.asarray(colscale))
    w_gl = jnp.pad(w_in[:, GATE_COL0:], ((0, 0), (0, HEAD_DIM - NSA_HEADS * 3))).astype(BF16)
    gates = _gates(h, w_gl)

    ckv = zb[:, COL_CK * HEAD_DIM:(COL_CK + 4) * HEAD_DIM]
    xr = ckv.reshape(SEQ // CMP_STRIDE, CMP_STRIDE, 4, HEAD_DIM).transpose(2, 0, 1, 3)
    xr = xr.reshape(4, SEQ // CMP_STRIDE, CMP_STRIDE * HEAD_DIM)
    pe = jnp.stack([cmp_pe_k, cmp_pe_v]).reshape(2, 2, 1, CMP_STRIDE * HEAD_DIM).astype(F32)
    w1 = jnp.stack([cmp_w1_k, cmp_w1_v]).reshape(2, 2, CMP_STRIDE * HEAD_DIM, HEAD_DIM).astype(BF16)
    w2 = jnp.stack([cmp_w2_k, cmp_w2_v]).astype(BF16)
    kvc = _compress(xr, pe, w1, w2)

    o_m = _moba(zb, near)
    o_c, selb = _cmp_select(zb, kvc, jnp.asarray(_overlap_matrix(), BF16), cmpb)
    o_s = _slc(zb, selb, near)
    o_w = _win(zb, near, win2)

    h1, h1b = _outproj(h, o_m, o_c, o_s, o_w, gates, jnp.asarray(_gate_expand(), BF16),
                       w_out.astype(BF16), ln1_g.reshape(1, -1), ln1_b.reshape(1, -1))
    f = _ffn(h1b, w_ff1.astype(BF16), w_ff2.astype(BF16))
    return _final(h1, h1b, f, p, w_ple_gate.astype(BF16), w_ple.astype(BF16),
                  ln2_g.reshape(1, -1), ln2_b.reshape(1, -1))


def kernel(x, p, w_in, cmp_pe_k, cmp_w1_k, cmp_w2_k, cmp_pe_v, cmp_w1_v, cmp_w2_v, rel_bias, w_out,
           ln1_g, ln1_b, w_ff1, w_ff2, w_ple, w_ple_gate, ln2_g, ln2_b):
    tiles = _bias_tiles(rel_bias)
    h = x[0]
    for i in range(w_in.shape[0]):
        h = _layer(h, p[i, 0], w_in[i], cmp_pe_k[i], cmp_w1_k[i], cmp_w2_k[i], cmp_pe_v[i], cmp_w1_v[i],
                   cmp_w2_v[i], tiles, w_out[i], ln1_g[i], ln1_b[i], w_ff1[i], w_ff2[i], w_ple[i],
                   w_ple_gate[i], ln2_g[i], ln2_b[i])
    return h[None]
```

```python
import functools
import math

import numpy as np
import jax
import jax.numpy as jnp
from jax import lax
from jax.experimental import pallas as pl
from jax.experimental.pallas import tpu as pltpu

D_MODEL = 2048
SEQ = 8192
HEAD_DIM = 128
N_HEADS = 16
MOBA_HEADS = 8
NSA_HEADS = 8
NSA_KV_HEADS = 2
NSA_GROUP = 4
MOBA_BLOCK = 256
MOBA_TOPK = 3
CMP_LEN = 32
CMP_STRIDE = 16
SLC_LEN = 64
SLC_TOPK = 16
WINDOW = 512
N_BUCKETS = 32
MAX_DISTANCE = 128
D_FF = 4 * D_MODEL
D_PLE = 256
LN_EPS = 1e-5
DN_ALPHA = 2.0 ** 0.25
NEG = -1e30
TINY = 1e-30
FORCE_SCORE = 1e4

N_CMP = (SEQ - CMP_LEN) // CMP_STRIDE + 1
N_SLC = SEQ // SLC_LEN
TQ = 256
N_QT = SEQ // TQ
CMP_PAD = 16
CMP_ROWS = CMP_PAD + 512
CMP_WIN = 32
SLC_PER_TILE = TQ // SLC_LEN
HP = 2
KG = 4
GROUP = KG * TQ
COL_MQ, COL_MK, COL_MV, COL_NQ = 0, 8, 16, 24
COL_CK, COL_CV, COL_SK, COL_SV, COL_WK, COL_WV = 32, 34, 36, 38, 40, 42
N_COLBLK = 44
GATE_COL0 = N_COLBLK * HEAD_DIM

VMEM_LIMIT = 48 * 1024 * 1024

BF16 = jnp.bfloat16
F32 = jnp.float32
_NT = (((1,), (1,)), ((), ()))


def _cparams(*sem):
    return pltpu.CompilerParams(dimension_semantics=sem, vmem_limit_bytes=VMEM_LIMIT)


def _bucket_table():
    n = np.arange(1024, dtype=np.int32)
    max_exact = N_BUCKETS // 2
    ratio = np.maximum(n, 1).astype(np.float32) / np.float32(max_exact)
    large = max_exact + (np.log(ratio).astype(np.float32) / np.float32(math.log(MAX_DISTANCE / max_exact))
                         * np.float32(N_BUCKETS - max_exact)).astype(np.int32)
    large = np.minimum(large, N_BUCKETS - 1)
    return np.where(n < max_exact, n, large).astype(np.int32)


def _bias_kernel(tab_ref, near_ref, cmpb_ref, *, steps):
    h = pl.program_id(0)
    far = tab_ref[h, N_BUCKETS - 1]

    def lookup(d):
        val = jnp.full(d.shape, tab_ref[h, 0], F32)
        for start, bucket in steps:
            val = jnp.where(d >= start, tab_ref[h, bucket], val)
        return jnp.where(d >= 0, val - far, NEG)

    a = lax.broadcasted_iota(jnp.int32, (TQ, TQ), 0)
    b = lax.broadcasted_iota(jnp.int32, (TQ, TQ), 1)
    near_ref[0, 0] = lookup(a - b)
    near_ref[1, 0] = lookup(a - b + TQ)
    aw = lax.broadcasted_iota(jnp.int32, (TQ, CMP_WIN), 0)
    w = lax.broadcasted_iota(jnp.int32, (TQ, CMP_WIN), 1)
    cm = lookup(aw - CMP_STRIDE * w + (TQ - CMP_LEN + 1))
    cmpb_ref[0, 0] = jnp.where(w >= CMP_PAD, cm, NEG)
    cmpb_ref[1, 0] = cm


def _bias_tiles(rel_bias):
    bucket = _bucket_table()
    steps = tuple((int(n), int(bucket[n])) for n in range(1, bucket.size) if bucket[n] != bucket[n - 1])
    near, cmpb = pl.pallas_call(
        functools.partial(_bias_kernel, steps=steps),
        grid=(N_HEADS,),
        in_specs=[pl.BlockSpec(memory_space=pltpu.SMEM)],
        out_specs=[pl.BlockSpec((2, 1, TQ, TQ), lambda h: (0, h, 0, 0)),
                   pl.BlockSpec((2, 1, TQ, CMP_WIN), lambda h: (0, h, 0, 0))],
        out_shape=[jax.ShapeDtypeStruct((2, N_HEADS, TQ, TQ), F32),
                   jax.ShapeDtypeStruct((2, N_HEADS, TQ, CMP_WIN), F32)],
        compiler_params=_cparams("parallel"),
        name="bias_tiles",
    )(rel_bias.T.astype(F32))
    a = np.arange(TQ)[:, None]
    b = np.arange(TQ)[None, :]
    win2 = jnp.asarray(np.where(a < b, 0.0, NEG).astype(np.float32))
    return near, win2, cmpb


def _inproj_kernel(x_ref, w_ref, cs_ref, o_ref, xb_ref):
    @pl.when(pl.program_id(1) == 0)
    def _():
        xb_ref[...] = x_ref[...].astype(BF16)

    acc = jnp.dot(xb_ref[...], w_ref[...], preferred_element_type=F32)
    o_ref[...] = (acc * cs_ref[...]).astype(o_ref.dtype)


def _inproj(x, w, colscale, tm=1024, tn=512):
    m, k = x.shape
    n = w.shape[1]
    return pl.pallas_call(
        _inproj_kernel,
        grid=(m // tm, n // tn),
        in_specs=[pl.BlockSpec((tm, k), lambda i, j: (i, 0)),
                  pl.BlockSpec((k, tn), lambda i, j: (0, j)),
                  pl.BlockSpec((1, tn), lambda i, j: (0, j))],
        out_specs=pl.BlockSpec((tm, tn), lambda i, j: (i, j)),
        out_shape=jax.ShapeDtypeStruct((m, n), BF16),
        scratch_shapes=[pltpu.VMEM((tm, k), BF16)],
        compiler_params=_cparams("parallel", "arbitrary"),
        name="inproj",
    )(x, w, colscale)


def _gates_kernel(x_ref, w_ref, o_ref):
    z = jnp.dot(x_ref[...].astype(BF16), w_ref[...], preferred_element_type=F32)
    o_ref[...] = jax.nn.sigmoid(z)


def _gates(x, wg, tm=1024):
    m, k = x.shape
    return pl.pallas_call(
        _gates_kernel,
        grid=(m // tm,),
        in_specs=[pl.BlockSpec((tm, k), lambda i: (i, 0)),
                  pl.BlockSpec((k, HEAD_DIM), lambda i: (0, 0))],
        out_specs=pl.BlockSpec((tm, HEAD_DIM), lambda i: (i, 0)),
        out_shape=jax.ShapeDtypeStruct((m, HEAD_DIM), F32),
        compiler_params=_cparams("parallel"),
        name="gates",
    )(x, wg)


def _gelu_tanh(x):
    return 0.5 * x * (1.0 + jnp.tanh(math.sqrt(2.0 / math.pi) * (x + 0.044715 * (x * x * x))))


def _compress_kernel(x_ref, pe_ref, w1_ref, w2_ref, o_ref):
    x = x_ref[0].astype(F32)
    lo = jnp.dot((x + pe_ref[0, 0]).astype(BF16), w1_ref[0, 0], preferred_element_type=F32)
    hi = jnp.dot((x + pe_ref[0, 1]).astype(BF16), w1_ref[0, 1], preferred_element_type=F32)
    y = lo + pltpu.roll(hi, 511, 0)
    h = _gelu_tanh(y).astype(BF16)
    out = jnp.dot(h, w2_ref[0], preferred_element_type=F32)
    row = lax.broadcasted_iota(jnp.int32, out.shape, 0)
    out = jnp.where(row < N_CMP, out, 0.0)
    o_ref[0, :CMP_PAD, :] = jnp.zeros((CMP_PAD, HEAD_DIM), o_ref.dtype)
    o_ref[0, CMP_PAD:, :] = out.astype(o_ref.dtype)


def _compress(xr, pe, w1, w2):
    return pl.pallas_call(
        _compress_kernel,
        grid=(4,),
        in_specs=[pl.BlockSpec((1, 512, 2048), lambda i: (i, 0, 0)),
                  pl.BlockSpec((1, 2, 1, 2048), lambda i: (i // 2, 0, 0, 0)),
                  pl.BlockSpec((1, 2, 2048, HEAD_DIM), lambda i: (i // 2, 0, 0, 0)),
                  pl.BlockSpec((1, HEAD_DIM, HEAD_DIM), lambda i: (i // 2, 0, 0))],
        out_specs=pl.BlockSpec((1, CMP_ROWS, HEAD_DIM), lambda i: (i, 0, 0)),
        out_shape=jax.ShapeDtypeStruct((4, CMP_ROWS, HEAD_DIM), BF16),
        compiler_params=_cparams("parallel"),
        name="compress",
    )(xr, pe, w1, w2)


def _split3(x):
    hi = x.astype(BF16)
    r1 = x - hi.astype(F32)
    mid = r1.astype(BF16)
    lo = (r1 - mid.astype(F32)).astype(BF16)
    return hi, mid, lo


def _topk_bias(score, k):
    lane = lax.broadcasted_iota(jnp.int32, score.shape, 1).astype(F32)
    selb = jnp.full(score.shape, NEG, F32)
    s = score
    for _ in range(k):
        mx = jnp.max(s, axis=1, keepdims=True)
        idx = jnp.min(jnp.where(s == mx, lane, 1e9), axis=1, keepdims=True)
        hit = lane == idx
        selb = jnp.where(hit, 0.0, selb)
        s = jnp.where(hit, -3e38, s)
    return selb


def _lane_tile_max(s):
    m = s[:, :HEAD_DIM]
    for c in range(1, s.shape[1] // HEAD_DIM):
        m = jnp.maximum(m, s[:, c * HEAD_DIM:(c + 1) * HEAD_DIM])
    return m


def _sparse_flash(qs, selbs, k_ref, v_ref, kcols, pat_ref, d0s, d1s, i, sub, s_ref, mx_ref, acc_ref):
    heads = range(len(qs))
    lane = lax.broadcasted_iota(jnp.int32, (TQ, HEAD_DIM), 1)
    far = lane < (i - 1) * sub
    q_near = [jnp.concatenate([qs[h], selbs[h].astype(BF16)], axis=1) for h in heads]
    q_far = [jnp.concatenate([qs[h], jnp.where(far, selbs[h], NEG).astype(BF16)], axis=1) for h in heads]

    neg_tile = jnp.full((TQ, TQ), NEG, F32)
    for h in heads:
        for u in range(1, KG):
            s_ref[h, jnp.minimum(i + u, N_QT - 1)] = neg_tile
        mx_ref[h] = jnp.full((TQ, HEAD_DIM), NEG, F32)
        acc_ref[h] = jnp.zeros((TQ, 2 * HEAD_DIM), F32)

    def k_aug(h, off, n):
        c = kcols[h]
        return jnp.concatenate([k_ref[pl.ds(off, n), c:c + HEAD_DIM], pat_ref[pl.ds(off, n), :]], axis=1)

    def far_group(g, carry):
        off = pl.multiple_of(g * GROUP, GROUP)
        for h in heads:
            s = lax.dot_general(q_far[h], k_aug(h, off, GROUP), _NT, preferred_element_type=F32)
            for u in range(KG):
                s_ref[h, g * KG + u] = s[:, u * TQ:(u + 1) * TQ]
            mx_ref[h] = jnp.maximum(mx_ref[h], _lane_tile_max(s))
        return carry

    n_far = lax.shift_right_logical(jnp.maximum(i - 1, 0) + (KG - 1), 2)
    lax.fori_loop(0, n_far, far_group, 0)

    def near_tile(j, biases):
        off = pl.multiple_of(j * TQ, TQ)
        for h in heads:
            s = lax.dot_general(q_near[h], k_aug(h, off, TQ), _NT, preferred_element_type=F32) + biases[h]
            s_ref[h, j] = s
            mx_ref[h] = jnp.maximum(mx_ref[h], _lane_tile_max(s))

    @pl.when(i >= 1)
    def _():
        near_tile(i - 1, d1s)

    near_tile(i, d0s)

    for h in heads:
        m = jnp.max(mx_ref[h], axis=1, keepdims=True)
        mx_ref[h] = jnp.broadcast_to(m, (TQ, HEAD_DIM))
    ones_v = jnp.ones((GROUP, HEAD_DIM), BF16)

    def pv_group(g, carry):
        off = pl.multiple_of(g * GROUP, GROUP)
        for h in heads:
            m = mx_ref[h]
            m2 = jnp.concatenate([m, m], axis=1)
            p = jnp.concatenate([jnp.exp(s_ref[h, g * KG + u] - m2).astype(BF16) for u in range(KG)], axis=1)
            c = kcols[h]
            v_aug = jnp.concatenate([v_ref[pl.ds(off, GROUP), c:c + HEAD_DIM], ones_v], axis=1)
            acc_ref[h] += jnp.dot(p, v_aug, preferred_element_type=F32)
        return carry

    lax.fori_loop(0, lax.shift_right_logical(i + KG, 2), pv_group, 0)
    outs = []
    for h in heads:
        acc = acc_ref[h]
        outs.append(acc[:, :HEAD_DIM] / acc[:, HEAD_DIM:])
    return outs


_FLASH_SCRATCH = [pltpu.VMEM((HP, N_QT, TQ, TQ), F32),
                  pltpu.VMEM((HP, TQ, HEAD_DIM), F32),
                  pltpu.VMEM((HP, TQ, 2 * HEAD_DIM), F32)]


def _moba_kernel(q_ref, k_ref, v_ref, pat_ref, d0_ref, d1_ref, o_ref, kmean_ref, s_ref, mx_ref, acc_ref):
    i = pl.program_id(1)

    @pl.when(i == 0)
    def _():
        blk = lax.broadcasted_iota(jnp.int32, (HEAD_DIM, SEQ), 0)
        pos = lax.broadcasted_iota(jnp.int32, (HEAD_DIM, SEQ), 1)
        ind = jnp.where(lax.shift_right_logical(pos, 8) == blk, 1.0 / MOBA_BLOCK, 0.0).astype(BF16)
        for h in range(HP):
            kmean_ref[h] = jnp.dot(ind, k_ref[:, h * HEAD_DIM:(h + 1) * HEAD_DIM], preferred_element_type=F32)

    lane = lax.broadcasted_iota(jnp.int32, (TQ, HEAD_DIM), 1)
    past = lane < i
    qs, selbs = [], []
    for h in range(HP):
        q = q_ref[:, h * HEAD_DIM:(h + 1) * HEAD_DIM]
        gate = jnp.zeros((TQ, HEAD_DIM), F32)
        for part in _split3(kmean_ref[h]):
            gate = gate + lax.dot_general(q, part, _NT, preferred_element_type=F32)
        selb = _topk_bias(jnp.where(past, gate, NEG), MOBA_TOPK)
        qs.append(q)
        selbs.append(jnp.where(past, selb, jnp.where(lane == i, 0.0, NEG)))
    cols = [h * HEAD_DIM for h in range(HP)]
    outs = _sparse_flash(qs, selbs, k_ref, v_ref, cols, pat_ref, [d0_ref[0, h] for h in range(HP)],
                         [d1_ref[0, h] for h in range(HP)], i, 1, s_ref, mx_ref, acc_ref)
    for h in range(HP):
        o_ref[:, h * HEAD_DIM:(h + 1) * HEAD_DIM] = outs[h].astype(o_ref.dtype)


def _moba(zb, near, pat):
    w = HP * HEAD_DIM
    return pl.pallas_call(
        _moba_kernel,
        grid=(MOBA_HEADS // HP, N_QT),
        in_specs=[pl.BlockSpec((TQ, w), lambda h, i: (i, COL_MQ // HP + h)),
                  pl.BlockSpec((SEQ, w), lambda h, i: (0, COL_MK // HP + h)),
                  pl.BlockSpec((SEQ, w), lambda h, i: (0, COL_MV // HP + h)),
                  pl.BlockSpec((SEQ, HEAD_DIM), lambda h, i: (0, 0)),
                  pl.BlockSpec((1, HP, TQ, TQ), lambda h, i: (0, h, 0, 0)),
                  pl.BlockSpec((1, HP, TQ, TQ), lambda h, i: (1, h, 0, 0))],
        out_specs=pl.BlockSpec((TQ, w), lambda h, i: (i, h)),
        out_shape=jax.ShapeDtypeStruct((SEQ, MOBA_HEADS * HEAD_DIM), BF16),
        scratch_shapes=[pltpu.VMEM((HP, HEAD_DIM, HEAD_DIM), F32)] + _FLASH_SCRATCH,
        compiler_params=_cparams("parallel", "arbitrary"),
        name="moba",
    )(zb, zb, zb, pat, near, near)


def _cmp_kernel(q_ref, kc_ref, vc_ref, ov_ref, cb_ref, oc_ref, sel_ref):
    i = pl.program_id(1)
    woff = pl.multiple_of(i * CMP_STRIDE, CMP_STRIDE)
    k_far = kc_ref[0, CMP_PAD:, :]
    v_far = vc_ref[0, CMP_PAD:, :]
    k_win = kc_ref[0, pl.ds(woff, CMP_WIN), :]
    v_win = vc_ref[0, pl.ds(woff, CMP_WIN), :]
    n_far = lax.broadcasted_iota(jnp.int32, (TQ, 512), 1)
    far_ok = n_far < i * CMP_STRIDE - CMP_PAD
    p_far_sum = jnp.zeros((TQ, 512), F32)
    p_win_sum = jnp.zeros((TQ, CMP_WIN), F32)
    for r in range(NSA_GROUP):
        q = q_ref[:, r * HEAD_DIM:(r + 1) * HEAD_DIM]
        s_far = jnp.where(far_ok, lax.dot_general(q, k_far, _NT, preferred_element_type=F32), NEG)
        bias_w = cb_ref[0, r]
        s_win = lax.dot_general(q, k_win, _NT, preferred_element_type=F32) + bias_w
        win_ok = bias_w > 0.5 * NEG
        mx = jnp.maximum(jnp.max(s_far, axis=1, keepdims=True), jnp.max(s_win, axis=1, keepdims=True))
        e_far = jnp.where(far_ok, jnp.exp(s_far - mx), 0.0)
        e_win = jnp.where(win_ok, jnp.exp(s_win - mx), 0.0)
        den = jnp.sum(e_far, axis=1, keepdims=True) + jnp.sum(e_win, axis=1, keepdims=True)
        inv = 1.0 / jnp.maximum(den, TINY)
        p_far = e_far * inv
        p_win = e_win * inv
        o = (jnp.dot(p_far.astype(BF16), v_far, preferred_element_type=F32)
             + jnp.dot(p_win.astype(BF16), v_win, preferred_element_type=F32))
        oc_ref[:, r * HEAD_DIM:(r + 1) * HEAD_DIM] = o
        p_far_sum = p_far_sum + p_far
        p_win_sum = p_win_sum + p_win

    ov_far = ov_ref[CMP_PAD:, :]
    ov_win = ov_ref[pl.ds(woff, CMP_WIN), :]
    imp = jnp.zeros((TQ, N_SLC), F32)
    for part in _split3(p_far_sum):
        imp = imp + jnp.dot(part, ov_far, preferred_element_type=F32)
    for part in _split3(p_win_sum):
        imp = imp + jnp.dot(part, ov_win, preferred_element_type=F32)

    jj = lax.broadcasted_iota(jnp.int32, (TQ, N_SLC), 1)
    a = lax.broadcasted_iota(jnp.int32, (TQ, N_SLC), 0)
    cur = i * SLC_PER_TILE + lax.shift_right_logical(a, 6)
    allowed = jj <= cur
    forced = (jj == 0) | (jj == cur) | (jj == cur - 1)
    score = jnp.where(allowed, jnp.where(forced, FORCE_SCORE, imp), -1.0)
    selb = _topk_bias(score, SLC_TOPK)
    sel_ref[0] = jnp.where(allowed, selb, NEG).astype(sel_ref.dtype)


def _cmp_select(zb, kvc, ov, cmpb):
    return pl.pallas_call(
        _cmp_kernel,
        grid=(NSA_KV_HEADS, N_QT),
        in_specs=[pl.BlockSpec((TQ, NSA_GROUP * HEAD_DIM), lambda g, i: (i, COL_NQ // NSA_GROUP + g)),
                  pl.BlockSpec((1, CMP_ROWS, HEAD_DIM), lambda g, i: (g, 0, 0)),
                  pl.BlockSpec((1, CMP_ROWS, HEAD_DIM), lambda g, i: (2 + g, 0, 0)),
                  pl.BlockSpec((CMP_ROWS, N_SLC), lambda g, i: (0, 0)),
                  pl.BlockSpec((1, NSA_GROUP, TQ, CMP_WIN), lambda g, i: (jnp.minimum(i, 1), 2 + g, 0, 0))],
        out_specs=[pl.BlockSpec((TQ, NSA_GROUP * HEAD_DIM), lambda g, i: (i, g)),
                   pl.BlockSpec((1, TQ, N_SLC), lambda g, i: (g, i, 0))],
        out_shape=[jax.ShapeDtypeStruct((SEQ, NSA_HEADS * HEAD_DIM), F32),
                   jax.ShapeDtypeStruct((NSA_KV_HEADS, SEQ, N_SLC), BF16)],
        compiler_params=_cparams("parallel", "arbitrary"),
        name="cmp_select",
    )(zb, kvc, kvc, ov, cmpb)


def _slc_kernel(q_ref, sel_ref, k_ref, v_ref, pat_ref, d0_ref, d1_ref, o_ref, s_ref, mx_ref, acc_ref):
    i = pl.program_id(1)
    selb = sel_ref[0].astype(F32)
    qs = [q_ref[:, h * HEAD_DIM:(h + 1) * HEAD_DIM] for h in range(HP)]
    outs = _sparse_flash(qs, [selb] * HP, k_ref, v_ref, [0] * HP, pat_ref,
                         [d0_ref[0, h] for h in range(HP)], [d1_ref[0, h] for h in range(HP)],
                         i, SLC_PER_TILE, s_ref, mx_ref, acc_ref)
    for h in range(HP):
        o_ref[:, h * HEAD_DIM:(h + 1) * HEAD_DIM] = outs[h].astype(o_ref.dtype)


def _slc(zb, selb, near, pat):
    w = HP * HEAD_DIM
    per_group = NSA_GROUP // HP
    return pl.pallas_call(
        _slc_kernel,
        grid=(NSA_HEADS // HP, N_QT),
        in_specs=[pl.BlockSpec((TQ, w), lambda h, i: (i, COL_NQ // HP + h)),
                  pl.BlockSpec((1, TQ, N_SLC), lambda h, i: (h // per_group, i, 0)),
                  pl.BlockSpec((SEQ, HEAD_DIM), lambda h, i: (0, COL_SK + h // per_group)),
                  pl.BlockSpec((SEQ, HEAD_DIM), lambda h, i: (0, COL_SV + h // per_group)),
                  pl.BlockSpec((SEQ, HEAD_DIM), lambda h, i: (0, 0)),
                  pl.BlockSpec((1, HP, TQ, TQ), lambda h, i: (0, MOBA_HEADS // HP + h, 0, 0)),
                  pl.BlockSpec((1, HP, TQ, TQ), lambda h, i: (1, MOBA_HEADS // HP + h, 0, 0))],
        out_specs=pl.BlockSpec((TQ, w), lambda h, i: (i, h)),
        out_shape=jax.ShapeDtypeStruct((SEQ, NSA_HEADS * HEAD_DIM), F32),
        scratch_shapes=_FLASH_SCRATCH,
        compiler_params=_cparams("parallel", "arbitrary"),
        name="slc",
    )(zb, selb, zb, zb, pat, near, near)


def _win_kernel(q_ref, k_ref, v_ref, d0_ref, d1_ref, w2_ref, o_ref):
    i = pl.program_id(1)
    offs = [pl.multiple_of(jnp.maximum(i - b, 0) * TQ, TQ) for b in range(3)]
    k_all = jnp.concatenate([k_ref[pl.ds(off, TQ), :] for off in offs], axis=0)
    v_all = jnp.concatenate([v_ref[pl.ds(off, TQ), :] for off in offs], axis=0)
    v_aug = jnp.concatenate([v_all, jnp.ones((3 * TQ, HEAD_DIM), BF16)], axis=1)
    mask2 = jnp.where(i >= 2, w2_ref[...], NEG)
    for r in range(NSA_GROUP):
        q = q_ref[:, r * HEAD_DIM:(r + 1) * HEAD_DIM]
        bias = jnp.concatenate([d0_ref[0, r], jnp.where(i >= 1, d1_ref[0, r], NEG), mask2], axis=1)
        s = lax.dot_general(q, k_all, _NT, preferred_element_type=F32) + bias
        m = jnp.max(s, axis=1, keepdims=True)
        pv = jnp.dot(jnp.exp(s - m).astype(BF16), v_aug, preferred_element_type=F32)
        o_ref[:, r * HEAD_DIM:(r + 1) * HEAD_DIM] = pv[:, :HEAD_DIM] / pv[:, HEAD_DIM:]


def _win(zb, near, win2):
    return pl.pallas_call(
        _win_kernel,
        grid=(NSA_KV_HEADS, N_QT),
        in_specs=[pl.BlockSpec((TQ, NSA_GROUP * HEAD_DIM), lambda g, i: (i, COL_NQ // NSA_GROUP + g)),
                  pl.BlockSpec((SEQ, HEAD_DIM), lambda g, i: (0, COL_WK + g)),
                  pl.BlockSpec((SEQ, HEAD_DIM), lambda g, i: (0, COL_WV + g)),
                  pl.BlockSpec((1, NSA_GROUP, TQ, TQ), lambda g, i: (0, 2 + g, 0, 0)),
                  pl.BlockSpec((1, NSA_GROUP, TQ, TQ), lambda g, i: (1, 2 + g, 0, 0)),
                  pl.BlockSpec((TQ, TQ), lambda g, i: (0, 0))],
        out_specs=pl.BlockSpec((TQ, NSA_GROUP * HEAD_DIM), lambda g, i: (i, g)),
        out_shape=jax.ShapeDtypeStruct((SEQ, NSA_HEADS * HEAD_DIM), F32),
        compiler_params=_cparams("parallel", "arbitrary"),
        name="win",
    )(zb, zb, zb, near, near, win2)


def _layer_norm(y, g, b):
    mu = jnp.mean(y, axis=-1, keepdims=True)
    yc = y - mu
    var = jnp.mean(yc * yc, axis=-1, keepdims=True)
    return yc * lax.rsqrt(var + LN_EPS) * g + b


def _outproj_kernel(x_ref, om_ref, oc_ref, os_ref, ow_ref, g_ref, e_ref, w_ref, lg_ref, lb_ref,
                    h_ref, hb_ref):
    gate = g_ref[...]
    parts = _split3(gate)
    o_n = jnp.zeros(oc_ref.shape, F32)
    for c, branch in enumerate((oc_ref, os_ref, ow_ref)):
        gexp = jnp.zeros(oc_ref.shape, F32)
        for part in parts:
            gexp = gexp + jnp.dot(part, e_ref[c], preferred_element_type=F32)
        o_n = o_n + gexp * branch[...]
    o = jnp.concatenate([om_ref[...], o_n.astype(BF16)], axis=1)
    a = jnp.dot(o, w_ref[...], preferred_element_type=F32)
    h = _layer_norm(DN_ALPHA * x_ref[...] + a, lg_ref[...], lb_ref[...])
    h_ref[...] = h
    hb_ref[...] = h.astype(BF16)


def _outproj(x, o_m, o_c, o_s, o_w, gates, expand, w_out, ln_g, ln_b, tm=256):
    half = NSA_HEADS * HEAD_DIM
    row = lambda i: (i, 0)
    fixed2 = lambda i: (0, 0)
    return pl.pallas_call(
        _outproj_kernel,
        grid=(SEQ // tm,),
        in_specs=[pl.BlockSpec((tm, D_MODEL), row),
                  pl.BlockSpec((tm, half), row), pl.BlockSpec((tm, half), row),
                  pl.BlockSpec((tm, half), row), pl.BlockSpec((tm, half), row),
                  pl.BlockSpec((tm, HEAD_DIM), row),
                  pl.BlockSpec((3, HEAD_DIM, half), lambda i: (0, 0, 0)),
                  pl.BlockSpec((D_MODEL, D_MODEL), fixed2),
                  pl.BlockSpec((1, D_MODEL), fixed2), pl.BlockSpec((1, D_MODEL), fixed2)],
        out_specs=[pl.BlockSpec((tm, D_MODEL), row), pl.BlockSpec((tm, D_MODEL), row)],
        out_shape=[jax.ShapeDtypeStruct((SEQ, D_MODEL), F32),
                   jax.ShapeDtypeStruct((SEQ, D_MODEL), BF16)],
        compiler_params=_cparams("parallel"),
        name="outproj_ln",
    )(x, o_m, o_c, o_s, o_w, gates, expand, w_out, ln_g, ln_b)


def _ffn_kernel(hb_ref, w1_ref, w2_ref, f_ref):
    @pl.when(pl.program_id(1) == 0)
    def _():
        f_ref[...] = jnp.zeros(f_ref.shape, F32)

    u = jnp.maximum(jnp.dot(hb_ref[...], w1_ref[...], preferred_element_type=F32), 0.0)
    f_ref[...] += jnp.dot((u * u).astype(BF16), w2_ref[...], preferred_element_type=F32)


def _ffn(hb, w1, w2, tm=1024, tf=512):
    return pl.pallas_call(
        _ffn_kernel,
        grid=(SEQ // tm, D_FF // tf),
        in_specs=[pl.BlockSpec((tm, D_MODEL), lambda i, c: (i, 0)),
                  pl.BlockSpec((D_MODEL, tf), lambda i, c: (0, c)),
                  pl.BlockSpec((tf, D_MODEL), lambda i, c: (c, 0))],
        out_specs=pl.BlockSpec((tm, D_MODEL), lambda i, c: (i, 0)),
        out_shape=jax.ShapeDtypeStruct((SEQ, D_MODEL), F32),
        compiler_params=_cparams("parallel", "arbitrary"),
        name="ffn",
    )(hb, w1, w2)


def _final_kernel(h_ref, hb_ref, f_ref, p_ref, wg_ref, wp_ref, lg_ref, lb_ref, o_ref):
    gate = jax.nn.sigmoid(jnp.dot(hb_ref[...], wg_ref[...], preferred_element_type=F32))
    emb = jnp.dot(p_ref[...].astype(BF16), wp_ref[...], preferred_element_type=F32)
    y = DN_ALPHA * h_ref[...] + f_ref[...] + gate * emb
    o_ref[...] = _layer_norm(y, lg_ref[...], lb_ref[...])


def _final(h, hb, f, p, wg, wp, ln_g, ln_b, tm=256):
    row = lambda i: (i, 0)
    fixed2 = lambda i: (0, 0)
    return pl.pallas_call(
        _final_kernel,
        grid=(SEQ // tm,),
        in_specs=[pl.BlockSpec((tm, D_MODEL), row), pl.BlockSpec((tm, D_MODEL), row),
                  pl.BlockSpec((tm, D_MODEL), row), pl.BlockSpec((tm, D_PLE), row),
                  pl.BlockSpec((D_MODEL, D_MODEL), fixed2), pl.BlockSpec((D_PLE, D_MODEL), fixed2),
                  pl.BlockSpec((1, D_MODEL), fixed2), pl.BlockSpec((1, D_MODEL), fixed2)],
        out_specs=pl.BlockSpec((tm, D_MODEL), row),
        out_shape=jax.ShapeDtypeStruct((SEQ, D_MODEL), F32),
        compiler_params=_cparams("parallel"),
        name="final_ln",
    )(h, hb, f, p, wg, wp, ln_g, ln_b)


def _overlap_matrix():
    n = np.arange(512)[:, None]
    j = np.arange(N_SLC)[None, :]
    ov = ((n * CMP_STRIDE < j * SLC_LEN + SLC_LEN) & (n * CMP_STRIDE + CMP_LEN > j * SLC_LEN) & (n < N_CMP))
    return np.concatenate([np.zeros((CMP_PAD, N_SLC), np.float32), ov.astype(np.float32)], axis=0)


def _gate_expand():
    e = np.zeros((3, HEAD_DIM, NSA_HEADS * HEAD_DIM), np.float32)
    for c in range(3):
        for h in range(NSA_HEADS):
            e[c, 3 * h + c, h * HEAD_DIM:(h + 1) * HEAD_DIM] = 1.0
    return e


def _layer(h, p, w_in, cmp_pe_k, cmp_w1_k, cmp_w2_k, cmp_pe_v, cmp_w1_v, cmp_w2_v, tiles, w_out,
           ln1_g, ln1_b, w_ff1, w_ff2, w_ple, w_ple_gate, ln2_g, ln2_b):
    near, win2, cmpb = tiles
    scale = HEAD_DIM ** -0.5
    colscale = np.ones((1, GATE_COL0), np.float32)
    colscale[0, COL_MQ * HEAD_DIM:(COL_MQ + MOBA_HEADS) * HEAD_DIM] = scale
    colscale[0, COL_NQ * HEAD_DIM:(COL_NQ + NSA_HEADS) * HEAD_DIM] = scale
    zb = _inproj(h, w_in[:, :GATE_COL0].astype(BF16), jnp.asarray(colscale))
    w_gl = jnp.pad(w_in[:, GATE_COL0:], ((0, 0), (0, HEAD_DIM - NSA_HEADS * 3))).astype(BF16)
    gates = _gates(h, w_gl)

    ckv = zb[:, COL_CK * HEAD_DIM:(COL_CK + 4) * HEAD_DIM]
    xr = ckv.reshape(SEQ // CMP_STRIDE, CMP_STRIDE, 4, HEAD_DIM).transpose(2, 0, 1, 3)
    xr = xr.reshape(4, SEQ // CMP_STRIDE, CMP_STRIDE * HEAD_DIM)
    pe = jnp.stack([cmp_pe_k, cmp_pe_v]).reshape(2, 2, 1, CMP_STRIDE * HEAD_DIM).astype(F32)
    w1 = jnp.stack([cmp_w1_k, cmp_w1_v]).reshape(2, 2, CMP_STRIDE * HEAD_DIM, HEAD_DIM).astype(BF16)
    w2 = jnp.stack([cmp_w2_k, cmp_w2_v]).astype(BF16)
    kvc = _compress(xr, pe, w1, w2)

    key = lax.broadcasted_iota(jnp.int32, (SEQ, HEAD_DIM), 0)
    lane = lax.broadcasted_iota(jnp.int32, (SEQ, HEAD_DIM), 1)
    pat_moba = (lane == key // MOBA_BLOCK).astype(BF16)
    pat_slc = (lane == key // SLC_LEN).astype(BF16)
    o_m = _moba(zb, near, pat_moba)
    o_c, selb = _cmp_select(zb, kvc, jnp.asarray(_overlap_matrix(), BF16), cmpb)
    o_s = _slc(zb, selb, near, pat_slc)
    o_w = _win(zb, near, win2)

    h1, h1b = _outproj(h, o_m, o_c, o_s, o_w, gates, jnp.asarray(_gate_expand(), BF16),
                       w_out.astype(BF16), ln1_g.reshape(1, -1), ln1_b.reshape(1, -1))
    f = _ffn(h1b, w_ff1.astype(BF16), w_ff2.astype(BF16))
    return _final(h1, h1b, f, p, w_ple_gate.astype(BF16), w_ple.astype(BF16),
                  ln2_g.reshape(1, -1), ln2_b.reshape(1, -1))


def kernel(x, p, w_in, cmp_pe_k, cmp_w1_k, cmp_w2_k, cmp_pe_v, cmp_w1_v, cmp_w2_v, rel_bias, w_out,
           ln1_g, ln1_b, w_ff1, w_ff2, w_ple, w_ple_gate, ln2_g, ln2_b):
    tiles = _bias_tiles(rel_bias)
    h = x[0]
    for i in range(w_in.shape[0]):
        h = _layer(h, p[i, 0], w_in[i], cmp_pe_k[i], cmp_w1_k[i], cmp_w2_k[i], cmp_pe_v[i], cmp_w1_v[i],
                   cmp_w2_v[i], tiles, w_out[i], ln1_g[i], ln1_b[i], w_ff1[i], w_ff2[i], w_ple[i],
                   w_ple_gate[i], ln2_g[i], ln2_b[i])
    return h[None]
```

```python
import functools
import math

import numpy as np
import jax
import jax.numpy as jnp
from jax import lax
from jax.experimental import pallas as pl
from jax.experimental.pallas import tpu as pltpu

D_MODEL = 2048
SEQ = 8192
HEAD_DIM = 128
N_HEADS = 16
MOBA_HEADS = 8
NSA_HEADS = 8
NSA_KV_HEADS = 2
NSA_GROUP = 4
MOBA_BLOCK = 256
MOBA_TOPK = 3
CMP_LEN = 32
CMP_STRIDE = 16
SLC_LEN = 64
SLC_TOPK = 16
WINDOW = 512
N_BUCKETS = 32
MAX_DISTANCE = 128
D_FF = 4 * D_MODEL
D_PLE = 256
LN_EPS = 1e-5
DN_ALPHA = 2.0 ** 0.25
NEG = -1e30
TINY = 1e-30
FORCE_SCORE = 1e4

N_CMP = (SEQ - CMP_LEN) // CMP_STRIDE + 1
N_SLC = SEQ // SLC_LEN
TQ = 256
N_QT = SEQ // TQ
CMP_WIN = 32
SLC_PER_TILE = TQ // SLC_LEN
HP = 2
KG = 4
GROUP = KG * TQ
COL_MQ, COL_MK, COL_MV, COL_NQ = 0, 8, 16, 24
COL_CK, COL_CV, COL_SK, COL_SV, COL_WK, COL_WV = 32, 34, 36, 38, 40, 42
N_COLBLK = 44
GATE_COL0 = N_COLBLK * HEAD_DIM

VMEM_LIMIT = 48 * 1024 * 1024
FFN_VMEM_LIMIT = 56 * 1024 * 1024

BF16 = jnp.bfloat16
F32 = jnp.float32
_NT = (((1,), (1,)), ((), ()))


def _cparams(*sem):
    return pltpu.CompilerParams(dimension_semantics=sem, vmem_limit_bytes=VMEM_LIMIT)


def _bucket_table():
    n = np.arange(1024, dtype=np.int32)
    max_exact = N_BUCKETS // 2
    ratio = np.maximum(n, 1).astype(np.float32) / np.float32(max_exact)
    large = max_exact + (np.log(ratio).astype(np.float32) / np.float32(math.log(MAX_DISTANCE / max_exact))
                         * np.float32(N_BUCKETS - max_exact)).astype(np.int32)
    large = np.minimum(large, N_BUCKETS - 1)
    return np.where(n < max_exact, n, large).astype(np.int32)


def _bias_kernel(tab_ref, near_ref, cmpb_ref, *, steps):
    h = pl.program_id(0)
    far = tab_ref[h, N_BUCKETS - 1]

    def lookup(d):
        val = jnp.full(d.shape, tab_ref[h, 0], F32)
        for start, bucket in steps:
            val = jnp.where(d >= start, tab_ref[h, bucket], val)
        return jnp.where(d >= 0, val - far, NEG)

    a = lax.broadcasted_iota(jnp.int32, (TQ, TQ), 0)
    b = lax.broadcasted_iota(jnp.int32, (TQ, TQ), 1)
    near_ref[0, 0] = lookup(a - b)
    near_ref[1, 0] = lookup(a - b + TQ)
    w = lax.broadcasted_iota(jnp.int32, (CMP_WIN, TQ), 0)
    aw = lax.broadcasted_iota(jnp.int32, (CMP_WIN, TQ), 1)
    cmpb_ref[0, 0] = lookup(aw - CMP_STRIDE * w - (CMP_LEN - 1))
    cmpb_ref[1, 0] = lookup(aw - CMP_STRIDE * w + (TQ - CMP_LEN + 1))


def _bias_tiles(rel_bias):
    bucket = _bucket_table()
    steps = tuple((int(n), int(bucket[n])) for n in range(1, bucket.size) if bucket[n] != bucket[n - 1])
    near, cmpb = pl.pallas_call(
        functools.partial(_bias_kernel, steps=steps),
        grid=(N_HEADS,),
        in_specs=[pl.BlockSpec(memory_space=pltpu.SMEM)],
        out_specs=[pl.BlockSpec((2, 1, TQ, TQ), lambda h: (0, h, 0, 0)),
                   pl.BlockSpec((2, 1, CMP_WIN, TQ), lambda h: (0, h, 0, 0))],
        out_shape=[jax.ShapeDtypeStruct((2, N_HEADS, TQ, TQ), F32),
                   jax.ShapeDtypeStruct((2, N_HEADS, CMP_WIN, TQ), F32)],
        compiler_params=_cparams("parallel"),
        name="bias_tiles",
    )(rel_bias.T.astype(F32))
    a = np.arange(TQ)[:, None]
    b = np.arange(TQ)[None, :]
    win2 = jnp.asarray(np.where(a < b, 0.0, NEG).astype(np.float32))
    return near, win2, cmpb


def _inproj_kernel(x_ref, w_ref, cs_ref, wg_ref, o_ref, g_ref, xb_ref):
    @pl.when(pl.program_id(1) == 0)
    def _():
        xb = x_ref[...].astype(BF16)
        xb_ref[...] = xb
        g_ref[...] = jax.nn.sigmoid(jnp.dot(xb, wg_ref[...], preferred_element_type=F32))

    acc = jnp.dot(xb_ref[...], w_ref[...].astype(BF16), preferred_element_type=F32)
    o_ref[...] = (acc * cs_ref[...]).astype(o_ref.dtype)


def _inproj(x, w_in, colscale, w_gate, tm=1024, tn=512):
    m, k = x.shape
    return pl.pallas_call(
        _inproj_kernel,
        grid=(m // tm, GATE_COL0 // tn),
        in_specs=[pl.BlockSpec((tm, k), lambda i, j: (i, 0)),
                  pl.BlockSpec((k, tn), lambda i, j: (0, j)),
                  pl.BlockSpec((1, tn), lambda i, j: (0, j)),
                  pl.BlockSpec((k, HEAD_DIM), lambda i, j: (0, 0))],
        out_specs=[pl.BlockSpec((tm, tn), lambda i, j: (i, j)),
                   pl.BlockSpec((tm, HEAD_DIM), lambda i, j: (i, 0))],
        out_shape=[jax.ShapeDtypeStruct((m, GATE_COL0), BF16),
                   jax.ShapeDtypeStruct((m, HEAD_DIM), F32)],
        scratch_shapes=[pltpu.VMEM((tm, k), BF16)],
        compiler_params=_cparams("parallel", "arbitrary"),
        name="inproj",
    )(x, w_in, colscale, w_gate)


def _gelu_tanh(x):
    return 0.5 * x * (1.0 + jnp.tanh(math.sqrt(2.0 / math.pi) * (x + 0.044715 * (x * x * x))))


def _compress_kernel(xk_ref, xv_ref, pe_ref, w1_ref, w2_ref, kc_ref, vct_ref):
    def compress(x_ref, kv):
        x = x_ref[0].astype(F32)
        lo = jnp.dot((x + pe_ref[kv, 0]).astype(BF16), w1_ref[kv, 0], preferred_element_type=F32)
        hi = jnp.dot((x + pe_ref[kv, 1]).astype(BF16), w1_ref[kv, 1], preferred_element_type=F32)
        y = lo + pltpu.roll(hi, N_CMP, 0)
        out = jnp.dot(_gelu_tanh(y).astype(BF16), w2_ref[kv], preferred_element_type=F32)
        row = lax.broadcasted_iota(jnp.int32, out.shape, 0)
        return jnp.where(row < N_CMP, out, 0.0)

    kc_ref[0] = compress(xk_ref, 0).astype(kc_ref.dtype)
    vct_ref[0] = compress(xv_ref, 1).T.astype(vct_ref.dtype)


def _compress(xr, pe, w1, w2):
    full = lambda g: (0, 0, 0, 0)
    return pl.pallas_call(
        _compress_kernel,
        grid=(NSA_KV_HEADS,),
        in_specs=[pl.BlockSpec((1, 512, 2048), lambda g: (g, 0, 0)),
                  pl.BlockSpec((1, 512, 2048), lambda g: (NSA_KV_HEADS + g, 0, 0)),
                  pl.BlockSpec((2, 2, 1, 2048), full),
                  pl.BlockSpec((2, 2, 2048, HEAD_DIM), full),
                  pl.BlockSpec((2, HEAD_DIM, HEAD_DIM), lambda g: (0, 0, 0))],
        out_specs=[pl.BlockSpec((1, 512, HEAD_DIM), lambda g: (g, 0, 0)),
                   pl.BlockSpec((1, HEAD_DIM, 512), lambda g: (g, 0, 0))],
        out_shape=[jax.ShapeDtypeStruct((NSA_KV_HEADS, 512, HEAD_DIM), BF16),
                   jax.ShapeDtypeStruct((NSA_KV_HEADS, HEAD_DIM, 512), BF16)],
        compiler_params=_cparams("parallel"),
        name="compress",
    )(xr, xr, pe, w1, w2)


def _split3(x):
    hi = x.astype(BF16)
    r1 = x - hi.astype(F32)
    mid = r1.astype(BF16)
    lo = (r1 - mid.astype(F32)).astype(BF16)
    return hi, mid, lo


def _topk_bias_t(score, k):
    row = lax.broadcasted_iota(jnp.int32, score.shape, 0).astype(F32)
    selb = jnp.full(score.shape, NEG, F32)
    s = score
    for _ in range(k):
        mx = jnp.max(s, axis=0, keepdims=True)
        idx = jnp.min(jnp.where(s == mx, row, 1e9), axis=0, keepdims=True)
        hit = row == idx
        selb = jnp.where(hit, 0.0, selb)
        s = jnp.where(hit, -3e38, s)
    return selb


def _lane_tile_max(s):
    m = s[:, :HEAD_DIM]
    for c in range(1, s.shape[1] // HEAD_DIM):
        m = jnp.maximum(m, s[:, c * HEAD_DIM:(c + 1) * HEAD_DIM])
    return m


def _sparse_flash(qs, selbs, k_ref, v_ref, kcols, pat_ref, d0s, d1s, i, sub, s_ref, mx_ref, acc_ref):
    heads = range(len(qs))
    lane = lax.broadcasted_iota(jnp.int32, (TQ, HEAD_DIM), 1)
    far = lane < (i - 1) * sub
    q_near = [jnp.concatenate([qs[h], selbs[h].astype(BF16)], axis=1) for h in heads]
    q_far = [jnp.concatenate([qs[h], jnp.where(far, selbs[h], NEG).astype(BF16)], axis=1) for h in heads]

    neg_tile = jnp.full((TQ, TQ), NEG, F32)
    for h in heads:
        for u in range(1, KG):
            s_ref[h, jnp.minimum(i + u, N_QT - 1)] = neg_tile
        mx_ref[h] = jnp.full((TQ, HEAD_DIM), NEG, F32)
        acc_ref[h] = jnp.zeros((TQ, 2 * HEAD_DIM), F32)

    def k_aug(h, off, n):
        c = kcols[h]
        return jnp.concatenate([k_ref[pl.ds(off, n), c:c + HEAD_DIM], pat_ref[pl.ds(off, n), :]], axis=1)

    def far_group(g, carry):
        off = pl.multiple_of(g * GROUP, GROUP)
        for h in heads:
            s = lax.dot_general(q_far[h], k_aug(h, off, GROUP), _NT, preferred_element_type=F32)
            for u in range(KG):
                s_ref[h, g * KG + u] = s[:, u * TQ:(u + 1) * TQ]
            mx_ref[h] = jnp.maximum(mx_ref[h], _lane_tile_max(s))
        return carry

    n_far = lax.shift_right_logical(jnp.maximum(i - 1, 0) + (KG - 1), 2)
    lax.fori_loop(0, n_far, far_group, 0)

    prev = jnp.maximum(i - 1, 0)
    off_prev = pl.multiple_of(prev * TQ, TQ)
    off_own = pl.multiple_of(i * TQ, TQ)
    for h in heads:
        k2 = jnp.concatenate([k_aug(h, off_prev, TQ), k_aug(h, off_own, TQ)], axis=0)
        bias = jnp.concatenate([jnp.where(i >= 1, d1s[h], NEG), d0s[h]], axis=1)
        s = lax.dot_general(q_near[h], k2, _NT, preferred_element_type=F32) + bias
        s_ref[h, prev] = s[:, :TQ]
        s_ref[h, i] = s[:, TQ:]
        m = jnp.max(jnp.maximum(mx_ref[h], _lane_tile_max(s)), axis=1, keepdims=True)
        mx_ref[h] = jnp.broadcast_to(m, (TQ, HEAD_DIM))
    ones_v = jnp.ones((GROUP, HEAD_DIM), BF16)

    def pv_group(g, carry):
        off = pl.multiple_of(g * GROUP, GROUP)
        for h in heads:
            m = mx_ref[h]
            m2 = jnp.concatenate([m, m], axis=1)
            p = jnp.concatenate([jnp.exp(s_ref[h, g * KG + u] - m2).astype(BF16) for u in range(KG)], axis=1)
            c = kcols[h]
            v_aug = jnp.concatenate([v_ref[pl.ds(off, GROUP), c:c + HEAD_DIM], ones_v], axis=1)
            acc_ref[h] += jnp.dot(p, v_aug, preferred_element_type=F32)
        return carry

    lax.fori_loop(0, lax.shift_right_logical(i + KG, 2), pv_group, 0)
    outs = []
    for h in heads:
        acc = acc_ref[h]
        outs.append(acc[:, :HEAD_DIM] / acc[:, HEAD_DIM:])
    return outs


_FLASH_SCRATCH = [pltpu.VMEM((HP, N_QT, TQ, TQ), F32),
                  pltpu.VMEM((HP, TQ, HEAD_DIM), F32),
                  pltpu.VMEM((HP, TQ, 2 * HEAD_DIM), F32)]


def _moba_kernel(q_ref, k_ref, v_ref, pat_ref, d0_ref, d1_ref, o_ref, kmean_ref, s_ref, mx_ref, acc_ref):
    i = pl.program_id(1)

    @pl.when(i == 0)
    def _():
        blk = lax.broadcasted_iota(jnp.int32, (HEAD_DIM, SEQ), 0)
        pos = lax.broadcasted_iota(jnp.int32, (HEAD_DIM, SEQ), 1)
        ind = jnp.where(lax.shift_right_logical(pos, 8) == blk, 1.0 / MOBA_BLOCK, 0.0).astype(BF16)
        for h in range(HP):
            kmean_ref[h] = jnp.dot(ind, k_ref[:, h * HEAD_DIM:(h + 1) * HEAD_DIM], preferred_element_type=F32)

    blk_t = lax.broadcasted_iota(jnp.int32, (HEAD_DIM, TQ), 0)
    past = blk_t < i
    qs, selbs = [], []
    for h in range(HP):
        q = q_ref[:, h * HEAD_DIM:(h + 1) * HEAD_DIM]
        gate = jnp.zeros((HEAD_DIM, TQ), F32)
        for part in _split3(kmean_ref[h]):
            gate = gate + lax.dot_general(part, q, _NT, preferred_element_type=F32)
        selb = _topk_bias_t(jnp.where(past, gate, NEG), MOBA_TOPK)
        selb = jnp.where(past, selb, jnp.where(blk_t == i, 0.0, NEG))
        qs.append(q)
        selbs.append(selb.T)
    cols = [h * HEAD_DIM for h in range(HP)]
    outs = _sparse_flash(qs, selbs, k_ref, v_ref, cols, pat_ref, [d0_ref[0, h] for h in range(HP)],
                         [d1_ref[0, h] for h in range(HP)], i, 1, s_ref, mx_ref, acc_ref)
    for h in range(HP):
        o_ref[:, h * HEAD_DIM:(h + 1) * HEAD_DIM] = outs[h].astype(o_ref.dtype)


def _moba(zb, near, pat):
    w = HP * HEAD_DIM
    return pl.pallas_call(
        _moba_kernel,
        grid=(MOBA_HEADS // HP, N_QT),
        in_specs=[pl.BlockSpec((TQ, w), lambda h, i: (i, COL_MQ // HP + h)),
                  pl.BlockSpec((SEQ, w), lambda h, i: (0, COL_MK // HP + h)),
                  pl.BlockSpec((SEQ, w), lambda h, i: (0, COL_MV // HP + h)),
                  pl.BlockSpec((SEQ, HEAD_DIM), lambda h, i: (0, 0)),
                  pl.BlockSpec((1, HP, TQ, TQ), lambda h, i: (0, h, 0, 0)),
                  pl.BlockSpec((1, HP, TQ, TQ), lambda h, i: (1, h, 0, 0))],
        out_specs=pl.BlockSpec((TQ, w), lambda h, i: (i, h)),
        out_shape=jax.ShapeDtypeStruct((SEQ, MOBA_HEADS * HEAD_DIM), BF16),
        scratch_shapes=[pltpu.VMEM((HP, HEAD_DIM, HEAD_DIM), F32)] + _FLASH_SCRATCH,
        compiler_params=_cparams("parallel", "arbitrary"),
        name="moba",
    )(zb, zb, zb, pat, near, near)


def _cmp_kernel(q_ref, kc_ref, vct_ref, ovt_ref, cb_ref, oc_ref, sel_ref, s_ref):
    i = pl.program_id(0)
    win0 = pl.multiple_of(jnp.maximum(i * CMP_STRIDE - CMP_STRIDE, 0), CMP_STRIDE)
    n = lax.broadcasted_iota(jnp.int32, (512, TQ), 0)
    reachable = n < win0 + CMP_WIN
    blk = lax.broadcasted_iota(jnp.int32, (N_SLC, TQ), 0)
    cur = i * SLC_PER_TILE + lax.shift_right_logical(lax.broadcasted_iota(jnp.int32, (N_SLC, TQ), 1), 6)
    allowed = blk <= cur
    forced = (blk == 0) | (blk == cur) | (blk == cur - 1)
    for g in range(NSA_KV_HEADS):
        p_sum = jnp.zeros((512, TQ), F32)
        for r in range(NSA_GROUP):
            hd = g * NSA_GROUP + r
            q = q_ref[:, hd * HEAD_DIM:(hd + 1) * HEAD_DIM]
            s_ref[...] = lax.dot_general(kc_ref[g], q, _NT, preferred_element_type=F32)
            s_ref[pl.ds(win0, CMP_WIN), :] += cb_ref[0, hd]
            s = jnp.where(reachable, s_ref[...], NEG)
            valid = s > 0.5 * NEG
            e = jnp.where(valid, jnp.exp(s - jnp.max(s, axis=0, keepdims=True)), 0.0)
            p = e * (1.0 / jnp.maximum(jnp.sum(e, axis=0, keepdims=True), TINY))
            o_t = jnp.dot(vct_ref[g], p.astype(BF16), preferred_element_type=F32)
            oc_ref[:, hd * HEAD_DIM:(hd + 1) * HEAD_DIM] = o_t.T
            p_sum = p_sum + p
        imp = jnp.zeros((N_SLC, TQ), F32)
        for part in _split3(p_sum):
            imp = imp + jnp.dot(ovt_ref[...], part, preferred_element_type=F32)
        score = jnp.where(allowed, jnp.where(forced, FORCE_SCORE, imp), -1.0)
        selb = jnp.where(allowed, _topk_bias_t(score, SLC_TOPK), NEG)
        sel_ref[g] = selb.T.astype(sel_ref.dtype)


def _cmp_select(zb, kc, vct, ovt, cmpb):
    w = NSA_HEADS * HEAD_DIM
    return pl.pallas_call(
        _cmp_kernel,
        grid=(N_QT,),
        in_specs=[pl.BlockSpec((TQ, w), lambda i: (i, COL_NQ // NSA_HEADS)),
                  pl.BlockSpec((NSA_KV_HEADS, 512, HEAD_DIM), lambda i: (0, 0, 0)),
                  pl.BlockSpec((NSA_KV_HEADS, HEAD_DIM, 512), lambda i: (0, 0, 0)),
                  pl.BlockSpec((N_SLC, 512), lambda i: (0, 0)),
                  pl.BlockSpec((1, NSA_HEADS, CMP_WIN, TQ), lambda i: (jnp.minimum(i, 1), 1, 0, 0))],
        out_specs=[pl.BlockSpec((TQ, w), lambda i: (i, 0)),
                   pl.BlockSpec((NSA_KV_HEADS, TQ, N_SLC), lambda i: (0, i, 0))],
        out_shape=[jax.ShapeDtypeStruct((SEQ, w), F32),
                   jax.ShapeDtypeStruct((NSA_KV_HEADS, SEQ, N_SLC), BF16)],
        scratch_shapes=[pltpu.VMEM((512, TQ), F32)],
        compiler_params=_cparams("arbitrary"),
        name="cmp_select",
    )(zb, kc, vct, ovt, cmpb)


def _slc_kernel(q_ref, sel_ref, k_ref, v_ref, pat_ref, d0_ref, d1_ref, o_ref, s_ref, mx_ref, acc_ref):
    i = pl.program_id(1)
    selb = sel_ref[0].astype(F32)
    qs = [q_ref[:, h * HEAD_DIM:(h + 1) * HEAD_DIM] for h in range(HP)]
    outs = _sparse_flash(qs, [selb] * HP, k_ref, v_ref, [0] * HP, pat_ref,
                         [d0_ref[0, h] for h in range(HP)], [d1_ref[0, h] for h in range(HP)],
                         i, SLC_PER_TILE, s_ref, mx_ref, acc_ref)
    for h in range(HP):
        o_ref[:, h * HEAD_DIM:(h + 1) * HEAD_DIM] = outs[h].astype(o_ref.dtype)


def _slc(zb, selb, near, pat):
    w = HP * HEAD_DIM
    per_group = NSA_GROUP // HP
    return pl.pallas_call(
        _slc_kernel,
        grid=(NSA_HEADS // HP, N_QT),
        in_specs=[pl.BlockSpec((TQ, w), lambda h, i: (i, COL_NQ // HP + h)),
                  pl.BlockSpec((1, TQ, N_SLC), lambda h, i: (h // per_group, i, 0)),
                  pl.BlockSpec((SEQ, HEAD_DIM), lambda h, i: (0, COL_SK + h // per_group)),
                  pl.BlockSpec((SEQ, HEAD_DIM), lambda h, i: (0, COL_SV + h // per_group)),
                  pl.BlockSpec((SEQ, HEAD_DIM), lambda h, i: (0, 0)),
                  pl.BlockSpec((1, HP, TQ, TQ), lambda h, i: (0, MOBA_HEADS // HP + h, 0, 0)),
                  pl.BlockSpec((1, HP, TQ, TQ), lambda h, i: (1, MOBA_HEADS // HP + h, 0, 0))],
        out_specs=pl.BlockSpec((TQ, w), lambda h, i: (i, h)),
        out_shape=jax.ShapeDtypeStruct((SEQ, NSA_HEADS * HEAD_DIM), F32),
        scratch_shapes=_FLASH_SCRATCH,
        compiler_params=_cparams("parallel", "arbitrary"),
        name="slc",
    )(zb, selb, zb, zb, pat, near, near)


def _win_kernel(q_ref, k_ref, v_ref, d0_ref, d1_ref, w2_ref, o_ref):
    i = pl.program_id(1)
    offs = [pl.multiple_of(jnp.maximum(i - b, 0) * TQ, TQ) for b in range(3)]
    k_all = jnp.concatenate([k_ref[pl.ds(off, TQ), :] for off in offs], axis=0)
    v_all = jnp.concatenate([v_ref[pl.ds(off, TQ), :] for off in offs], axis=0)
    v_aug = jnp.concatenate([v_all, jnp.ones((3 * TQ, HEAD_DIM), BF16)], axis=1)
    mask2 = jnp.where(i >= 2, w2_ref[...], NEG)
    for r in range(NSA_GROUP):
        q = q_ref[:, r * HEAD_DIM:(r + 1) * HEAD_DIM]
        bias = jnp.concatenate([d0_ref[0, r], jnp.where(i >= 1, d1_ref[0, r], NEG), mask2], axis=1)
        s = lax.dot_general(q, k_all, _NT, preferred_element_type=F32) + bias
        m = jnp.max(s, axis=1, keepdims=True)
        pv = jnp.dot(jnp.exp(s - m).astype(BF16), v_aug, preferred_element_type=F32)
        o_ref[:, r * HEAD_DIM:(r + 1) * HEAD_DIM] = pv[:, :HEAD_DIM] / pv[:, HEAD_DIM:]


def _win(zb, near, win2):
    return pl.pallas_call(
        _win_kernel,
        grid=(NSA_KV_HEADS, N_QT),
        in_specs=[pl.BlockSpec((TQ, NSA_GROUP * HEAD_DIM), lambda g, i: (i, COL_NQ // NSA_GROUP + g)),
                  pl.BlockSpec((SEQ, HEAD_DIM), lambda g, i: (0, COL_WK + g)),
                  pl.BlockSpec((SEQ, HEAD_DIM), lambda g, i: (0, COL_WV + g)),
                  pl.BlockSpec((1, NSA_GROUP, TQ, TQ), lambda g, i: (0, 2 + g, 0, 0)),
                  pl.BlockSpec((1, NSA_GROUP, TQ, TQ), lambda g, i: (1, 2 + g, 0, 0)),
                  pl.BlockSpec((TQ, TQ), lambda g, i: (0, 0))],
        out_specs=pl.BlockSpec((TQ, NSA_GROUP * HEAD_DIM), lambda g, i: (i, g)),
        out_shape=jax.ShapeDtypeStruct((SEQ, NSA_HEADS * HEAD_DIM), F32),
        compiler_params=_cparams("parallel", "arbitrary"),
        name="win",
    )(zb, zb, zb, near, near, win2)


def _layer_norm(y, g, b):
    mu = jnp.mean(y, axis=-1, keepdims=True)
    yc = y - mu
    var = jnp.mean(yc * yc, axis=-1, keepdims=True)
    return yc * lax.rsqrt(var + LN_EPS) * g + b


def _outproj_kernel(x_ref, om_ref, oc_ref, os_ref, ow_ref, g_ref, e_ref, w_ref, lg_ref, lb_ref,
                    h_ref, hb_ref):
    gate = g_ref[...]
    parts = _split3(gate)
    o_n = jnp.zeros(oc_ref.shape, F32)
    for c, branch in enumerate((oc_ref, os_ref, ow_ref)):
        gexp = jnp.zeros(oc_ref.shape, F32)
        for part in parts:
            gexp = gexp + jnp.dot(part, e_ref[c], preferred_element_type=F32)
        o_n = o_n + gexp * branch[...]
    o = jnp.concatenate([om_ref[...], o_n.astype(BF16)], axis=1)
    a = jnp.dot(o, w_ref[...], preferred_element_type=F32)
    h = _layer_norm(DN_ALPHA * x_ref[...] + a, lg_ref[...], lb_ref[...])
    h_ref[...] = h
    hb_ref[...] = h.astype(BF16)


def _outproj(x, o_m, o_c, o_s, o_w, gates, expand, w_out, ln_g, ln_b, tm=256):
    half = NSA_HEADS * HEAD_DIM
    row = lambda i: (i, 0)
    fixed2 = lambda i: (0, 0)
    return pl.pallas_call(
        _outproj_kernel,
        grid=(SEQ // tm,),
        in_specs=[pl.BlockSpec((tm, D_MODEL), row),
                  pl.BlockSpec((tm, half), row), pl.BlockSpec((tm, half), row),
                  pl.BlockSpec((tm, half), row), pl.BlockSpec((tm, half), row),
                  pl.BlockSpec((tm, HEAD_DIM), row),
                  pl.BlockSpec((3, HEAD_DIM, half), lambda i: (0, 0, 0)),
                  pl.BlockSpec((D_MODEL, D_MODEL), fixed2),
                  pl.BlockSpec((1, D_MODEL), fixed2), pl.BlockSpec((1, D_MODEL), fixed2)],
        out_specs=[pl.BlockSpec((tm, D_MODEL), row), pl.BlockSpec((tm, D_MODEL), row)],
        out_shape=[jax.ShapeDtypeStruct((SEQ, D_MODEL), F32),
                   jax.ShapeDtypeStruct((SEQ, D_MODEL), BF16)],
        compiler_params=_cparams("parallel"),
        name="outproj_ln",
    )(x, o_m, o_c, o_s, o_w, gates, expand, w_out, ln_g, ln_b)


def _ffn_kernel(hb_ref, w1_ref, w2_ref, f_ref):
    @pl.when(pl.program_id(1) == 0)
    def _():
        f_ref[...] = jnp.zeros(f_ref.shape, F32)

    u = jnp.maximum(jnp.dot(hb_ref[...], w1_ref[...].astype(BF16), preferred_element_type=F32), 0.0)
    f_ref[...] += jnp.dot((u * u).astype(BF16), w2_ref[...].astype(BF16), preferred_element_type=F32)


def _ffn(hb, w1, w2, tm=1024, tf=512):
    return pl.pallas_call(
        _ffn_kernel,
        grid=(SEQ // tm, D_FF // tf),
        in_specs=[pl.BlockSpec((tm, D_MODEL), lambda i, c: (i, 0)),
                  pl.BlockSpec((D_MODEL, tf), lambda i, c: (0, c)),
                  pl.BlockSpec((tf, D_MODEL), lambda i, c: (c, 0))],
        out_specs=pl.BlockSpec((tm, D_MODEL), lambda i, c: (i, 0)),
        out_shape=jax.ShapeDtypeStruct((SEQ, D_MODEL), F32),
        compiler_params=pltpu.CompilerParams(dimension_semantics=("parallel", "arbitrary"),
                                             vmem_limit_bytes=FFN_VMEM_LIMIT),
        name="ffn",
    )(hb, w1, w2)


def _final_kernel(h_ref, hb_ref, f_ref, p_ref, wg_ref, wp_ref, lg_ref, lb_ref, o_ref):
    gate = jax.nn.sigmoid(jnp.dot(hb_ref[...], wg_ref[...], preferred_element_type=F32))
    emb = jnp.dot(p_ref[...].astype(BF16), wp_ref[...], preferred_element_type=F32)
    y = DN_ALPHA * h_ref[...] + f_ref[...] + gate * emb
    o_ref[...] = _layer_norm(y, lg_ref[...], lb_ref[...])


def _final(h, hb, f, p, wg, wp, ln_g, ln_b, tm=256):
    row = lambda i: (i, 0)
    fixed2 = lambda i: (0, 0)
    return pl.pallas_call(
        _final_kernel,
        grid=(SEQ // tm,),
        in_specs=[pl.BlockSpec((tm, D_MODEL), row), pl.BlockSpec((tm, D_MODEL), row),
                  pl.BlockSpec((tm, D_MODEL), row), pl.BlockSpec((tm, D_PLE), row),
                  pl.BlockSpec((D_MODEL, D_MODEL), fixed2), pl.BlockSpec((D_PLE, D_MODEL), fixed2),
                  pl.BlockSpec((1, D_MODEL), fixed2), pl.BlockSpec((1, D_MODEL), fixed2)],
        out_specs=pl.BlockSpec((tm, D_MODEL), row),
        out_shape=jax.ShapeDtypeStruct((SEQ, D_MODEL), F32),
        compiler_params=_cparams("parallel"),
        name="final_ln",
    )(h, hb, f, p, wg, wp, ln_g, ln_b)


def _overlap_matrix_t():
    j = np.arange(N_SLC)[:, None]
    n = np.arange(512)[None, :]
    ov = ((n * CMP_STRIDE < j * SLC_LEN + SLC_LEN) & (n * CMP_STRIDE + CMP_LEN > j * SLC_LEN) & (n < N_CMP))
    return ov.astype(np.float32)


def _gate_expand():
    e = np.zeros((3, HEAD_DIM, NSA_HEADS * HEAD_DIM), np.float32)
    for c in range(3):
        for h in range(NSA_HEADS):
            e[c, 3 * h + c, h * HEAD_DIM:(h + 1) * HEAD_DIM] = 1.0
    return e


def _layer(h, p, w_in, cmp_pe_k, cmp_w1_k, cmp_w2_k, cmp_pe_v, cmp_w1_v, cmp_w2_v, tiles, w_out,
           ln1_g, ln1_b, w_ff1, w_ff2, w_ple, w_ple_gate, ln2_g, ln2_b):
    near, win2, cmpb = tiles
    scale = HEAD_DIM ** -0.5
    colscale = np.ones((1, GATE_COL0), np.float32)
    colscale[0, COL_MQ * HEAD_DIM:(COL_MQ + MOBA_HEADS) * HEAD_DIM] = scale
    colscale[0, COL_NQ * HEAD_DIM:(COL_NQ + NSA_HEADS) * HEAD_DIM] = scale
    w_gl = jnp.pad(w_in[:, GATE_COL0:], ((0, 0), (0, HEAD_DIM - NSA_HEADS * 3))).astype(BF16)
    zb, gates = _inproj(h, w_in, jnp.asarray(colscale), w_gl)

    ckv = zb[:, COL_CK * HEAD_DIM:(COL_CK + 4) * HEAD_DIM]
    xr = ckv.reshape(SEQ // CMP_STRIDE, CMP_STRIDE, 4, HEAD_DIM).transpose(2, 0, 1, 3)
    xr = xr.reshape(4, SEQ // CMP_STRIDE, CMP_STRIDE * HEAD_DIM)
    pe = jnp.stack([cmp_pe_k, cmp_pe_v]).reshape(2, 2, 1, CMP_STRIDE * HEAD_DIM).astype(F32)
    w1 = jnp.stack([cmp_w1_k, cmp_w1_v]).reshape(2, 2, CMP_STRIDE * HEAD_DIM, HEAD_DIM).astype(BF16)
    w2 = jnp.stack([cmp_w2_k, cmp_w2_v]).astype(BF16)
    kc, vct = _compress(xr, pe, w1, w2)

    key = lax.broadcasted_iota(jnp.int32, (SEQ, HEAD_DIM), 0)
    lane = lax.broadcasted_iota(jnp.int32, (SEQ, HEAD_DIM), 1)
    pat_moba = (lane == key // MOBA_BLOCK).astype(BF16)
    pat_slc = (lane == key // SLC_LEN).astype(BF16)
    o_m = _moba(zb, near, pat_moba)
    o_c, selb = _cmp_select(zb, kc, vct, jnp.asarray(_overlap_matrix_t(), BF16), cmpb)
    o_s = _slc(zb, selb, near, pat_slc)
    o_w = _win(zb, near, win2)

    h1, h1b = _outproj(h, o_m, o_c, o_s, o_w, gates, jnp.asarray(_gate_expand(), BF16),
                       w_out.astype(BF16), ln1_g.reshape(1, -1), ln1_b.reshape(1, -1))
    f = _ffn(h1b, w_ff1, w_ff2)
    return _final(h1, h1b, f, p, w_ple_gate.astype(BF16), w_ple.astype(BF16),
                  ln2_g.reshape(1, -1), ln2_b.reshape(1, -1))


def kernel(x, p, w_in, cmp_pe_k, cmp_w1_k, cmp_w2_k, cmp_pe_v, cmp_w1_v, cmp_w2_v, rel_bias, w_out,
           ln1_g, ln1_b, w_ff1, w_ff2, w_ple, w_ple_gate, ln2_g, ln2_b):
    tiles = _bias_tiles(rel_bias)
    h = x[0]
    for i in range(w_in.shape[0]):
        h = _layer(h, p[i, 0], w_in[i], cmp_pe_k[i], cmp_w1_k[i], cmp_w2_k[i], cmp_pe_v[i], cmp_w1_v[i],
                   cmp_w2_v[i], tiles, w_out[i], ln1_g[i], ln1_b[i], w_ff1[i], w_ff2[i], w_ple[i],
                   w_ple_gate[i], ln2_g[i], ln2_b[i])
    return h[None]
```

```python
import functools
import math

import numpy as np
import jax
import jax.numpy as jnp
from jax import lax
from jax.experimental import pallas as pl
from jax.experimental.pallas import tpu as pltpu

D_MODEL = 2048
SEQ = 8192
HEAD_DIM = 128
N_HEADS = 16
MOBA_HEADS = 8
NSA_HEADS = 8
NSA_KV_HEADS = 2
NSA_GROUP = 4
MOBA_BLOCK = 256
MOBA_TOPK = 3
CMP_LEN = 32
CMP_STRIDE = 16
SLC_LEN = 64
SLC_TOPK = 16
WINDOW = 512
N_BUCKETS = 32
MAX_DISTANCE = 128
D_FF = 4 * D_MODEL
D_PLE = 256
LN_EPS = 1e-5
DN_ALPHA = 2.0 ** 0.25
NEG = -1e30
TINY = 1e-30
FORCE_SCORE = 1e4

N_CMP = (SEQ - CMP_LEN) // CMP_STRIDE + 1
N_SLC = SEQ // SLC_LEN
TQ = 256
N_QT = SEQ // TQ
CMP_WIN = 32
SLC_PER_TILE = TQ // SLC_LEN
HP_MOBA = 2
HP_SLC = 4
KG = 4
GROUP = KG * TQ
COL_MQ, COL_MK, COL_MV, COL_NQ = 0, 8, 16, 24
COL_CK, COL_CV, COL_SK, COL_SV, COL_WK, COL_WV = 32, 34, 36, 38, 40, 42
N_COLBLK = 44
GATE_COL0 = N_COLBLK * HEAD_DIM

VMEM_LIMIT = 48 * 1024 * 1024
BIG_VMEM_LIMIT = 56 * 1024 * 1024

BF16 = jnp.bfloat16
F32 = jnp.float32
_NT = (((1,), (1,)), ((), ()))


def _cparams(*sem, vmem=VMEM_LIMIT):
    return pltpu.CompilerParams(dimension_semantics=sem, vmem_limit_bytes=vmem)


def _bucket_table():
    n = np.arange(1024, dtype=np.int32)
    max_exact = N_BUCKETS // 2
    ratio = np.maximum(n, 1).astype(np.float32) / np.float32(max_exact)
    large = max_exact + (np.log(ratio).astype(np.float32) / np.float32(math.log(MAX_DISTANCE / max_exact))
                         * np.float32(N_BUCKETS - max_exact)).astype(np.int32)
    large = np.minimum(large, N_BUCKETS - 1)
    return np.where(n < max_exact, n, large).astype(np.int32)


def _bias_kernel(tab_ref, near_ref, cmpb_ref, *, steps):
    h = pl.program_id(0)
    far = tab_ref[h, N_BUCKETS - 1]

    def lookup(d):
        val = jnp.full(d.shape, tab_ref[h, 0], F32)
        for start, bucket in steps:
            val = jnp.where(d >= start, tab_ref[h, bucket], val)
        return jnp.where(d >= 0, val - far, NEG)

    a = lax.broadcasted_iota(jnp.int32, (TQ, TQ), 0)
    b = lax.broadcasted_iota(jnp.int32, (TQ, TQ), 1)
    near_ref[0, 0] = lookup(a - b)
    near_ref[1, 0] = lookup(a - b + TQ)
    w = lax.broadcasted_iota(jnp.int32, (CMP_WIN, TQ), 0)
    aw = lax.broadcasted_iota(jnp.int32, (CMP_WIN, TQ), 1)
    cmpb_ref[0, 0] = lookup(aw - CMP_STRIDE * w - (CMP_LEN - 1))
    cmpb_ref[1, 0] = lookup(aw - CMP_STRIDE * w + (TQ - CMP_LEN + 1))


def _bias_tiles(rel_bias):
    bucket = _bucket_table()
    steps = tuple((int(n), int(bucket[n])) for n in range(1, bucket.size) if bucket[n] != bucket[n - 1])
    near, cmpb = pl.pallas_call(
        functools.partial(_bias_kernel, steps=steps),
        grid=(N_HEADS,),
        in_specs=[pl.BlockSpec(memory_space=pltpu.SMEM)],
        out_specs=[pl.BlockSpec((2, 1, TQ, TQ), lambda h: (0, h, 0, 0)),
                   pl.BlockSpec((2, 1, CMP_WIN, TQ), lambda h: (0, h, 0, 0))],
        out_shape=[jax.ShapeDtypeStruct((2, N_HEADS, TQ, TQ), F32),
                   jax.ShapeDtypeStruct((2, N_HEADS, CMP_WIN, TQ), F32)],
        compiler_params=_cparams("parallel"),
        name="bias_tiles",
    )(rel_bias.T.astype(F32))
    a = np.arange(TQ)[:, None]
    b = np.arange(TQ)[None, :]
    win2 = jnp.asarray(np.where(a < b, 0.0, NEG).astype(np.float32))
    return near, win2, cmpb


def _inproj_kernel(x_ref, w_ref, cs_ref, wg_ref, o_ref, g_ref, xb_ref):
    @pl.when(pl.program_id(1) == 0)
    def _():
        xb = x_ref[...].astype(BF16)
        xb_ref[...] = xb
        g_ref[...] = jax.nn.sigmoid(jnp.dot(xb, wg_ref[...], preferred_element_type=F32))

    acc = jnp.dot(xb_ref[...], w_ref[...].astype(BF16), preferred_element_type=F32)
    o_ref[...] = (acc * cs_ref[...]).astype(o_ref.dtype)


def _inproj(x, w_in, colscale, w_gate, tm=1024, tn=512):
    m, k = x.shape
    return pl.pallas_call(
        _inproj_kernel,
        grid=(m // tm, GATE_COL0 // tn),
        in_specs=[pl.BlockSpec((tm, k), lambda i, j: (i, 0)),
                  pl.BlockSpec((k, tn), lambda i, j: (0, j)),
                  pl.BlockSpec((1, tn), lambda i, j: (0, j)),
                  pl.BlockSpec((k, HEAD_DIM), lambda i, j: (0, 0))],
        out_specs=[pl.BlockSpec((tm, tn), lambda i, j: (i, j)),
                   pl.BlockSpec((tm, HEAD_DIM), lambda i, j: (i, 0))],
        out_shape=[jax.ShapeDtypeStruct((m, GATE_COL0), BF16),
                   jax.ShapeDtypeStruct((m, HEAD_DIM), F32)],
        scratch_shapes=[pltpu.VMEM((tm, k), BF16)],
        compiler_params=_cparams("parallel", "arbitrary"),
        name="inproj",
    )(x, w_in, colscale, w_gate)


def _gelu_tanh(x):
    return 0.5 * x * (1.0 + jnp.tanh(math.sqrt(2.0 / math.pi) * (x + 0.044715 * (x * x * x))))


def _compress_kernel(xk_ref, xv_ref, pe_ref, w1_ref, w2_ref, kc_ref, vct_ref):
    def compress(x_ref, kv):
        x = x_ref[0].astype(F32)
        lo = jnp.dot((x + pe_ref[kv, 0]).astype(BF16), w1_ref[kv, 0], preferred_element_type=F32)
        hi = jnp.dot((x + pe_ref[kv, 1]).astype(BF16), w1_ref[kv, 1], preferred_element_type=F32)
        y = lo + pltpu.roll(hi, N_CMP, 0)
        out = jnp.dot(_gelu_tanh(y).astype(BF16), w2_ref[kv], preferred_element_type=F32)
        row = lax.broadcasted_iota(jnp.int32, out.shape, 0)
        return jnp.where(row < N_CMP, out, 0.0)

    kc_ref[0] = compress(xk_ref, 0).astype(kc_ref.dtype)
    vct_ref[0] = compress(xv_ref, 1).T.astype(vct_ref.dtype)


def _compress(xr, pe, w1, w2):
    full = lambda g: (0, 0, 0, 0)
    return pl.pallas_call(
        _compress_kernel,
        grid=(NSA_KV_HEADS,),
        in_specs=[pl.BlockSpec((1, 512, 2048), lambda g: (g, 0, 0)),
                  pl.BlockSpec((1, 512, 2048), lambda g: (NSA_KV_HEADS + g, 0, 0)),
                  pl.BlockSpec((2, 2, 1, 2048), full),
                  pl.BlockSpec((2, 2, 2048, HEAD_DIM), full),
                  pl.BlockSpec((2, HEAD_DIM, HEAD_DIM), lambda g: (0, 0, 0))],
        out_specs=[pl.BlockSpec((1, 512, HEAD_DIM), lambda g: (g, 0, 0)),
                   pl.BlockSpec((1, HEAD_DIM, 512), lambda g: (g, 0, 0))],
        out_shape=[jax.ShapeDtypeStruct((NSA_KV_HEADS, 512, HEAD_DIM), BF16),
                   jax.ShapeDtypeStruct((NSA_KV_HEADS, HEAD_DIM, 512), BF16)],
        compiler_params=_cparams("parallel"),
        name="compress",
    )(xr, xr, pe, w1, w2)


def _split3(x):
    hi = x.astype(BF16)
    r1 = x - hi.astype(F32)
    mid = r1.astype(BF16)
    lo = (r1 - mid.astype(F32)).astype(BF16)
    return hi, mid, lo


def _topk_bias_t(score, k):
    row = lax.broadcasted_iota(jnp.int32, score.shape, 0).astype(F32)
    selb = jnp.full(score.shape, NEG, F32)
    s = score
    for _ in range(k):
        mx = jnp.max(s, axis=0, keepdims=True)
        idx = jnp.min(jnp.where(s == mx, row, 1e9), axis=0, keepdims=True)
        hit = row == idx
        selb = jnp.where(hit, 0.0, selb)
        s = jnp.where(hit, -3e38, s)
    return selb


def _lane_tile_max(s):
    m = s[:, :HEAD_DIM]
    for c in range(1, s.shape[1] // HEAD_DIM):
        m = jnp.maximum(m, s[:, c * HEAD_DIM:(c + 1) * HEAD_DIM])
    return m


def _sparse_flash(qs, selbs, k_ref, v_ref, kcols, pat_ref, d0s, d1s, i, sub, s_ref, mx_ref, acc_ref):
    heads = range(len(qs))
    lane = lax.broadcasted_iota(jnp.int32, (TQ, HEAD_DIM), 1)
    far = lane < (i - 1) * sub
    q_near = [jnp.concatenate([qs[h], selbs[h].astype(BF16)], axis=1) for h in heads]
    q_far = [jnp.concatenate([qs[h], jnp.where(far, selbs[h], NEG).astype(BF16)], axis=1) for h in heads]

    def k_aug(h, off, n):
        c = kcols[h]
        return jnp.concatenate([k_ref[pl.ds(off, n), c:c + HEAD_DIM], pat_ref[pl.ds(off, n), :]], axis=1)

    neg_tile = jnp.full((TQ, TQ), NEG, F32)
    for h in heads:
        for u in range(1, KG):
            s_ref[h, jnp.minimum(i + u, N_QT - 1)] = neg_tile
        mx_ref[h] = jnp.full((TQ, HEAD_DIM), NEG, F32)
        acc_ref[h] = jnp.zeros((TQ, 2 * HEAD_DIM), F32)

    def far_group(g, carry):
        off = pl.multiple_of(g * GROUP, GROUP)
        for h in heads:
            s = lax.dot_general(q_far[h], k_aug(h, off, GROUP), _NT, preferred_element_type=F32)
            for u in range(KG):
                s_ref[h, g * KG + u] = s[:, u * TQ:(u + 1) * TQ]
            mx_ref[h] = jnp.maximum(mx_ref[h], _lane_tile_max(s))
        return carry

    n_far = lax.shift_right_logical(jnp.maximum(i - 1, 0) + (KG - 1), 2)
    lax.fori_loop(0, n_far, far_group, 0)

    prev = jnp.maximum(i - 1, 0)
    off_prev = pl.multiple_of(prev * TQ, TQ)
    off_own = pl.multiple_of(i * TQ, TQ)
    for h in heads:
        k2 = jnp.concatenate([k_aug(h, off_prev, TQ), k_aug(h, off_own, TQ)], axis=0)
        bias = jnp.concatenate([jnp.where(i >= 1, d1s[h], NEG), d0s[h]], axis=1)
        s = lax.dot_general(q_near[h], k2, _NT, preferred_element_type=F32) + bias
        s_ref[h, prev] = s[:, :TQ]
        s_ref[h, i] = s[:, TQ:]
        m = jnp.max(jnp.maximum(mx_ref[h], _lane_tile_max(s)), axis=1, keepdims=True)
        mx_ref[h] = jnp.broadcast_to(m, (TQ, HEAD_DIM))
    ones_v = jnp.ones((GROUP, HEAD_DIM), BF16)

    def pv_group(g, carry):
        off = pl.multiple_of(g * GROUP, GROUP)
        for h in heads:
            m = mx_ref[h]
            m2 = jnp.concatenate([m, m], axis=1)
            p = jnp.concatenate([jnp.exp(s_ref[h, g * KG + u] - m2).astype(BF16) for u in range(KG)], axis=1)
            c = kcols[h]
            v_aug = jnp.concatenate([v_ref[pl.ds(off, GROUP), c:c + HEAD_DIM], ones_v], axis=1)
            acc_ref[h] += jnp.dot(p, v_aug, preferred_element_type=F32)
        return carry

    lax.fori_loop(0, lax.shift_right_logical(i + KG, 2), pv_group, 0)
    outs = []
    for h in heads:
        acc = acc_ref[h]
        outs.append(acc[:, :HEAD_DIM] / acc[:, HEAD_DIM:])
    return outs


def _flash_scratch(hp):
    return [pltpu.VMEM((hp, N_QT, TQ, TQ), F32),
            pltpu.VMEM((hp, TQ, HEAD_DIM), F32),
            pltpu.VMEM((hp, TQ, 2 * HEAD_DIM), F32)]


def _moba_kernel(q_ref, k_ref, v_ref, pat_ref, d0_ref, d1_ref, o_ref, kmean_ref, *flash_scratch):
    i = pl.program_id(1)
    hp = HP_MOBA

    @pl.when(i == 0)
    def _():
        blk = lax.broadcasted_iota(jnp.int32, (HEAD_DIM, SEQ), 0)
        pos = lax.broadcasted_iota(jnp.int32, (HEAD_DIM, SEQ), 1)
        ind = jnp.where(lax.shift_right_logical(pos, 8) == blk, 1.0 / MOBA_BLOCK, 0.0).astype(BF16)
        for h in range(hp):
            kmean_ref[h] = jnp.dot(ind, k_ref[:, h * HEAD_DIM:(h + 1) * HEAD_DIM], preferred_element_type=F32)

    blk_t = lax.broadcasted_iota(jnp.int32, (HEAD_DIM, TQ), 0)
    past = blk_t < i
    qs, selbs = [], []
    for h in range(hp):
        q = q_ref[:, h * HEAD_DIM:(h + 1) * HEAD_DIM]
        gate = jnp.zeros((HEAD_DIM, TQ), F32)
        for part in _split3(kmean_ref[h]):
            gate = gate + lax.dot_general(part, q, _NT, preferred_element_type=F32)
        selb = _topk_bias_t(jnp.where(past, gate, NEG), MOBA_TOPK)
        selb = jnp.where(past, selb, jnp.where(blk_t == i, 0.0, NEG))
        qs.append(q)
        selbs.append(selb.T)
    cols = [h * HEAD_DIM for h in range(hp)]
    outs = _sparse_flash(qs, selbs, k_ref, v_ref, cols, pat_ref, [d0_ref[0, h] for h in range(hp)],
                         [d1_ref[0, h] for h in range(hp)], i, 1, *flash_scratch)
    for h in range(hp):
        o_ref[:, h * HEAD_DIM:(h + 1) * HEAD_DIM] = outs[h].astype(o_ref.dtype)


def _moba(zb, near, pat):
    hp = HP_MOBA
    w = hp * HEAD_DIM
    return pl.pallas_call(
        _moba_kernel,
        grid=(MOBA_HEADS // hp, N_QT),
        in_specs=[pl.BlockSpec((TQ, w), lambda h, i: (i, COL_MQ // hp + h)),
                  pl.BlockSpec((SEQ, w), lambda h, i: (0, COL_MK // hp + h)),
                  pl.BlockSpec((SEQ, w), lambda h, i: (0, COL_MV // hp + h)),
                  pl.BlockSpec((SEQ, HEAD_DIM), lambda h, i: (0, 0)),
                  pl.BlockSpec((1, hp, TQ, TQ), lambda h, i: (0, h, 0, 0)),
                  pl.BlockSpec((1, hp, TQ, TQ), lambda h, i: (1, h, 0, 0))],
        out_specs=pl.BlockSpec((TQ, w), lambda h, i: (i, h)),
        out_shape=jax.ShapeDtypeStruct((SEQ, MOBA_HEADS * HEAD_DIM), BF16),
        scratch_shapes=[pltpu.VMEM((hp, HEAD_DIM, HEAD_DIM), F32)] + _flash_scratch(hp),
        compiler_params=_cparams("parallel", "arbitrary"),
        name="moba",
    )(zb, zb, zb, pat, near, near)


def _cmp_kernel(q_ref, kc_ref, vct_ref, ovt_ref, cb_ref, oc_ref, sel_ref, s_ref):
    i = pl.program_id(0)
    win0 = pl.multiple_of(jnp.maximum(i * CMP_STRIDE - CMP_STRIDE, 0), CMP_STRIDE)
    n = lax.broadcasted_iota(jnp.int32, (512, TQ), 0)
    reachable = n < win0 + CMP_WIN
    blk = lax.broadcasted_iota(jnp.int32, (N_SLC, TQ), 0)
    cur = i * SLC_PER_TILE + lax.shift_right_logical(lax.broadcasted_iota(jnp.int32, (N_SLC, TQ), 1), 6)
    allowed = blk <= cur
    forced = (blk == 0) | (blk == cur) | (blk == cur - 1)
    for g in range(NSA_KV_HEADS):
        p_sum = jnp.zeros((512, TQ), F32)
        for r in range(NSA_GROUP):
            hd = g * NSA_GROUP + r
            q = q_ref[:, hd * HEAD_DIM:(hd + 1) * HEAD_DIM]
            s_ref[...] = lax.dot_general(kc_ref[g], q, _NT, preferred_element_type=F32)
            s_ref[pl.ds(win0, CMP_WIN), :] += cb_ref[0, hd]
            s = jnp.where(reachable, s_ref[...], NEG)
            valid = s > 0.5 * NEG
            e = jnp.where(valid, jnp.exp(s - jnp.max(s, axis=0, keepdims=True)), 0.0)
            p = e * (1.0 / jnp.maximum(jnp.sum(e, axis=0, keepdims=True), TINY))
            o_t = jnp.dot(vct_ref[g], p.astype(BF16), preferred_element_type=F32)
            oc_ref[:, hd * HEAD_DIM:(hd + 1) * HEAD_DIM] = o_t.T
            p_sum = p_sum + p
        imp = jnp.zeros((N_SLC, TQ), F32)
        for part in _split3(p_sum):
            imp = imp + jnp.dot(ovt_ref[...], part, preferred_element_type=F32)
        score = jnp.where(allowed, jnp.where(forced, FORCE_SCORE, imp), -1.0)
        selb = jnp.where(allowed, _topk_bias_t(score, SLC_TOPK), NEG)
        sel_ref[g] = selb.T.astype(sel_ref.dtype)


def _cmp_select(zb, kc, vct, ovt, cmpb):
    w = NSA_HEADS * HEAD_DIM
    return pl.pallas_call(
        _cmp_kernel,
        grid=(N_QT,),
        in_specs=[pl.BlockSpec((TQ, w), lambda i: (i, COL_NQ // NSA_HEADS)),
                  pl.BlockSpec((NSA_KV_HEADS, 512, HEAD_DIM), lambda i: (0, 0, 0)),
                  pl.BlockSpec((NSA_KV_HEADS, HEAD_DIM, 512), lambda i: (0, 0, 0)),
                  pl.BlockSpec((N_SLC, 512), lambda i: (0, 0)),
                  pl.BlockSpec((1, NSA_HEADS, CMP_WIN, TQ), lambda i: (jnp.minimum(i, 1), 1, 0, 0))],
        out_specs=[pl.BlockSpec((TQ, w), lambda i: (i, 0)),
                   pl.BlockSpec((NSA_KV_HEADS, TQ, N_SLC), lambda i: (0, i, 0))],
        out_shape=[jax.ShapeDtypeStruct((SEQ, w), F32),
                   jax.ShapeDtypeStruct((NSA_KV_HEADS, SEQ, N_SLC), BF16)],
        scratch_shapes=[pltpu.VMEM((512, TQ), F32)],
        compiler_params=_cparams("arbitrary"),
        name="cmp_select",
    )(zb, kc, vct, ovt, cmpb)


def _slc_kernel(q_ref, sel_ref, k_ref, v_ref, pat_ref, d0_ref, d1_ref, o_ref, *flash_scratch):
    i = pl.program_id(1)
    hp = HP_SLC
    selb = sel_ref[0].astype(F32)
    qs = [q_ref[:, h * HEAD_DIM:(h + 1) * HEAD_DIM] for h in range(hp)]
    outs = _sparse_flash(qs, [selb] * hp, k_ref, v_ref, [0] * hp, pat_ref,
                         [d0_ref[0, h] for h in range(hp)], [d1_ref[0, h] for h in range(hp)],
                         i, SLC_PER_TILE, *flash_scratch)
    for h in range(hp):
        o_ref[:, h * HEAD_DIM:(h + 1) * HEAD_DIM] = outs[h].astype(o_ref.dtype)


def _slc(zb, selb, near, pat):
    hp = HP_SLC
    w = hp * HEAD_DIM
    per_group = NSA_GROUP // hp
    return pl.pallas_call(
        _slc_kernel,
        grid=(NSA_HEADS // hp, N_QT),
        in_specs=[pl.BlockSpec((TQ, w), lambda h, i: (i, COL_NQ // hp + h)),
                  pl.BlockSpec((1, TQ, N_SLC), lambda h, i: (h // per_group, i, 0)),
                  pl.BlockSpec((SEQ, HEAD_DIM), lambda h, i: (0, COL_SK + h // per_group)),
                  pl.BlockSpec((SEQ, HEAD_DIM), lambda h, i: (0, COL_SV + h // per_group)),
                  pl.BlockSpec((SEQ, HEAD_DIM), lambda h, i: (0, 0)),
                  pl.BlockSpec((1, hp, TQ, TQ), lambda h, i: (0, MOBA_HEADS // hp + h, 0, 0)),
                  pl.BlockSpec((1, hp, TQ, TQ), lambda h, i: (1, MOBA_HEADS // hp + h, 0, 0))],
        out_specs=pl.BlockSpec((TQ, w), lambda h, i: (i, h)),
        out_shape=jax.ShapeDtypeStruct((SEQ, NSA_HEADS * HEAD_DIM), F32),
        scratch_shapes=_flash_scratch(hp),
        compiler_params=_cparams("parallel", "arbitrary", vmem=BIG_VMEM_LIMIT),
        name="slc",
    )(zb, selb, zb, zb, pat, near, near)


def _win_kernel(q_ref, k_ref, v_ref, d0_ref, d1_ref, w2_ref, o_ref):
    i = pl.program_id(1)
    offs = [pl.multiple_of(jnp.maximum(i - b, 0) * TQ, TQ) for b in range(3)]
    k_all = jnp.concatenate([k_ref[pl.ds(off, TQ), :] for off in offs], axis=0)
    v_all = jnp.concatenate([v_ref[pl.ds(off, TQ), :] for off in offs], axis=0)
    v_aug = jnp.concatenate([v_all, jnp.ones((3 * TQ, HEAD_DIM), BF16)], axis=1)
    mask2 = jnp.where(i >= 2, w2_ref[...], NEG)
    for r in range(NSA_GROUP):
        q = q_ref[:, r * HEAD_DIM:(r + 1) * HEAD_DIM]
        bias = jnp.concatenate([d0_ref[0, r], jnp.where(i >= 1, d1_ref[0, r], NEG), mask2], axis=1)
        s = lax.dot_general(q, k_all, _NT, preferred_element_type=F32) + bias
        m = jnp.max(s, axis=1, keepdims=True)
        pv = jnp.dot(jnp.exp(s - m).astype(BF16), v_aug, preferred_element_type=F32)
        o_ref[:, r * HEAD_DIM:(r + 1) * HEAD_DIM] = pv[:, :HEAD_DIM] / pv[:, HEAD_DIM:]


def _win(zb, near, win2):
    return pl.pallas_call(
        _win_kernel,
        grid=(NSA_KV_HEADS, N_QT),
        in_specs=[pl.BlockSpec((TQ, NSA_GROUP * HEAD_DIM), lambda g, i: (i, COL_NQ // NSA_GROUP + g)),
                  pl.BlockSpec((SEQ, HEAD_DIM), lambda g, i: (0, COL_WK + g)),
                  pl.BlockSpec((SEQ, HEAD_DIM), lambda g, i: (0, COL_WV + g)),
                  pl.BlockSpec((1, NSA_GROUP, TQ, TQ), lambda g, i: (0, 2 + g, 0, 0)),
                  pl.BlockSpec((1, NSA_GROUP, TQ, TQ), lambda g, i: (1, 2 + g, 0, 0)),
                  pl.BlockSpec((TQ, TQ), lambda g, i: (0, 0))],
        out_specs=pl.BlockSpec((TQ, NSA_GROUP * HEAD_DIM), lambda g, i: (i, g)),
        out_shape=jax.ShapeDtypeStruct((SEQ, NSA_HEADS * HEAD_DIM), F32),
        compiler_params=_cparams("parallel", "arbitrary"),
        name="win",
    )(zb, zb, zb, near, near, win2)


def _layer_norm(y, g, b):
    mu = jnp.mean(y, axis=-1, keepdims=True)
    yc = y - mu
    var = jnp.mean(yc * yc, axis=-1, keepdims=True)
    return yc * lax.rsqrt(var + LN_EPS) * g + b


def _outproj_kernel(x_ref, om_ref, oc_ref, os_ref, ow_ref, g_ref, e_ref, w_ref, lg_ref, lb_ref,
                    h_ref, hb_ref):
    gate = g_ref[...]
    parts = _split3(gate)
    o_n = jnp.zeros(oc_ref.shape, F32)
    for c, branch in enumerate((oc_ref, os_ref, ow_ref)):
        gexp = jnp.zeros(oc_ref.shape, F32)
        for part in parts:
            gexp = gexp + jnp.dot(part, e_ref[c], preferred_element_type=F32)
        o_n = o_n + gexp * branch[...]
    o = jnp.concatenate([om_ref[...], o_n.astype(BF16)], axis=1)
    a = jnp.dot(o, w_ref[...], preferred_element_type=F32)
    h = _layer_norm(DN_ALPHA * x_ref[...] + a, lg_ref[...], lb_ref[...])
    h_ref[...] = h
    hb_ref[...] = h.astype(BF16)


def _outproj(x, o_m, o_c, o_s, o_w, gates, expand, w_out, ln_g, ln_b, tm=256):
    half = NSA_HEADS * HEAD_DIM
    row = lambda i: (i, 0)
    fixed2 = lambda i: (0, 0)
    return pl.pallas_call(
        _outproj_kernel,
        grid=(SEQ // tm,),
        in_specs=[pl.BlockSpec((tm, D_MODEL), row),
                  pl.BlockSpec((tm, half), row), pl.BlockSpec((tm, half), row),
                  pl.BlockSpec((tm, half), row), pl.BlockSpec((tm, half), row),
                  pl.BlockSpec((tm, HEAD_DIM), row),
                  pl.BlockSpec((3, HEAD_DIM, half), lambda i: (0, 0, 0)),
                  pl.BlockSpec((D_MODEL, D_MODEL), fixed2),
                  pl.BlockSpec((1, D_MODEL), fixed2), pl.BlockSpec((1, D_MODEL), fixed2)],
        out_specs=[pl.BlockSpec((tm, D_MODEL), row), pl.BlockSpec((tm, D_MODEL), row)],
        out_shape=[jax.ShapeDtypeStruct((SEQ, D_MODEL), F32),
                   jax.ShapeDtypeStruct((SEQ, D_MODEL), BF16)],
        compiler_params=_cparams("parallel"),
        name="outproj_ln",
    )(x, o_m, o_c, o_s, o_w, gates, expand, w_out, ln_g, ln_b)


def _ffn_kernel(hb_ref, w1_ref, w2_ref, f_ref):
    @pl.when(pl.program_id(1) == 0)
    def _():
        f_ref[...] = jnp.zeros(f_ref.shape, F32)

    u = jnp.maximum(jnp.dot(hb_ref[...], w1_ref[...].astype(BF16), preferred_element_type=F32), 0.0)
    f_ref[...] += jnp.dot((u * u).astype(BF16), w2_ref[...].astype(BF16), preferred_element_type=F32)


def _ffn(hb, w1, w2, tm=1024, tf=512):
    return pl.pallas_call(
        _ffn_kernel,
        grid=(SEQ // tm, D_FF // tf),
        in_specs=[pl.BlockSpec((tm, D_MODEL), lambda i, c: (i, 0)),
                  pl.BlockSpec((D_MODEL, tf), lambda i, c: (0, c)),
                  pl.BlockSpec((tf, D_MODEL), lambda i, c: (c, 0))],
        out_specs=pl.BlockSpec((tm, D_MODEL), lambda i, c: (i, 0)),
        out_shape=jax.ShapeDtypeStruct((SEQ, D_MODEL), F32),
        compiler_params=_cparams("parallel", "arbitrary", vmem=BIG_VMEM_LIMIT),
        name="ffn",
    )(hb, w1, w2)


def _final_kernel(h_ref, hb_ref, f_ref, p_ref, wg_ref, wp_ref, lg_ref, lb_ref, o_ref):
    gate = jax.nn.sigmoid(jnp.dot(hb_ref[...], wg_ref[...], preferred_element_type=F32))
    emb = jnp.dot(p_ref[...].astype(BF16), wp_ref[...], preferred_element_type=F32)
    y = DN_ALPHA * h_ref[...] + f_ref[...] + gate * emb
    o_ref[...] = _layer_norm(y, lg_ref[...], lb_ref[...])


def _final(h, hb, f, p, wg, wp, ln_g, ln_b, tm=256):
    row = lambda i: (i, 0)
    fixed2 = lambda i: (0, 0)
    return pl.pallas_call(
        _final_kernel,
        grid=(SEQ // tm,),
        in_specs=[pl.BlockSpec((tm, D_MODEL), row), pl.BlockSpec((tm, D_MODEL), row),
                  pl.BlockSpec((tm, D_MODEL), row), pl.BlockSpec((tm, D_PLE), row),
                  pl.BlockSpec((D_MODEL, D_MODEL), fixed2), pl.BlockSpec((D_PLE, D_MODEL), fixed2),
                  pl.BlockSpec((1, D_MODEL), fixed2), pl.BlockSpec((1, D_MODEL), fixed2)],
        out_specs=pl.BlockSpec((tm, D_MODEL), row),
        out_shape=jax.ShapeDtypeStruct((SEQ, D_MODEL), F32),
        compiler_params=_cparams("parallel"),
        name="final_ln",
    )(h, hb, f, p, wg, wp, ln_g, ln_b)


def _overlap_matrix_t():
    j = np.arange(N_SLC)[:, None]
    n = np.arange(512)[None, :]
    ov = ((n * CMP_STRIDE < j * SLC_LEN + SLC_LEN) & (n * CMP_STRIDE + CMP_LEN > j * SLC_LEN) & (n < N_CMP))
    return ov.astype(np.float32)


def _gate_expand():
    e = np.zeros((3, HEAD_DIM, NSA_HEADS * HEAD_DIM), np.float32)
    for c in range(3):
        for h in range(NSA_HEADS):
            e[c, 3 * h + c, h * HEAD_DIM:(h + 1) * HEAD_DIM] = 1.0
    return e


def _layer(h, p, w_in, cmp_pe_k, cmp_w1_k, cmp_w2_k, cmp_pe_v, cmp_w1_v, cmp_w2_v, tiles, w_out,
           ln1_g, ln1_b, w_ff1, w_ff2, w_ple, w_ple_gate, ln2_g, ln2_b):
    near, win2, cmpb = tiles
    scale = HEAD_DIM ** -0.5
    colscale = np.ones((1, GATE_COL0), np.float32)
    colscale[0, COL_MQ * HEAD_DIM:(COL_MQ + MOBA_HEADS) * HEAD_DIM] = scale
    colscale[0, COL_NQ * HEAD_DIM:(COL_NQ + NSA_HEADS) * HEAD_DIM] = scale
    w_gl = jnp.pad(w_in[:, GATE_COL0:], ((0, 0), (0, HEAD_DIM - NSA_HEADS * 3))).astype(BF16)
    zb, gates = _inproj(h, w_in, jnp.asarray(colscale), w_gl)

    ckv = zb[:, COL_CK * HEAD_DIM:(COL_CK + 4) * HEAD_DIM]
    xr = ckv.reshape(SEQ // CMP_STRIDE, CMP_STRIDE, 4, HEAD_DIM).transpose(2, 0, 1, 3)
    xr = xr.reshape(4, SEQ // CMP_STRIDE, CMP_STRIDE * HEAD_DIM)
    pe = jnp.stack([cmp_pe_k, cmp_pe_v]).reshape(2, 2, 1, CMP_STRIDE * HEAD_DIM).astype(F32)
    w1 = jnp.stack([cmp_w1_k, cmp_w1_v]).reshape(2, 2, CMP_STRIDE * HEAD_DIM, HEAD_DIM).astype(BF16)
    w2 = jnp.stack([cmp_w2_k, cmp_w2_v]).astype(BF16)
    kc, vct = _compress(xr, pe, w1, w2)

    key = lax.broadcasted_iota(jnp.int32, (SEQ, HEAD_DIM), 0)
    lane = lax.broadcasted_iota(jnp.int32, (SEQ, HEAD_DIM), 1)
    pat_moba = (lane == key // MOBA_BLOCK).astype(BF16)
    pat_slc = (lane == key // SLC_LEN).astype(BF16)
    o_m = _moba(zb, near, pat_moba)
    o_c, selb = _cmp_select(zb, kc, vct, jnp.asarray(_overlap_matrix_t(), BF16), cmpb)
    o_s = _slc(zb, selb, near, pat_slc)
    o_w = _win(zb, near, win2)

    h1, h1b = _outproj(h, o_m, o_c, o_s, o_w, gates, jnp.asarray(_gate_expand(), BF16),
                       w_out.astype(BF16), ln1_g.reshape(1, -1), ln1_b.reshape(1, -1))
    f = _ffn(h1b, w_ff1, w_ff2)
    return _final(h1, h1b, f, p, w_ple_gate.astype(BF16), w_ple.astype(BF16),
                  ln2_g.reshape(1, -1), ln2_b.reshape(1, -1))


def kernel(x, p, w_in, cmp_pe_k, cmp_w1_k, cmp_w2_k, cmp_pe_v, cmp_w1_v, cmp_w2_v, rel_bias, w_out,
           ln1_g, ln1_b, w_ff1, w_ff2, w_ple, w_ple_gate, ln2_g, ln2_b):
    tiles = _bias_tiles(rel_bias)
    h = x[0]
    for i in range(w_in.shape[0]):
        h = _layer(h, p[i, 0], w_in[i], cmp_pe_k[i], cmp_w1_k[i], cmp_w2_k[i], cmp_pe_v[i], cmp_w1_v[i],
                   cmp_w2_v[i], tiles, w_out[i], ln1_g[i], ln1_b[i], w_ff1[i], w_ff2[i], w_ple[i],
                   w_ple_gate[i], ln2_g[i], ln2_b[i])
    return h[None]
```

```python
import functools
import math

import numpy as np
import jax
import jax.numpy as jnp
from jax import lax
from jax.experimental import pallas as pl
from jax.experimental.pallas import tpu as pltpu

D_MODEL = 2048
SEQ = 8192
HEAD_DIM = 128
N_HEADS = 16
MOBA_HEADS = 8
NSA_HEADS = 8
NSA_KV_HEADS = 2
NSA_GROUP = 4
MOBA_BLOCK = 256
MOBA_TOPK = 3
CMP_LEN = 32
CMP_STRIDE = 16
SLC_LEN = 64
SLC_TOPK = 16
WINDOW = 512
N_BUCKETS = 32
MAX_DISTANCE = 128
D_FF = 4 * D_MODEL
D_PLE = 256
LN_EPS = 1e-5
DN_ALPHA = 2.0 ** 0.25
NEG = -1e30
TINY = 1e-30
FORCE_SCORE = 1e4

N_CMP = (SEQ - CMP_LEN) // CMP_STRIDE + 1
N_SLC = SEQ // SLC_LEN
TQ = 256
N_QT = SEQ // TQ
CMP_WIN = 32
SLC_PER_TILE = TQ // SLC_LEN
HP_MOBA = 2
HP_SLC = 4
KG = 4
GROUP = KG * TQ
COL_MQ, COL_MK, COL_MV, COL_NQ = 0, 8, 16, 24
COL_CK, COL_CV, COL_SK, COL_SV, COL_WK, COL_WV = 32, 34, 36, 38, 40, 42
N_COLBLK = 44
GATE_COL0 = N_COLBLK * HEAD_DIM

VMEM_LIMIT = 48 * 1024 * 1024
BIG_VMEM_LIMIT = 56 * 1024 * 1024

BF16 = jnp.bfloat16
F32 = jnp.float32
_NT = (((1,), (1,)), ((), ()))


def _cparams(*sem, vmem=VMEM_LIMIT):
    return pltpu.CompilerParams(dimension_semantics=sem, vmem_limit_bytes=vmem)


def _bucket_table():
    n = np.arange(1024, dtype=np.int32)
    max_exact = N_BUCKETS // 2
    ratio = np.maximum(n, 1).astype(np.float32) / np.float32(max_exact)
    large = max_exact + (np.log(ratio).astype(np.float32) / np.float32(math.log(MAX_DISTANCE / max_exact))
                         * np.float32(N_BUCKETS - max_exact)).astype(np.int32)
    large = np.minimum(large, N_BUCKETS - 1)
    return np.where(n < max_exact, n, large).astype(np.int32)


def _bias_kernel(tab_ref, near_ref, cmpb_ref, *, steps):
    h = pl.program_id(0)
    far = tab_ref[h, N_BUCKETS - 1]

    def lookup(d):
        val = jnp.full(d.shape, tab_ref[h, 0], F32)
        for start, bucket in steps:
            val = jnp.where(d >= start, tab_ref[h, bucket], val)
        return jnp.where(d >= 0, val - far, NEG)

    a = lax.broadcasted_iota(jnp.int32, (TQ, TQ), 0)
    b = lax.broadcasted_iota(jnp.int32, (TQ, TQ), 1)
    near_ref[0, 0] = lookup(a - b)
    near_ref[1, 0] = lookup(a - b + TQ)
    w = lax.broadcasted_iota(jnp.int32, (CMP_WIN, TQ), 0)
    aw = lax.broadcasted_iota(jnp.int32, (CMP_WIN, TQ), 1)
    cmpb_ref[0, 0] = lookup(aw - CMP_STRIDE * w - (CMP_LEN - 1))
    cmpb_ref[1, 0] = lookup(aw - CMP_STRIDE * w + (TQ - CMP_LEN + 1))


def _bias_tiles(rel_bias):
    bucket = _bucket_table()
    steps = tuple((int(n), int(bucket[n])) for n in range(1, bucket.size) if bucket[n] != bucket[n - 1])
    near, cmpb = pl.pallas_call(
        functools.partial(_bias_kernel, steps=steps),
        grid=(N_HEADS,),
        in_specs=[pl.BlockSpec(memory_space=pltpu.SMEM)],
        out_specs=[pl.BlockSpec((2, 1, TQ, TQ), lambda h: (0, h, 0, 0)),
                   pl.BlockSpec((2, 1, CMP_WIN, TQ), lambda h: (0, h, 0, 0))],
        out_shape=[jax.ShapeDtypeStruct((2, N_HEADS, TQ, TQ), F32),
                   jax.ShapeDtypeStruct((2, N_HEADS, CMP_WIN, TQ), F32)],
        compiler_params=_cparams("parallel"),
        name="bias_tiles",
    )(rel_bias.T.astype(F32))
    a = np.arange(TQ)[:, None]
    b = np.arange(TQ)[None, :]
    win2 = jnp.asarray(np.where(a < b, 0.0, NEG).astype(np.float32))
    return near, win2, cmpb


def _inproj_kernel(x_ref, w_ref, cs_ref, wg_ref, o_ref, g_ref, xb_ref):
    @pl.when(pl.program_id(1) == 0)
    def _():
        xb = x_ref[...].astype(BF16)
        xb_ref[...] = xb
        g_ref[...] = jax.nn.sigmoid(jnp.dot(xb, wg_ref[...], preferred_element_type=F32))

    acc = jnp.dot(xb_ref[...], w_ref[...].astype(BF16), preferred_element_type=F32)
    o_ref[...] = (acc * cs_ref[...]).astype(o_ref.dtype)


def _inproj(x, w_in, colscale, w_gate, tm=1024, tn=512):
    m, k = x.shape
    return pl.pallas_call(
        _inproj_kernel,
        grid=(m // tm, GATE_COL0 // tn),
        in_specs=[pl.BlockSpec((tm, k), lambda i, j: (i, 0)),
                  pl.BlockSpec((k, tn), lambda i, j: (0, j)),
                  pl.BlockSpec((1, tn), lambda i, j: (0, j)),
                  pl.BlockSpec((k, HEAD_DIM), lambda i, j: (0, 0))],
        out_specs=[pl.BlockSpec((tm, tn), lambda i, j: (i, j)),
                   pl.BlockSpec((tm, HEAD_DIM), lambda i, j: (i, 0))],
        out_shape=[jax.ShapeDtypeStruct((m, GATE_COL0), BF16),
                   jax.ShapeDtypeStruct((m, HEAD_DIM), F32)],
        scratch_shapes=[pltpu.VMEM((tm, k), BF16)],
        compiler_params=_cparams("parallel", "arbitrary"),
        name="inproj",
    )(x, w_in, colscale, w_gate)


def _gelu_tanh(x):
    return 0.5 * x * (1.0 + jnp.tanh(math.sqrt(2.0 / math.pi) * (x + 0.044715 * (x * x * x))))


def _compress_kernel(xk_ref, xv_ref, pe_ref, w1_ref, w2_ref, kc_ref, vct_ref):
    def compress(x_ref, kv):
        x = x_ref[0].astype(F32)
        lo = jnp.dot((x + pe_ref[kv, 0]).astype(BF16), w1_ref[kv, 0], preferred_element_type=F32)
        hi = jnp.dot((x + pe_ref[kv, 1]).astype(BF16), w1_ref[kv, 1], preferred_element_type=F32)
        y = lo + pltpu.roll(hi, N_CMP, 0)
        out = jnp.dot(_gelu_tanh(y).astype(BF16), w2_ref[kv], preferred_element_type=F32)
        row = lax.broadcasted_iota(jnp.int32, out.shape, 0)
        return jnp.where(row < N_CMP, out, 0.0)

    kc_ref[0] = compress(xk_ref, 0).astype(kc_ref.dtype)
    vct_ref[0] = compress(xv_ref, 1).T.astype(vct_ref.dtype)


def _compress(xr, pe, w1, w2):
    full = lambda g: (0, 0, 0, 0)
    return pl.pallas_call(
        _compress_kernel,
        grid=(NSA_KV_HEADS,),
        in_specs=[pl.BlockSpec((1, 512, 2048), lambda g: (g, 0, 0)),
                  pl.BlockSpec((1, 512, 2048), lambda g: (NSA_KV_HEADS + g, 0, 0)),
                  pl.BlockSpec((2, 2, 1, 2048), full),
                  pl.BlockSpec((2, 2, 2048, HEAD_DIM), full),
                  pl.BlockSpec((2, HEAD_DIM, HEAD_DIM), lambda g: (0, 0, 0))],
        out_specs=[pl.BlockSpec((1, 512, HEAD_DIM), lambda g: (g, 0, 0)),
                   pl.BlockSpec((1, HEAD_DIM, 512), lambda g: (g, 0, 0))],
        out_shape=[jax.ShapeDtypeStruct((NSA_KV_HEADS, 512, HEAD_DIM), BF16),
                   jax.ShapeDtypeStruct((NSA_KV_HEADS, HEAD_DIM, 512), BF16)],
        compiler_params=_cparams("parallel"),
        name="compress",
    )(xr, xr, pe, w1, w2)


def _split3(x):
    hi = x.astype(BF16)
    r1 = x - hi.astype(F32)
    mid = r1.astype(BF16)
    lo = (r1 - mid.astype(F32)).astype(BF16)
    return hi, mid, lo


def _topk_bias_t(score, k):
    row = lax.broadcasted_iota(jnp.int32, score.shape, 0).astype(F32)
    selb = jnp.full(score.shape, NEG, F32)
    s = score
    for _ in range(k):
        mx = jnp.max(s, axis=0, keepdims=True)
        idx = jnp.min(jnp.where(s == mx, row, 1e9), axis=0, keepdims=True)
        hit = row == idx
        selb = jnp.where(hit, 0.0, selb)
        s = jnp.where(hit, -3e38, s)
    return selb


def _lane_tile_max(s):
    m = s[:, :HEAD_DIM]
    for c in range(1, s.shape[1] // HEAD_DIM):
        m = jnp.maximum(m, s[:, c * HEAD_DIM:(c + 1) * HEAD_DIM])
    return m


def _sparse_flash(qs, selbs, k_ref, v_ref, kcols, pat_ref, d0s, d1s, i, sub, s_ref, mx_ref, acc_ref):
    heads = range(len(qs))
    lane = lax.broadcasted_iota(jnp.int32, (TQ, HEAD_DIM), 1)
    far = lane < (i - 1) * sub
    q_near = [jnp.concatenate([qs[h], selbs[h].astype(BF16)], axis=1) for h in heads]
    q_far = [jnp.concatenate([qs[h], jnp.where(far, selbs[h], NEG).astype(BF16)], axis=1) for h in heads]

    def k_aug(h, off, n):
        c = kcols[h]
        return jnp.concatenate([k_ref[pl.ds(off, n), c:c + HEAD_DIM], pat_ref[pl.ds(off, n), :]], axis=1)

    neg_tile = jnp.full((TQ, TQ), NEG, F32)
    for h in heads:
        for u in range(1, KG):
            s_ref[h, jnp.minimum(i + u, N_QT - 1)] = neg_tile
        mx_ref[h] = jnp.full((TQ, HEAD_DIM), NEG, F32)
        acc_ref[h] = jnp.zeros((TQ, 2 * HEAD_DIM), F32)

    def far_group(g, carry):
        off = pl.multiple_of(g * GROUP, GROUP)
        for h in heads:
            s = lax.dot_general(q_far[h], k_aug(h, off, GROUP), _NT, preferred_element_type=F32)
            for u in range(KG):
                s_ref[h, g * KG + u] = s[:, u * TQ:(u + 1) * TQ]
            mx_ref[h] = jnp.maximum(mx_ref[h], _lane_tile_max(s))
        return carry

    n_far = lax.shift_right_logical(jnp.maximum(i - 1, 0) + (KG - 1), 2)
    lax.fori_loop(0, n_far, far_group, 0)

    prev = jnp.maximum(i - 1, 0)
    off_prev = pl.multiple_of(prev * TQ, TQ)
    off_own = pl.multiple_of(i * TQ, TQ)
    for h in heads:
        k2 = jnp.concatenate([k_aug(h, off_prev, TQ), k_aug(h, off_own, TQ)], axis=0)
        bias = jnp.concatenate([jnp.where(i >= 1, d1s[h], NEG), d0s[h]], axis=1)
        s = lax.dot_general(q_near[h], k2, _NT, preferred_element_type=F32) + bias
        s_ref[h, prev] = s[:, :TQ]
        s_ref[h, i] = s[:, TQ:]
        m = jnp.max(jnp.maximum(mx_ref[h], _lane_tile_max(s)), axis=1, keepdims=True)
        mx_ref[h] = jnp.broadcast_to(m, (TQ, HEAD_DIM))
    ones_v = jnp.ones((GROUP, HEAD_DIM), BF16)

    def pv_group(g, carry):
        off = pl.multiple_of(g * GROUP, GROUP)
        for h in heads:
            m = mx_ref[h]
            m2 = jnp.concatenate([m, m], axis=1)
            p = jnp.concatenate([jnp.exp(s_ref[h, g * KG + u] - m2).astype(BF16) for u in range(KG)], axis=1)
            c = kcols[h]
            v_aug = jnp.concatenate([v_ref[pl.ds(off, GROUP), c:c + HEAD_DIM], ones_v], axis=1)
            acc_ref[h] += jnp.dot(p, v_aug, preferred_element_type=F32)
        return carry

    lax.fori_loop(0, lax.shift_right_logical(i + KG, 2), pv_group, 0)
    outs = []
    for h in heads:
        acc = acc_ref[h]
        outs.append(acc[:, :HEAD_DIM] / acc[:, HEAD_DIM:])
    return outs


def _flash_scratch(hp):
    return [pltpu.VMEM((hp, N_QT, TQ, TQ), F32),
            pltpu.VMEM((hp, TQ, HEAD_DIM), F32),
            pltpu.VMEM((hp, TQ, 2 * HEAD_DIM), F32)]


def _moba_kernel(q_ref, k_ref, v_ref, pat_ref, d0_ref, d1_ref, o_ref, kmean_ref, *flash_scratch):
    i = pl.program_id(1)
    hp = HP_MOBA

    @pl.when(i == 0)
    def _():
        blk = lax.broadcasted_iota(jnp.int32, (HEAD_DIM, SEQ), 0)
        pos = lax.broadcasted_iota(jnp.int32, (HEAD_DIM, SEQ), 1)
        ind = jnp.where(lax.shift_right_logical(pos, 8) == blk, 1.0 / MOBA_BLOCK, 0.0).astype(BF16)
        for h in range(hp):
            kmean_ref[h] = jnp.dot(ind, k_ref[:, h * HEAD_DIM:(h + 1) * HEAD_DIM], preferred_element_type=F32)

    blk_t = lax.broadcasted_iota(jnp.int32, (HEAD_DIM, TQ), 0)
    past = blk_t < i
    qs, selbs = [], []
    for h in range(hp):
        q = q_ref[:, h * HEAD_DIM:(h + 1) * HEAD_DIM]
        gate = jnp.zeros((HEAD_DIM, TQ), F32)
        for part in _split3(kmean_ref[h]):
            gate = gate + lax.dot_general(part, q, _NT, preferred_element_type=F32)
        selb = _topk_bias_t(jnp.where(past, gate, NEG), MOBA_TOPK)
        selb = jnp.where(past, selb, jnp.where(blk_t == i, 0.0, NEG))
        qs.append(q)
        selbs.append(selb.T)
    cols = [h * HEAD_DIM for h in range(hp)]
    outs = _sparse_flash(qs, selbs, k_ref, v_ref, cols, pat_ref, [d0_ref[0, h] for h in range(hp)],
                         [d1_ref[0, h] for h in range(hp)], i, 1, *flash_scratch)
    for h in range(hp):
        o_ref[:, h * HEAD_DIM:(h + 1) * HEAD_DIM] = outs[h].astype(o_ref.dtype)


def _moba(zb, near, pat):
    hp = HP_MOBA
    w = hp * HEAD_DIM
    return pl.pallas_call(
        _moba_kernel,
        grid=(MOBA_HEADS // hp, N_QT),
        in_specs=[pl.BlockSpec((TQ, w), lambda h, i: (i, COL_MQ // hp + h)),
                  pl.BlockSpec((SEQ, w), lambda h, i: (0, COL_MK // hp + h)),
                  pl.BlockSpec((SEQ, w), lambda h, i: (0, COL_MV // hp + h)),
                  pl.BlockSpec((SEQ, HEAD_DIM), lambda h, i: (0, 0)),
                  pl.BlockSpec((1, hp, TQ, TQ), lambda h, i: (0, h, 0, 0)),
                  pl.BlockSpec((1, hp, TQ, TQ), lambda h, i: (1, h, 0, 0))],
        out_specs=pl.BlockSpec((TQ, w), lambda h, i: (i, h)),
        out_shape=jax.ShapeDtypeStruct((SEQ, MOBA_HEADS * HEAD_DIM), BF16),
        scratch_shapes=[pltpu.VMEM((hp, HEAD_DIM, HEAD_DIM), F32)] + _flash_scratch(hp),
        compiler_params=_cparams("parallel", "arbitrary"),
        name="moba",
    )(zb, zb, zb, pat, near, near)


def _cmp_kernel(q_ref, kc_ref, vct_ref, ovt_ref, cb_ref, oc_ref, sel_ref, s_ref):
    i = pl.program_id(0)
    win0 = pl.multiple_of(jnp.maximum(i * CMP_STRIDE - CMP_STRIDE, 0), CMP_STRIDE)
    n = lax.broadcasted_iota(jnp.int32, (512, TQ), 0)
    reachable = n < win0 + CMP_WIN
    blk = lax.broadcasted_iota(jnp.int32, (N_SLC, TQ), 0)
    cur = i * SLC_PER_TILE + lax.shift_right_logical(lax.broadcasted_iota(jnp.int32, (N_SLC, TQ), 1), 6)
    allowed = blk <= cur
    forced = (blk == 0) | (blk == cur) | (blk == cur - 1)
    for g in range(NSA_KV_HEADS):
        p_sum = jnp.zeros((512, TQ), F32)
        for r in range(NSA_GROUP):
            hd = g * NSA_GROUP + r
            q = q_ref[:, hd * HEAD_DIM:(hd + 1) * HEAD_DIM]
            s_ref[...] = lax.dot_general(kc_ref[g], q, _NT, preferred_element_type=F32)
            s_ref[pl.ds(win0, CMP_WIN), :] += cb_ref[0, hd]
            s = jnp.where(reachable, s_ref[...], NEG)
            valid = s > 0.5 * NEG
            e = jnp.where(valid, jnp.exp(s - jnp.max(s, axis=0, keepdims=True)), 0.0)
            p = e * (1.0 / jnp.maximum(jnp.sum(e, axis=0, keepdims=True), TINY))
            o_t = jnp.dot(vct_ref[g], p.astype(BF16), preferred_element_type=F32)
            oc_ref[:, hd * HEAD_DIM:(hd + 1) * HEAD_DIM] = o_t.T
            p_sum = p_sum + p
        imp = jnp.zeros((N_SLC, TQ), F32)
        for part in _split3(p_sum):
            imp = imp + jnp.dot(ovt_ref[...], part, preferred_element_type=F32)
        score = jnp.where(allowed, jnp.where(forced, FORCE_SCORE, imp), -1.0)
        selb = jnp.where(allowed, _topk_bias_t(score, SLC_TOPK), NEG)
        sel_ref[g] = selb.T.astype(sel_ref.dtype)


def _cmp_select(zb, kc, vct, ovt, cmpb):
    w = NSA_HEADS * HEAD_DIM
    return pl.pallas_call(
        _cmp_kernel,
        grid=(N_QT,),
        in_specs=[pl.BlockSpec((TQ, w), lambda i: (i, COL_NQ // NSA_HEADS)),
                  pl.BlockSpec((NSA_KV_HEADS, 512, HEAD_DIM), lambda i: (0, 0, 0)),
                  pl.BlockSpec((NSA_KV_HEADS, HEAD_DIM, 512), lambda i: (0, 0, 0)),
                  pl.BlockSpec((N_SLC, 512), lambda i: (0, 0)),
                  pl.BlockSpec((1, NSA_HEADS, CMP_WIN, TQ), lambda i: (jnp.minimum(i, 1), 1, 0, 0))],
        out_specs=[pl.BlockSpec((TQ, w), lambda i: (i, 0)),
                   pl.BlockSpec((NSA_KV_HEADS, TQ, N_SLC), lambda i: (0, i, 0))],
        out_shape=[jax.ShapeDtypeStruct((SEQ, w), F32),
                   jax.ShapeDtypeStruct((NSA_KV_HEADS, SEQ, N_SLC), BF16)],
        scratch_shapes=[pltpu.VMEM((512, TQ), F32)],
        compiler_params=_cparams("arbitrary"),
        name="cmp_select",
    )(zb, kc, vct, ovt, cmpb)


def _slc_kernel(q_ref, sel_ref, k_ref, v_ref, pat_ref, d0_ref, d1_ref, o_ref, *flash_scratch):
    i = pl.program_id(1)
    hp = HP_SLC
    selb = sel_ref[0].astype(F32)
    qs = [q_ref[:, h * HEAD_DIM:(h + 1) * HEAD_DIM] for h in range(hp)]
    outs = _sparse_flash(qs, [selb] * hp, k_ref, v_ref, [0] * hp, pat_ref,
                         [d0_ref[0, h] for h in range(hp)], [d1_ref[0, h] for h in range(hp)],
                         i, SLC_PER_TILE, *flash_scratch)
    for h in range(hp):
        o_ref[:, h * HEAD_DIM:(h + 1) * HEAD_DIM] = outs[h].astype(o_ref.dtype)


def _slc(zb, selb, near, pat):
    hp = HP_SLC
    w = hp * HEAD_DIM
    per_group = NSA_GROUP // hp
    return pl.pallas_call(
        _slc_kernel,
        grid=(NSA_HEADS // hp, N_QT),
        in_specs=[pl.BlockSpec((TQ, w), lambda h, i: (i, COL_NQ // hp + h)),
                  pl.BlockSpec((1, TQ, N_SLC), lambda h, i: (h // per_group, i, 0)),
                  pl.BlockSpec((SEQ, HEAD_DIM), lambda h, i: (0, COL_SK + h // per_group)),
                  pl.BlockSpec((SEQ, HEAD_DIM), lambda h, i: (0, COL_SV + h // per_group)),
                  pl.BlockSpec((SEQ, HEAD_DIM), lambda h, i: (0, 0)),
                  pl.BlockSpec((1, hp, TQ, TQ), lambda h, i: (0, MOBA_HEADS // hp + h, 0, 0)),
                  pl.BlockSpec((1, hp, TQ, TQ), lambda h, i: (1, MOBA_HEADS // hp + h, 0, 0))],
        out_specs=pl.BlockSpec((TQ, w), lambda h, i: (i, h)),
        out_shape=jax.ShapeDtypeStruct((SEQ, NSA_HEADS * HEAD_DIM), F32),
        scratch_shapes=_flash_scratch(hp),
        compiler_params=_cparams("parallel", "arbitrary", vmem=BIG_VMEM_LIMIT),
        name="slc",
    )(zb, selb, zb, zb, pat, near, near)


def _win_kernel(q_ref, k_ref, v_ref, d0_ref, d1_ref, w2_ref, o_ref):
    i = pl.program_id(1)
    offs = [pl.multiple_of(jnp.maximum(i - b, 0) * TQ, TQ) for b in range(3)]
    k_all = jnp.concatenate([k_ref[pl.ds(off, TQ), :] for off in offs], axis=0)
    v_all = jnp.concatenate([v_ref[pl.ds(off, TQ), :] for off in offs], axis=0)
    v_aug = jnp.concatenate([v_all, jnp.ones((3 * TQ, HEAD_DIM), BF16)], axis=1)
    mask2 = jnp.where(i >= 2, w2_ref[...], NEG)
    for r in range(NSA_GROUP):
        q = q_ref[:, r * HEAD_DIM:(r + 1) * HEAD_DIM]
        bias = jnp.concatenate([d0_ref[0, r], jnp.where(i >= 1, d1_ref[0, r], NEG), mask2], axis=1)
        s = lax.dot_general(q, k_all, _NT, preferred_element_type=F32) + bias
        m = jnp.max(s, axis=1, keepdims=True)
        pv = jnp.dot(jnp.exp(s - m).astype(BF16), v_aug, preferred_element_type=F32)
        o_ref[:, r * HEAD_DIM:(r + 1) * HEAD_DIM] = pv[:, :HEAD_DIM] / pv[:, HEAD_DIM:]


def _win(zb, near, win2):
    return pl.pallas_call(
        _win_kernel,
        grid=(NSA_KV_HEADS, N_QT),
        in_specs=[pl.BlockSpec((TQ, NSA_GROUP * HEAD_DIM), lambda g, i: (i, COL_NQ // NSA_GROUP + g)),
                  pl.BlockSpec((SEQ, HEAD_DIM), lambda g, i: (0, COL_WK + g)),
                  pl.BlockSpec((SEQ, HEAD_DIM), lambda g, i: (0, COL_WV + g)),
                  pl.BlockSpec((1, NSA_GROUP, TQ, TQ), lambda g, i: (0, 2 + g, 0, 0)),
                  pl.BlockSpec((1, NSA_GROUP, TQ, TQ), lambda g, i: (1, 2 + g, 0, 0)),
                  pl.BlockSpec((TQ, TQ), lambda g, i: (0, 0))],
        out_specs=pl.BlockSpec((TQ, NSA_GROUP * HEAD_DIM), lambda g, i: (i, g)),
        out_shape=jax.ShapeDtypeStruct((SEQ, NSA_HEADS * HEAD_DIM), F32),
        compiler_params=_cparams("parallel", "arbitrary"),
        name="win",
    )(zb, zb, zb, near, near, win2)


def _layer_norm(y, g, b):
    mu = jnp.mean(y, axis=-1, keepdims=True)
    yc = y - mu
    var = jnp.mean(yc * yc, axis=-1, keepdims=True)
    return yc * lax.rsqrt(var + LN_EPS) * g + b


ROW_CHUNK = 256


def _outproj_kernel(x_ref, om_ref, oc_ref, os_ref, ow_ref, g_ref, e_ref, w_ref, lg_ref, lb_ref,
                    h_ref, hb_ref):
    for c0 in range(0, x_ref.shape[0], ROW_CHUNK):
        rows = slice(c0, c0 + ROW_CHUNK)
        gate = g_ref[rows, :]
        hi = gate.astype(BF16)
        lo = (gate - hi.astype(F32)).astype(BF16)
        g2 = jnp.concatenate([hi, lo], axis=1)
        o_n = jnp.zeros((ROW_CHUNK, oc_ref.shape[1]), F32)
        for c, branch in enumerate((oc_ref, os_ref, ow_ref)):
            o_n = o_n + jnp.dot(g2, e_ref[c], preferred_element_type=F32) * branch[rows, :]
        o = jnp.concatenate([om_ref[rows, :], o_n.astype(BF16)], axis=1)
        a = jnp.dot(o, w_ref[...], preferred_element_type=F32)
        h = _layer_norm(DN_ALPHA * x_ref[rows, :] + a, lg_ref[...], lb_ref[...])
        h_ref[rows, :] = h
        hb_ref[rows, :] = h.astype(BF16)


def _resident(shape):
    return pl.BlockSpec(shape, lambda i: (0,) * len(shape), pipeline_mode=pl.Buffered(1))


def _outproj(x, o_m, o_c, o_s, o_w, gates, expand, w_out, ln_g, ln_b, tm=512):
    half = NSA_HEADS * HEAD_DIM
    row = lambda i: (i, 0)
    return pl.pallas_call(
        _outproj_kernel,
        grid=(SEQ // tm,),
        in_specs=[pl.BlockSpec((tm, D_MODEL), row),
                  pl.BlockSpec((tm, half), row), pl.BlockSpec((tm, half), row),
                  pl.BlockSpec((tm, half), row), pl.BlockSpec((tm, half), row),
                  pl.BlockSpec((tm, HEAD_DIM), row),
                  _resident((3, 2 * HEAD_DIM, half)),
                  _resident((D_MODEL, D_MODEL)),
                  _resident((1, D_MODEL)), _resident((1, D_MODEL))],
        out_specs=[pl.BlockSpec((tm, D_MODEL), row), pl.BlockSpec((tm, D_MODEL), row)],
        out_shape=[jax.ShapeDtypeStruct((SEQ, D_MODEL), F32),
                   jax.ShapeDtypeStruct((SEQ, D_MODEL), BF16)],
        compiler_params=_cparams("parallel", vmem=BIG_VMEM_LIMIT),
        name="outproj_ln",
    )(x, o_m, o_c, o_s, o_w, gates, expand, w_out, ln_g, ln_b)


def _ffn_kernel(hb_ref, w1_ref, w2_ref, f_ref):
    @pl.when(pl.program_id(1) == 0)
    def _():
        f_ref[...] = jnp.zeros(f_ref.shape, F32)

    u = jnp.maximum(jnp.dot(hb_ref[...], w1_ref[...].astype(BF16), preferred_element_type=F32), 0.0)
    f_ref[...] += jnp.dot((u * u).astype(BF16), w2_ref[...].astype(BF16), preferred_element_type=F32)


def _ffn(hb, w1, w2, tm=1024, tf=512):
    return pl.pallas_call(
        _ffn_kernel,
        grid=(SEQ // tm, D_FF // tf),
        in_specs=[pl.BlockSpec((tm, D_MODEL), lambda i, c: (i, 0)),
                  pl.BlockSpec((D_MODEL, tf), lambda i, c: (0, c)),
                  pl.BlockSpec((tf, D_MODEL), lambda i, c: (c, 0))],
        out_specs=pl.BlockSpec((tm, D_MODEL), lambda i, c: (i, 0)),
        out_shape=jax.ShapeDtypeStruct((SEQ, D_MODEL), F32),
        compiler_params=_cparams("parallel", "arbitrary", vmem=BIG_VMEM_LIMIT),
        name="ffn",
    )(hb, w1, w2)


def _final_kernel(h_ref, hb_ref, f_ref, p_ref, wg_ref, wp_ref, lg_ref, lb_ref, o_ref):
    for c0 in range(0, h_ref.shape[0], ROW_CHUNK):
        rows = slice(c0, c0 + ROW_CHUNK)
        gate = jax.nn.sigmoid(jnp.dot(hb_ref[rows, :], wg_ref[...], preferred_element_type=F32))
        emb = jnp.dot(p_ref[rows, :].astype(BF16), wp_ref[...], preferred_element_type=F32)
        y = DN_ALPHA * h_ref[rows, :] + f_ref[rows, :] + gate * emb
        o_ref[rows, :] = _layer_norm(y, lg_ref[...], lb_ref[...])


def _final(h, hb, f, p, wg, wp, ln_g, ln_b, tm=512):
    row = lambda i: (i, 0)
    return pl.pallas_call(
        _final_kernel,
        grid=(SEQ // tm,),
        in_specs=[pl.BlockSpec((tm, D_MODEL), row), pl.BlockSpec((tm, D_MODEL), row),
                  pl.BlockSpec((tm, D_MODEL), row), pl.BlockSpec((tm, D_PLE), row),
                  _resident((D_MODEL, D_MODEL)), _resident((D_PLE, D_MODEL)),
                  _resident((1, D_MODEL)), _resident((1, D_MODEL))],
        out_specs=pl.BlockSpec((tm, D_MODEL), row),
        out_shape=jax.ShapeDtypeStruct((SEQ, D_MODEL), F32),
        compiler_params=_cparams("parallel"),
        name="final_ln",
    )(h, hb, f, p, wg, wp, ln_g, ln_b)


def _overlap_matrix_t():
    j = np.arange(N_SLC)[:, None]
    n = np.arange(512)[None, :]
    ov = ((n * CMP_STRIDE < j * SLC_LEN + SLC_LEN) & (n * CMP_STRIDE + CMP_LEN > j * SLC_LEN) & (n < N_CMP))
    return ov.astype(np.float32)


def _gate_expand():
    e = np.zeros((3, 2 * HEAD_DIM, NSA_HEADS * HEAD_DIM), np.float32)
    for c in range(3):
        for h in range(NSA_HEADS):
            e[c, 3 * h + c, h * HEAD_DIM:(h + 1) * HEAD_DIM] = 1.0
            e[c, HEAD_DIM + 3 * h + c, h * HEAD_DIM:(h + 1) * HEAD_DIM] = 1.0
    return e


def _layer(h, p, w_in, cmp_pe_k, cmp_w1_k, cmp_w2_k, cmp_pe_v, cmp_w1_v, cmp_w2_v, tiles, w_out,
           ln1_g, ln1_b, w_ff1, w_ff2, w_ple, w_ple_gate, ln2_g, ln2_b):
    near, win2, cmpb = tiles
    scale = HEAD_DIM ** -0.5
    colscale = np.ones((1, GATE_COL0), np.float32)
    colscale[0, COL_MQ * HEAD_DIM:(COL_MQ + MOBA_HEADS) * HEAD_DIM] = scale
    colscale[0, COL_NQ * HEAD_DIM:(COL_NQ + NSA_HEADS) * HEAD_DIM] = scale
    w_gl = jnp.pad(w_in[:, GATE_COL0:], ((0, 0), (0, HEAD_DIM - NSA_HEADS * 3))).astype(BF16)
    zb, gates = _inproj(h, w_in, jnp.asarray(colscale), w_gl)

    ckv = zb[:, COL_CK * HEAD_DIM:(COL_CK + 4) * HEAD_DIM]
    xr = ckv.reshape(SEQ // CMP_STRIDE, CMP_STRIDE, 4, HEAD_DIM).transpose(2, 0, 1, 3)
    xr = xr.reshape(4, SEQ // CMP_STRIDE, CMP_STRIDE * HEAD_DIM)
    pe = jnp.stack([cmp_pe_k, cmp_pe_v]).reshape(2, 2, 1, CMP_STRIDE * HEAD_DIM).astype(F32)
    w1 = jnp.stack([cmp_w1_k, cmp_w1_v]).reshape(2, 2, CMP_STRIDE * HEAD_DIM, HEAD_DIM).astype(BF16)
    w2 = jnp.stack([cmp_w2_k, cmp_w2_v]).astype(BF16)
    kc, vct = _compress(xr, pe, w1, w2)

    key = lax.broadcasted_iota(jnp.int32, (SEQ, HEAD_DIM), 0)
    lane = lax.broadcasted_iota(jnp.int32, (SEQ, HEAD_DIM), 1)
    pat_moba = (lane == key // MOBA_BLOCK).astype(BF16)
    pat_slc = (lane == key // SLC_LEN).astype(BF16)
    o_m = _moba(zb, near, pat_moba)
    o_c, selb = _cmp_select(zb, kc, vct, jnp.asarray(_overlap_matrix_t(), BF16), cmpb)
    o_s = _slc(zb, selb, near, pat_slc)
    o_w = _win(zb, near, win2)

    h1, h1b = _outproj(h, o_m, o_c, o_s, o_w, gates, jnp.asarray(_gate_expand(), BF16),
                       w_out.astype(BF16), ln1_g.reshape(1, -1), ln1_b.reshape(1, -1))
    f = _ffn(h1b, w_ff1, w_ff2)
    return _final(h1, h1b, f, p, w_ple_gate.astype(BF16), w_ple.astype(BF16),
                  ln2_g.reshape(1, -1), ln2_b.reshape(1, -1))


def kernel(x, p, w_in, cmp_pe_k, cmp_w1_k, cmp_w2_k, cmp_pe_v, cmp_w1_v, cmp_w2_v, rel_bias, w_out,
           ln1_g, ln1_b, w_ff1, w_ff2, w_ple, w_ple_gate, ln2_g, ln2_b):
    tiles = _bias_tiles(rel_bias)
    h = x[0]
    for i in range(w_in.shape[0]):
        h = _layer(h, p[i, 0], w_in[i], cmp_pe_k[i], cmp_w1_k[i], cmp_w2_k[i], cmp_pe_v[i], cmp_w1_v[i],
                   cmp_w2_v[i], tiles, w_out[i], ln1_g[i], ln1_b[i], w_ff1[i], w_ff2[i], w_ple[i],
                   w_ple_gate[i], ln2_g[i], ln2_b[i])
    return h[None]
```

```python
import functools
import math

import numpy as np
import jax
import jax.numpy as jnp
from jax import lax
from jax.experimental import pallas as pl
from jax.experimental.pallas import tpu as pltpu

D_MODEL = 2048
SEQ = 8192
HEAD_DIM = 128
N_HEADS = 16
MOBA_HEADS = 8
NSA_HEADS = 8
NSA_KV_HEADS = 2
NSA_GROUP = 4
MOBA_BLOCK = 256
MOBA_TOPK = 3
CMP_LEN = 32
CMP_STRIDE = 16
SLC_LEN = 64
SLC_TOPK = 16
WINDOW = 512
N_BUCKETS = 32
MAX_DISTANCE = 128
D_FF = 4 * D_MODEL
D_PLE = 256
LN_EPS = 1e-5
DN_ALPHA = 2.0 ** 0.25
NEG = -1e30
TINY = 1e-30
FORCE_SCORE = 1e4

N_CMP = (SEQ - CMP_LEN) // CMP_STRIDE + 1
N_SLC = SEQ // SLC_LEN
TQ = 256
N_QT = SEQ // TQ
CMP_WIN = 32
SLC_PER_TILE = TQ // SLC_LEN
HP_MOBA = 4
HP_SLC = 4
KG = 4
GROUP = KG * TQ
COL_MQ, COL_MK, COL_MV, COL_NQ = 0, 8, 16, 24
COL_CK, COL_CV, COL_SK, COL_SV, COL_WK, COL_WV = 32, 34, 36, 38, 40, 42
N_COLBLK = 44
GATE_COL0 = N_COLBLK * HEAD_DIM

VMEM_LIMIT = 48 * 1024 * 1024
BIG_VMEM_LIMIT = 56 * 1024 * 1024

BF16 = jnp.bfloat16
F32 = jnp.float32
_NT = (((1,), (1,)), ((), ()))


def _cparams(*sem, vmem=VMEM_LIMIT):
    return pltpu.CompilerParams(dimension_semantics=sem, vmem_limit_bytes=vmem)


def _bucket_table():
    n = np.arange(1024, dtype=np.int32)
    max_exact = N_BUCKETS // 2
    ratio = np.maximum(n, 1).astype(np.float32) / np.float32(max_exact)
    large = max_exact + (np.log(ratio).astype(np.float32) / np.float32(math.log(MAX_DISTANCE / max_exact))
                         * np.float32(N_BUCKETS - max_exact)).astype(np.int32)
    large = np.minimum(large, N_BUCKETS - 1)
    return np.where(n < max_exact, n, large).astype(np.int32)


def _bias_kernel(tab_ref, near_ref, cmpb_ref, *, steps):
    h = pl.program_id(0)
    far = tab_ref[h, N_BUCKETS - 1]

    def lookup(d):
        val = jnp.full(d.shape, tab_ref[h, 0], F32)
        for start, bucket in steps:
            val = jnp.where(d >= start, tab_ref[h, bucket], val)
        return jnp.where(d >= 0, val - far, NEG)

    a = lax.broadcasted_iota(jnp.int32, (TQ, TQ), 0)
    b = lax.broadcasted_iota(jnp.int32, (TQ, TQ), 1)
    near_ref[0, 0] = lookup(a - b)
    near_ref[1, 0] = lookup(a - b + TQ)
    w = lax.broadcasted_iota(jnp.int32, (CMP_WIN, TQ), 0)
    aw = lax.broadcasted_iota(jnp.int32, (CMP_WIN, TQ), 1)
    cmpb_ref[0, 0] = lookup(aw - CMP_STRIDE * w - (CMP_LEN - 1))
    cmpb_ref[1, 0] = lookup(aw - CMP_STRIDE * w + (TQ - CMP_LEN + 1))


def _bias_tiles(rel_bias):
    bucket = _bucket_table()
    steps = tuple((int(n), int(bucket[n])) for n in range(1, bucket.size) if bucket[n] != bucket[n - 1])
    near, cmpb = pl.pallas_call(
        functools.partial(_bias_kernel, steps=steps),
        grid=(N_HEADS,),
        in_specs=[pl.BlockSpec(memory_space=pltpu.SMEM)],
        out_specs=[pl.BlockSpec((2, 1, TQ, TQ), lambda h: (0, h, 0, 0)),
                   pl.BlockSpec((2, 1, CMP_WIN, TQ), lambda h: (0, h, 0, 0))],
        out_shape=[jax.ShapeDtypeStruct((2, N_HEADS, TQ, TQ), F32),
                   jax.ShapeDtypeStruct((2, N_HEADS, CMP_WIN, TQ), F32)],
        compiler_params=_cparams("parallel"),
        name="bias_tiles",
    )(rel_bias.T.astype(F32))
    a = np.arange(TQ)[:, None]
    b = np.arange(TQ)[None, :]
    win2 = jnp.asarray(np.where(a < b, 0.0, NEG).astype(np.float32))
    return near, win2, cmpb


def _inproj_kernel(x_ref, w_ref, cs_ref, wg_ref, o_ref, g_ref, xb_ref):
    @pl.when(pl.program_id(1) == 0)
    def _():
        xb = x_ref[...].astype(BF16)
        xb_ref[...] = xb
        g_ref[...] = jax.nn.sigmoid(jnp.dot(xb, wg_ref[...], preferred_element_type=F32))

    acc = jnp.dot(xb_ref[...], w_ref[...].astype(BF16), preferred_element_type=F32)
    o_ref[...] = (acc * cs_ref[...]).astype(o_ref.dtype)


def _inproj(x, w_in, colscale, w_gate, tm=1024, tn=512):
    m, k = x.shape
    return pl.pallas_call(
        _inproj_kernel,
        grid=(m // tm, GATE_COL0 // tn),
        in_specs=[pl.BlockSpec((tm, k), lambda i, j: (i, 0)),
                  pl.BlockSpec((k, tn), lambda i, j: (0, j)),
                  pl.BlockSpec((1, tn), lambda i, j: (0, j)),
                  pl.BlockSpec((k, HEAD_DIM), lambda i, j: (0, 0))],
        out_specs=[pl.BlockSpec((tm, tn), lambda i, j: (i, j)),
                   pl.BlockSpec((tm, HEAD_DIM), lambda i, j: (i, 0))],
        out_shape=[jax.ShapeDtypeStruct((m, GATE_COL0), BF16),
                   jax.ShapeDtypeStruct((m, HEAD_DIM), F32)],
        scratch_shapes=[pltpu.VMEM((tm, k), BF16)],
        compiler_params=_cparams("parallel", "arbitrary"),
        name="inproj",
    )(x, w_in, colscale, w_gate)


def _gelu_tanh(x):
    return 0.5 * x * (1.0 + jnp.tanh(math.sqrt(2.0 / math.pi) * (x + 0.044715 * (x * x * x))))


def _compress_kernel(xk_ref, xv_ref, pe_ref, w1_ref, w2_ref, kc_ref, vct_ref):
    def compress(x_ref, kv):
        x = x_ref[0].astype(F32)
        lo = jnp.dot((x + pe_ref[kv, 0]).astype(BF16), w1_ref[kv, 0], preferred_element_type=F32)
        hi = jnp.dot((x + pe_ref[kv, 1]).astype(BF16), w1_ref[kv, 1], preferred_element_type=F32)
        y = lo + pltpu.roll(hi, N_CMP, 0)
        out = jnp.dot(_gelu_tanh(y).astype(BF16), w2_ref[kv], preferred_element_type=F32)
        row = lax.broadcasted_iota(jnp.int32, out.shape, 0)
        return jnp.where(row < N_CMP, out, 0.0)

    kc_ref[0] = compress(xk_ref, 0).astype(kc_ref.dtype)
    vct_ref[0] = compress(xv_ref, 1).T.astype(vct_ref.dtype)


def _compress(xr, pe, w1, w2):
    full = lambda g: (0, 0, 0, 0)
    return pl.pallas_call(
        _compress_kernel,
        grid=(NSA_KV_HEADS,),
        in_specs=[pl.BlockSpec((1, 512, 2048), lambda g: (g, 0, 0)),
                  pl.BlockSpec((1, 512, 2048), lambda g: (NSA_KV_HEADS + g, 0, 0)),
                  pl.BlockSpec((2, 2, 1, 2048), full),
                  pl.BlockSpec((2, 2, 2048, HEAD_DIM), full),
                  pl.BlockSpec((2, HEAD_DIM, HEAD_DIM), lambda g: (0, 0, 0))],
        out_specs=[pl.BlockSpec((1, 512, HEAD_DIM), lambda g: (g, 0, 0)),
                   pl.BlockSpec((1, HEAD_DIM, 512), lambda g: (g, 0, 0))],
        out_shape=[jax.ShapeDtypeStruct((NSA_KV_HEADS, 512, HEAD_DIM), BF16),
                   jax.ShapeDtypeStruct((NSA_KV_HEADS, HEAD_DIM, 512), BF16)],
        compiler_params=_cparams("parallel"),
        name="compress",
    )(xr, xr, pe, w1, w2)


def _split3(x):
    hi = x.astype(BF16)
    r1 = x - hi.astype(F32)
    mid = r1.astype(BF16)
    lo = (r1 - mid.astype(F32)).astype(BF16)
    return hi, mid, lo


def _topk_bias_t(score, k):
    row = lax.broadcasted_iota(jnp.int32, score.shape, 0).astype(F32)
    selb = jnp.full(score.shape, NEG, F32)
    s = score
    for _ in range(k):
        mx = jnp.max(s, axis=0, keepdims=True)
        idx = jnp.min(jnp.where(s == mx, row, 1e9), axis=0, keepdims=True)
        hit = row == idx
        selb = jnp.where(hit, 0.0, selb)
        s = jnp.where(hit, -3e38, s)
    return selb


def _lane_tile_max(s):
    m = s[:, :HEAD_DIM]
    for c in range(1, s.shape[1] // HEAD_DIM):
        m = jnp.maximum(m, s[:, c * HEAD_DIM:(c + 1) * HEAD_DIM])
    return m


def _sparse_flash(qs, selbs, k_ref, v_ref, kcols, pat_ref, d0s, d1s, i, sub, t_ref, r_ref, mx_ref, acc_ref):
    heads = range(len(qs))
    lane = lax.broadcasted_iota(jnp.int32, (TQ, HEAD_DIM), 1)
    far = lane < (i - 1) * sub
    q_near = [jnp.concatenate([qs[h], selbs[h].astype(BF16)], axis=1) for h in heads]
    q_far = [jnp.concatenate([qs[h], jnp.where(far, selbs[h], NEG).astype(BF16)], axis=1) for h in heads]

    def k_aug(h, off, n):
        c = kcols[h]
        return jnp.concatenate([k_ref[pl.ds(off, n), c:c + HEAD_DIM], pat_ref[pl.ds(off, n), :]], axis=1)

    def v_aug(h, off, n):
        c = kcols[h]
        return jnp.concatenate([v_ref[pl.ds(off, n), c:c + HEAD_DIM], jnp.ones((n, HEAD_DIM), BF16)], axis=1)

    for h in heads:
        mx_ref[h] = jnp.full((TQ, HEAD_DIM), NEG, F32)

    def far_group(g, carry):
        off = pl.multiple_of(g * GROUP, GROUP)
        for h in heads:
            s = lax.dot_general(q_far[h], k_aug(h, off, GROUP), _NT, preferred_element_type=F32)
            ref = _lane_tile_max(s)
            t = (s - jnp.concatenate([ref] * (GROUP // HEAD_DIM), axis=1)).astype(t_ref.dtype)
            for u in range(KG):
                t_ref[h, g * KG + u] = t[:, u * TQ:(u + 1) * TQ]
            r_ref[h, g] = ref
            mx_ref[h] = jnp.maximum(mx_ref[h], ref)
        return carry

    n_far = lax.shift_right_logical(jnp.maximum(i - 1, 0) + (KG - 1), 2)
    lax.fori_loop(0, n_far, far_group, 0)

    off_prev = pl.multiple_of(jnp.maximum(i - 1, 0) * TQ, TQ)
    off_own = pl.multiple_of(i * TQ, TQ)
    for h in heads:
        k2 = jnp.concatenate([k_aug(h, off_prev, TQ), k_aug(h, off_own, TQ)], axis=0)
        v2 = jnp.concatenate([v_aug(h, off_prev, TQ), v_aug(h, off_own, TQ)], axis=0)
        bias = jnp.concatenate([jnp.where(i >= 1, d1s[h], NEG), d0s[h]], axis=1)
        s = lax.dot_general(q_near[h], k2, _NT, preferred_element_type=F32) + bias
        m = jnp.max(jnp.maximum(mx_ref[h], _lane_tile_max(s)), axis=1, keepdims=True)
        mx_ref[h] = jnp.broadcast_to(m, (TQ, HEAD_DIM))
        acc_ref[h] = jnp.dot(jnp.exp(s - m).astype(BF16), v2, preferred_element_type=F32)

    def pv_group(g, carry):
        off = pl.multiple_of(g * GROUP, GROUP)
        for h in heads:
            d = r_ref[h, g] - mx_ref[h]
            d2 = jnp.concatenate([d, d], axis=1)
            p = jnp.concatenate([jnp.exp(t_ref[h, g * KG + u].astype(F32) + d2).astype(BF16)
                                 for u in range(KG)], axis=1)
            acc_ref[h] += jnp.dot(p, v_aug(h, off, GROUP), preferred_element_type=F32)
        return carry

    lax.fori_loop(0, n_far, pv_group, 0)
    outs = []
    for h in heads:
        acc = acc_ref[h]
        outs.append(acc[:, :HEAD_DIM] / acc[:, HEAD_DIM:])
    return outs


def _flash_scratch(hp, logit_dtype):
    return [pltpu.VMEM((hp, N_QT, TQ, TQ), logit_dtype),
            pltpu.VMEM((hp, SEQ // GROUP, TQ, HEAD_DIM), F32),
            pltpu.VMEM((hp, TQ, HEAD_DIM), F32),
            pltpu.VMEM((hp, TQ, 2 * HEAD_DIM), F32)]


def _moba_kernel(q_ref, k_ref, v_ref, pat_ref, d0_ref, d1_ref, o_ref, kmean_ref, *flash_scratch):
    i = pl.program_id(1)
    hp = HP_MOBA

    @pl.when(i == 0)
    def _():
        blk = lax.broadcasted_iota(jnp.int32, (HEAD_DIM, SEQ), 0)
        pos = lax.broadcasted_iota(jnp.int32, (HEAD_DIM, SEQ), 1)
        ind = jnp.where(lax.shift_right_logical(pos, 8) == blk, 1.0 / MOBA_BLOCK, 0.0).astype(BF16)
        for h in range(hp):
            kmean_ref[h] = jnp.dot(ind, k_ref[:, h * HEAD_DIM:(h + 1) * HEAD_DIM], preferred_element_type=F32)

    blk_t = lax.broadcasted_iota(jnp.int32, (HEAD_DIM, TQ), 0)
    past = blk_t < i
    qs, selbs = [], []
    for h in range(hp):
        q = q_ref[:, h * HEAD_DIM:(h + 1) * HEAD_DIM]
        gate = jnp.zeros((HEAD_DIM, TQ), F32)
        for part in _split3(kmean_ref[h]):
            gate = gate + lax.dot_general(part, q, _NT, preferred_element_type=F32)
        selb = _topk_bias_t(jnp.where(past, gate, NEG), MOBA_TOPK)
        selb = jnp.where(past, selb, jnp.where(blk_t == i, 0.0, NEG))
        qs.append(q)
        selbs.append(selb.T)
    cols = [h * HEAD_DIM for h in range(hp)]
    outs = _sparse_flash(qs, selbs, k_ref, v_ref, cols, pat_ref, [d0_ref[0, h] for h in range(hp)],
                         [d1_ref[0, h] for h in range(hp)], i, 1, *flash_scratch)
    for h in range(hp):
        o_ref[:, h * HEAD_DIM:(h + 1) * HEAD_DIM] = outs[h].astype(o_ref.dtype)


def _moba(zb, near, pat):
    hp = HP_MOBA
    w = hp * HEAD_DIM
    return pl.pallas_call(
        _moba_kernel,
        grid=(MOBA_HEADS // hp, N_QT),
        in_specs=[pl.BlockSpec((TQ, w), lambda h, i: (i, COL_MQ // hp + h)),
                  pl.BlockSpec((SEQ, w), lambda h, i: (0, COL_MK // hp + h), pipeline_mode=pl.Buffered(1)),
                  pl.BlockSpec((SEQ, w), lambda h, i: (0, COL_MV // hp + h), pipeline_mode=pl.Buffered(1)),
                  pl.BlockSpec((SEQ, HEAD_DIM), lambda h, i: (0, 0), pipeline_mode=pl.Buffered(1)),
                  pl.BlockSpec((1, hp, TQ, TQ), lambda h, i: (0, h, 0, 0)),
                  pl.BlockSpec((1, hp, TQ, TQ), lambda h, i: (1, h, 0, 0))],
        out_specs=pl.BlockSpec((TQ, w), lambda h, i: (i, h)),
        out_shape=jax.ShapeDtypeStruct((SEQ, MOBA_HEADS * HEAD_DIM), BF16),
        scratch_shapes=[pltpu.VMEM((hp, HEAD_DIM, HEAD_DIM), F32)] + _flash_scratch(hp, BF16),
        compiler_params=_cparams("parallel", "arbitrary", vmem=BIG_VMEM_LIMIT),
        name="moba",
    )(zb, zb, zb, pat, near, near)


def _cmp_kernel(q_ref, kc_ref, vct_ref, ovt_ref, cb_ref, oc_ref, sel_ref, s_ref):
    i = pl.program_id(0)
    win0 = pl.multiple_of(jnp.maximum(i * CMP_STRIDE - CMP_STRIDE, 0), CMP_STRIDE)
    n = lax.broadcasted_iota(jnp.int32, (512, TQ), 0)
    reachable = n < win0 + CMP_WIN
    blk = lax.broadcasted_iota(jnp.int32, (N_SLC, TQ), 0)
    cur = i * SLC_PER_TILE + lax.shift_right_logical(lax.broadcasted_iota(jnp.int32, (N_SLC, TQ), 1), 6)
    allowed = blk <= cur
    forced = (blk == 0) | (blk == cur) | (blk == cur - 1)
    for g in range(NSA_KV_HEADS):
        p_sum = jnp.zeros((512, TQ), F32)
        for r in range(NSA_GROUP):
            hd = g * NSA_GROUP + r
            q = q_ref[:, hd * HEAD_DIM:(hd + 1) * HEAD_DIM]
            s_ref[...] = lax.dot_general(kc_ref[g], q, _NT, preferred_element_type=F32)
            s_ref[pl.ds(win0, CMP_WIN), :] += cb_ref[0, hd]
            s = jnp.where(reachable, s_ref[...], NEG)
            valid = s > 0.5 * NEG
            e = jnp.where(valid, jnp.exp(s - jnp.max(s, axis=0, keepdims=True)), 0.0)
            p = e * (1.0 / jnp.maximum(jnp.sum(e, axis=0, keepdims=True), TINY))
            o_t = jnp.dot(vct_ref[g], p.astype(BF16), preferred_element_type=F32)
            oc_ref[:, hd * HEAD_DIM:(hd + 1) * HEAD_DIM] = o_t.T
            p_sum = p_sum + p
        imp = jnp.zeros((N_SLC, TQ), F32)
        for part in _split3(p_sum):
            imp = imp + jnp.dot(ovt_ref[...], part, preferred_element_type=F32)
        score = jnp.where(allowed, jnp.where(forced, FORCE_SCORE, imp), -1.0)
        selb = jnp.where(allowed, _topk_bias_t(score, SLC_TOPK), NEG)
        sel_ref[g] = selb.T.astype(sel_ref.dtype)


def _cmp_select(zb, kc, vct, ovt, cmpb):
    w = NSA_HEADS * HEAD_DIM
    return pl.pallas_call(
        _cmp_kernel,
        grid=(N_QT,),
        in_specs=[pl.BlockSpec((TQ, w), lambda i: (i, COL_NQ // NSA_HEADS)),
                  pl.BlockSpec((NSA_KV_HEADS, 512, HEAD_DIM), lambda i: (0, 0, 0)),
                  pl.BlockSpec((NSA_KV_HEADS, HEAD_DIM, 512), lambda i: (0, 0, 0)),
                  pl.BlockSpec((N_SLC, 512), lambda i: (0, 0)),
                  pl.BlockSpec((1, NSA_HEADS, CMP_WIN, TQ), lambda i: (jnp.minimum(i, 1), 1, 0, 0))],
        out_specs=[pl.BlockSpec((TQ, w), lambda i: (i, 0)),
                   pl.BlockSpec((NSA_KV_HEADS, TQ, N_SLC), lambda i: (0, i, 0))],
        out_shape=[jax.ShapeDtypeStruct((SEQ, w), F32),
                   jax.ShapeDtypeStruct((NSA_KV_HEADS, SEQ, N_SLC), BF16)],
        scratch_shapes=[pltpu.VMEM((512, TQ), F32)],
        compiler_params=_cparams("arbitrary"),
        name="cmp_select",
    )(zb, kc, vct, ovt, cmpb)


def _slc_kernel(q_ref, sel_ref, k_ref, v_ref, pat_ref, d0_ref, d1_ref, o_ref, *flash_scratch):
    i = pl.program_id(1)
    hp = HP_SLC
    selb = sel_ref[0].astype(F32)
    qs = [q_ref[:, h * HEAD_DIM:(h + 1) * HEAD_DIM] for h in range(hp)]
    outs = _sparse_flash(qs, [selb] * hp, k_ref, v_ref, [0] * hp, pat_ref,
                         [d0_ref[0, h] for h in range(hp)], [d1_ref[0, h] for h in range(hp)],
                         i, SLC_PER_TILE, *flash_scratch)
    for h in range(hp):
        o_ref[:, h * HEAD_DIM:(h + 1) * HEAD_DIM] = outs[h].astype(o_ref.dtype)


def _slc(zb, selb, near, pat):
    hp = HP_SLC
    w = hp * HEAD_DIM
    per_group = NSA_GROUP // hp
    return pl.pallas_call(
        _slc_kernel,
        grid=(NSA_HEADS // hp, N_QT),
        in_specs=[pl.BlockSpec((TQ, w), lambda h, i: (i, COL_NQ // hp + h)),
                  pl.BlockSpec((1, TQ, N_SLC), lambda h, i: (h // per_group, i, 0)),
                  pl.BlockSpec((SEQ, HEAD_DIM), lambda h, i: (0, COL_SK + h // per_group),
                               pipeline_mode=pl.Buffered(1)),
                  pl.BlockSpec((SEQ, HEAD_DIM), lambda h, i: (0, COL_SV + h // per_group),
                               pipeline_mode=pl.Buffered(1)),
                  pl.BlockSpec((SEQ, HEAD_DIM), lambda h, i: (0, 0), pipeline_mode=pl.Buffered(1)),
                  pl.BlockSpec((1, hp, TQ, TQ), lambda h, i: (0, MOBA_HEADS // hp + h, 0, 0)),
                  pl.BlockSpec((1, hp, TQ, TQ), lambda h, i: (1, MOBA_HEADS // hp + h, 0, 0))],
        out_specs=pl.BlockSpec((TQ, w), lambda h, i: (i, h)),
        out_shape=jax.ShapeDtypeStruct((SEQ, NSA_HEADS * HEAD_DIM), F32),
        scratch_shapes=_flash_scratch(hp, F32),
        compiler_params=_cparams("parallel", "arbitrary", vmem=BIG_VMEM_LIMIT),
        name="slc",
    )(zb, selb, zb, zb, pat, near, near)


def _win_kernel(q_ref, k_ref, v_ref, d0_ref, d1_ref, w2_ref, o_ref):
    i = pl.program_id(1)
    offs = [pl.multiple_of(jnp.maximum(i - b, 0) * TQ, TQ) for b in range(3)]
    k_all = jnp.concatenate([k_ref[pl.ds(off, TQ), :] for off in offs], axis=0)
    v_all = jnp.concatenate([v_ref[pl.ds(off, TQ), :] for off in offs], axis=0)
    v_aug = jnp.concatenate([v_all, jnp.ones((3 * TQ, HEAD_DIM), BF16)], axis=1)
    mask2 = jnp.where(i >= 2, w2_ref[...], NEG)
    for r in range(NSA_GROUP):
        q = q_ref[:, r * HEAD_DIM:(r + 1) * HEAD_DIM]
        bias = jnp.concatenate([d0_ref[0, r], jnp.where(i >= 1, d1_ref[0, r], NEG), mask2], axis=1)
        s = lax.dot_general(q, k_all, _NT, preferred_element_type=F32) + bias
        m = jnp.max(s, axis=1, keepdims=True)
        pv = jnp.dot(jnp.exp(s - m).astype(BF16), v_aug, preferred_element_type=F32)
        o_ref[:, r * HEAD_DIM:(r + 1) * HEAD_DIM] = pv[:, :HEAD_DIM] / pv[:, HEAD_DIM:]


def _win(zb, near, win2):
    return pl.pallas_call(
        _win_kernel,
        grid=(NSA_KV_HEADS, N_QT),
        in_specs=[pl.BlockSpec((TQ, NSA_GROUP * HEAD_DIM), lambda g, i: (i, COL_NQ // NSA_GROUP + g)),
                  pl.BlockSpec((SEQ, HEAD_DIM), lambda g, i: (0, COL_WK + g)),
                  pl.BlockSpec((SEQ, HEAD_DIM), lambda g, i: (0, COL_WV + g)),
                  pl.BlockSpec((1, NSA_GROUP, TQ, TQ), lambda g, i: (0, 2 + g, 0, 0)),
                  pl.BlockSpec((1, NSA_GROUP, TQ, TQ), lambda g, i: (1, 2 + g, 0, 0)),
                  pl.BlockSpec((TQ, TQ), lambda g, i: (0, 0))],
        out_specs=pl.BlockSpec((TQ, NSA_GROUP * HEAD_DIM), lambda g, i: (i, g)),
        out_shape=jax.ShapeDtypeStruct((SEQ, NSA_HEADS * HEAD_DIM), F32),
        compiler_params=_cparams("parallel", "arbitrary"),
        name="win",
    )(zb, zb, zb, near, near, win2)


def _layer_norm(y, g, b):
    mu = jnp.mean(y, axis=-1, keepdims=True)
    yc = y - mu
    var = jnp.mean(yc * yc, axis=-1, keepdims=True)
    return yc * lax.rsqrt(var + LN_EPS) * g + b


ROW_CHUNK = 256


def _outproj_kernel(x_ref, om_ref, oc_ref, os_ref, ow_ref, g_ref, e_ref, w_ref, lg_ref, lb_ref,
                    h_ref, hb_ref):
    for c0 in range(0, x_ref.shape[0], ROW_CHUNK):
        rows = slice(c0, c0 + ROW_CHUNK)
        gate = g_ref[rows, :]
        hi = gate.astype(BF16)
        lo = (gate - hi.astype(F32)).astype(BF16)
        g2 = jnp.concatenate([hi, lo], axis=1)
        o_n = jnp.zeros((ROW_CHUNK, oc_ref.shape[1]), F32)
        for c, branch in enumerate((oc_ref, os_ref, ow_ref)):
            o_n = o_n + jnp.dot(g2, e_ref[c], preferred_element_type=F32) * branch[rows, :]
        o = jnp.concatenate([om_ref[rows, :], o_n.astype(BF16)], axis=1)
        a = jnp.dot(o, w_ref[...], preferred_element_type=F32)
        h = _layer_norm(DN_ALPHA * x_ref[rows, :] + a, lg_ref[...], lb_ref[...])
        h_ref[rows, :] = h
        hb_ref[rows, :] = h.astype(BF16)


def _resident(shape):
    return pl.BlockSpec(shape, lambda i: (0,) * len(shape), pipeline_mode=pl.Buffered(1))


def _outproj(x, o_m, o_c, o_s, o_w, gates, expand, w_out, ln_g, ln_b, tm=512):
    half = NSA_HEADS * HEAD_DIM
    row = lambda i: (i, 0)
    return pl.pallas_call(
        _outproj_kernel,
        grid=(SEQ // tm,),
        in_specs=[pl.BlockSpec((tm, D_MODEL), row),
                  pl.BlockSpec((tm, half), row), pl.BlockSpec((tm, half), row),
                  pl.BlockSpec((tm, half), row), pl.BlockSpec((tm, half), row),
                  pl.BlockSpec((tm, HEAD_DIM), row),
                  _resident((3, 2 * HEAD_DIM, half)),
                  _resident((D_MODEL, D_MODEL)),
                  _resident((1, D_MODEL)), _resident((1, D_MODEL))],
        out_specs=[pl.BlockSpec((tm, D_MODEL), row), pl.BlockSpec((tm, D_MODEL), row)],
        out_shape=[jax.ShapeDtypeStruct((SEQ, D_MODEL), F32),
                   jax.ShapeDtypeStruct((SEQ, D_MODEL), BF16)],
        compiler_params=_cparams("parallel", vmem=BIG_VMEM_LIMIT),
        name="outproj_ln",
    )(x, o_m, o_c, o_s, o_w, gates, expand, w_out, ln_g, ln_b)


def _ffn_kernel(hb_ref, w1_ref, w2_ref, f_ref):
    @pl.when(pl.program_id(1) == 0)
    def _():
        f_ref[...] = jnp.zeros(f_ref.shape, F32)

    u = jnp.maximum(jnp.dot(hb_ref[...], w1_ref[...].astype(BF16), preferred_element_type=F32), 0.0)
    f_ref[...] += jnp.dot((u * u).astype(BF16), w2_ref[...].astype(BF16), preferred_element_type=F32)


def _ffn(hb, w1, w2, tm=1024, tf=512):
    return pl.pallas_call(
        _ffn_kernel,
        grid=(SEQ // tm, D_FF // tf),
        in_specs=[pl.BlockSpec((tm, D_MODEL), lambda i, c: (i, 0)),
                  pl.BlockSpec((D_MODEL, tf), lambda i, c: (0, c)),
                  pl.BlockSpec((tf, D_MODEL), lambda i, c: (c, 0))],
        out_specs=pl.BlockSpec((tm, D_MODEL), lambda i, c: (i, 0)),
        out_shape=jax.ShapeDtypeStruct((SEQ, D_MODEL), F32),
        compiler_params=_cparams("parallel", "arbitrary", vmem=BIG_VMEM_LIMIT),
        name="ffn",
    )(hb, w1, w2)


def _final_kernel(h_ref, hb_ref, f_ref, p_ref, wg_ref, wp_ref, lg_ref, lb_ref, o_ref):
    for c0 in range(0, h_ref.shape[0], ROW_CHUNK):
        rows = slice(c0, c0 + ROW_CHUNK)
        gate = jax.nn.sigmoid(jnp.dot(hb_ref[rows, :], wg_ref[...], preferred_element_type=F32))
        emb = jnp.dot(p_ref[rows, :].astype(BF16), wp_ref[...], preferred_element_type=F32)
        y = DN_ALPHA * h_ref[rows, :] + f_ref[rows, :] + gate * emb
        o_ref[rows, :] = _layer_norm(y, lg_ref[...], lb_ref[...])


def _final(h, hb, f, p, wg, wp, ln_g, ln_b, tm=512):
    row = lambda i: (i, 0)
    return pl.pallas_call(
        _final_kernel,
        grid=(SEQ // tm,),
        in_specs=[pl.BlockSpec((tm, D_MODEL), row), pl.BlockSpec((tm, D_MODEL), row),
                  pl.BlockSpec((tm, D_MODEL), row), pl.BlockSpec((tm, D_PLE), row),
                  _resident((D_MODEL, D_MODEL)), _resident((D_PLE, D_MODEL)),
                  _resident((1, D_MODEL)), _resident((1, D_MODEL))],
        out_specs=pl.BlockSpec((tm, D_MODEL), row),
        out_shape=jax.ShapeDtypeStruct((SEQ, D_MODEL), F32),
        compiler_params=_cparams("parallel"),
        name="final_ln",
    )(h, hb, f, p, wg, wp, ln_g, ln_b)


def _overlap_matrix_t():
    j = np.arange(N_SLC)[:, None]
    n = np.arange(512)[None, :]
    ov = ((n * CMP_STRIDE < j * SLC_LEN + SLC_LEN) & (n * CMP_STRIDE + CMP_LEN > j * SLC_LEN) & (n < N_CMP))
    return ov.astype(np.float32)


def _gate_expand():
    e = np.zeros((3, 2 * HEAD_DIM, NSA_HEADS * HEAD_DIM), np.float32)
    for c in range(3):
        for h in range(NSA_HEADS):
            e[c, 3 * h + c, h * HEAD_DIM:(h + 1) * HEAD_DIM] = 1.0
            e[c, HEAD_DIM + 3 * h + c, h * HEAD_DIM:(h + 1) * HEAD_DIM] = 1.0
    return e


def _layer(h, p, w_in, cmp_pe_k, cmp_w1_k, cmp_w2_k, cmp_pe_v, cmp_w1_v, cmp_w2_v, tiles, w_out,
           ln1_g, ln1_b, w_ff1, w_ff2, w_ple, w_ple_gate, ln2_g, ln2_b):
    near, win2, cmpb = tiles
    scale = HEAD_DIM ** -0.5
    colscale = np.ones((1, GATE_COL0), np.float32)
    colscale[0, COL_MQ * HEAD_DIM:(COL_MQ + MOBA_HEADS) * HEAD_DIM] = scale
    colscale[0, COL_NQ * HEAD_DIM:(COL_NQ + NSA_HEADS) * HEAD_DIM] = scale
    w_gl = jnp.pad(w_in[:, GATE_COL0:], ((0, 0), (0, HEAD_DIM - NSA_HEADS * 3))).astype(BF16)
    zb, gates = _inproj(h, w_in, jnp.asarray(colscale), w_gl)

    ckv = zb[:, COL_CK * HEAD_DIM:(COL_CK + 4) * HEAD_DIM]
    xr = ckv.reshape(SEQ // CMP_STRIDE, CMP_STRIDE, 4, HEAD_DIM).transpose(2, 0, 1, 3)
    xr = xr.reshape(4, SEQ // CMP_STRIDE, CMP_STRIDE * HEAD_DIM)
    pe = jnp.stack([cmp_pe_k, cmp_pe_v]).reshape(2, 2, 1, CMP_STRIDE * HEAD_DIM).astype(F32)
    w1 = jnp.stack([cmp_w1_k, cmp_w1_v]).reshape(2, 2, CMP_STRIDE * HEAD_DIM, HEAD_DIM).astype(BF16)
    w2 = jnp.stack([cmp_w2_k, cmp_w2_v]).astype(BF16)
    kc, vct = _compress(xr, pe, w1, w2)

    key = lax.broadcasted_iota(jnp.int32, (SEQ, HEAD_DIM), 0)
    lane = lax.broadcasted_iota(jnp.int32, (SEQ, HEAD_DIM), 1)
    pat_moba = (lane == key // MOBA_BLOCK).astype(BF16)
    pat_slc = (lane == key // SLC_LEN).astype(BF16)
    o_m = _moba(zb, near, pat_moba)
    o_c, selb = _cmp_select(zb, kc, vct, jnp.asarray(_overlap_matrix_t(), BF16), cmpb)
    o_s = _slc(zb, selb, near, pat_slc)
    o_w = _win(zb, near, win2)

    h1, h1b = _outproj(h, o_m, o_c, o_s, o_w, gates, jnp.asarray(_gate_expand(), BF16),
                       w_out.astype(BF16), ln1_g.reshape(1, -1), ln1_b.reshape(1, -1))
    f = _ffn(h1b, w_ff1, w_ff2)
    return _final(h1, h1b, f, p, w_ple_gate.astype(BF16), w_ple.astype(BF16),
                  ln2_g.reshape(1, -1), ln2_b.reshape(1, -1))


def kernel(x, p, w_in, cmp_pe_k, cmp_w1_k, cmp_w2_k, cmp_pe_v, cmp_w1_v, cmp_w2_v, rel_bias, w_out,
           ln1_g, ln1_b, w_ff1, w_ff2, w_ple, w_ple_gate, ln2_g, ln2_b):
    tiles = _bias_tiles(rel_bias)
    h = x[0]
    for i in range(w_in.shape[0]):
        h = _layer(h, p[i, 0], w_in[i], cmp_pe_k[i], cmp_w1_k[i], cmp_w2_k[i], cmp_pe_v[i], cmp_w1_v[i],
                   cmp_w2_v[i], tiles, w_out[i], ln1_g[i], ln1_b[i], w_ff1[i], w_ff2[i], w_ple[i],
                   w_ple_gate[i], ln2_g[i], ln2_b[i])
    return h[None]
```

```python
import functools
import math

import numpy as np
import jax
import jax.numpy as jnp
from jax import lax
from jax.experimental import pallas as pl
from jax.experimental.pallas import tpu as pltpu

D_MODEL = 2048
SEQ = 8192
HEAD_DIM = 128
N_HEADS = 16
MOBA_HEADS = 8
NSA_HEADS = 8
NSA_KV_HEADS = 2
NSA_GROUP = 4
MOBA_BLOCK = 256
MOBA_TOPK = 3
CMP_LEN = 32
CMP_STRIDE = 16
SLC_LEN = 64
SLC_TOPK = 16
WINDOW = 512
N_BUCKETS = 32
MAX_DISTANCE = 128
D_FF = 4 * D_MODEL
D_PLE = 256
LN_EPS = 1e-5
DN_ALPHA = 2.0 ** 0.25
NEG = -1e30
TINY = 1e-30
FORCE_SCORE = 1e4

N_CMP = (SEQ - CMP_LEN) // CMP_STRIDE + 1
N_SLC = SEQ // SLC_LEN
TQ = 256
N_QT = SEQ // TQ
CMP_WIN = 32
SLC_PER_TILE = TQ // SLC_LEN
HP_MOBA = 4
HP_SLC = 4
KG = 4
GROUP = KG * TQ
COL_MQ, COL_MK, COL_MV, COL_NQ = 0, 8, 16, 24
COL_CK, COL_CV, COL_SK, COL_SV, COL_WK, COL_WV = 32, 34, 36, 38, 40, 42
N_COLBLK = 44
GATE_COL0 = N_COLBLK * HEAD_DIM

VMEM_LIMIT = 48 * 1024 * 1024
BIG_VMEM_LIMIT = 56 * 1024 * 1024

BF16 = jnp.bfloat16
F32 = jnp.float32
_NT = (((1,), (1,)), ((), ()))


def _cparams(*sem, vmem=VMEM_LIMIT):
    return pltpu.CompilerParams(dimension_semantics=sem, vmem_limit_bytes=vmem)


def _bucket_table():
    n = np.arange(1024, dtype=np.int32)
    max_exact = N_BUCKETS // 2
    ratio = np.maximum(n, 1).astype(np.float32) / np.float32(max_exact)
    large = max_exact + (np.log(ratio).astype(np.float32) / np.float32(math.log(MAX_DISTANCE / max_exact))
                         * np.float32(N_BUCKETS - max_exact)).astype(np.int32)
    large = np.minimum(large, N_BUCKETS - 1)
    return np.where(n < max_exact, n, large).astype(np.int32)


def _bias_kernel(tab_ref, near_ref, cmpb_ref, *, steps):
    h = pl.program_id(0)
    far = tab_ref[h, N_BUCKETS - 1]

    def lookup(d):
        val = jnp.full(d.shape, tab_ref[h, 0], F32)
        for start, bucket in steps:
            val = jnp.where(d >= start, tab_ref[h, bucket], val)
        return jnp.where(d >= 0, val - far, NEG)

    a = lax.broadcasted_iota(jnp.int32, (TQ, TQ), 0)
    b = lax.broadcasted_iota(jnp.int32, (TQ, TQ), 1)
    near_ref[0, 0] = lookup(a - b)
    near_ref[1, 0] = lookup(a - b + TQ)
    w = lax.broadcasted_iota(jnp.int32, (CMP_WIN, TQ), 0)
    aw = lax.broadcasted_iota(jnp.int32, (CMP_WIN, TQ), 1)
    cmpb_ref[0, 0] = lookup(aw - CMP_STRIDE * w - (CMP_LEN - 1))
    cmpb_ref[1, 0] = lookup(aw - CMP_STRIDE * w + (TQ - CMP_LEN + 1))


def _bias_tiles(rel_bias):
    bucket = _bucket_table()
    steps = tuple((int(n), int(bucket[n])) for n in range(1, bucket.size) if bucket[n] != bucket[n - 1])
    near, cmpb = pl.pallas_call(
        functools.partial(_bias_kernel, steps=steps),
        grid=(N_HEADS,),
        in_specs=[pl.BlockSpec(memory_space=pltpu.SMEM)],
        out_specs=[pl.BlockSpec((2, 1, TQ, TQ), lambda h: (0, h, 0, 0)),
                   pl.BlockSpec((2, 1, CMP_WIN, TQ), lambda h: (0, h, 0, 0))],
        out_shape=[jax.ShapeDtypeStruct((2, N_HEADS, TQ, TQ), F32),
                   jax.ShapeDtypeStruct((2, N_HEADS, CMP_WIN, TQ), F32)],
        compiler_params=_cparams("parallel"),
        name="bias_tiles",
    )(rel_bias.T.astype(F32))
    a = np.arange(TQ)[:, None]
    b = np.arange(TQ)[None, :]
    win2 = jnp.asarray(np.where(a < b, 0.0, NEG).astype(np.float32))
    return near, win2, cmpb


INPROJ_TN = 4 * HEAD_DIM
CKV_TILE = COL_CK * HEAD_DIM // INPROJ_TN


def _inproj_kernel(x_ref, w_ref, cs_ref, wg_ref, o_ref, g_ref, ckv_ref, xb_ref):
    j = pl.program_id(1)

    @pl.when(j == 0)
    def _():
        xb = x_ref[...].astype(BF16)
        xb_ref[...] = xb
        g_ref[...] = jax.nn.sigmoid(jnp.dot(xb, wg_ref[...], preferred_element_type=F32))

    acc = jnp.dot(xb_ref[...], w_ref[...].astype(BF16), preferred_element_type=F32)
    z = (acc * cs_ref[...]).astype(o_ref.dtype)
    o_ref[...] = z

    @pl.when(j == CKV_TILE)
    def _():
        for c in range(INPROJ_TN // HEAD_DIM):
            ckv_ref[c] = z[:, c * HEAD_DIM:(c + 1) * HEAD_DIM]


def _inproj(x, w_in, colscale, w_gate, tm=1024):
    m, k = x.shape
    tn = INPROJ_TN
    return pl.pallas_call(
        _inproj_kernel,
        grid=(m // tm, GATE_COL0 // tn),
        in_specs=[pl.BlockSpec((tm, k), lambda i, j: (i, 0)),
                  pl.BlockSpec((k, tn), lambda i, j: (0, j)),
                  pl.BlockSpec((1, tn), lambda i, j: (0, j)),
                  pl.BlockSpec((k, HEAD_DIM), lambda i, j: (0, 0))],
        out_specs=[pl.BlockSpec((tm, tn), lambda i, j: (i, j)),
                   pl.BlockSpec((tm, HEAD_DIM), lambda i, j: (i, 0)),
                   pl.BlockSpec((tn // HEAD_DIM, tm, HEAD_DIM), lambda i, j: (0, i, 0))],
        out_shape=[jax.ShapeDtypeStruct((m, GATE_COL0), BF16),
                   jax.ShapeDtypeStruct((m, HEAD_DIM), F32),
                   jax.ShapeDtypeStruct((tn // HEAD_DIM, m, HEAD_DIM), BF16)],
        scratch_shapes=[pltpu.VMEM((tm, k), BF16)],
        compiler_params=_cparams("parallel", "arbitrary"),
        name="inproj",
    )(x, w_in, colscale, w_gate)


def _gelu_tanh(x):
    return 0.5 * x * (1.0 + jnp.tanh(math.sqrt(2.0 / math.pi) * (x + 0.044715 * (x * x * x))))


def _compress_kernel(xk_ref, xv_ref, pek_ref, w1k_ref, w2k_ref, pev_ref, w1v_ref, w2v_ref, kc_ref, vct_ref):
    def compress(x_ref, pe_ref, w1_ref, w2_ref):
        x = x_ref[0].astype(F32)
        lo = jnp.dot((x + pe_ref[0]).astype(BF16), w1_ref[0].astype(BF16), preferred_element_type=F32)
        hi = jnp.dot((x + pe_ref[1]).astype(BF16), w1_ref[1].astype(BF16), preferred_element_type=F32)
        y = lo + pltpu.roll(hi, N_CMP, 0)
        out = jnp.dot(_gelu_tanh(y).astype(BF16), w2_ref[...].astype(BF16), preferred_element_type=F32)
        row = lax.broadcasted_iota(jnp.int32, out.shape, 0)
        return jnp.where(row < N_CMP, out, 0.0)

    kc_ref[0] = compress(xk_ref, pek_ref, w1k_ref, w2k_ref).astype(kc_ref.dtype)
    vct_ref[0] = compress(xv_ref, pev_ref, w1v_ref, w2v_ref).T.astype(vct_ref.dtype)


def _compress(xr, pe_k, w1_k, w2_k, pe_v, w1_v, w2_v):
    half = CMP_STRIDE * HEAD_DIM
    weights = [pl.BlockSpec((2, 1, half), lambda g: (0, 0, 0)),
               pl.BlockSpec((2, half, HEAD_DIM), lambda g: (0, 0, 0)),
               pl.BlockSpec((HEAD_DIM, HEAD_DIM), lambda g: (0, 0))]
    return pl.pallas_call(
        _compress_kernel,
        grid=(NSA_KV_HEADS,),
        in_specs=[pl.BlockSpec((1, 512, half), lambda g: (g, 0, 0)),
                  pl.BlockSpec((1, 512, half), lambda g: (NSA_KV_HEADS + g, 0, 0))] + weights + weights,
        out_specs=[pl.BlockSpec((1, 512, HEAD_DIM), lambda g: (g, 0, 0)),
                   pl.BlockSpec((1, HEAD_DIM, 512), lambda g: (g, 0, 0))],
        out_shape=[jax.ShapeDtypeStruct((NSA_KV_HEADS, 512, HEAD_DIM), BF16),
                   jax.ShapeDtypeStruct((NSA_KV_HEADS, HEAD_DIM, 512), BF16)],
        compiler_params=_cparams("parallel"),
        name="compress",
    )(xr, xr, pe_k, w1_k, w2_k, pe_v, w1_v, w2_v)


def _split3(x):
    hi = x.astype(BF16)
    r1 = x - hi.astype(F32)
    mid = r1.astype(BF16)
    lo = (r1 - mid.astype(F32)).astype(BF16)
    return hi, mid, lo


def _topk_bias_t(score, k):
    row = lax.broadcasted_iota(jnp.int32, score.shape, 0).astype(F32)
    selb = jnp.full(score.shape, NEG, F32)
    s = score
    for _ in range(k):
        mx = jnp.max(s, axis=0, keepdims=True)
        idx = jnp.min(jnp.where(s == mx, row, 1e9), axis=0, keepdims=True)
        hit = row == idx
        selb = jnp.where(hit, 0.0, selb)
        s = jnp.where(hit, -3e38, s)
    return selb


def _lane_tile_max(s):
    m = s[:, :HEAD_DIM]
    for c in range(1, s.shape[1] // HEAD_DIM):
        m = jnp.maximum(m, s[:, c * HEAD_DIM:(c + 1) * HEAD_DIM])
    return m


def _sparse_flash(qs, selbs, k_ref, v_ref, kcols, pat_ref, d0s, d1s, i, sub, t_ref, r_ref, mx_ref, acc_ref):
    heads = range(len(qs))
    lane = lax.broadcasted_iota(jnp.int32, (TQ, HEAD_DIM), 1)
    far = lane < (i - 1) * sub
    q_near = [jnp.concatenate([qs[h], selbs[h].astype(BF16)], axis=1) for h in heads]
    q_far = [jnp.concatenate([qs[h], jnp.where(far, selbs[h], NEG).astype(BF16)], axis=1) for h in heads]

    def k_aug(h, off, n):
        c = kcols[h]
        return jnp.concatenate([k_ref[pl.ds(off, n), c:c + HEAD_DIM], pat_ref[pl.ds(off, n), :]], axis=1)

    def v_aug(h, off, n):
        c = kcols[h]
        return jnp.concatenate([v_ref[pl.ds(off, n), c:c + HEAD_DIM], jnp.ones((n, HEAD_DIM), BF16)], axis=1)

    for h in heads:
        mx_ref[h] = jnp.full((TQ, HEAD_DIM), NEG, F32)

    def far_group(g, carry):
        off = pl.multiple_of(g * GROUP, GROUP)
        for h in heads:
            s = lax.dot_general(q_far[h], k_aug(h, off, GROUP), _NT, preferred_element_type=F32)
            ref = _lane_tile_max(s)
            t = (s - jnp.concatenate([ref] * (GROUP // HEAD_DIM), axis=1)).astype(t_ref.dtype)
            for u in range(KG):
                t_ref[h, g * KG + u] = t[:, u * TQ:(u + 1) * TQ]
            r_ref[h, g] = ref
            mx_ref[h] = jnp.maximum(mx_ref[h], ref)
        return carry

    n_far = lax.shift_right_logical(jnp.maximum(i - 1, 0) + (KG - 1), 2)
    lax.fori_loop(0, n_far, far_group, 0)

    off_prev = pl.multiple_of(jnp.maximum(i - 1, 0) * TQ, TQ)
    off_own = pl.multiple_of(i * TQ, TQ)
    for h in heads:
        k2 = jnp.concatenate([k_aug(h, off_prev, TQ), k_aug(h, off_own, TQ)], axis=0)
        v2 = jnp.concatenate([v_aug(h, off_prev, TQ), v_aug(h, off_own, TQ)], axis=0)
        bias = jnp.concatenate([jnp.where(i >= 1, d1s[h], NEG), d0s[h]], axis=1)
        s = lax.dot_general(q_near[h], k2, _NT, preferred_element_type=F32) + bias
        m = jnp.max(jnp.maximum(mx_ref[h], _lane_tile_max(s)), axis=1, keepdims=True)
        mx_ref[h] = jnp.broadcast_to(m, (TQ, HEAD_DIM))
        acc_ref[h] = jnp.dot(jnp.exp(s - m).astype(BF16), v2, preferred_element_type=F32)

    def pv_group(g, carry):
        off = pl.multiple_of(g * GROUP, GROUP)
        for h in heads:
            d = r_ref[h, g] - mx_ref[h]
            d2 = jnp.concatenate([d, d], axis=1)
            p = jnp.concatenate([jnp.exp(t_ref[h, g * KG + u].astype(F32) + d2).astype(BF16)
                                 for u in range(KG)], axis=1)
            acc_ref[h] += jnp.dot(p, v_aug(h, off, GROUP), preferred_element_type=F32)
        return carry

    lax.fori_loop(0, n_far, pv_group, 0)
    outs = []
    for h in heads:
        acc = acc_ref[h]
        outs.append(acc[:, :HEAD_DIM] / acc[:, HEAD_DIM:])
    return outs


def _flash_scratch(hp, logit_dtype):
    return [pltpu.VMEM((hp, N_QT, TQ, TQ), logit_dtype),
            pltpu.VMEM((hp, SEQ // GROUP, TQ, HEAD_DIM), F32),
            pltpu.VMEM((hp, TQ, HEAD_DIM), F32),
            pltpu.VMEM((hp, TQ, 2 * HEAD_DIM), F32)]


def _moba_kernel(q_ref, k_ref, v_ref, pat_ref, d0_ref, d1_ref, o_ref, kmean_ref, *flash_scratch):
    i = pl.program_id(1)
    hp = HP_MOBA

    @pl.when(i == 0)
    def _():
        blk = lax.broadcasted_iota(jnp.int32, (HEAD_DIM, SEQ), 0)
        pos = lax.broadcasted_iota(jnp.int32, (HEAD_DIM, SEQ), 1)
        ind = jnp.where(lax.shift_right_logical(pos, 8) == blk, 1.0 / MOBA_BLOCK, 0.0).astype(BF16)
        for h in range(hp):
            kmean_ref[h] = jnp.dot(ind, k_ref[:, h * HEAD_DIM:(h + 1) * HEAD_DIM], preferred_element_type=F32)

    blk_t = lax.broadcasted_iota(jnp.int32, (HEAD_DIM, TQ), 0)
    past = blk_t < i
    qs, selbs = [], []
    for h in range(hp):
        q = q_ref[:, h * HEAD_DIM:(h + 1) * HEAD_DIM]
        gate = jnp.zeros((HEAD_DIM, TQ), F32)
        for part in _split3(kmean_ref[h]):
            gate = gate + lax.dot_general(part, q, _NT, preferred_element_type=F32)
        selb = _topk_bias_t(jnp.where(past, gate, NEG), MOBA_TOPK)
        selb = jnp.where(past, selb, jnp.where(blk_t == i, 0.0, NEG))
        qs.append(q)
        selbs.append(selb.T)
    cols = [h * HEAD_DIM for h in range(hp)]
    outs = _sparse_flash(qs, selbs, k_ref, v_ref, cols, pat_ref, [d0_ref[0, h] for h in range(hp)],
                         [d1_ref[0, h] for h in range(hp)], i, 1, *flash_scratch)
    for h in range(hp):
        o_ref[:, h * HEAD_DIM:(h + 1) * HEAD_DIM] = outs[h].astype(o_ref.dtype)


def _moba(zb, near, pat):
    hp = HP_MOBA
    w = hp * HEAD_DIM
    return pl.pallas_call(
        _moba_kernel,
        grid=(MOBA_HEADS // hp, N_QT),
        in_specs=[pl.BlockSpec((TQ, w), lambda h, i: (i, COL_MQ // hp + h)),
                  pl.BlockSpec((SEQ, w), lambda h, i: (0, COL_MK // hp + h), pipeline_mode=pl.Buffered(1)),
                  pl.BlockSpec((SEQ, w), lambda h, i: (0, COL_MV // hp + h), pipeline_mode=pl.Buffered(1)),
                  pl.BlockSpec((SEQ, HEAD_DIM), lambda h, i: (0, 0), pipeline_mode=pl.Buffered(1)),
                  pl.BlockSpec((1, hp, TQ, TQ), lambda h, i: (0, h, 0, 0)),
                  pl.BlockSpec((1, hp, TQ, TQ), lambda h, i: (1, h, 0, 0))],
        out_specs=pl.BlockSpec((TQ, w), lambda h, i: (i, h)),
        out_shape=jax.ShapeDtypeStruct((SEQ, MOBA_HEADS * HEAD_DIM), BF16),
        scratch_shapes=[pltpu.VMEM((hp, HEAD_DIM, HEAD_DIM), F32)] + _flash_scratch(hp, BF16),
        compiler_params=_cparams("parallel", "arbitrary", vmem=BIG_VMEM_LIMIT),
        name="moba",
    )(zb, zb, zb, pat, near, near)


def _cmp_kernel(q_ref, kc_ref, vct_ref, ovt_ref, cb_ref, oc_ref, sel_ref, s_ref):
    i = pl.program_id(0)
    win0 = pl.multiple_of(jnp.maximum(i * CMP_STRIDE - CMP_STRIDE, 0), CMP_STRIDE)
    blk = lax.broadcasted_iota(jnp.int32, (N_SLC, TQ), 0)
    cur = i * SLC_PER_TILE + lax.shift_right_logical(lax.broadcasted_iota(jnp.int32, (N_SLC, TQ), 1), 6)
    allowed = blk <= cur
    forced = (blk == 0) | (blk == cur) | (blk == cur - 1)

    def body(nk):
        reachable = lax.broadcasted_iota(jnp.int32, (nk, TQ), 0) < win0 + CMP_WIN
        for g in range(NSA_KV_HEADS):
            p_sum = jnp.zeros((nk, TQ), F32)
            for r in range(NSA_GROUP):
                hd = g * NSA_GROUP + r
                q = q_ref[:, hd * HEAD_DIM:(hd + 1) * HEAD_DIM]
                s_ref[:nk, :] = lax.dot_general(kc_ref[g, :nk, :], q, _NT, preferred_element_type=F32)
                s_ref[pl.ds(win0, CMP_WIN), :] += cb_ref[0, hd]
                s = jnp.where(reachable, s_ref[:nk, :], NEG)
                valid = s > 0.5 * NEG
                e = jnp.where(valid, jnp.exp(s - jnp.max(s, axis=0, keepdims=True)), 0.0)
                p = e * (1.0 / jnp.maximum(jnp.sum(e, axis=0, keepdims=True), TINY))
                o_t = jnp.dot(vct_ref[g, :, :nk], p.astype(BF16), preferred_element_type=F32)
                oc_ref[:, hd * HEAD_DIM:(hd + 1) * HEAD_DIM] = o_t.T
                p_sum = p_sum + p
            imp = jnp.zeros((N_SLC, TQ), F32)
            for part in _split3(p_sum):
                imp = imp + jnp.dot(ovt_ref[:, :nk], part, preferred_element_type=F32)
            score = jnp.where(allowed, jnp.where(forced, FORCE_SCORE, imp), -1.0)
            selb = jnp.where(allowed, _topk_bias_t(score, SLC_TOPK), NEG)
            sel_ref[g] = selb.T.astype(sel_ref.dtype)

    for c in range(1, 512 // HEAD_DIM + 1):
        @pl.when(lax.shift_right_logical(i, 3) == c - 1)
        def _():
            body(c * HEAD_DIM)


def _cmp_select(zb, kc, vct, ovt, cmpb):
    w = NSA_HEADS * HEAD_DIM
    return pl.pallas_call(
        _cmp_kernel,
        grid=(N_QT,),
        in_specs=[pl.BlockSpec((TQ, w), lambda i: (i, COL_NQ // NSA_HEADS)),
                  pl.BlockSpec((NSA_KV_HEADS, 512, HEAD_DIM), lambda i: (0, 0, 0)),
                  pl.BlockSpec((NSA_KV_HEADS, HEAD_DIM, 512), lambda i: (0, 0, 0)),
                  pl.BlockSpec((N_SLC, 512), lambda i: (0, 0)),
                  pl.BlockSpec((1, NSA_HEADS, CMP_WIN, TQ), lambda i: (jnp.minimum(i, 1), 1, 0, 0))],
        out_specs=[pl.BlockSpec((TQ, w), lambda i: (i, 0)),
                   pl.BlockSpec((NSA_KV_HEADS, TQ, N_SLC), lambda i: (0, i, 0))],
        out_shape=[jax.ShapeDtypeStruct((SEQ, w), F32),
                   jax.ShapeDtypeStruct((NSA_KV_HEADS, SEQ, N_SLC), BF16)],
        scratch_shapes=[pltpu.VMEM((512, TQ), F32)],
        compiler_params=_cparams("arbitrary"),
        name="cmp_select",
    )(zb, kc, vct, ovt, cmpb)


def _slc_kernel(q_ref, sel_ref, k_ref, v_ref, pat_ref, d0_ref, d1_ref, o_ref, *flash_scratch):
    i = pl.program_id(1)
    hp = HP_SLC
    selb = sel_ref[0].astype(F32)
    qs = [q_ref[:, h * HEAD_DIM:(h + 1) * HEAD_DIM] for h in range(hp)]
    outs = _sparse_flash(qs, [selb] * hp, k_ref, v_ref, [0] * hp, pat_ref,
                         [d0_ref[0, h] for h in range(hp)], [d1_ref[0, h] for h in range(hp)],
                         i, SLC_PER_TILE, *flash_scratch)
    for h in range(hp):
        o_ref[:, h * HEAD_DIM:(h + 1) * HEAD_DIM] = outs[h].astype(o_ref.dtype)


def _slc(zb, selb, near, pat):
    hp = HP_SLC
    w = hp * HEAD_DIM
    per_group = NSA_GROUP // hp
    return pl.pallas_call(
        _slc_kernel,
        grid=(NSA_HEADS // hp, N_QT),
        in_specs=[pl.BlockSpec((TQ, w), lambda h, i: (i, COL_NQ // hp + h)),
                  pl.BlockSpec((1, TQ, N_SLC), lambda h, i: (h // per_group, i, 0)),
                  pl.BlockSpec((SEQ, HEAD_DIM), lambda h, i: (0, COL_SK + h // per_group),
                               pipeline_mode=pl.Buffered(1)),
                  pl.BlockSpec((SEQ, HEAD_DIM), lambda h, i: (0, COL_SV + h // per_group),
                               pipeline_mode=pl.Buffered(1)),
                  pl.BlockSpec((SEQ, HEAD_DIM), lambda h, i: (0, 0), pipeline_mode=pl.Buffered(1)),
                  pl.BlockSpec((1, hp, TQ, TQ), lambda h, i: (0, MOBA_HEADS // hp + h, 0, 0)),
                  pl.BlockSpec((1, hp, TQ, TQ), lambda h, i: (1, MOBA_HEADS // hp + h, 0, 0))],
        out_specs=pl.BlockSpec((TQ, w), lambda h, i: (i, h)),
        out_shape=jax.ShapeDtypeStruct((SEQ, NSA_HEADS * HEAD_DIM), F32),
        scratch_shapes=_flash_scratch(hp, F32),
        compiler_params=_cparams("parallel", "arbitrary", vmem=BIG_VMEM_LIMIT),
        name="slc",
    )(zb, selb, zb, zb, pat, near, near)


def _win_kernel(q_ref, k_ref, v_ref, d0_ref, d1_ref, w2_ref, o_ref):
    i = pl.program_id(1)
    offs = [pl.multiple_of(jnp.maximum(i - b, 0) * TQ, TQ) for b in range(3)]
    k_all = jnp.concatenate([k_ref[pl.ds(off, TQ), :] for off in offs], axis=0)
    v_all = jnp.concatenate([v_ref[pl.ds(off, TQ), :] for off in offs], axis=0)
    v_aug = jnp.concatenate([v_all, jnp.ones((3 * TQ, HEAD_DIM), BF16)], axis=1)
    mask2 = jnp.where(i >= 2, w2_ref[...], NEG)
    for r in range(NSA_GROUP):
        q = q_ref[:, r * HEAD_DIM:(r + 1) * HEAD_DIM]
        bias = jnp.concatenate([d0_ref[0, r], jnp.where(i >= 1, d1_ref[0, r], NEG), mask2], axis=1)
        s = lax.dot_general(q, k_all, _NT, preferred_element_type=F32) + bias
        m = jnp.max(s, axis=1, keepdims=True)
        pv = jnp.dot(jnp.exp(s - m).astype(BF16), v_aug, preferred_element_type=F32)
        o_ref[:, r * HEAD_DIM:(r + 1) * HEAD_DIM] = pv[:, :HEAD_DIM] / pv[:, HEAD_DIM:]


def _win(zb, near, win2):
    return pl.pallas_call(
        _win_kernel,
        grid=(NSA_KV_HEADS, N_QT),
        in_specs=[pl.BlockSpec((TQ, NSA_GROUP * HEAD_DIM), lambda g, i: (i, COL_NQ // NSA_GROUP + g)),
                  pl.BlockSpec((SEQ, HEAD_DIM), lambda g, i: (0, COL_WK + g)),
                  pl.BlockSpec((SEQ, HEAD_DIM), lambda g, i: (0, COL_WV + g)),
                  pl.BlockSpec((1, NSA_GROUP, TQ, TQ), lambda g, i: (0, 2 + g, 0, 0)),
                  pl.BlockSpec((1, NSA_GROUP, TQ, TQ), lambda g, i: (1, 2 + g, 0, 0)),
                  pl.BlockSpec((TQ, TQ), lambda g, i: (0, 0))],
        out_specs=pl.BlockSpec((TQ, NSA_GROUP * HEAD_DIM), lambda g, i: (i, g)),
        out_shape=jax.ShapeDtypeStruct((SEQ, NSA_HEADS * HEAD_DIM), F32),
        compiler_params=_cparams("parallel", "arbitrary"),
        name="win",
    )(zb, zb, zb, near, near, win2)


def _layer_norm(y, g, b):
    mu = jnp.mean(y, axis=-1, keepdims=True)
    yc = y - mu
    var = jnp.mean(yc * yc, axis=-1, keepdims=True)
    return yc * lax.rsqrt(var + LN_EPS) * g + b


ROW_CHUNK = 256


def _outproj_kernel(x_ref, om_ref, oc_ref, os_ref, ow_ref, g_ref, e_ref, w_ref, lg_ref, lb_ref,
                    h_ref, hb_ref):
    for c0 in range(0, x_ref.shape[0], ROW_CHUNK):
        rows = slice(c0, c0 + ROW_CHUNK)
        gate = g_ref[rows, :]
        hi = gate.astype(BF16)
        lo = (gate - hi.astype(F32)).astype(BF16)
        g2 = jnp.concatenate([hi, lo], axis=1)
        o_n = jnp.zeros((ROW_CHUNK, oc_ref.shape[1]), F32)
        for c, branch in enumerate((oc_ref, os_ref, ow_ref)):
            o_n = o_n + jnp.dot(g2, e_ref[c], preferred_element_type=F32) * branch[rows, :]
        o = jnp.concatenate([om_ref[rows, :], o_n.astype(BF16)], axis=1)
        a = jnp.dot(o, w_ref[...], preferred_element_type=F32)
        h = _layer_norm(DN_ALPHA * x_ref[rows, :] + a, lg_ref[...], lb_ref[...])
        h_ref[rows, :] = h
        hb_ref[rows, :] = h.astype(BF16)


def _resident(shape):
    return pl.BlockSpec(shape, lambda i: (0,) * len(shape), pipeline_mode=pl.Buffered(1))


def _outproj(x, o_m, o_c, o_s, o_w, gates, expand, w_out, ln_g, ln_b, tm=512):
    half = NSA_HEADS * HEAD_DIM
    row = lambda i: (i, 0)
    return pl.pallas_call(
        _outproj_kernel,
        grid=(SEQ // tm,),
        in_specs=[pl.BlockSpec((tm, D_MODEL), row),
                  pl.BlockSpec((tm, half), row), pl.BlockSpec((tm, half), row),
                  pl.BlockSpec((tm, half), row), pl.BlockSpec((tm, half), row),
                  pl.BlockSpec((tm, HEAD_DIM), row),
                  _resident((3, 2 * HEAD_DIM, half)),
                  _resident((D_MODEL, D_MODEL)),
                  _resident((1, D_MODEL)), _resident((1, D_MODEL))],
        out_specs=[pl.BlockSpec((tm, D_MODEL), row), pl.BlockSpec((tm, D_MODEL), row)],
        out_shape=[jax.ShapeDtypeStruct((SEQ, D_MODEL), F32),
                   jax.ShapeDtypeStruct((SEQ, D_MODEL), BF16)],
        compiler_params=_cparams("parallel", vmem=BIG_VMEM_LIMIT),
        name="outproj_ln",
    )(x, o_m, o_c, o_s, o_w, gates, expand, w_out, ln_g, ln_b)


def _ffn_kernel(hb_ref, w1_ref, w2_ref, f_ref):
    @pl.when(pl.program_id(1) == 0)
    def _():
        f_ref[...] = jnp.zeros(f_ref.shape, F32)

    u = jnp.maximum(jnp.dot(hb_ref[...], w1_ref[...].astype(BF16), preferred_element_type=F32), 0.0)
    f_ref[...] += jnp.dot((u * u).astype(BF16), w2_ref[...].astype(BF16), preferred_element_type=F32)


def _ffn(hb, w1, w2, tm=1024, tf=512):
    return pl.pallas_call(
        _ffn_kernel,
        grid=(SEQ // tm, D_FF // tf),
        in_specs=[pl.BlockSpec((tm, D_MODEL), lambda i, c: (i, 0)),
                  pl.BlockSpec((D_MODEL, tf), lambda i, c: (0, c)),
                  pl.BlockSpec((tf, D_MODEL), lambda i, c: (c, 0))],
        out_specs=pl.BlockSpec((tm, D_MODEL), lambda i, c: (i, 0)),
        out_shape=jax.ShapeDtypeStruct((SEQ, D_MODEL), F32),
        compiler_params=_cparams("parallel", "arbitrary", vmem=BIG_VMEM_LIMIT),
        name="ffn",
    )(hb, w1, w2)


def _final_kernel(h_ref, hb_ref, f_ref, p_ref, wg_ref, wp_ref, lg_ref, lb_ref, o_ref):
    for c0 in range(0, h_ref.shape[0], ROW_CHUNK):
        rows = slice(c0, c0 + ROW_CHUNK)
        gate = jax.nn.sigmoid(jnp.dot(hb_ref[rows, :], wg_ref[...], preferred_element_type=F32))
        emb = jnp.dot(p_ref[rows, :].astype(BF16), wp_ref[...], preferred_element_type=F32)
        y = DN_ALPHA * h_ref[rows, :] + f_ref[rows, :] + gate * emb
        o_ref[rows, :] = _layer_norm(y, lg_ref[...], lb_ref[...])


def _final(h, hb, f, p, wg, wp, ln_g, ln_b, tm=512):
    row = lambda i: (i, 0)
    return pl.pallas_call(
        _final_kernel,
        grid=(SEQ // tm,),
        in_specs=[pl.BlockSpec((tm, D_MODEL), row), pl.BlockSpec((tm, D_MODEL), row),
                  pl.BlockSpec((tm, D_MODEL), row), pl.BlockSpec((tm, D_PLE), row),
                  _resident((D_MODEL, D_MODEL)), _resident((D_PLE, D_MODEL)),
                  _resident((1, D_MODEL)), _resident((1, D_MODEL))],
        out_specs=pl.BlockSpec((tm, D_MODEL), row),
        out_shape=jax.ShapeDtypeStruct((SEQ, D_MODEL), F32),
        compiler_params=_cparams("parallel"),
        name="final_ln",
    )(h, hb, f, p, wg, wp, ln_g, ln_b)


def _overlap_matrix_t():
    j = np.arange(N_SLC)[:, None]
    n = np.arange(512)[None, :]
    ov = ((n * CMP_STRIDE < j * SLC_LEN + SLC_LEN) & (n * CMP_STRIDE + CMP_LEN > j * SLC_LEN) & (n < N_CMP))
    return ov.astype(np.float32)


def _gate_expand():
    e = np.zeros((3, 2 * HEAD_DIM, NSA_HEADS * HEAD_DIM), np.float32)
    for c in range(3):
        for h in range(NSA_HEADS):
            e[c, 3 * h + c, h * HEAD_DIM:(h + 1) * HEAD_DIM] = 1.0
            e[c, HEAD_DIM + 3 * h + c, h * HEAD_DIM:(h + 1) * HEAD_DIM] = 1.0
    return e


def _layer(h, p, w_in, cmp_pe_k, cmp_w1_k, cmp_w2_k, cmp_pe_v, cmp_w1_v, cmp_w2_v, tiles, w_out,
           ln1_g, ln1_b, w_ff1, w_ff2, w_ple, w_ple_gate, ln2_g, ln2_b):
    near, win2, cmpb = tiles
    scale = HEAD_DIM ** -0.5
    colscale = np.ones((1, GATE_COL0), np.float32)
    colscale[0, COL_MQ * HEAD_DIM:(COL_MQ + MOBA_HEADS) * HEAD_DIM] = scale
    colscale[0, COL_NQ * HEAD_DIM:(COL_NQ + NSA_HEADS) * HEAD_DIM] = scale
    w_gl = jnp.pad(w_in[:, GATE_COL0:], ((0, 0), (0, HEAD_DIM - NSA_HEADS * 3))).astype(BF16)
    zb, gates, ckv = _inproj(h, w_in, jnp.asarray(colscale), w_gl)

    half = CMP_STRIDE * HEAD_DIM
    xr = ckv.reshape(4, SEQ // CMP_STRIDE, half)
    kc, vct = _compress(xr, cmp_pe_k.reshape(2, 1, half), cmp_w1_k.reshape(2, half, HEAD_DIM), cmp_w2_k,
                        cmp_pe_v.reshape(2, 1, half), cmp_w1_v.reshape(2, half, HEAD_DIM), cmp_w2_v)

    key = lax.broadcasted_iota(jnp.int32, (SEQ, HEAD_DIM), 0)
    lane = lax.broadcasted_iota(jnp.int32, (SEQ, HEAD_DIM), 1)
    pat_moba = (lane == key // MOBA_BLOCK).astype(BF16)
    pat_slc = (lane == key // SLC_LEN).astype(BF16)
    o_m = _moba(zb, near, pat_moba)
    o_c, selb = _cmp_select(zb, kc, vct, jnp.asarray(_overlap_matrix_t(), BF16), cmpb)
    o_s = _slc(zb, selb, near, pat_slc)
    o_w = _win(zb, near, win2)

    h1, h1b = _outproj(h, o_m, o_c, o_s, o_w, gates, jnp.asarray(_gate_expand(), BF16),
                       w_out.astype(BF16), ln1_g.reshape(1, -1), ln1_b.reshape(1, -1))
    f = _ffn(h1b, w_ff1, w_ff2)
    return _final(h1, h1b, f, p, w_ple_gate.astype(BF16), w_ple.astype(BF16),
                  ln2_g.reshape(1, -1), ln2_b.reshape(1, -1))


def kernel(x, p, w_in, cmp_pe_k, cmp_w1_k, cmp_w2_k, cmp_pe_v, cmp_w1_v, cmp_w2_v, rel_bias, w_out,
           ln1_g, ln1_b, w_ff1, w_ff2, w_ple, w_ple_gate, ln2_g, ln2_b):
    tiles = _bias_tiles(rel_bias)
    h = x[0]
    for i in range(w_in.shape[0]):
        h = _layer(h, p[i, 0], w_in[i], cmp_pe_k[i], cmp_w1_k[i], cmp_w2_k[i], cmp_pe_v[i], cmp_w1_v[i],
                   cmp_w2_v[i], tiles, w_out[i], ln1_g[i], ln1_b[i], w_ff1[i], w_ff2[i], w_ple[i],
                   w_ple_gate[i], ln2_g[i], ln2_b[i])
    return h[None]
```

```python
import functools
import math

import numpy as np
import jax
import jax.numpy as jnp
from jax import lax
from jax.experimental import pallas as pl
from jax.experimental.pallas import tpu as pltpu

D_MODEL = 2048
SEQ = 8192
HEAD_DIM = 128
N_HEADS = 16
MOBA_HEADS = 8
NSA_HEADS = 8
NSA_KV_HEADS = 2
NSA_GROUP = 4
MOBA_BLOCK = 256
MOBA_TOPK = 3
CMP_LEN = 32
CMP_STRIDE = 16
SLC_LEN = 64
SLC_TOPK = 16
WINDOW = 512
N_BUCKETS = 32
MAX_DISTANCE = 128
D_FF = 4 * D_MODEL
D_PLE = 256
LN_EPS = 1e-5
DN_ALPHA = 2.0 ** 0.25
NEG = -1e30
TINY = 1e-30
FORCE_SCORE = 1e4

N_CMP = (SEQ - CMP_LEN) // CMP_STRIDE + 1
N_SLC = SEQ // SLC_LEN
TQ = 256
N_QT = SEQ // TQ
CMP_WIN = 32
SLC_PER_TILE = TQ // SLC_LEN
HP_MOBA = 4
HP_SLC = 4
KG = 4
GROUP = KG * TQ
COL_MQ, COL_MK, COL_MV, COL_NQ = 0, 8, 16, 24
COL_CK, COL_CV, COL_SK, COL_SV, COL_WK, COL_WV = 32, 34, 36, 38, 40, 42
N_COLBLK = 44
GATE_COL0 = N_COLBLK * HEAD_DIM

VMEM_LIMIT = 48 * 1024 * 1024
BIG_VMEM_LIMIT = 56 * 1024 * 1024

BF16 = jnp.bfloat16
F32 = jnp.float32
_NT = (((1,), (1,)), ((), ()))


def _cparams(*sem, vmem=VMEM_LIMIT):
    return pltpu.CompilerParams(dimension_semantics=sem, vmem_limit_bytes=vmem)


def _bucket_table():
    n = np.arange(1024, dtype=np.int32)
    max_exact = N_BUCKETS // 2
    ratio = np.maximum(n, 1).astype(np.float32) / np.float32(max_exact)
    large = max_exact + (np.log(ratio).astype(np.float32) / np.float32(math.log(MAX_DISTANCE / max_exact))
                         * np.float32(N_BUCKETS - max_exact)).astype(np.int32)
    large = np.minimum(large, N_BUCKETS - 1)
    return np.where(n < max_exact, n, large).astype(np.int32)


def _bias_kernel(tab_ref, near_ref, cmpb_ref, *, steps):
    h = pl.program_id(0)
    far = tab_ref[h, N_BUCKETS - 1]

    def lookup(d):
        val = jnp.full(d.shape, tab_ref[h, 0], F32)
        for start, bucket in steps:
            val = jnp.where(d >= start, tab_ref[h, bucket], val)
        return jnp.where(d >= 0, val - far, NEG)

    a = lax.broadcasted_iota(jnp.int32, (TQ, TQ), 0)
    b = lax.broadcasted_iota(jnp.int32, (TQ, TQ), 1)
    near_ref[0, 0] = lookup(a - b)
    near_ref[1, 0] = lookup(a - b + TQ)
    w = lax.broadcasted_iota(jnp.int32, (CMP_WIN, TQ), 0)
    aw = lax.broadcasted_iota(jnp.int32, (CMP_WIN, TQ), 1)
    cmpb_ref[0, 0] = lookup(aw - CMP_STRIDE * w - (CMP_LEN - 1))
    cmpb_ref[1, 0] = lookup(aw - CMP_STRIDE * w + (TQ - CMP_LEN + 1))


def _bias_tiles(rel_bias):
    bucket = _bucket_table()
    steps = tuple((int(n), int(bucket[n])) for n in range(1, bucket.size) if bucket[n] != bucket[n - 1])
    near, cmpb = pl.pallas_call(
        functools.partial(_bias_kernel, steps=steps),
        grid=(N_HEADS,),
        in_specs=[pl.BlockSpec(memory_space=pltpu.SMEM)],
        out_specs=[pl.BlockSpec((2, 1, TQ, TQ), lambda h: (0, h, 0, 0)),
                   pl.BlockSpec((2, 1, CMP_WIN, TQ), lambda h: (0, h, 0, 0))],
        out_shape=[jax.ShapeDtypeStruct((2, N_HEADS, TQ, TQ), F32),
                   jax.ShapeDtypeStruct((2, N_HEADS, CMP_WIN, TQ), F32)],
        compiler_params=_cparams("parallel"),
        name="bias_tiles",
    )(rel_bias.T.astype(F32))
    a = np.arange(TQ)[:, None]
    b = np.arange(TQ)[None, :]
    win2 = jnp.asarray(np.where(a < b, 0.0, NEG).astype(np.float32))
    return near, win2, cmpb


INPROJ_TN = 4 * HEAD_DIM
CKV_TILE = COL_CK * HEAD_DIM // INPROJ_TN


def _inproj_kernel(x_ref, w_ref, cs_ref, wg_ref, o_ref, g_ref, ckv_ref, xb_ref):
    j = pl.program_id(1)

    @pl.when(j == 0)
    def _():
        xb = x_ref[...].astype(BF16)
        xb_ref[...] = xb
        g_ref[...] = jax.nn.sigmoid(lax.dot_general(xb, wg_ref[...], _NT, preferred_element_type=F32))

    acc = lax.dot_general(xb_ref[...], w_ref[...].astype(BF16), _NT, preferred_element_type=F32)
    z = (acc * cs_ref[...]).astype(o_ref.dtype)
    o_ref[...] = z

    @pl.when(j == CKV_TILE)
    def _():
        for c in range(INPROJ_TN // HEAD_DIM):
            ckv_ref[c] = z[:, c * HEAD_DIM:(c + 1) * HEAD_DIM]


def _inproj(x, w_in_t, colscale, w_gate_t, tm=1024):
    m, k = x.shape
    tn = INPROJ_TN
    return pl.pallas_call(
        _inproj_kernel,
        grid=(m // tm, GATE_COL0 // tn),
        in_specs=[pl.BlockSpec((tm, k), lambda i, j: (i, 0)),
                  pl.BlockSpec((tn, k), lambda i, j: (j, 0)),
                  pl.BlockSpec((1, tn), lambda i, j: (0, j)),
                  pl.BlockSpec((HEAD_DIM, k), lambda i, j: (0, 0))],
        out_specs=[pl.BlockSpec((tm, tn), lambda i, j: (i, j)),
                   pl.BlockSpec((tm, HEAD_DIM), lambda i, j: (i, 0)),
                   pl.BlockSpec((tn // HEAD_DIM, tm, HEAD_DIM), lambda i, j: (0, i, 0))],
        out_shape=[jax.ShapeDtypeStruct((m, GATE_COL0), BF16),
                   jax.ShapeDtypeStruct((m, HEAD_DIM), F32),
                   jax.ShapeDtypeStruct((tn // HEAD_DIM, m, HEAD_DIM), BF16)],
        scratch_shapes=[pltpu.VMEM((tm, k), BF16)],
        compiler_params=_cparams("parallel", "arbitrary"),
        name="inproj",
    )(x, w_in_t, colscale, w_gate_t)


def _gelu_tanh(x):
    return 0.5 * x * (1.0 + jnp.tanh(math.sqrt(2.0 / math.pi) * (x + 0.044715 * (x * x * x))))


def _compress_kernel(xk_ref, xv_ref, pek_ref, w1k_ref, w2k_ref, pev_ref, w1v_ref, w2v_ref, kc_ref, vct_ref):
    def compress(x_ref, pe_ref, w1_ref, w2_ref):
        x = x_ref[0].astype(F32)
        lo = jnp.dot((x + pe_ref[0]).astype(BF16), w1_ref[0].astype(BF16), preferred_element_type=F32)
        hi = jnp.dot((x + pe_ref[1]).astype(BF16), w1_ref[1].astype(BF16), preferred_element_type=F32)
        y = lo + pltpu.roll(hi, N_CMP, 0)
        out = jnp.dot(_gelu_tanh(y).astype(BF16), w2_ref[...].astype(BF16), preferred_element_type=F32)
        row = lax.broadcasted_iota(jnp.int32, out.shape, 0)
        return jnp.where(row < N_CMP, out, 0.0)

    kc_ref[0] = compress(xk_ref, pek_ref, w1k_ref, w2k_ref).astype(kc_ref.dtype)
    vct_ref[0] = compress(xv_ref, pev_ref, w1v_ref, w2v_ref).T.astype(vct_ref.dtype)


def _compress(xr, pe_k, w1_k, w2_k, pe_v, w1_v, w2_v):
    half = CMP_STRIDE * HEAD_DIM
    weights = [pl.BlockSpec((2, 1, half), lambda g: (0, 0, 0)),
               pl.BlockSpec((2, half, HEAD_DIM), lambda g: (0, 0, 0)),
               pl.BlockSpec((HEAD_DIM, HEAD_DIM), lambda g: (0, 0))]
    return pl.pallas_call(
        _compress_kernel,
        grid=(NSA_KV_HEADS,),
        in_specs=[pl.BlockSpec((1, 512, half), lambda g: (g, 0, 0)),
                  pl.BlockSpec((1, 512, half), lambda g: (NSA_KV_HEADS + g, 0, 0))] + weights + weights,
        out_specs=[pl.BlockSpec((1, 512, HEAD_DIM), lambda g: (g, 0, 0)),
                   pl.BlockSpec((1, HEAD_DIM, 512), lambda g: (g, 0, 0))],
        out_shape=[jax.ShapeDtypeStruct((NSA_KV_HEADS, 512, HEAD_DIM), BF16),
                   jax.ShapeDtypeStruct((NSA_KV_HEADS, HEAD_DIM, 512), BF16)],
        compiler_params=_cparams("parallel"),
        name="compress",
    )(xr, xr, pe_k, w1_k, w2_k, pe_v, w1_v, w2_v)


def _split3(x):
    hi = x.astype(BF16)
    r1 = x - hi.astype(F32)
    mid = r1.astype(BF16)
    lo = (r1 - mid.astype(F32)).astype(BF16)
    return hi, mid, lo


def _topk_bias_t(score, k):
    row = lax.broadcasted_iota(jnp.int32, score.shape, 0).astype(F32)
    selb = jnp.full(score.shape, NEG, F32)
    s = score
    for _ in range(k):
        mx = jnp.max(s, axis=0, keepdims=True)
        idx = jnp.min(jnp.where(s == mx, row, 1e9), axis=0, keepdims=True)
        hit = row == idx
        selb = jnp.where(hit, 0.0, selb)
        s = jnp.where(hit, -3e38, s)
    return selb


def _lane_tile_max(s):
    m = s[:, :HEAD_DIM]
    for c in range(1, s.shape[1] // HEAD_DIM):
        m = jnp.maximum(m, s[:, c * HEAD_DIM:(c + 1) * HEAD_DIM])
    return m


def _sparse_flash(qs, selbs, k_ref, v_ref, kcols, pat_ref, d0s, d1s, i, sub, t_ref, r_ref, mx_ref, acc_ref):
    heads = range(len(qs))
    lane = lax.broadcasted_iota(jnp.int32, (TQ, HEAD_DIM), 1)
    far = lane < (i - 1) * sub
    q_near = [jnp.concatenate([qs[h], selbs[h].astype(BF16)], axis=1) for h in heads]
    q_far = [jnp.concatenate([qs[h], jnp.where(far, selbs[h], NEG).astype(BF16)], axis=1) for h in heads]

    def k_aug(h, off, n):
        c = kcols[h]
        return jnp.concatenate([k_ref[pl.ds(off, n), c:c + HEAD_DIM], pat_ref[pl.ds(off, n), :]], axis=1)

    def v_aug(h, off, n):
        c = kcols[h]
        return jnp.concatenate([v_ref[pl.ds(off, n), c:c + HEAD_DIM], jnp.ones((n, HEAD_DIM), BF16)], axis=1)

    for h in heads:
        mx_ref[h] = jnp.full((TQ, HEAD_DIM), NEG, F32)

    def far_group(g, carry):
        off = pl.multiple_of(g * GROUP, GROUP)
        for h in heads:
            s = lax.dot_general(q_far[h], k_aug(h, off, GROUP), _NT, preferred_element_type=F32)
            ref = _lane_tile_max(s)
            t = (s - jnp.concatenate([ref] * (GROUP // HEAD_DIM), axis=1)).astype(t_ref.dtype)
            for u in range(KG):
                t_ref[h, g * KG + u] = t[:, u * TQ:(u + 1) * TQ]
            r_ref[h, g] = ref
            mx_ref[h] = jnp.maximum(mx_ref[h], ref)
        return carry

    n_far = lax.shift_right_logical(jnp.maximum(i - 1, 0) + (KG - 1), 2)
    lax.fori_loop(0, n_far, far_group, 0)

    off_prev = pl.multiple_of(jnp.maximum(i - 1, 0) * TQ, TQ)
    off_own = pl.multiple_of(i * TQ, TQ)
    for h in heads:
        k2 = jnp.concatenate([k_aug(h, off_prev, TQ), k_aug(h, off_own, TQ)], axis=0)
        v2 = jnp.concatenate([v_aug(h, off_prev, TQ), v_aug(h, off_own, TQ)], axis=0)
        bias = jnp.concatenate([jnp.where(i >= 1, d1s[h], NEG), d0s[h]], axis=1)
        s = lax.dot_general(q_near[h], k2, _NT, preferred_element_type=F32) + bias
        m = jnp.max(jnp.maximum(mx_ref[h], _lane_tile_max(s)), axis=1, keepdims=True)
        mx_ref[h] = jnp.broadcast_to(m, (TQ, HEAD_DIM))
        acc_ref[h] = jnp.dot(jnp.exp(s - m).astype(BF16), v2, preferred_element_type=F32)

    def pv_group(g, carry):
        off = pl.multiple_of(g * GROUP, GROUP)
        for h in heads:
            d = r_ref[h, g] - mx_ref[h]
            d2 = jnp.concatenate([d, d], axis=1)
            p = jnp.concatenate([jnp.exp(t_ref[h, g * KG + u].astype(F32) + d2).astype(BF16)
                                 for u in range(KG)], axis=1)
            acc_ref[h] += jnp.dot(p, v_aug(h, off, GROUP), preferred_element_type=F32)
        return carry

    lax.fori_loop(0, n_far, pv_group, 0)
    outs = []
    for h in heads:
        acc = acc_ref[h]
        outs.append(acc[:, :HEAD_DIM] / acc[:, HEAD_DIM:])
    return outs


def _flash_scratch(hp, logit_dtype):
    return [pltpu.VMEM((hp, N_QT, TQ, TQ), logit_dtype),
            pltpu.VMEM((hp, SEQ // GROUP, TQ, HEAD_DIM), F32),
            pltpu.VMEM((hp, TQ, HEAD_DIM), F32),
            pltpu.VMEM((hp, TQ, 2 * HEAD_DIM), F32)]


def _moba_kernel(q_ref, k_ref, v_ref, pat_ref, d0_ref, d1_ref, o_ref, kmean_ref, *flash_scratch):
    i = pl.program_id(1)
    hp = HP_MOBA

    @pl.when(i == 0)
    def _():
        blk = lax.broadcasted_iota(jnp.int32, (HEAD_DIM, SEQ), 0)
        pos = lax.broadcasted_iota(jnp.int32, (HEAD_DIM, SEQ), 1)
        ind = jnp.where(lax.shift_right_logical(pos, 8) == blk, 1.0 / MOBA_BLOCK, 0.0).astype(BF16)
        for h in range(hp):
            kmean_ref[h] = jnp.dot(ind, k_ref[:, h * HEAD_DIM:(h + 1) * HEAD_DIM], preferred_element_type=F32)

    blk_t = lax.broadcasted_iota(jnp.int32, (HEAD_DIM, TQ), 0)
    past = blk_t < i
    qs, selbs = [], []
    for h in range(hp):
        q = q_ref[:, h * HEAD_DIM:(h + 1) * HEAD_DIM]
        gate = jnp.zeros((HEAD_DIM, TQ), F32)
        for part in _split3(kmean_ref[h]):
            gate = gate + lax.dot_general(part, q, _NT, preferred_element_type=F32)
        selb = _topk_bias_t(jnp.where(past, gate, NEG), MOBA_TOPK)
        selb = jnp.where(past, selb, jnp.where(blk_t == i, 0.0, NEG))
        qs.append(q)
        selbs.append(selb.T)
    cols = [h * HEAD_DIM for h in range(hp)]
    outs = _sparse_flash(qs, selbs, k_ref, v_ref, cols, pat_ref, [d0_ref[0, h] for h in range(hp)],
                         [d1_ref[0, h] for h in range(hp)], i, 1, *flash_scratch)
    for h in range(hp):
        o_ref[:, h * HEAD_DIM:(h + 1) * HEAD_DIM] = outs[h].astype(o_ref.dtype)


def _moba(zb, near, pat):
    hp = HP_MOBA
    w = hp * HEAD_DIM
    return pl.pallas_call(
        _moba_kernel,
        grid=(MOBA_HEADS // hp, N_QT),
        in_specs=[pl.BlockSpec((TQ, w), lambda h, i: (i, COL_MQ // hp + h)),
                  pl.BlockSpec((SEQ, w), lambda h, i: (0, COL_MK // hp + h), pipeline_mode=pl.Buffered(1)),
                  pl.BlockSpec((SEQ, w), lambda h, i: (0, COL_MV // hp + h), pipeline_mode=pl.Buffered(1)),
                  pl.BlockSpec((SEQ, HEAD_DIM), lambda h, i: (0, 0), pipeline_mode=pl.Buffered(1)),
                  pl.BlockSpec((1, hp, TQ, TQ), lambda h, i: (0, h, 0, 0)),
                  pl.BlockSpec((1, hp, TQ, TQ), lambda h, i: (1, h, 0, 0))],
        out_specs=pl.BlockSpec((TQ, w), lambda h, i: (i, h)),
        out_shape=jax.ShapeDtypeStruct((SEQ, MOBA_HEADS * HEAD_DIM), BF16),
        scratch_shapes=[pltpu.VMEM((hp, HEAD_DIM, HEAD_DIM), F32)] + _flash_scratch(hp, BF16),
        compiler_params=_cparams("parallel", "arbitrary", vmem=BIG_VMEM_LIMIT),
        name="moba",
    )(zb, zb, zb, pat, near, near)


def _cmp_kernel(q_ref, kc_ref, vct_ref, ovt_ref, cb_ref, oc_ref, sel_ref, s_ref):
    i = pl.program_id(0)
    win0 = pl.multiple_of(jnp.maximum(i * CMP_STRIDE - CMP_STRIDE, 0), CMP_STRIDE)
    blk = lax.broadcasted_iota(jnp.int32, (N_SLC, TQ), 0)
    cur = i * SLC_PER_TILE + lax.shift_right_logical(lax.broadcasted_iota(jnp.int32, (N_SLC, TQ), 1), 6)
    allowed = blk <= cur
    forced = (blk == 0) | (blk == cur) | (blk == cur - 1)

    def body(nk):
        reachable = lax.broadcasted_iota(jnp.int32, (nk, TQ), 0) < win0 + CMP_WIN
        for g in range(NSA_KV_HEADS):
            p_sum = jnp.zeros((nk, TQ), F32)
            for r in range(NSA_GROUP):
                hd = g * NSA_GROUP + r
                q = q_ref[:, hd * HEAD_DIM:(hd + 1) * HEAD_DIM]
                s_ref[:nk, :] = lax.dot_general(kc_ref[g, :nk, :], q, _NT, preferred_element_type=F32)
                s_ref[pl.ds(win0, CMP_WIN), :] += cb_ref[0, hd]
                s = jnp.where(reachable, s_ref[:nk, :], NEG)
                valid = s > 0.5 * NEG
                e = jnp.where(valid, jnp.exp(s - jnp.max(s, axis=0, keepdims=True)), 0.0)
                p = e * (1.0 / jnp.maximum(jnp.sum(e, axis=0, keepdims=True), TINY))
                o_t = jnp.dot(vct_ref[g, :, :nk], p.astype(BF16), preferred_element_type=F32)
                oc_ref[:, hd * HEAD_DIM:(hd + 1) * HEAD_DIM] = o_t.T
                p_sum = p_sum + p
            imp = jnp.zeros((N_SLC, TQ), F32)
            for part in _split3(p_sum):
                imp = imp + jnp.dot(ovt_ref[:, :nk], part, preferred_element_type=F32)
            score = jnp.where(allowed, jnp.where(forced, FORCE_SCORE, imp), -1.0)
            selb = jnp.where(allowed, _topk_bias_t(score, SLC_TOPK), NEG)
            sel_ref[g] = selb.T.astype(sel_ref.dtype)

    for c in range(1, 512 // HEAD_DIM + 1):
        @pl.when(lax.shift_right_logical(i, 3) == c - 1)
        def _():
            body(c * HEAD_DIM)


def _cmp_select(zb, kc, vct, ovt, cmpb):
    w = NSA_HEADS * HEAD_DIM
    return pl.pallas_call(
        _cmp_kernel,
        grid=(N_QT,),
        in_specs=[pl.BlockSpec((TQ, w), lambda i: (i, COL_NQ // NSA_HEADS)),
                  pl.BlockSpec((NSA_KV_HEADS, 512, HEAD_DIM), lambda i: (0, 0, 0)),
                  pl.BlockSpec((NSA_KV_HEADS, HEAD_DIM, 512), lambda i: (0, 0, 0)),
                  pl.BlockSpec((N_SLC, 512), lambda i: (0, 0)),
                  pl.BlockSpec((1, NSA_HEADS, CMP_WIN, TQ), lambda i: (jnp.minimum(i, 1), 1, 0, 0))],
        out_specs=[pl.BlockSpec((TQ, w), lambda i: (i, 0)),
                   pl.BlockSpec((NSA_KV_HEADS, TQ, N_SLC), lambda i: (0, i, 0))],
        out_shape=[jax.ShapeDtypeStruct((SEQ, w), F32),
                   jax.ShapeDtypeStruct((NSA_KV_HEADS, SEQ, N_SLC), BF16)],
        scratch_shapes=[pltpu.VMEM((512, TQ), F32)],
        compiler_params=_cparams("arbitrary"),
        name="cmp_select",
    )(zb, kc, vct, ovt, cmpb)


def _slc_kernel(q_ref, sel_ref, k_ref, v_ref, pat_ref, d0_ref, d1_ref, o_ref, *flash_scratch):
    i = pl.program_id(1)
    hp = HP_SLC
    selb = sel_ref[0].astype(F32)
    qs = [q_ref[:, h * HEAD_DIM:(h + 1) * HEAD_DIM] for h in range(hp)]
    outs = _sparse_flash(qs, [selb] * hp, k_ref, v_ref, [0] * hp, pat_ref,
                         [d0_ref[0, h] for h in range(hp)], [d1_ref[0, h] for h in range(hp)],
                         i, SLC_PER_TILE, *flash_scratch)
    for h in range(hp):
        o_ref[:, h * HEAD_DIM:(h + 1) * HEAD_DIM] = outs[h].astype(o_ref.dtype)


def _slc(zb, selb, near, pat):
    hp = HP_SLC
    w = hp * HEAD_DIM
    per_group = NSA_GROUP // hp
    return pl.pallas_call(
        _slc_kernel,
        grid=(NSA_HEADS // hp, N_QT),
        in_specs=[pl.BlockSpec((TQ, w), lambda h, i: (i, COL_NQ // hp + h)),
                  pl.BlockSpec((1, TQ, N_SLC), lambda h, i: (h // per_group, i, 0)),
                  pl.BlockSpec((SEQ, HEAD_DIM), lambda h, i: (0, COL_SK + h // per_group),
                               pipeline_mode=pl.Buffered(1)),
                  pl.BlockSpec((SEQ, HEAD_DIM), lambda h, i: (0, COL_SV + h // per_group),
                               pipeline_mode=pl.Buffered(1)),
                  pl.BlockSpec((SEQ, HEAD_DIM), lambda h, i: (0, 0), pipeline_mode=pl.Buffered(1)),
                  pl.BlockSpec((1, hp, TQ, TQ), lambda h, i: (0, MOBA_HEADS // hp + h, 0, 0)),
                  pl.BlockSpec((1, hp, TQ, TQ), lambda h, i: (1, MOBA_HEADS // hp + h, 0, 0))],
        out_specs=pl.BlockSpec((TQ, w), lambda h, i: (i, h)),
        out_shape=jax.ShapeDtypeStruct((SEQ, NSA_HEADS * HEAD_DIM), F32),
        scratch_shapes=_flash_scratch(hp, F32),
        compiler_params=_cparams("parallel", "arbitrary", vmem=BIG_VMEM_LIMIT),
        name="slc",
    )(zb, selb, zb, zb, pat, near, near)


def _win_kernel(q_ref, k_ref, v_ref, d0_ref, d1_ref, w2_ref, o_ref):
    i = pl.program_id(1)
    offs = [pl.multiple_of(jnp.maximum(i - b, 0) * TQ, TQ) for b in range(3)]
    k_all = jnp.concatenate([k_ref[pl.ds(off, TQ), :] for off in offs], axis=0)
    v_all = jnp.concatenate([v_ref[pl.ds(off, TQ), :] for off in offs], axis=0)
    v_aug = jnp.concatenate([v_all, jnp.ones((3 * TQ, HEAD_DIM), BF16)], axis=1)
    mask2 = jnp.where(i >= 2, w2_ref[...], NEG)
    for r in range(NSA_GROUP):
        q = q_ref[:, r * HEAD_DIM:(r + 1) * HEAD_DIM]
        bias = jnp.concatenate([d0_ref[0, r], jnp.where(i >= 1, d1_ref[0, r], NEG), mask2], axis=1)
        s = lax.dot_general(q, k_all, _NT, preferred_element_type=F32) + bias
        m = jnp.max(s, axis=1, keepdims=True)
        pv = jnp.dot(jnp.exp(s - m).astype(BF16), v_aug, preferred_element_type=F32)
        o_ref[:, r * HEAD_DIM:(r + 1) * HEAD_DIM] = pv[:, :HEAD_DIM] / pv[:, HEAD_DIM:]


def _win(zb, near, win2):
    return pl.pallas_call(
        _win_kernel,
        grid=(NSA_KV_HEADS, N_QT),
        in_specs=[pl.BlockSpec((TQ, NSA_GROUP * HEAD_DIM), lambda g, i: (i, COL_NQ // NSA_GROUP + g)),
                  pl.BlockSpec((SEQ, HEAD_DIM), lambda g, i: (0, COL_WK + g)),
                  pl.BlockSpec((SEQ, HEAD_DIM), lambda g, i: (0, COL_WV + g)),
                  pl.BlockSpec((1, NSA_GROUP, TQ, TQ), lambda g, i: (0, 2 + g, 0, 0)),
                  pl.BlockSpec((1, NSA_GROUP, TQ, TQ), lambda g, i: (1, 2 + g, 0, 0)),
                  pl.BlockSpec((TQ, TQ), lambda g, i: (0, 0))],
        out_specs=pl.BlockSpec((TQ, NSA_GROUP * HEAD_DIM), lambda g, i: (i, g)),
        out_shape=jax.ShapeDtypeStruct((SEQ, NSA_HEADS * HEAD_DIM), F32),
        compiler_params=_cparams("parallel", "arbitrary"),
        name="win",
    )(zb, zb, zb, near, near, win2)


def _layer_norm(y, g, b):
    mu = jnp.mean(y, axis=-1, keepdims=True)
    yc = y - mu
    var = jnp.mean(yc * yc, axis=-1, keepdims=True)
    return yc * lax.rsqrt(var + LN_EPS) * g + b


ROW_CHUNK = 256


def _outproj_kernel(x_ref, om_ref, oc_ref, os_ref, ow_ref, g_ref, e_ref, w_ref, lg_ref, lb_ref,
                    h_ref, hb_ref):
    for c0 in range(0, x_ref.shape[0], ROW_CHUNK):
        rows = slice(c0, c0 + ROW_CHUNK)
        gate = g_ref[rows, :]
        hi = gate.astype(BF16)
        lo = (gate - hi.astype(F32)).astype(BF16)
        g2 = jnp.concatenate([hi, lo], axis=1)
        o_n = jnp.zeros((ROW_CHUNK, oc_ref.shape[1]), F32)
        for c, branch in enumerate((oc_ref, os_ref, ow_ref)):
            o_n = o_n + jnp.dot(g2, e_ref[c], preferred_element_type=F32) * branch[rows, :]
        o = jnp.concatenate([om_ref[rows, :], o_n.astype(BF16)], axis=1)
        a = jnp.dot(o, w_ref[...], preferred_element_type=F32)
        h = _layer_norm(DN_ALPHA * x_ref[rows, :] + a, lg_ref[...], lb_ref[...])
        h_ref[rows, :] = h
        hb_ref[rows, :] = h.astype(BF16)


def _resident(shape):
    return pl.BlockSpec(shape, lambda i: (0,) * len(shape), pipeline_mode=pl.Buffered(1))


def _outproj(x, o_m, o_c, o_s, o_w, gates, expand, w_out, ln_g, ln_b, tm=512):
    half = NSA_HEADS * HEAD_DIM
    row = lambda i: (i, 0)
    return pl.pallas_call(
        _outproj_kernel,
        grid=(SEQ // tm,),
        in_specs=[pl.BlockSpec((tm, D_MODEL), row),
                  pl.BlockSpec((tm, half), row), pl.BlockSpec((tm, half), row),
                  pl.BlockSpec((tm, half), row), pl.BlockSpec((tm, half), row),
                  pl.BlockSpec((tm, HEAD_DIM), row),
                  _resident((3, 2 * HEAD_DIM, half)),
                  _resident((D_MODEL, D_MODEL)),
                  _resident((1, D_MODEL)), _resident((1, D_MODEL))],
        out_specs=[pl.BlockSpec((tm, D_MODEL), row), pl.BlockSpec((tm, D_MODEL), row)],
        out_shape=[jax.ShapeDtypeStruct((SEQ, D_MODEL), F32),
                   jax.ShapeDtypeStruct((SEQ, D_MODEL), BF16)],
        compiler_params=_cparams("parallel", vmem=BIG_VMEM_LIMIT),
        name="outproj_ln",
    )(x, o_m, o_c, o_s, o_w, gates, expand, w_out, ln_g, ln_b)


def _ffn_kernel(hb_ref, w1_ref, w2_ref, f_ref):
    @pl.when(pl.program_id(1) == 0)
    def _():
        f_ref[...] = jnp.zeros(f_ref.shape, F32)

    u = jnp.maximum(jnp.dot(hb_ref[...], w1_ref[...].astype(BF16), preferred_element_type=F32), 0.0)
    f_ref[...] += jnp.dot((u * u).astype(BF16), w2_ref[...].astype(BF16), preferred_element_type=F32)


def _ffn(hb, w1, w2, tm=1024, tf=512):
    return pl.pallas_call(
        _ffn_kernel,
        grid=(SEQ // tm, D_FF // tf),
        in_specs=[pl.BlockSpec((tm, D_MODEL), lambda i, c: (i, 0)),
                  pl.BlockSpec((D_MODEL, tf), lambda i, c: (0, c)),
                  pl.BlockSpec((tf, D_MODEL), lambda i, c: (c, 0))],
        out_specs=pl.BlockSpec((tm, D_MODEL), lambda i, c: (i, 0)),
        out_shape=jax.ShapeDtypeStruct((SEQ, D_MODEL), F32),
        compiler_params=_cparams("parallel", "arbitrary", vmem=BIG_VMEM_LIMIT),
        name="ffn",
    )(hb, w1, w2)


def _final_kernel(h_ref, hb_ref, f_ref, p_ref, wg_ref, wp_ref, lg_ref, lb_ref, o_ref):
    for c0 in range(0, h_ref.shape[0], ROW_CHUNK):
        rows = slice(c0, c0 + ROW_CHUNK)
        gate = jax.nn.sigmoid(jnp.dot(hb_ref[rows, :], wg_ref[...], preferred_element_type=F32))
        emb = jnp.dot(p_ref[rows, :].astype(BF16), wp_ref[...], preferred_element_type=F32)
        y = DN_ALPHA * h_ref[rows, :] + f_ref[rows, :] + gate * emb
        o_ref[rows, :] = _layer_norm(y, lg_ref[...], lb_ref[...])


def _final(h, hb, f, p, wg, wp, ln_g, ln_b, tm=512):
    row = lambda i: (i, 0)
    return pl.pallas_call(
        _final_kernel,
        grid=(SEQ // tm,),
        in_specs=[pl.BlockSpec((tm, D_MODEL), row), pl.BlockSpec((tm, D_MODEL), row),
                  pl.BlockSpec((tm, D_MODEL), row), pl.BlockSpec((tm, D_PLE), row),
                  _resident((D_MODEL, D_MODEL)), _resident((D_PLE, D_MODEL)),
                  _resident((1, D_MODEL)), _resident((1, D_MODEL))],
        out_specs=pl.BlockSpec((tm, D_MODEL), row),
        out_shape=jax.ShapeDtypeStruct((SEQ, D_MODEL), F32),
        compiler_params=_cparams("parallel"),
        name="final_ln",
    )(h, hb, f, p, wg, wp, ln_g, ln_b)


def _overlap_matrix_t():
    j = np.arange(N_SLC)[:, None]
    n = np.arange(512)[None, :]
    ov = ((n * CMP_STRIDE < j * SLC_LEN + SLC_LEN) & (n * CMP_STRIDE + CMP_LEN > j * SLC_LEN) & (n < N_CMP))
    return ov.astype(np.float32)


def _gate_expand():
    e = np.zeros((3, 2 * HEAD_DIM, NSA_HEADS * HEAD_DIM), np.float32)
    for c in range(3):
        for h in range(NSA_HEADS):
            e[c, 3 * h + c, h * HEAD_DIM:(h + 1) * HEAD_DIM] = 1.0
            e[c, HEAD_DIM + 3 * h + c, h * HEAD_DIM:(h + 1) * HEAD_DIM] = 1.0
    return e


def _layer(h, p, w_in, cmp_pe_k, cmp_w1_k, cmp_w2_k, cmp_pe_v, cmp_w1_v, cmp_w2_v, tiles, w_out,
           ln1_g, ln1_b, w_ff1, w_ff2, w_ple, w_ple_gate, ln2_g, ln2_b):
    near, win2, cmpb = tiles
    scale = HEAD_DIM ** -0.5
    colscale = np.ones((1, GATE_COL0), np.float32)
    colscale[0, COL_MQ * HEAD_DIM:(COL_MQ + MOBA_HEADS) * HEAD_DIM] = scale
    colscale[0, COL_NQ * HEAD_DIM:(COL_NQ + NSA_HEADS) * HEAD_DIM] = scale
    w_in_t = w_in.T
    w_gl_t = jnp.pad(w_in_t[GATE_COL0:], ((0, HEAD_DIM - NSA_HEADS * 3), (0, 0))).astype(BF16)
    zb, gates, ckv = _inproj(h, w_in_t, jnp.asarray(colscale), w_gl_t)

    half = CMP_STRIDE * HEAD_DIM
    xr = ckv.reshape(4, SEQ // CMP_STRIDE, half)
    kc, vct = _compress(xr, cmp_pe_k.reshape(2, 1, half), cmp_w1_k.reshape(2, half, HEAD_DIM), cmp_w2_k,
                        cmp_pe_v.reshape(2, 1, half), cmp_w1_v.reshape(2, half, HEAD_DIM), cmp_w2_v)

    key = lax.broadcasted_iota(jnp.int32, (SEQ, HEAD_DIM), 0)
    lane = lax.broadcasted_iota(jnp.int32, (SEQ, HEAD_DIM), 1)
    pat_moba = (lane == key // MOBA_BLOCK).astype(BF16)
    pat_slc = (lane == key // SLC_LEN).astype(BF16)
    o_m = _moba(zb, near, pat_moba)
    o_c, selb = _cmp_select(zb, kc, vct, jnp.asarray(_overlap_matrix_t(), BF16), cmpb)
    o_s = _slc(zb, selb, near, pat_slc)
    o_w = _win(zb, near, win2)

    h1, h1b = _outproj(h, o_m, o_c, o_s, o_w, gates, jnp.asarray(_gate_expand(), BF16),
                       w_out.astype(BF16), ln1_g.reshape(1, -1), ln1_b.reshape(1, -1))
    f = _ffn(h1b, w_ff1, w_ff2)
    return _final(h1, h1b, f, p, w_ple_gate.astype(BF16), w_ple.astype(BF16),
                  ln2_g.reshape(1, -1), ln2_b.reshape(1, -1))


def kernel(x, p, w_in, cmp_pe_k, cmp_w1_k, cmp_w2_k, cmp_pe_v, cmp_w1_v, cmp_w2_v, rel_bias, w_out,
           ln1_g, ln1_b, w_ff1, w_ff2, w_ple, w_ple_gate, ln2_g, ln2_b):
    tiles = _bias_tiles(rel_bias)
    h = x[0]
    for i in range(w_in.shape[0]):
        h = _layer(h, p[i, 0], w_in[i], cmp_pe_k[i], cmp_w1_k[i], cmp_w2_k[i], cmp_pe_v[i], cmp_w1_v[i],
                   cmp_w2_v[i], tiles, w_out[i], ln1_g[i], ln1_b[i], w_ff1[i], w_ff2[i], w_ple[i],
                   w_ple_gate[i], ln2_g[i], ln2_b[i])
    return h[None]
```

```python
import functools
import math

import numpy as np
import jax
import jax.numpy as jnp
from jax import lax
from jax.experimental import pallas as pl
from jax.experimental.pallas import tpu as pltpu

D_MODEL = 2048
SEQ = 8192
HEAD_DIM = 128
N_HEADS = 16
MOBA_HEADS = 8
NSA_HEADS = 8
NSA_KV_HEADS = 2
NSA_GROUP = 4
MOBA_BLOCK = 256
MOBA_TOPK = 3
CMP_LEN = 32
CMP_STRIDE = 16
SLC_LEN = 64
SLC_TOPK = 16
WINDOW = 512
N_BUCKETS = 32
MAX_DISTANCE = 128
D_FF = 4 * D_MODEL
D_PLE = 256
LN_EPS = 1e-5
DN_ALPHA = 2.0 ** 0.25
NEG = -1e30
TINY = 1e-30
FORCE_SCORE = 1e4

N_CMP = (SEQ - CMP_LEN) // CMP_STRIDE + 1
N_SLC = SEQ // SLC_LEN
TQ = 256
N_QT = SEQ // TQ
CMP_WIN = 32
SLC_PER_TILE = TQ // SLC_LEN
HP_MOBA = 4
HP_SLC = 4
KG = 4
GROUP = KG * TQ
COL_MQ, COL_MK, COL_MV, COL_NQ = 0, 8, 16, 24
COL_CK, COL_CV, COL_SK, COL_SV, COL_WK, COL_WV = 32, 34, 36, 38, 40, 42
N_COLBLK = 44
GATE_COL0 = N_COLBLK * HEAD_DIM

VMEM_LIMIT = 48 * 1024 * 1024
BIG_VMEM_LIMIT = 56 * 1024 * 1024

BF16 = jnp.bfloat16
F32 = jnp.float32
_NT = (((1,), (1,)), ((), ()))


def _cparams(*sem, vmem=VMEM_LIMIT):
    return pltpu.CompilerParams(dimension_semantics=sem, vmem_limit_bytes=vmem)


def _bucket_table():
    n = np.arange(1024, dtype=np.int32)
    max_exact = N_BUCKETS // 2
    ratio = np.maximum(n, 1).astype(np.float32) / np.float32(max_exact)
    large = max_exact + (np.log(ratio).astype(np.float32) / np.float32(math.log(MAX_DISTANCE / max_exact))
                         * np.float32(N_BUCKETS - max_exact)).astype(np.int32)
    large = np.minimum(large, N_BUCKETS - 1)
    return np.where(n < max_exact, n, large).astype(np.int32)


def _bias_kernel(tab_ref, near_ref, cmpb_ref, *, steps):
    h = pl.program_id(0)
    far = tab_ref[h, N_BUCKETS - 1]

    def lookup(d):
        val = jnp.full(d.shape, tab_ref[h, 0], F32)
        for start, bucket in steps:
            val = jnp.where(d >= start, tab_ref[h, bucket], val)
        return jnp.where(d >= 0, val - far, NEG)

    a = lax.broadcasted_iota(jnp.int32, (TQ, TQ), 0)
    b = lax.broadcasted_iota(jnp.int32, (TQ, TQ), 1)
    near_ref[0, 0] = lookup(a - b)
    near_ref[1, 0] = lookup(a - b + TQ)
    w = lax.broadcasted_iota(jnp.int32, (CMP_WIN, TQ), 0)
    aw = lax.broadcasted_iota(jnp.int32, (CMP_WIN, TQ), 1)
    cmpb_ref[0, 0] = lookup(aw - CMP_STRIDE * w - (CMP_LEN - 1))
    cmpb_ref[1, 0] = lookup(aw - CMP_STRIDE * w + (TQ - CMP_LEN + 1))


def _bias_tiles(rel_bias):
    bucket = _bucket_table()
    steps = tuple((int(n), int(bucket[n])) for n in range(1, bucket.size) if bucket[n] != bucket[n - 1])
    near, cmpb = pl.pallas_call(
        functools.partial(_bias_kernel, steps=steps),
        grid=(N_HEADS,),
        in_specs=[pl.BlockSpec(memory_space=pltpu.SMEM)],
        out_specs=[pl.BlockSpec((2, 1, TQ, TQ), lambda h: (0, h, 0, 0)),
                   pl.BlockSpec((2, 1, CMP_WIN, TQ), lambda h: (0, h, 0, 0))],
        out_shape=[jax.ShapeDtypeStruct((2, N_HEADS, TQ, TQ), F32),
                   jax.ShapeDtypeStruct((2, N_HEADS, CMP_WIN, TQ), F32)],
        compiler_params=_cparams("parallel"),
        name="bias_tiles",
    )(rel_bias.T.astype(F32))
    a = np.arange(TQ)[:, None]
    b = np.arange(TQ)[None, :]
    win2 = jnp.asarray(np.where(a < b, 0.0, NEG).astype(np.float32))
    return near, win2, cmpb


INPROJ_TN = 4 * HEAD_DIM
CKV_TILE = COL_CK * HEAD_DIM // INPROJ_TN


def _inproj_kernel(x_ref, w_ref, cs_ref, wg_ref, o_ref, g_ref, ckv_ref, xb_ref):
    j = pl.program_id(1)

    @pl.when(j == 0)
    def _():
        xb = x_ref[...].astype(BF16)
        xb_ref[...] = xb
        g_ref[...] = jax.nn.sigmoid(lax.dot_general(xb, wg_ref[...], _NT, preferred_element_type=F32))

    acc = lax.dot_general(xb_ref[...], w_ref[...].astype(BF16), _NT, preferred_element_type=F32)
    z = (acc * cs_ref[...]).astype(o_ref.dtype)
    o_ref[...] = z

    @pl.when(j == CKV_TILE)
    def _():
        for c in range(INPROJ_TN // HEAD_DIM):
            ckv_ref[c] = z[:, c * HEAD_DIM:(c + 1) * HEAD_DIM]


def _inproj(x, w_in_t, colscale, w_gate_t, tm=1024):
    m, k = x.shape
    tn = INPROJ_TN
    return pl.pallas_call(
        _inproj_kernel,
        grid=(m // tm, GATE_COL0 // tn),
        in_specs=[pl.BlockSpec((tm, k), lambda i, j: (i, 0)),
                  pl.BlockSpec((tn, k), lambda i, j: (j, 0)),
                  pl.BlockSpec((1, tn), lambda i, j: (0, j)),
                  pl.BlockSpec((HEAD_DIM, k), lambda i, j: (0, 0))],
        out_specs=[pl.BlockSpec((tm, tn), lambda i, j: (i, j)),
                   pl.BlockSpec((tm, HEAD_DIM), lambda i, j: (i, 0)),
                   pl.BlockSpec((tn // HEAD_DIM, tm, HEAD_DIM), lambda i, j: (0, i, 0))],
        out_shape=[jax.ShapeDtypeStruct((m, GATE_COL0), BF16),
                   jax.ShapeDtypeStruct((m, HEAD_DIM), F32),
                   jax.ShapeDtypeStruct((tn // HEAD_DIM, m, HEAD_DIM), BF16)],
        scratch_shapes=[pltpu.VMEM((tm, k), BF16)],
        compiler_params=_cparams("parallel", "arbitrary"),
        name="inproj",
    )(x, w_in_t, colscale, w_gate_t)


def _gelu_tanh(x):
    return 0.5 * x * (1.0 + jnp.tanh(math.sqrt(2.0 / math.pi) * (x + 0.044715 * (x * x * x))))


def _compress_kernel(xk_ref, xv_ref, pek_ref, w1k_ref, w2k_ref, pev_ref, w1v_ref, w2v_ref, kc_ref, vct_ref):
    def compress(x_ref, pe_ref, w1_ref, w2_ref):
        x = x_ref[0].astype(F32)
        lo = jnp.dot((x + pe_ref[0]).astype(BF16), w1_ref[0].astype(BF16), preferred_element_type=F32)
        hi = jnp.dot((x + pe_ref[1]).astype(BF16), w1_ref[1].astype(BF16), preferred_element_type=F32)
        y = lo + pltpu.roll(hi, N_CMP, 0)
        out = jnp.dot(_gelu_tanh(y).astype(BF16), w2_ref[...].astype(BF16), preferred_element_type=F32)
        row = lax.broadcasted_iota(jnp.int32, out.shape, 0)
        return jnp.where(row < N_CMP, out, 0.0)

    kc_ref[0] = compress(xk_ref, pek_ref, w1k_ref, w2k_ref).astype(kc_ref.dtype)
    vct_ref[0] = compress(xv_ref, pev_ref, w1v_ref, w2v_ref).T.astype(vct_ref.dtype)


def _compress(xr, pe_k, w1_k, w2_k, pe_v, w1_v, w2_v):
    half = CMP_STRIDE * HEAD_DIM
    weights = [pl.BlockSpec((2, 1, half), lambda g: (0, 0, 0)),
               pl.BlockSpec((2, half, HEAD_DIM), lambda g: (0, 0, 0)),
               pl.BlockSpec((HEAD_DIM, HEAD_DIM), lambda g: (0, 0))]
    return pl.pallas_call(
        _compress_kernel,
        grid=(NSA_KV_HEADS,),
        in_specs=[pl.BlockSpec((1, 512, half), lambda g: (g, 0, 0)),
                  pl.BlockSpec((1, 512, half), lambda g: (NSA_KV_HEADS + g, 0, 0))] + weights + weights,
        out_specs=[pl.BlockSpec((1, 512, HEAD_DIM), lambda g: (g, 0, 0)),
                   pl.BlockSpec((1, HEAD_DIM, 512), lambda g: (g, 0, 0))],
        out_shape=[jax.ShapeDtypeStruct((NSA_KV_HEADS, 512, HEAD_DIM), BF16),
                   jax.ShapeDtypeStruct((NSA_KV_HEADS, HEAD_DIM, 512), BF16)],
        compiler_params=_cparams("parallel"),
        name="compress",
    )(xr, xr, pe_k, w1_k, w2_k, pe_v, w1_v, w2_v)


def _split3(x):
    hi = x.astype(BF16)
    r1 = x - hi.astype(F32)
    mid = r1.astype(BF16)
    lo = (r1 - mid.astype(F32)).astype(BF16)
    return hi, mid, lo


def _topk_bias_t(score, k):
    row = lax.broadcasted_iota(jnp.int32, score.shape, 0).astype(F32)
    selb = jnp.full(score.shape, NEG, F32)
    s = score
    for _ in range(k):
        mx = jnp.max(s, axis=0, keepdims=True)
        idx = jnp.min(jnp.where(s == mx, row, 1e9), axis=0, keepdims=True)
        hit = row == idx
        selb = jnp.where(hit, 0.0, selb)
        s = jnp.where(hit, -3e38, s)
    return selb


def _lane_tile_max(s):
    m = s[:, :HEAD_DIM]
    for c in range(1, s.shape[1] // HEAD_DIM):
        m = jnp.maximum(m, s[:, c * HEAD_DIM:(c + 1) * HEAD_DIM])
    return m


def _sparse_flash(qs, selbs, k_ref, v_ref, kcols, pat_ref, d0s, d1s, i, sub, scratch):
    t_ref, *maybe_r_ref, mx_ref, acc_ref = scratch
    r_ref = maybe_r_ref[0] if maybe_r_ref else None
    heads = range(len(qs))
    lane = lax.broadcasted_iota(jnp.int32, (TQ, HEAD_DIM), 1)
    far = lane < (i - 1) * sub
    q_near = [jnp.concatenate([qs[h], selbs[h].astype(BF16)], axis=1) for h in heads]
    q_far = [jnp.concatenate([qs[h], jnp.where(far, selbs[h], NEG).astype(BF16)], axis=1) for h in heads]

    def k_aug(h, off, n):
        c = kcols[h]
        return jnp.concatenate([k_ref[pl.ds(off, n), c:c + HEAD_DIM], pat_ref[pl.ds(off, n), :]], axis=1)

    def v_aug(h, off, n):
        c = kcols[h]
        return jnp.concatenate([v_ref[pl.ds(off, n), c:c + HEAD_DIM], jnp.ones((n, HEAD_DIM), BF16)], axis=1)

    for h in heads:
        mx_ref[h] = jnp.full((TQ, HEAD_DIM), NEG, F32)

    def far_group(g, carry):
        off = pl.multiple_of(g * GROUP, GROUP)
        for h in heads:
            s = lax.dot_general(q_far[h], k_aug(h, off, GROUP), _NT, preferred_element_type=F32)
            ref = _lane_tile_max(s)
            if r_ref is not None:
                s = s - jnp.concatenate([ref] * (GROUP // HEAD_DIM), axis=1)
                r_ref[h, g] = ref
            for u in range(KG):
                t_ref[h, g * KG + u] = s[:, u * TQ:(u + 1) * TQ].astype(t_ref.dtype)
            mx_ref[h] = jnp.maximum(mx_ref[h], ref)
        return carry

    n_far = lax.shift_right_logical(jnp.maximum(i - 1, 0) + (KG - 1), KG.bit_length() - 1)
    lax.fori_loop(0, n_far, far_group, 0)

    off_prev = pl.multiple_of(jnp.maximum(i - 1, 0) * TQ, TQ)
    off_own = pl.multiple_of(i * TQ, TQ)
    for h in heads:
        k2 = jnp.concatenate([k_aug(h, off_prev, TQ), k_aug(h, off_own, TQ)], axis=0)
        v2 = jnp.concatenate([v_aug(h, off_prev, TQ), v_aug(h, off_own, TQ)], axis=0)
        bias = jnp.concatenate([jnp.where(i >= 1, d1s[h], NEG), d0s[h]], axis=1)
        s = lax.dot_general(q_near[h], k2, _NT, preferred_element_type=F32) + bias
        m = jnp.max(jnp.maximum(mx_ref[h], _lane_tile_max(s)), axis=1, keepdims=True)
        mx_ref[h] = jnp.broadcast_to(m, (TQ, HEAD_DIM))
        acc_ref[h] = jnp.dot(jnp.exp(s - m).astype(BF16), v2, preferred_element_type=F32)

    def pv_group(g, carry):
        off = pl.multiple_of(g * GROUP, GROUP)
        for h in heads:
            d = -mx_ref[h] if r_ref is None else r_ref[h, g] - mx_ref[h]
            d2 = jnp.concatenate([d, d], axis=1)
            p = jnp.concatenate([jnp.exp(t_ref[h, g * KG + u].astype(F32) + d2).astype(BF16)
                                 for u in range(KG)], axis=1)
            acc_ref[h] += jnp.dot(p, v_aug(h, off, GROUP), preferred_element_type=F32)
        return carry

    lax.fori_loop(0, n_far, pv_group, 0)
    outs = []
    for h in heads:
        acc = acc_ref[h]
        outs.append(acc[:, :HEAD_DIM] / acc[:, HEAD_DIM:])
    return outs


def _flash_scratch(hp, relative_bf16):
    logits = [pltpu.VMEM((hp, N_QT, TQ, TQ), BF16),
              pltpu.VMEM((hp, SEQ // GROUP, TQ, HEAD_DIM), F32)]
    if not relative_bf16:
        logits = [pltpu.VMEM((hp, N_QT, TQ, TQ), F32)]
    return logits + [pltpu.VMEM((hp, TQ, HEAD_DIM), F32),
                     pltpu.VMEM((hp, TQ, 2 * HEAD_DIM), F32)]


def _moba_kernel(q_ref, k_ref, v_ref, pat_ref, d0_ref, d1_ref, o_ref, kmean_ref, *flash_scratch):
    i = pl.program_id(1)
    hp = HP_MOBA

    @pl.when(i == 0)
    def _():
        blk = lax.broadcasted_iota(jnp.int32, (HEAD_DIM, SEQ), 0)
        pos = lax.broadcasted_iota(jnp.int32, (HEAD_DIM, SEQ), 1)
        ind = jnp.where(lax.shift_right_logical(pos, 8) == blk, 1.0 / MOBA_BLOCK, 0.0).astype(BF16)
        for h in range(hp):
            kmean_ref[h] = jnp.dot(ind, k_ref[:, h * HEAD_DIM:(h + 1) * HEAD_DIM], preferred_element_type=F32)

    blk_t = lax.broadcasted_iota(jnp.int32, (HEAD_DIM, TQ), 0)
    past = blk_t < i
    qs, selbs = [], []
    for h in range(hp):
        q = q_ref[:, h * HEAD_DIM:(h + 1) * HEAD_DIM]
        gate = jnp.zeros((HEAD_DIM, TQ), F32)
        for part in _split3(kmean_ref[h]):
            gate = gate + lax.dot_general(part, q, _NT, preferred_element_type=F32)
        selb = _topk_bias_t(jnp.where(past, gate, NEG), MOBA_TOPK)
        selb = jnp.where(past, selb, jnp.where(blk_t == i, 0.0, NEG))
        qs.append(q)
        selbs.append(selb.T)
    cols = [h * HEAD_DIM for h in range(hp)]
    outs = _sparse_flash(qs, selbs, k_ref, v_ref, cols, pat_ref, [d0_ref[0, h] for h in range(hp)],
                         [d1_ref[0, h] for h in range(hp)], i, 1, flash_scratch)
    for h in range(hp):
        o_ref[:, h * HEAD_DIM:(h + 1) * HEAD_DIM] = outs[h].astype(o_ref.dtype)


def _moba(zb, near, pat):
    hp = HP_MOBA
    w = hp * HEAD_DIM
    return pl.pallas_call(
        _moba_kernel,
        grid=(MOBA_HEADS // hp, N_QT),
        in_specs=[pl.BlockSpec((TQ, w), lambda h, i: (i, COL_MQ // hp + h)),
                  pl.BlockSpec((SEQ, w), lambda h, i: (0, COL_MK // hp + h), pipeline_mode=pl.Buffered(1)),
                  pl.BlockSpec((SEQ, w), lambda h, i: (0, COL_MV // hp + h), pipeline_mode=pl.Buffered(1)),
                  pl.BlockSpec((SEQ, HEAD_DIM), lambda h, i: (0, 0), pipeline_mode=pl.Buffered(1)),
                  pl.BlockSpec((1, hp, TQ, TQ), lambda h, i: (0, h, 0, 0)),
                  pl.BlockSpec((1, hp, TQ, TQ), lambda h, i: (1, h, 0, 0))],
        out_specs=pl.BlockSpec((TQ, w), lambda h, i: (i, h)),
        out_shape=jax.ShapeDtypeStruct((SEQ, MOBA_HEADS * HEAD_DIM), BF16),
        scratch_shapes=[pltpu.VMEM((hp, HEAD_DIM, HEAD_DIM), F32)] + _flash_scratch(hp, True),
        compiler_params=_cparams("parallel", "arbitrary", vmem=BIG_VMEM_LIMIT),
        name="moba",
    )(zb, zb, zb, pat, near, near)


def _cmp_kernel(q_ref, kc_ref, vct_ref, ovt_ref, cb_ref, oc_ref, sel_ref, s_ref):
    i = pl.program_id(0)
    win0 = pl.multiple_of(jnp.maximum(i * CMP_STRIDE - CMP_STRIDE, 0), CMP_STRIDE)
    blk = lax.broadcasted_iota(jnp.int32, (N_SLC, TQ), 0)
    cur = i * SLC_PER_TILE + lax.shift_right_logical(lax.broadcasted_iota(jnp.int32, (N_SLC, TQ), 1), 6)
    allowed = blk <= cur
    forced = (blk == 0) | (blk == cur) | (blk == cur - 1)

    def body(nk):
        reachable = lax.broadcasted_iota(jnp.int32, (nk, TQ), 0) < win0 + CMP_WIN
        for g in range(NSA_KV_HEADS):
            p_sum = jnp.zeros((nk, TQ), F32)
            for r in range(NSA_GROUP):
                hd = g * NSA_GROUP + r
                q = q_ref[:, hd * HEAD_DIM:(hd + 1) * HEAD_DIM]
                s_ref[:nk, :] = lax.dot_general(kc_ref[g, :nk, :], q, _NT, preferred_element_type=F32)
                s_ref[pl.ds(win0, CMP_WIN), :] += cb_ref[0, hd]
                s = jnp.where(reachable, s_ref[:nk, :], NEG)
                valid = s > 0.5 * NEG
                e = jnp.where(valid, jnp.exp(s - jnp.max(s, axis=0, keepdims=True)), 0.0)
                p = e * (1.0 / jnp.maximum(jnp.sum(e, axis=0, keepdims=True), TINY))
                o_t = jnp.dot(vct_ref[g, :, :nk], p.astype(BF16), preferred_element_type=F32)
                oc_ref[:, hd * HEAD_DIM:(hd + 1) * HEAD_DIM] = o_t.T
                p_sum = p_sum + p
            imp = jnp.zeros((N_SLC, TQ), F32)
            for part in _split3(p_sum):
                imp = imp + jnp.dot(ovt_ref[:, :nk], part, preferred_element_type=F32)
            picked = _topk_bias_t(jnp.where(allowed & ~forced, imp, -1.0), SLC_TOPK - 3)
            selb = jnp.where(allowed, jnp.where(forced, 0.0, picked), NEG)
            sel_ref[g] = selb.T.astype(sel_ref.dtype)

    for c in range(1, 512 // HEAD_DIM + 1):
        @pl.when(lax.shift_right_logical(i, 3) == c - 1)
        def _():
            body(c * HEAD_DIM)


def _cmp_select(zb, kc, vct, ovt, cmpb):
    w = NSA_HEADS * HEAD_DIM
    return pl.pallas_call(
        _cmp_kernel,
        grid=(N_QT,),
        in_specs=[pl.BlockSpec((TQ, w), lambda i: (i, COL_NQ // NSA_HEADS)),
                  pl.BlockSpec((NSA_KV_HEADS, 512, HEAD_DIM), lambda i: (0, 0, 0)),
                  pl.BlockSpec((NSA_KV_HEADS, HEAD_DIM, 512), lambda i: (0, 0, 0)),
                  pl.BlockSpec((N_SLC, 512), lambda i: (0, 0)),
                  pl.BlockSpec((1, NSA_HEADS, CMP_WIN, TQ), lambda i: (jnp.minimum(i, 1), 1, 0, 0))],
        out_specs=[pl.BlockSpec((TQ, w), lambda i: (i, 0)),
                   pl.BlockSpec((NSA_KV_HEADS, TQ, N_SLC), lambda i: (0, i, 0))],
        out_shape=[jax.ShapeDtypeStruct((SEQ, w), F32),
                   jax.ShapeDtypeStruct((NSA_KV_HEADS, SEQ, N_SLC), BF16)],
        scratch_shapes=[pltpu.VMEM((512, TQ), F32)],
        compiler_params=_cparams("arbitrary"),
        name="cmp_select",
    )(zb, kc, vct, ovt, cmpb)


def _slc_kernel(q_ref, sel_ref, k_ref, v_ref, pat_ref, d0_ref, d1_ref, o_ref, *flash_scratch):
    i = pl.program_id(1)
    hp = HP_SLC
    selb = sel_ref[0].astype(F32)
    qs = [q_ref[:, h * HEAD_DIM:(h + 1) * HEAD_DIM] for h in range(hp)]
    outs = _sparse_flash(qs, [selb] * hp, k_ref, v_ref, [0] * hp, pat_ref,
                         [d0_ref[0, h] for h in range(hp)], [d1_ref[0, h] for h in range(hp)],
                         i, SLC_PER_TILE, flash_scratch)
    for h in range(hp):
        o_ref[:, h * HEAD_DIM:(h + 1) * HEAD_DIM] = outs[h].astype(o_ref.dtype)


def _slc(zb, selb, near, pat):
    hp = HP_SLC
    w = hp * HEAD_DIM
    per_group = NSA_GROUP // hp
    return pl.pallas_call(
        _slc_kernel,
        grid=(NSA_HEADS // hp, N_QT),
        in_specs=[pl.BlockSpec((TQ, w), lambda h, i: (i, COL_NQ // hp + h)),
                  pl.BlockSpec((1, TQ, N_SLC), lambda h, i: (h // per_group, i, 0)),
                  pl.BlockSpec((SEQ, HEAD_DIM), lambda h, i: (0, COL_SK + h // per_group),
                               pipeline_mode=pl.Buffered(1)),
                  pl.BlockSpec((SEQ, HEAD_DIM), lambda h, i: (0, COL_SV + h // per_group),
                               pipeline_mode=pl.Buffered(1)),
                  pl.BlockSpec((SEQ, HEAD_DIM), lambda h, i: (0, 0), pipeline_mode=pl.Buffered(1)),
                  pl.BlockSpec((1, hp, TQ, TQ), lambda h, i: (0, MOBA_HEADS // hp + h, 0, 0)),
                  pl.BlockSpec((1, hp, TQ, TQ), lambda h, i: (1, MOBA_HEADS // hp + h, 0, 0))],
        out_specs=pl.BlockSpec((TQ, w), lambda h, i: (i, h)),
        out_shape=jax.ShapeDtypeStruct((SEQ, NSA_HEADS * HEAD_DIM), F32),
        scratch_shapes=_flash_scratch(hp, False),
        compiler_params=_cparams("parallel", "arbitrary", vmem=BIG_VMEM_LIMIT),
        name="slc",
    )(zb, selb, zb, zb, pat, near, near)


def _win_kernel(q_ref, k_ref, v_ref, d0_ref, d1_ref, w2_ref, o_ref):
    i = pl.program_id(0)
    offs = [pl.multiple_of(jnp.maximum(i - b, 0) * TQ, TQ) for b in range(3)]
    mask2 = jnp.where(i >= 2, w2_ref[...], NEG)
    ones = jnp.ones((3 * TQ, HEAD_DIM), BF16)
    for g in range(NSA_KV_HEADS):
        cols = slice(g * HEAD_DIM, (g + 1) * HEAD_DIM)
        k_all = jnp.concatenate([k_ref[pl.ds(off, TQ), cols] for off in offs], axis=0)
        v_aug = jnp.concatenate([jnp.concatenate([v_ref[pl.ds(off, TQ), cols] for off in offs], axis=0), ones],
                                axis=1)
        for r in range(NSA_GROUP):
            hd = g * NSA_GROUP + r
            q = q_ref[:, hd * HEAD_DIM:(hd + 1) * HEAD_DIM]
            bias = jnp.concatenate([d0_ref[0, hd], jnp.where(i >= 1, d1_ref[0, hd], NEG), mask2], axis=1)
            s = lax.dot_general(q, k_all, _NT, preferred_element_type=F32) + bias
            m = jnp.max(s, axis=1, keepdims=True)
            pv = jnp.dot(jnp.exp(s - m).astype(BF16), v_aug, preferred_element_type=F32)
            o_ref[:, hd * HEAD_DIM:(hd + 1) * HEAD_DIM] = pv[:, :HEAD_DIM] / pv[:, HEAD_DIM:]


def _win(zb, near, win2):
    w = NSA_HEADS * HEAD_DIM
    kv_w = NSA_KV_HEADS * HEAD_DIM
    return pl.pallas_call(
        _win_kernel,
        grid=(N_QT,),
        in_specs=[pl.BlockSpec((TQ, w), lambda i: (i, COL_NQ // NSA_HEADS)),
                  pl.BlockSpec((SEQ, kv_w), lambda i: (0, COL_WK // NSA_KV_HEADS), pipeline_mode=pl.Buffered(1)),
                  pl.BlockSpec((SEQ, kv_w), lambda i: (0, COL_WV // NSA_KV_HEADS), pipeline_mode=pl.Buffered(1)),
                  pl.BlockSpec((1, NSA_HEADS, TQ, TQ), lambda i: (0, 1, 0, 0), pipeline_mode=pl.Buffered(1)),
                  pl.BlockSpec((1, NSA_HEADS, TQ, TQ), lambda i: (1, 1, 0, 0), pipeline_mode=pl.Buffered(1)),
                  pl.BlockSpec((TQ, TQ), lambda i: (0, 0), pipeline_mode=pl.Buffered(1))],
        out_specs=pl.BlockSpec((TQ, w), lambda i: (i, 0)),
        out_shape=jax.ShapeDtypeStruct((SEQ, w), F32),
        compiler_params=_cparams("arbitrary"),
        name="win",
    )(zb, zb, zb, near, near, win2)


def _layer_norm(y, g, b):
    mu = jnp.mean(y, axis=-1, keepdims=True)
    yc = y - mu
    var = jnp.mean(yc * yc, axis=-1, keepdims=True)
    return yc * lax.rsqrt(var + LN_EPS) * g + b


ROW_CHUNK = 256


def _outproj_kernel(x_ref, om_ref, oc_ref, os_ref, ow_ref, g_ref, e_ref, w_ref, lg_ref, lb_ref,
                    h_ref, hb_ref):
    for c0 in range(0, x_ref.shape[0], ROW_CHUNK):
        rows = slice(c0, c0 + ROW_CHUNK)
        gate = g_ref[rows, :]
        hi = gate.astype(BF16)
        lo = (gate - hi.astype(F32)).astype(BF16)
        g2 = jnp.concatenate([hi, lo], axis=1)
        o_n = jnp.zeros((ROW_CHUNK, oc_ref.shape[1]), F32)
        for c, branch in enumerate((oc_ref, os_ref, ow_ref)):
            o_n = o_n + jnp.dot(g2, e_ref[c], preferred_element_type=F32) * branch[rows, :]
        o = jnp.concatenate([om_ref[rows, :], o_n.astype(BF16)], axis=1)
        a = jnp.dot(o, w_ref[...], preferred_element_type=F32)
        h = _layer_norm(DN_ALPHA * x_ref[rows, :] + a, lg_ref[...], lb_ref[...])
        h_ref[rows, :] = h
        hb_ref[rows, :] = h.astype(BF16)


def _resident(shape):
    return pl.BlockSpec(shape, lambda i: (0,) * len(shape), pipeline_mode=pl.Buffered(1))


def _outproj(x, o_m, o_c, o_s, o_w, gates, expand, w_out, ln_g, ln_b, tm=512):
    half = NSA_HEADS * HEAD_DIM
    row = lambda i: (i, 0)
    return pl.pallas_call(
        _outproj_kernel,
        grid=(SEQ // tm,),
        in_specs=[pl.BlockSpec((tm, D_MODEL), row),
                  pl.BlockSpec((tm, half), row), pl.BlockSpec((tm, half), row),
                  pl.BlockSpec((tm, half), row), pl.BlockSpec((tm, half), row),
                  pl.BlockSpec((tm, HEAD_DIM), row),
                  _resident((3, 2 * HEAD_DIM, half)),
                  _resident((D_MODEL, D_MODEL)),
                  _resident((1, D_MODEL)), _resident((1, D_MODEL))],
        out_specs=[pl.BlockSpec((tm, D_MODEL), row), pl.BlockSpec((tm, D_MODEL), row)],
        out_shape=[jax.ShapeDtypeStruct((SEQ, D_MODEL), F32),
                   jax.ShapeDtypeStruct((SEQ, D_MODEL), BF16)],
        compiler_params=_cparams("parallel", vmem=BIG_VMEM_LIMIT),
        name="outproj_ln",
    )(x, o_m, o_c, o_s, o_w, gates, expand, w_out, ln_g, ln_b)


def _ffn_kernel(hb_ref, w1_ref, w2_ref, f_ref):
    @pl.when(pl.program_id(1) == 0)
    def _():
        f_ref[...] = jnp.zeros(f_ref.shape, F32)

    u = jnp.maximum(jnp.dot(hb_ref[...], w1_ref[...].astype(BF16), preferred_element_type=F32), 0.0)
    f_ref[...] += jnp.dot((u * u).astype(BF16), w2_ref[...].astype(BF16), preferred_element_type=F32)


def _ffn(hb, w1, w2, tm=1024, tf=512):
    return pl.pallas_call(
        _ffn_kernel,
        grid=(SEQ // tm, D_FF // tf),
        in_specs=[pl.BlockSpec((tm, D_MODEL), lambda i, c: (i, 0)),
                  pl.BlockSpec((D_MODEL, tf), lambda i, c: (0, c)),
                  pl.BlockSpec((tf, D_MODEL), lambda i, c: (c, 0))],
        out_specs=pl.BlockSpec((tm, D_MODEL), lambda i, c: (i, 0)),
        out_shape=jax.ShapeDtypeStruct((SEQ, D_MODEL), F32),
        compiler_params=_cparams("parallel", "arbitrary", vmem=BIG_VMEM_LIMIT),
        name="ffn",
    )(hb, w1, w2)


def _final_kernel(h_ref, hb_ref, f_ref, p_ref, wg_ref, wp_ref, lg_ref, lb_ref, o_ref):
    for c0 in range(0, h_ref.shape[0], ROW_CHUNK):
        rows = slice(c0, c0 + ROW_CHUNK)
        gate = jax.nn.sigmoid(jnp.dot(hb_ref[rows, :], wg_ref[...], preferred_element_type=F32))
        emb = jnp.dot(p_ref[rows, :].astype(BF16), wp_ref[...], preferred_element_type=F32)
        y = DN_ALPHA * h_ref[rows, :] + f_ref[rows, :] + gate * emb
        o_ref[rows, :] = _layer_norm(y, lg_ref[...], lb_ref[...])


def _final(h, hb, f, p, wg, wp, ln_g, ln_b, tm=512):
    row = lambda i: (i, 0)
    return pl.pallas_call(
        _final_kernel,
        grid=(SEQ // tm,),
        in_specs=[pl.BlockSpec((tm, D_MODEL), row), pl.BlockSpec((tm, D_MODEL), row),
                  pl.BlockSpec((tm, D_MODEL), row), pl.BlockSpec((tm, D_PLE), row),
                  _resident((D_MODEL, D_MODEL)), _resident((D_PLE, D_MODEL)),
                  _resident((1, D_MODEL)), _resident((1, D_MODEL))],
        out_specs=pl.BlockSpec((tm, D_MODEL), row),
        out_shape=jax.ShapeDtypeStruct((SEQ, D_MODEL), F32),
        compiler_params=_cparams("parallel"),
        name="final_ln",
    )(h, hb, f, p, wg, wp, ln_g, ln_b)


def _overlap_matrix_t():
    j = np.arange(N_SLC)[:, None]
    n = np.arange(512)[None, :]
    ov = ((n * CMP_STRIDE < j * SLC_LEN + SLC_LEN) & (n * CMP_STRIDE + CMP_LEN > j * SLC_LEN) & (n < N_CMP))
    return ov.astype(np.float32)


def _gate_expand():
    e = np.zeros((3, 2 * HEAD_DIM, NSA_HEADS * HEAD_DIM), np.float32)
    for c in range(3):
        for h in range(NSA_HEADS):
            e[c, 3 * h + c, h * HEAD_DIM:(h + 1) * HEAD_DIM] = 1.0
            e[c, HEAD_DIM + 3 * h + c, h * HEAD_DIM:(h + 1) * HEAD_DIM] = 1.0
    return e


def _layer(h, p, w_in, cmp_pe_k, cmp_w1_k, cmp_w2_k, cmp_pe_v, cmp_w1_v, cmp_w2_v, tiles, w_out,
           ln1_g, ln1_b, w_ff1, w_ff2, w_ple, w_ple_gate, ln2_g, ln2_b):
    near, win2, cmpb = tiles
    scale = HEAD_DIM ** -0.5
    colscale = np.ones((1, GATE_COL0), np.float32)
    colscale[0, COL_MQ * HEAD_DIM:(COL_MQ + MOBA_HEADS) * HEAD_DIM] = scale
    colscale[0, COL_NQ * HEAD_DIM:(COL_NQ + NSA_HEADS) * HEAD_DIM] = scale
    w_in_t = w_in.T
    w_gl_t = jnp.pad(w_in_t[GATE_COL0:], ((0, HEAD_DIM - NSA_HEADS * 3), (0, 0))).astype(BF16)
    zb, gates, ckv = _inproj(h, w_in_t, jnp.asarray(colscale), w_gl_t)

    half = CMP_STRIDE * HEAD_DIM
    xr = ckv.reshape(4, SEQ // CMP_STRIDE, half)
    kc, vct = _compress(xr, cmp_pe_k.reshape(2, 1, half), cmp_w1_k.reshape(2, half, HEAD_DIM), cmp_w2_k,
                        cmp_pe_v.reshape(2, 1, half), cmp_w1_v.reshape(2, half, HEAD_DIM), cmp_w2_v)

    key = np.arange(SEQ)[:, None]
    lane = np.arange(HEAD_DIM)[None, :]
    pat_moba = jnp.asarray(lane == key // MOBA_BLOCK, BF16)
    pat_slc = jnp.asarray(lane == key // SLC_LEN, BF16)
    o_m = _moba(zb, near, pat_moba)
    o_c, selb = _cmp_select(zb, kc, vct, jnp.asarray(_overlap_matrix_t(), BF16), cmpb)
    o_s = _slc(zb, selb, near, pat_slc)
    o_w = _win(zb, near, win2)

    h1, h1b = _outproj(h, o_m, o_c, o_s, o_w, gates, jnp.asarray(_gate_expand(), BF16),
                       w_out.astype(BF16), ln1_g.reshape(1, -1), ln1_b.reshape(1, -1))
    f = _ffn(h1b, w_ff1, w_ff2)
    return _final(h1, h1b, f, p, w_ple_gate.astype(BF16), w_ple.astype(BF16),
                  ln2_g.reshape(1, -1), ln2_b.reshape(1, -1))


def kernel(x, p, w_in, cmp_pe_k, cmp_w1_k, cmp_w2_k, cmp_pe_v, cmp_w1_v, cmp_w2_v, rel_bias, w_out,
           ln1_g, ln1_b, w_ff1, w_ff2, w_ple, w_ple_gate, ln2_g, ln2_b):
    tiles = _bias_tiles(rel_bias)
    h = x[0]
    for i in range(w_in.shape[0]):
        h = _layer(h, p[i, 0], w_in[i], cmp_pe_k[i], cmp_w1_k[i], cmp_w2_k[i], cmp_pe_v[i], cmp_w1_v[i],
                   cmp_w2_v[i], tiles, w_out[i], ln1_g[i], ln1_b[i], w_ff1[i], w_ff2[i], w_ple[i],
                   w_ple_gate[i], ln2_g[i], ln2_b[i])
    return h[None]
```

```python
import functools
import math

import numpy as np
import jax
import jax.numpy as jnp
from jax import lax
from jax.experimental import pallas as pl
from jax.experimental.pallas import tpu as pltpu

D_MODEL = 2048
SEQ = 8192
HEAD_DIM = 128
N_HEADS = 16
MOBA_HEADS = 8
NSA_HEADS = 8
NSA_KV_HEADS = 2
NSA_GROUP = 4
MOBA_BLOCK = 256
MOBA_TOPK = 3
CMP_LEN = 32
CMP_STRIDE = 16
SLC_LEN = 64
SLC_TOPK = 16
WINDOW = 512
N_BUCKETS = 32
MAX_DISTANCE = 128
D_FF = 4 * D_MODEL
D_PLE = 256
LN_EPS = 1e-5
DN_ALPHA = 2.0 ** 0.25
NEG = -1e30
TINY = 1e-30
FORCE_SCORE = 1e4

N_CMP = (SEQ - CMP_LEN) // CMP_STRIDE + 1
N_SLC = SEQ // SLC_LEN
TQ = 256
N_QT = SEQ // TQ
CMP_WIN = 32
SLC_PER_TILE = TQ // SLC_LEN
HP_MOBA = 4
HP_SLC = 4
KG = 4
GROUP = KG * TQ
COL_MQ, COL_MK, COL_MV, COL_NQ = 0, 8, 16, 24
COL_CK, COL_CV, COL_SK, COL_SV, COL_WK, COL_WV = 32, 34, 36, 38, 40, 42
N_COLBLK = 44
GATE_COL0 = N_COLBLK * HEAD_DIM

VMEM_LIMIT = 48 * 1024 * 1024
BIG_VMEM_LIMIT = 56 * 1024 * 1024

BF16 = jnp.bfloat16
F32 = jnp.float32
_NT = (((1,), (1,)), ((), ()))


def _cparams(*sem, vmem=VMEM_LIMIT):
    return pltpu.CompilerParams(dimension_semantics=sem, vmem_limit_bytes=vmem)


def _bucket_table():
    n = np.arange(1024, dtype=np.int32)
    max_exact = N_BUCKETS // 2
    ratio = np.maximum(n, 1).astype(np.float32) / np.float32(max_exact)
    large = max_exact + (np.log(ratio).astype(np.float32) / np.float32(math.log(MAX_DISTANCE / max_exact))
                         * np.float32(N_BUCKETS - max_exact)).astype(np.int32)
    large = np.minimum(large, N_BUCKETS - 1)
    return np.where(n < max_exact, n, large).astype(np.int32)


def _bias_kernel(tab_ref, near_ref, cmpb_ref, *, steps):
    h = pl.program_id(0)
    far = tab_ref[h, N_BUCKETS - 1]

    def lookup(d):
        val = jnp.full(d.shape, tab_ref[h, 0], F32)
        for start, bucket in steps:
            val = jnp.where(d >= start, tab_ref[h, bucket], val)
        return jnp.where(d >= 0, val - far, NEG)

    a = lax.broadcasted_iota(jnp.int32, (TQ, TQ), 0)
    b = lax.broadcasted_iota(jnp.int32, (TQ, TQ), 1)
    near_ref[0, 0] = lookup(a - b)
    near_ref[1, 0] = lookup(a - b + TQ)
    w = lax.broadcasted_iota(jnp.int32, (CMP_WIN, TQ), 0)
    aw = lax.broadcasted_iota(jnp.int32, (CMP_WIN, TQ), 1)
    cmpb_ref[0, 0] = lookup(aw - CMP_STRIDE * w - (CMP_LEN - 1))
    cmpb_ref[1, 0] = lookup(aw - CMP_STRIDE * w + (TQ - CMP_LEN + 1))


def _bias_tiles(rel_bias):
    bucket = _bucket_table()
    steps = tuple((int(n), int(bucket[n])) for n in range(1, bucket.size) if bucket[n] != bucket[n - 1])
    near, cmpb = pl.pallas_call(
        functools.partial(_bias_kernel, steps=steps),
        grid=(N_HEADS,),
        in_specs=[pl.BlockSpec(memory_space=pltpu.SMEM)],
        out_specs=[pl.BlockSpec((2, 1, TQ, TQ), lambda h: (0, h, 0, 0)),
                   pl.BlockSpec((2, 1, CMP_WIN, TQ), lambda h: (0, h, 0, 0))],
        out_shape=[jax.ShapeDtypeStruct((2, N_HEADS, TQ, TQ), F32),
                   jax.ShapeDtypeStruct((2, N_HEADS, CMP_WIN, TQ), F32)],
        compiler_params=_cparams("parallel"),
        name="bias_tiles",
    )(rel_bias.T.astype(F32))
    a = np.arange(TQ)[:, None]
    b = np.arange(TQ)[None, :]
    win2 = jnp.asarray(np.where(a < b, 0.0, NEG).astype(np.float32))
    return near, win2, cmpb


INPROJ_TN = 4 * HEAD_DIM
CKV_TILE = COL_CK * HEAD_DIM // INPROJ_TN


def _inproj_kernel(x_ref, w_ref, cs_ref, wg_ref, o_ref, g_ref, ckv_ref, xb_ref):
    j = pl.program_id(1)

    @pl.when(j == 0)
    def _():
        xb = x_ref[...].astype(BF16)
        xb_ref[...] = xb
        g_ref[...] = jax.nn.sigmoid(lax.dot_general(xb, wg_ref[...], _NT, preferred_element_type=F32))

    acc = lax.dot_general(xb_ref[...], w_ref[...].astype(BF16), _NT, preferred_element_type=F32)
    z = (acc * cs_ref[...]).astype(o_ref.dtype)
    o_ref[...] = z

    @pl.when(j == CKV_TILE)
    def _():
        for c in range(INPROJ_TN // HEAD_DIM):
            ckv_ref[c] = z[:, c * HEAD_DIM:(c + 1) * HEAD_DIM]


def _inproj(x, w_in_t, colscale, w_gate_t, tm=1024):
    m, k = x.shape
    tn = INPROJ_TN
    return pl.pallas_call(
        _inproj_kernel,
        grid=(m // tm, GATE_COL0 // tn),
        in_specs=[pl.BlockSpec((tm, k), lambda i, j: (i, 0)),
                  pl.BlockSpec((tn, k), lambda i, j: (j, 0)),
                  pl.BlockSpec((1, tn), lambda i, j: (0, j)),
                  pl.BlockSpec((HEAD_DIM, k), lambda i, j: (0, 0))],
        out_specs=[pl.BlockSpec((tm, tn), lambda i, j: (i, j)),
                   pl.BlockSpec((tm, HEAD_DIM), lambda i, j: (i, 0)),
                   pl.BlockSpec((tn // HEAD_DIM, tm, HEAD_DIM), lambda i, j: (0, i, 0))],
        out_shape=[jax.ShapeDtypeStruct((m, GATE_COL0), BF16),
                   jax.ShapeDtypeStruct((m, HEAD_DIM), F32),
                   jax.ShapeDtypeStruct((tn // HEAD_DIM, m, HEAD_DIM), BF16)],
        scratch_shapes=[pltpu.VMEM((tm, k), BF16)],
        compiler_params=_cparams("parallel", "arbitrary"),
        name="inproj",
    )(x, w_in_t, colscale, w_gate_t)


def _gelu_tanh(x):
    return 0.5 * x * (1.0 + jnp.tanh(math.sqrt(2.0 / math.pi) * (x + 0.044715 * (x * x * x))))


def _compress_kernel(xk_ref, xv_ref, pek_ref, w1k_ref, w2k_ref, pev_ref, w1v_ref, w2v_ref, kc_ref, vct_ref):
    def compress(x_ref, pe_ref, w1_ref, w2_ref):
        x = x_ref[0].astype(F32)
        lo = jnp.dot((x + pe_ref[0]).astype(BF16), w1_ref[0].astype(BF16), preferred_element_type=F32)
        hi = jnp.dot((x + pe_ref[1]).astype(BF16), w1_ref[1].astype(BF16), preferred_element_type=F32)
        y = lo + pltpu.roll(hi, N_CMP, 0)
        out = jnp.dot(_gelu_tanh(y).astype(BF16), w2_ref[...].astype(BF16), preferred_element_type=F32)
        row = lax.broadcasted_iota(jnp.int32, out.shape, 0)
        return jnp.where(row < N_CMP, out, 0.0)

    kc_ref[0] = compress(xk_ref, pek_ref, w1k_ref, w2k_ref).astype(kc_ref.dtype)
    vct_ref[0] = compress(xv_ref, pev_ref, w1v_ref, w2v_ref).T.astype(vct_ref.dtype)


def _compress(xr, pe_k, w1_k, w2_k, pe_v, w1_v, w2_v):
    half = CMP_STRIDE * HEAD_DIM
    weights = [pl.BlockSpec((2, 1, half), lambda g: (0, 0, 0)),
               pl.BlockSpec((2, half, HEAD_DIM), lambda g: (0, 0, 0)),
               pl.BlockSpec((HEAD_DIM, HEAD_DIM), lambda g: (0, 0))]
    return pl.pallas_call(
        _compress_kernel,
        grid=(NSA_KV_HEADS,),
        in_specs=[pl.BlockSpec((1, 512, half), lambda g: (g, 0, 0)),
                  pl.BlockSpec((1, 512, half), lambda g: (NSA_KV_HEADS + g, 0, 0))] + weights + weights,
        out_specs=[pl.BlockSpec((1, 512, HEAD_DIM), lambda g: (g, 0, 0)),
                   pl.BlockSpec((1, HEAD_DIM, 512), lambda g: (g, 0, 0))],
        out_shape=[jax.ShapeDtypeStruct((NSA_KV_HEADS, 512, HEAD_DIM), BF16),
                   jax.ShapeDtypeStruct((NSA_KV_HEADS, HEAD_DIM, 512), BF16)],
        compiler_params=_cparams("parallel"),
        name="compress",
    )(xr, xr, pe_k, w1_k, w2_k, pe_v, w1_v, w2_v)


def _split3(x):
    hi = x.astype(BF16)
    r1 = x - hi.astype(F32)
    mid = r1.astype(BF16)
    lo = (r1 - mid.astype(F32)).astype(BF16)
    return hi, mid, lo


def _topk_bias_t(score, k):
    row = lax.broadcasted_iota(jnp.int32, score.shape, 0).astype(F32)
    selb = jnp.full(score.shape, NEG, F32)
    s = score
    for _ in range(k):
        mx = jnp.max(s, axis=0, keepdims=True)
        idx = jnp.min(jnp.where(s == mx, row, 1e9), axis=0, keepdims=True)
        hit = row == idx
        selb = jnp.where(hit, 0.0, selb)
        s = jnp.where(hit, -3e38, s)
    return selb


def _lane_tile_max(s):
    m = s[:, :HEAD_DIM]
    for c in range(1, s.shape[1] // HEAD_DIM):
        m = jnp.maximum(m, s[:, c * HEAD_DIM:(c + 1) * HEAD_DIM])
    return m


def _sparse_flash(qs, selbs, k_ref, v_ref, kcols, pat_ref, d0s, d1s, i, sub, scratch):
    t_ref, *maybe_r_ref, mx_ref, acc_ref = scratch
    r_ref = maybe_r_ref[0] if maybe_r_ref else None
    heads = range(len(qs))
    lane = lax.broadcasted_iota(jnp.int32, (TQ, HEAD_DIM), 1)
    far = lane < (i - 1) * sub
    q_near = [jnp.concatenate([qs[h], selbs[h].astype(BF16)], axis=1) for h in heads]
    q_far = [jnp.concatenate([qs[h], jnp.where(far, selbs[h], NEG).astype(BF16)], axis=1) for h in heads]

    def k_aug(h, off, n):
        c = kcols[h]
        return jnp.concatenate([k_ref[pl.ds(off, n), c:c + HEAD_DIM], pat_ref[pl.ds(off, n), :]], axis=1)

    def v_aug(h, off, n):
        c = kcols[h]
        return jnp.concatenate([v_ref[pl.ds(off, n), c:c + HEAD_DIM], jnp.ones((n, HEAD_DIM), BF16)], axis=1)

    for h in heads:
        mx_ref[h] = jnp.full((TQ, HEAD_DIM), NEG, F32)

    def far_group(g, carry):
        off = pl.multiple_of(g * GROUP, GROUP)
        for h in heads:
            s = lax.dot_general(q_far[h], k_aug(h, off, GROUP), _NT, preferred_element_type=F32)
            ref = _lane_tile_max(s)
            if r_ref is not None:
                s = s - jnp.concatenate([ref] * (GROUP // HEAD_DIM), axis=1)
                r_ref[h, g] = ref
            for u in range(KG):
                t_ref[h, g * KG + u] = s[:, u * TQ:(u + 1) * TQ].astype(t_ref.dtype)
            mx_ref[h] = jnp.maximum(mx_ref[h], ref)
        return carry

    n_far = lax.shift_right_logical(jnp.maximum(i - 1, 0) + (KG - 1), KG.bit_length() - 1)
    lax.fori_loop(0, n_far, far_group, 0)

    off_prev = pl.multiple_of(jnp.maximum(i - 1, 0) * TQ, TQ)
    off_own = pl.multiple_of(i * TQ, TQ)
    for h in heads:
        k2 = jnp.concatenate([k_aug(h, off_prev, TQ), k_aug(h, off_own, TQ)], axis=0)
        v2 = jnp.concatenate([v_aug(h, off_prev, TQ), v_aug(h, off_own, TQ)], axis=0)
        bias = jnp.concatenate([jnp.where(i >= 1, d1s[h], NEG), d0s[h]], axis=1)
        s = lax.dot_general(q_near[h], k2, _NT, preferred_element_type=F32) + bias
        m = jnp.max(jnp.maximum(mx_ref[h], _lane_tile_max(s)), axis=1, keepdims=True)
        mx_ref[h] = jnp.broadcast_to(m, (TQ, HEAD_DIM))
        acc_ref[h] = jnp.dot(jnp.exp(s - m).astype(BF16), v2, preferred_element_type=F32)

    def pv_group(g, carry):
        off = pl.multiple_of(g * GROUP, GROUP)
        for h in heads:
            d = -mx_ref[h] if r_ref is None else r_ref[h, g] - mx_ref[h]
            d2 = jnp.concatenate([d, d], axis=1)
            p = jnp.concatenate([jnp.exp(t_ref[h, g * KG + u].astype(F32) + d2).astype(BF16)
                                 for u in range(KG)], axis=1)
            acc_ref[h] += jnp.dot(p, v_aug(h, off, GROUP), preferred_element_type=F32)
        return carry

    lax.fori_loop(0, n_far, pv_group, 0)
    outs = []
    for h in heads:
        acc = acc_ref[h]
        outs.append(acc[:, :HEAD_DIM] / acc[:, HEAD_DIM:])
    return outs


def _flash_scratch(hp, relative_bf16):
    logits = [pltpu.VMEM((hp, N_QT, TQ, TQ), BF16),
              pltpu.VMEM((hp, SEQ // GROUP, TQ, HEAD_DIM), F32)]
    if not relative_bf16:
        logits = [pltpu.VMEM((hp, N_QT, TQ, TQ), F32)]
    return logits + [pltpu.VMEM((hp, TQ, HEAD_DIM), F32),
                     pltpu.VMEM((hp, TQ, 2 * HEAD_DIM), F32)]


SEL_CHUNK = 4096


def _moba_select_kernel(q_ref, k_ref, sel_ref):
    blk = lax.broadcasted_iota(jnp.int32, (HEAD_DIM, SEQ), 0)
    pos = lax.broadcasted_iota(jnp.int32, (HEAD_DIM, SEQ), 1)
    ind = jnp.where(lax.shift_right_logical(pos, 8) == blk, 1.0 / MOBA_BLOCK, 0.0).astype(BF16)
    n_blk = SEQ // MOBA_BLOCK
    kmean = _split3(jnp.dot(ind, k_ref[...], preferred_element_type=F32)[:n_blk])
    blk_t = lax.broadcasted_iota(jnp.int32, (n_blk, SEL_CHUNK), 0)
    col = lax.broadcasted_iota(jnp.int32, (n_blk, SEL_CHUNK), 1)
    no_block = jnp.full((HEAD_DIM - n_blk, SEL_CHUNK), NEG, F32)

    def chunk(c, carry):
        off = pl.multiple_of(c * SEL_CHUNK, SEL_CHUNK)
        q = q_ref[pl.ds(off, SEL_CHUNK), :]
        gate = jnp.zeros((n_blk, SEL_CHUNK), F32)
        for part in kmean:
            gate = gate + lax.dot_general(part, q, _NT, preferred_element_type=F32)
        own = lax.shift_right_logical(off + col, 8)
        past = blk_t < own
        selb = _topk_bias_t(jnp.where(past, gate, NEG), MOBA_TOPK)
        selb = jnp.where(past, selb, jnp.where(blk_t == own, 0.0, NEG))
        sel_ref[0, pl.ds(off, SEL_CHUNK), :] = jnp.concatenate([selb, no_block], axis=0).T.astype(sel_ref.dtype)
        return carry

    lax.fori_loop(0, SEQ // SEL_CHUNK, chunk, 0)


def _moba_select(zb):
    return pl.pallas_call(
        _moba_select_kernel,
        grid=(MOBA_HEADS,),
        in_specs=[pl.BlockSpec((SEQ, HEAD_DIM), lambda h: (0, COL_MQ + h)),
                  pl.BlockSpec((SEQ, HEAD_DIM), lambda h: (0, COL_MK + h))],
        out_specs=pl.BlockSpec((1, SEQ, HEAD_DIM), lambda h: (h, 0, 0)),
        out_shape=jax.ShapeDtypeStruct((MOBA_HEADS, SEQ, HEAD_DIM), BF16),
        compiler_params=_cparams("parallel"),
        name="moba_select",
    )(zb, zb)


def _moba_kernel(q_ref, sel_ref, k_ref, v_ref, pat_ref, d0_ref, d1_ref, o_ref, *flash_scratch):
    i = pl.program_id(1)
    hp = HP_MOBA
    qs = [q_ref[:, h * HEAD_DIM:(h + 1) * HEAD_DIM] for h in range(hp)]
    selbs = [sel_ref[h].astype(F32) for h in range(hp)]
    cols = [h * HEAD_DIM for h in range(hp)]
    outs = _sparse_flash(qs, selbs, k_ref, v_ref, cols, pat_ref, [d0_ref[0, h] for h in range(hp)],
                         [d1_ref[0, h] for h in range(hp)], i, 1, flash_scratch)
    for h in range(hp):
        o_ref[:, h * HEAD_DIM:(h + 1) * HEAD_DIM] = outs[h].astype(o_ref.dtype)


def _moba(zb, selb, near, pat):
    hp = HP_MOBA
    w = hp * HEAD_DIM
    return pl.pallas_call(
        _moba_kernel,
        grid=(MOBA_HEADS // hp, N_QT),
        in_specs=[pl.BlockSpec((TQ, w), lambda h, i: (i, COL_MQ // hp + h)),
                  pl.BlockSpec((hp, TQ, HEAD_DIM), lambda h, i: (h, i, 0)),
                  pl.BlockSpec((SEQ, w), lambda h, i: (0, COL_MK // hp + h), pipeline_mode=pl.Buffered(1)),
                  pl.BlockSpec((SEQ, w), lambda h, i: (0, COL_MV // hp + h), pipeline_mode=pl.Buffered(1)),
                  pl.BlockSpec((SEQ, HEAD_DIM), lambda h, i: (0, 0), pipeline_mode=pl.Buffered(1)),
                  pl.BlockSpec((1, hp, TQ, TQ), lambda h, i: (0, h, 0, 0)),
                  pl.BlockSpec((1, hp, TQ, TQ), lambda h, i: (1, h, 0, 0))],
        out_specs=pl.BlockSpec((TQ, w), lambda h, i: (i, h)),
        out_shape=jax.ShapeDtypeStruct((SEQ, MOBA_HEADS * HEAD_DIM), BF16),
        scratch_shapes=_flash_scratch(hp, True),
        compiler_params=_cparams("parallel", "arbitrary", vmem=BIG_VMEM_LIMIT),
        name="moba",
    )(zb, selb, zb, zb, pat, near, near)


def _cmp_kernel(q_ref, kc_ref, vct_ref, ovt_ref, cb_ref, oc_ref, sel_ref, s_ref):
    i = pl.program_id(0)
    win0 = pl.multiple_of(jnp.maximum(i * CMP_STRIDE - CMP_STRIDE, 0), CMP_STRIDE)

    def body(nk):
        nb = nk * CMP_STRIDE // SLC_LEN
        blk = lax.broadcasted_iota(jnp.int32, (nb, TQ), 0)
        cur = i * SLC_PER_TILE + lax.shift_right_logical(lax.broadcasted_iota(jnp.int32, (nb, TQ), 1), 6)
        allowed = blk <= cur
        forced = (blk == 0) | (blk == cur) | (blk == cur - 1)
        reachable = lax.broadcasted_iota(jnp.int32, (nk, TQ), 0) < win0 + CMP_WIN
        for g in range(NSA_KV_HEADS):
            p_sum = jnp.zeros((nk, TQ), F32)
            for r in range(NSA_GROUP):
                hd = g * NSA_GROUP + r
                q = q_ref[:, hd * HEAD_DIM:(hd + 1) * HEAD_DIM]
                s_ref[:nk, :] = lax.dot_general(kc_ref[g, :nk, :], q, _NT, preferred_element_type=F32)
                s_ref[pl.ds(win0, CMP_WIN), :] += cb_ref[0, hd]
                s = jnp.where(reachable, s_ref[:nk, :], NEG)
                valid = s > 0.5 * NEG
                e = jnp.where(valid, jnp.exp(s - jnp.max(s, axis=0, keepdims=True)), 0.0)
                p = e * (1.0 / jnp.maximum(jnp.sum(e, axis=0, keepdims=True), TINY))
                o_t = jnp.dot(vct_ref[g, :, :nk], p.astype(BF16), preferred_element_type=F32)
                oc_ref[:, hd * HEAD_DIM:(hd + 1) * HEAD_DIM] = o_t.T
                p_sum = p_sum + p
            imp = jnp.zeros((nb, TQ), F32)
            for part in _split3(p_sum):
                imp = imp + jnp.dot(ovt_ref[:nb, :nk], part, preferred_element_type=F32)
            picked = _topk_bias_t(jnp.where(allowed & ~forced, imp, -1.0), SLC_TOPK - 3)
            selb = jnp.where(allowed, jnp.where(forced, 0.0, picked), NEG)
            if nb < N_SLC:
                selb = jnp.concatenate([selb, jnp.full((N_SLC - nb, TQ), NEG, F32)], axis=0)
            sel_ref[g] = selb.T.astype(sel_ref.dtype)

    for c in range(1, 512 // HEAD_DIM + 1):
        @pl.when(lax.shift_right_logical(i, 3) == c - 1)
        def _():
            body(c * HEAD_DIM)


def _cmp_select(zb, kc, vct, ovt, cmpb):
    w = NSA_HEADS * HEAD_DIM
    return pl.pallas_call(
        _cmp_kernel,
        grid=(N_QT,),
        in_specs=[pl.BlockSpec((TQ, w), lambda i: (i, COL_NQ // NSA_HEADS)),
                  pl.BlockSpec((NSA_KV_HEADS, 512, HEAD_DIM), lambda i: (0, 0, 0)),
                  pl.BlockSpec((NSA_KV_HEADS, HEAD_DIM, 512), lambda i: (0, 0, 0)),
                  pl.BlockSpec((N_SLC, 512), lambda i: (0, 0)),
                  pl.BlockSpec((1, NSA_HEADS, CMP_WIN, TQ), lambda i: (jnp.minimum(i, 1), 1, 0, 0))],
        out_specs=[pl.BlockSpec((TQ, w), lambda i: (i, 0)),
                   pl.BlockSpec((NSA_KV_HEADS, TQ, N_SLC), lambda i: (0, i, 0))],
        out_shape=[jax.ShapeDtypeStruct((SEQ, w), F32),
                   jax.ShapeDtypeStruct((NSA_KV_HEADS, SEQ, N_SLC), BF16)],
        scratch_shapes=[pltpu.VMEM((512, TQ), F32)],
        compiler_params=_cparams("arbitrary"),
        name="cmp_select",
    )(zb, kc, vct, ovt, cmpb)


def _slc_kernel(q_ref, sel_ref, k_ref, v_ref, pat_ref, d0_ref, d1_ref, o_ref, *flash_scratch):
    i = pl.program_id(1)
    hp = HP_SLC
    selb = sel_ref[0].astype(F32)
    qs = [q_ref[:, h * HEAD_DIM:(h + 1) * HEAD_DIM] for h in range(hp)]
    outs = _sparse_flash(qs, [selb] * hp, k_ref, v_ref, [0] * hp, pat_ref,
                         [d0_ref[0, h] for h in range(hp)], [d1_ref[0, h] for h in range(hp)],
                         i, SLC_PER_TILE, flash_scratch)
    for h in range(hp):
        o_ref[:, h * HEAD_DIM:(h + 1) * HEAD_DIM] = outs[h].astype(o_ref.dtype)


def _slc(zb, selb, near, pat):
    hp = HP_SLC
    w = hp * HEAD_DIM
    per_group = NSA_GROUP // hp
    return pl.pallas_call(
        _slc_kernel,
        grid=(NSA_HEADS // hp, N_QT),
        in_specs=[pl.BlockSpec((TQ, w), lambda h, i: (i, COL_NQ // hp + h)),
                  pl.BlockSpec((1, TQ, N_SLC), lambda h, i: (h // per_group, i, 0)),
                  pl.BlockSpec((SEQ, HEAD_DIM), lambda h, i: (0, COL_SK + h // per_group),
                               pipeline_mode=pl.Buffered(1)),
                  pl.BlockSpec((SEQ, HEAD_DIM), lambda h, i: (0, COL_SV + h // per_group),
                               pipeline_mode=pl.Buffered(1)),
                  pl.BlockSpec((SEQ, HEAD_DIM), lambda h, i: (0, 0), pipeline_mode=pl.Buffered(1)),
                  pl.BlockSpec((1, hp, TQ, TQ), lambda h, i: (0, MOBA_HEADS // hp + h, 0, 0)),
                  pl.BlockSpec((1, hp, TQ, TQ), lambda h, i: (1, MOBA_HEADS // hp + h, 0, 0))],
        out_specs=pl.BlockSpec((TQ, w), lambda h, i: (i, h)),
        out_shape=jax.ShapeDtypeStruct((SEQ, NSA_HEADS * HEAD_DIM), F32),
        scratch_shapes=_flash_scratch(hp, False),
        compiler_params=_cparams("parallel", "arbitrary", vmem=BIG_VMEM_LIMIT),
        name="slc",
    )(zb, selb, zb, zb, pat, near, near)


def _win_kernel(q_ref, k_ref, v_ref, d0_ref, d1_ref, w2_ref, o_ref):
    i = pl.program_id(0)
    offs = [pl.multiple_of(jnp.maximum(i - b, 0) * TQ, TQ) for b in range(3)]
    mask2 = jnp.where(i >= 2, w2_ref[...], NEG)
    ones = jnp.ones((3 * TQ, HEAD_DIM), BF16)
    for g in range(NSA_KV_HEADS):
        cols = slice(g * HEAD_DIM, (g + 1) * HEAD_DIM)
        k_all = jnp.concatenate([k_ref[pl.ds(off, TQ), cols] for off in offs], axis=0)
        v_aug = jnp.concatenate([jnp.concatenate([v_ref[pl.ds(off, TQ), cols] for off in offs], axis=0), ones],
                                axis=1)
        for r in range(NSA_GROUP):
            hd = g * NSA_GROUP + r
            q = q_ref[:, hd * HEAD_DIM:(hd + 1) * HEAD_DIM]
            bias = jnp.concatenate([d0_ref[0, hd], jnp.where(i >= 1, d1_ref[0, hd], NEG), mask2], axis=1)
            s = lax.dot_general(q, k_all, _NT, preferred_element_type=F32) + bias
            m = jnp.max(s, axis=1, keepdims=True)
            pv = jnp.dot(jnp.exp(s - m).astype(BF16), v_aug, preferred_element_type=F32)
            o_ref[:, hd * HEAD_DIM:(hd + 1) * HEAD_DIM] = pv[:, :HEAD_DIM] / pv[:, HEAD_DIM:]


def _win(zb, near, win2):
    w = NSA_HEADS * HEAD_DIM
    kv_w = NSA_KV_HEADS * HEAD_DIM
    return pl.pallas_call(
        _win_kernel,
        grid=(N_QT,),
        in_specs=[pl.BlockSpec((TQ, w), lambda i: (i, COL_NQ // NSA_HEADS)),
                  pl.BlockSpec((SEQ, kv_w), lambda i: (0, COL_WK // NSA_KV_HEADS), pipeline_mode=pl.Buffered(1)),
                  pl.BlockSpec((SEQ, kv_w), lambda i: (0, COL_WV // NSA_KV_HEADS), pipeline_mode=pl.Buffered(1)),
                  pl.BlockSpec((1, NSA_HEADS, TQ, TQ), lambda i: (0, 1, 0, 0), pipeline_mode=pl.Buffered(1)),
                  pl.BlockSpec((1, NSA_HEADS, TQ, TQ), lambda i: (1, 1, 0, 0), pipeline_mode=pl.Buffered(1)),
                  pl.BlockSpec((TQ, TQ), lambda i: (0, 0), pipeline_mode=pl.Buffered(1))],
        out_specs=pl.BlockSpec((TQ, w), lambda i: (i, 0)),
        out_shape=jax.ShapeDtypeStruct((SEQ, w), F32),
        compiler_params=_cparams("arbitrary"),
        name="win",
    )(zb, zb, zb, near, near, win2)


def _layer_norm(y, g, b):
    mu = jnp.mean(y, axis=-1, keepdims=True)
    yc = y - mu
    var = jnp.mean(yc * yc, axis=-1, keepdims=True)
    return yc * lax.rsqrt(var + LN_EPS) * g + b


ROW_CHUNK = 256


def _outproj_kernel(x_ref, om_ref, oc_ref, os_ref, ow_ref, g_ref, e_ref, w_ref, lg_ref, lb_ref,
                    h_ref, hb_ref):
    for c0 in range(0, x_ref.shape[0], ROW_CHUNK):
        rows = slice(c0, c0 + ROW_CHUNK)
        gate = g_ref[rows, :]
        hi = gate.astype(BF16)
        lo = (gate - hi.astype(F32)).astype(BF16)
        g2 = jnp.concatenate([hi, lo], axis=1)
        o_n = jnp.zeros((ROW_CHUNK, oc_ref.shape[1]), F32)
        for c, branch in enumerate((oc_ref, os_ref, ow_ref)):
            o_n = o_n + jnp.dot(g2, e_ref[c], preferred_element_type=F32) * branch[rows, :]
        o = jnp.concatenate([om_ref[rows, :], o_n.astype(BF16)], axis=1)
        a = jnp.dot(o, w_ref[...], preferred_element_type=F32)
        h = _layer_norm(DN_ALPHA * x_ref[rows, :] + a, lg_ref[...], lb_ref[...])
        h_ref[rows, :] = h
        hb_ref[rows, :] = h.astype(BF16)


def _resident(shape):
    return pl.BlockSpec(shape, lambda i: (0,) * len(shape), pipeline_mode=pl.Buffered(1))


def _outproj(x, o_m, o_c, o_s, o_w, gates, expand, w_out, ln_g, ln_b, tm=512):
    half = NSA_HEADS * HEAD_DIM
    row = lambda i: (i, 0)
    return pl.pallas_call(
        _outproj_kernel,
        grid=(SEQ // tm,),
        in_specs=[pl.BlockSpec((tm, D_MODEL), row),
                  pl.BlockSpec((tm, half), row), pl.BlockSpec((tm, half), row),
                  pl.BlockSpec((tm, half), row), pl.BlockSpec((tm, half), row),
                  pl.BlockSpec((tm, HEAD_DIM), row),
                  _resident((3, 2 * HEAD_DIM, half)),
                  _resident((D_MODEL, D_MODEL)),
                  _resident((1, D_MODEL)), _resident((1, D_MODEL))],
        out_specs=[pl.BlockSpec((tm, D_MODEL), row), pl.BlockSpec((tm, D_MODEL), row)],
        out_shape=[jax.ShapeDtypeStruct((SEQ, D_MODEL), F32),
                   jax.ShapeDtypeStruct((SEQ, D_MODEL), BF16)],
        compiler_params=_cparams("parallel", vmem=BIG_VMEM_LIMIT),
        name="outproj_ln",
    )(x, o_m, o_c, o_s, o_w, gates, expand, w_out, ln_g, ln_b)


def _ffn_kernel(hb_ref, w1_ref, w2_ref, f_ref):
    @pl.when(pl.program_id(1) == 0)
    def _():
        f_ref[...] = jnp.zeros(f_ref.shape, F32)

    u = jnp.maximum(jnp.dot(hb_ref[...], w1_ref[...].astype(BF16), preferred_element_type=F32), 0.0)
    f_ref[...] += jnp.dot((u * u).astype(BF16), w2_ref[...].astype(BF16), preferred_element_type=F32)


def _ffn(hb, w1, w2, tm=1024, tf=512):
    return pl.pallas_call(
        _ffn_kernel,
        grid=(SEQ // tm, D_FF // tf),
        in_specs=[pl.BlockSpec((tm, D_MODEL), lambda i, c: (i, 0)),
                  pl.BlockSpec((D_MODEL, tf), lambda i, c: (0, c)),
                  pl.BlockSpec((tf, D_MODEL), lambda i, c: (c, 0))],
        out_specs=pl.BlockSpec((tm, D_MODEL), lambda i, c: (i, 0)),
        out_shape=jax.ShapeDtypeStruct((SEQ, D_MODEL), F32),
        compiler_params=_cparams("parallel", "arbitrary", vmem=BIG_VMEM_LIMIT),
        name="ffn",
    )(hb, w1, w2)


def _final_kernel(h_ref, hb_ref, f_ref, p_ref, wg_ref, wp_ref, lg_ref, lb_ref, o_ref):
    for c0 in range(0, h_ref.shape[0], ROW_CHUNK):
        rows = slice(c0, c0 + ROW_CHUNK)
        gate = jax.nn.sigmoid(jnp.dot(hb_ref[rows, :], wg_ref[...], preferred_element_type=F32))
        emb = jnp.dot(p_ref[rows, :].astype(BF16), wp_ref[...], preferred_element_type=F32)
        y = DN_ALPHA * h_ref[rows, :] + f_ref[rows, :] + gate * emb
        o_ref[rows, :] = _layer_norm(y, lg_ref[...], lb_ref[...])


def _final(h, hb, f, p, wg, wp, ln_g, ln_b, tm=512):
    row = lambda i: (i, 0)
    return pl.pallas_call(
        _final_kernel,
        grid=(SEQ // tm,),
        in_specs=[pl.BlockSpec((tm, D_MODEL), row), pl.BlockSpec((tm, D_MODEL), row),
                  pl.BlockSpec((tm, D_MODEL), row), pl.BlockSpec((tm, D_PLE), row),
                  _resident((D_MODEL, D_MODEL)), _resident((D_PLE, D_MODEL)),
                  _resident((1, D_MODEL)), _resident((1, D_MODEL))],
        out_specs=pl.BlockSpec((tm, D_MODEL), row),
        out_shape=jax.ShapeDtypeStruct((SEQ, D_MODEL), F32),
        compiler_params=_cparams("parallel"),
        name="final_ln",
    )(h, hb, f, p, wg, wp, ln_g, ln_b)


def _overlap_matrix_t():
    j = np.arange(N_SLC)[:, None]
    n = np.arange(512)[None, :]
    ov = ((n * CMP_STRIDE < j * SLC_LEN + SLC_LEN) & (n * CMP_STRIDE + CMP_LEN > j * SLC_LEN) & (n < N_CMP))
    return ov.astype(np.float32)


def _gate_expand():
    e = np.zeros((3, 2 * HEAD_DIM, NSA_HEADS * HEAD_DIM), np.float32)
    for c in range(3):
        for h in range(NSA_HEADS):
            e[c, 3 * h + c, h * HEAD_DIM:(h + 1) * HEAD_DIM] = 1.0
            e[c, HEAD_DIM + 3 * h + c, h * HEAD_DIM:(h + 1) * HEAD_DIM] = 1.0
    return e


def _layer(h, p, w_in, cmp_pe_k, cmp_w1_k, cmp_w2_k, cmp_pe_v, cmp_w1_v, cmp_w2_v, tiles, w_out,
           ln1_g, ln1_b, w_ff1, w_ff2, w_ple, w_ple_gate, ln2_g, ln2_b):
    near, win2, cmpb = tiles
    scale = HEAD_DIM ** -0.5
    colscale = np.ones((1, GATE_COL0), np.float32)
    colscale[0, COL_MQ * HEAD_DIM:(COL_MQ + MOBA_HEADS) * HEAD_DIM] = scale
    colscale[0, COL_NQ * HEAD_DIM:(COL_NQ + NSA_HEADS) * HEAD_DIM] = scale
    w_in_t = w_in.T
    w_gl_t = jnp.pad(w_in_t[GATE_COL0:], ((0, HEAD_DIM - NSA_HEADS * 3), (0, 0))).astype(BF16)
    zb, gates, ckv = _inproj(h, w_in_t, jnp.asarray(colscale), w_gl_t)

    half = CMP_STRIDE * HEAD_DIM
    xr = ckv.reshape(4, SEQ // CMP_STRIDE, half)
    kc, vct = _compress(xr, cmp_pe_k.reshape(2, 1, half), cmp_w1_k.reshape(2, half, HEAD_DIM), cmp_w2_k,
                        cmp_pe_v.reshape(2, 1, half), cmp_w1_v.reshape(2, half, HEAD_DIM), cmp_w2_v)

    key = np.arange(SEQ)[:, None]
    lane = np.arange(HEAD_DIM)[None, :]
    pat_moba = jnp.asarray(lane == key // MOBA_BLOCK, BF16)
    pat_slc = jnp.asarray(lane == key // SLC_LEN, BF16)
    o_m = _moba(zb, _moba_select(zb), near, pat_moba)
    o_c, selb = _cmp_select(zb, kc, vct, jnp.asarray(_overlap_matrix_t(), BF16), cmpb)
    o_s = _slc(zb, selb, near, pat_slc)
    o_w = _win(zb, near, win2)

    h1, h1b = _outproj(h, o_m, o_c, o_s, o_w, gates, jnp.asarray(_gate_expand(), BF16),
                       w_out.astype(BF16), ln1_g.reshape(1, -1), ln1_b.reshape(1, -1))
    f = _ffn(h1b, w_ff1, w_ff2)
    return _final(h1, h1b, f, p, w_ple_gate.astype(BF16), w_ple.astype(BF16),
                  ln2_g.reshape(1, -1), ln2_b.reshape(1, -1))


def kernel(x, p, w_in, cmp_pe_k, cmp_w1_k, cmp_w2_k, cmp_pe_v, cmp_w1_v, cmp_w2_v, rel_bias, w_out,
           ln1_g, ln1_b, w_ff1, w_ff2, w_ple, w_ple_gate, ln2_g, ln2_b):
    tiles = _bias_tiles(rel_bias)
    h = x[0]
    for i in range(w_in.shape[0]):
        h = _layer(h, p[i, 0], w_in[i], cmp_pe_k[i], cmp_w1_k[i], cmp_w2_k[i], cmp_pe_v[i], cmp_w1_v[i],
                   cmp_w2_v[i], tiles, w_out[i], ln1_g[i], ln1_b[i], w_ff1[i], w_ff2[i], w_ple[i],
                   w_ple_gate[i], ln2_g[i], ln2_b[i])
    return h[None]
```

```python
import functools
import math

import numpy as np
import jax
import jax.numpy as jnp
from jax import lax
from jax.experimental import pallas as pl
from jax.experimental.pallas import tpu as pltpu

D_MODEL = 2048
SEQ = 8192
HEAD_DIM = 128
N_HEADS = 16
MOBA_HEADS = 8
NSA_HEADS = 8
NSA_KV_HEADS = 2
NSA_GROUP = 4
MOBA_BLOCK = 256
MOBA_TOPK = 3
CMP_LEN = 32
CMP_STRIDE = 16
SLC_LEN = 64
SLC_TOPK = 16
WINDOW = 512
N_BUCKETS = 32
MAX_DISTANCE = 128
D_FF = 4 * D_MODEL
D_PLE = 256
LN_EPS = 1e-5
DN_ALPHA = 2.0 ** 0.25
NEG = -1e30
LOG2E = math.log2(math.e)
TINY = 1e-30
FORCE_SCORE = 1e4

N_CMP = (SEQ - CMP_LEN) // CMP_STRIDE + 1
N_SLC = SEQ // SLC_LEN
TQ = 256
N_QT = SEQ // TQ
CMP_WIN = 32
SLC_PER_TILE = TQ // SLC_LEN
HP_MOBA = 4
HP_SLC = 4
KG = 4
GROUP = KG * TQ
COL_MQ, COL_MK, COL_MV, COL_NQ = 0, 8, 16, 24
COL_CK, COL_CV, COL_SK, COL_SV, COL_WK, COL_WV = 32, 34, 36, 38, 40, 42
N_COLBLK = 44
GATE_COL0 = N_COLBLK * HEAD_DIM

VMEM_LIMIT = 48 * 1024 * 1024
BIG_VMEM_LIMIT = 56 * 1024 * 1024

BF16 = jnp.bfloat16
F32 = jnp.float32
_NT = (((1,), (1,)), ((), ()))


def _cparams(*sem, vmem=VMEM_LIMIT):
    return pltpu.CompilerParams(dimension_semantics=sem, vmem_limit_bytes=vmem)


def _bucket_table():
    n = np.arange(1024, dtype=np.int32)
    max_exact = N_BUCKETS // 2
    ratio = np.maximum(n, 1).astype(np.float32) / np.float32(max_exact)
    large = max_exact + (np.log(ratio).astype(np.float32) / np.float32(math.log(MAX_DISTANCE / max_exact))
                         * np.float32(N_BUCKETS - max_exact)).astype(np.int32)
    large = np.minimum(large, N_BUCKETS - 1)
    return np.where(n < max_exact, n, large).astype(np.int32)


def _bias_kernel(tab_ref, near_ref, cmpb_ref, *, steps):
    h = pl.program_id(0)
    far = tab_ref[h, N_BUCKETS - 1]

    def lookup(d):
        val = jnp.full(d.shape, tab_ref[h, 0], F32)
        for start, bucket in steps:
            val = jnp.where(d >= start, tab_ref[h, bucket], val)
        return jnp.where(d >= 0, (val - far) * LOG2E, NEG)

    a = lax.broadcasted_iota(jnp.int32, (TQ, TQ), 0)
    b = lax.broadcasted_iota(jnp.int32, (TQ, TQ), 1)
    near_ref[0, 0] = lookup(a - b)
    near_ref[1, 0] = lookup(a - b + TQ)
    w = lax.broadcasted_iota(jnp.int32, (CMP_WIN, TQ), 0)
    aw = lax.broadcasted_iota(jnp.int32, (CMP_WIN, TQ), 1)
    cmpb_ref[0, 0] = lookup(aw - CMP_STRIDE * w - (CMP_LEN - 1))
    cmpb_ref[1, 0] = lookup(aw - CMP_STRIDE * w + (TQ - CMP_LEN + 1))


def _bias_tiles(rel_bias):
    bucket = _bucket_table()
    steps = tuple((int(n), int(bucket[n])) for n in range(1, bucket.size) if bucket[n] != bucket[n - 1])
    near, cmpb = pl.pallas_call(
        functools.partial(_bias_kernel, steps=steps),
        grid=(N_HEADS,),
        in_specs=[pl.BlockSpec(memory_space=pltpu.SMEM)],
        out_specs=[pl.BlockSpec((2, 1, TQ, TQ), lambda h: (0, h, 0, 0)),
                   pl.BlockSpec((2, 1, CMP_WIN, TQ), lambda h: (0, h, 0, 0))],
        out_shape=[jax.ShapeDtypeStruct((2, N_HEADS, TQ, TQ), F32),
                   jax.ShapeDtypeStruct((2, N_HEADS, CMP_WIN, TQ), F32)],
        compiler_params=_cparams("parallel"),
        name="bias_tiles",
    )(rel_bias.T.astype(F32))
    a = np.arange(TQ)[:, None]
    b = np.arange(TQ)[None, :]
    win2 = jnp.asarray(np.where(a < b, 0.0, NEG).astype(np.float32))
    return near, win2, cmpb


INPROJ_TN = 4 * HEAD_DIM
CKV_TILE = COL_CK * HEAD_DIM // INPROJ_TN


def _inproj_kernel(x_ref, w_ref, cs_ref, wg_ref, o_ref, g_ref, ckv_ref, xb_ref):
    j = pl.program_id(1)

    @pl.when(j == 0)
    def _():
        xb = x_ref[...].astype(BF16)
        xb_ref[...] = xb
        g_ref[...] = jax.nn.sigmoid(lax.dot_general(xb, wg_ref[...], _NT, preferred_element_type=F32))

    acc = lax.dot_general(xb_ref[...], w_ref[...].astype(BF16), _NT, preferred_element_type=F32)
    z = (acc * cs_ref[...]).astype(o_ref.dtype)
    o_ref[...] = z

    @pl.when(j == CKV_TILE)
    def _():
        for c in range(INPROJ_TN // HEAD_DIM):
            ckv_ref[c] = z[:, c * HEAD_DIM:(c + 1) * HEAD_DIM]


def _inproj(x, w_in_t, colscale, w_gate_t, tm=1024):
    m, k = x.shape
    tn = INPROJ_TN
    return pl.pallas_call(
        _inproj_kernel,
        grid=(m // tm, GATE_COL0 // tn),
        in_specs=[pl.BlockSpec((tm, k), lambda i, j: (i, 0)),
                  pl.BlockSpec((tn, k), lambda i, j: (j, 0)),
                  pl.BlockSpec((1, tn), lambda i, j: (0, j)),
                  pl.BlockSpec((HEAD_DIM, k), lambda i, j: (0, 0))],
        out_specs=[pl.BlockSpec((tm, tn), lambda i, j: (i, j)),
                   pl.BlockSpec((tm, HEAD_DIM), lambda i, j: (i, 0)),
                   pl.BlockSpec((tn // HEAD_DIM, tm, HEAD_DIM), lambda i, j: (0, i, 0))],
        out_shape=[jax.ShapeDtypeStruct((m, GATE_COL0), BF16),
                   jax.ShapeDtypeStruct((m, HEAD_DIM), F32),
                   jax.ShapeDtypeStruct((tn // HEAD_DIM, m, HEAD_DIM), BF16)],
        scratch_shapes=[pltpu.VMEM((tm, k), BF16)],
        compiler_params=_cparams("parallel", "arbitrary"),
        name="inproj",
    )(x, w_in_t, colscale, w_gate_t)


def _gelu_tanh(x):
    return 0.5 * x * (1.0 + jnp.tanh(math.sqrt(2.0 / math.pi) * (x + 0.044715 * (x * x * x))))


def _compress_kernel(xk_ref, xv_ref, pek_ref, w1k_ref, w2k_ref, pev_ref, w1v_ref, w2v_ref, kc_ref, vct_ref):
    def compress(x_ref, pe_ref, w1_ref, w2_ref):
        x = x_ref[0].astype(F32)
        lo = jnp.dot((x + pe_ref[0]).astype(BF16), w1_ref[0].astype(BF16), preferred_element_type=F32)
        hi = jnp.dot((x + pe_ref[1]).astype(BF16), w1_ref[1].astype(BF16), preferred_element_type=F32)
        y = lo + pltpu.roll(hi, N_CMP, 0)
        out = jnp.dot(_gelu_tanh(y).astype(BF16), w2_ref[...].astype(BF16), preferred_element_type=F32)
        row = lax.broadcasted_iota(jnp.int32, out.shape, 0)
        return jnp.where(row < N_CMP, out, 0.0)

    kc_ref[0] = compress(xk_ref, pek_ref, w1k_ref, w2k_ref).astype(kc_ref.dtype)
    vct_ref[0] = compress(xv_ref, pev_ref, w1v_ref, w2v_ref).T.astype(vct_ref.dtype)


def _compress(xr, pe_k, w1_k, w2_k, pe_v, w1_v, w2_v):
    half = CMP_STRIDE * HEAD_DIM
    weights = [pl.BlockSpec((2, 1, half), lambda g: (0, 0, 0)),
               pl.BlockSpec((2, half, HEAD_DIM), lambda g: (0, 0, 0)),
               pl.BlockSpec((HEAD_DIM, HEAD_DIM), lambda g: (0, 0))]
    return pl.pallas_call(
        _compress_kernel,
        grid=(NSA_KV_HEADS,),
        in_specs=[pl.BlockSpec((1, 512, half), lambda g: (g, 0, 0)),
                  pl.BlockSpec((1, 512, half), lambda g: (NSA_KV_HEADS + g, 0, 0))] + weights + weights,
        out_specs=[pl.BlockSpec((1, 512, HEAD_DIM), lambda g: (g, 0, 0)),
                   pl.BlockSpec((1, HEAD_DIM, 512), lambda g: (g, 0, 0))],
        out_shape=[jax.ShapeDtypeStruct((NSA_KV_HEADS, 512, HEAD_DIM), BF16),
                   jax.ShapeDtypeStruct((NSA_KV_HEADS, HEAD_DIM, 512), BF16)],
        compiler_params=_cparams("parallel"),
        name="compress",
    )(xr, xr, pe_k, w1_k, w2_k, pe_v, w1_v, w2_v)


def _split3(x):
    hi = x.astype(BF16)
    r1 = x - hi.astype(F32)
    mid = r1.astype(BF16)
    lo = (r1 - mid.astype(F32)).astype(BF16)
    return hi, mid, lo


def _topk_bias_t(score, k):
    row = lax.broadcasted_iota(jnp.int32, score.shape, 0).astype(F32)
    selb = jnp.full(score.shape, NEG, F32)
    s = score
    for _ in range(k):
        mx = jnp.max(s, axis=0, keepdims=True)
        idx = jnp.min(jnp.where(s == mx, row, 1e9), axis=0, keepdims=True)
        hit = row == idx
        selb = jnp.where(hit, 0.0, selb)
        s = jnp.where(hit, -3e38, s)
    return selb


def _lane_tile_max(s):
    m = s[:, :HEAD_DIM]
    for c in range(1, s.shape[1] // HEAD_DIM):
        m = jnp.maximum(m, s[:, c * HEAD_DIM:(c + 1) * HEAD_DIM])
    return m


def _sparse_flash(qs, selbs, k_ref, v_ref, kcols, pat_ref, d0s, d1s, i, sub, scratch):
    t_ref, *maybe_r_ref, mx_ref, acc_ref = scratch
    r_ref = maybe_r_ref[0] if maybe_r_ref else None
    heads = range(len(qs))
    lane = lax.broadcasted_iota(jnp.int32, (TQ, HEAD_DIM), 1)
    far = lane < (i - 1) * sub
    q_near = [jnp.concatenate([qs[h], selbs[h].astype(BF16)], axis=1) for h in heads]
    q_far = [jnp.concatenate([qs[h], jnp.where(far, selbs[h], NEG).astype(BF16)], axis=1) for h in heads]

    def k_aug(h, off, n):
        c = kcols[h]
        return jnp.concatenate([k_ref[pl.ds(off, n), c:c + HEAD_DIM], pat_ref[pl.ds(off, n), :]], axis=1)

    def v_aug(h, off, n):
        c = kcols[h]
        return jnp.concatenate([v_ref[pl.ds(off, n), c:c + HEAD_DIM], jnp.ones((n, HEAD_DIM), BF16)], axis=1)

    for h in heads:
        mx_ref[h] = jnp.full((TQ, HEAD_DIM), NEG, F32)

    def far_group(g, carry):
        off = pl.multiple_of(g * GROUP, GROUP)
        for h in heads:
            s = lax.dot_general(q_far[h], k_aug(h, off, GROUP), _NT, preferred_element_type=F32)
            ref = _lane_tile_max(s)
            if r_ref is not None:
                s = s - jnp.concatenate([ref] * (GROUP // HEAD_DIM), axis=1)
                r_ref[h, g] = ref
            for u in range(KG):
                t_ref[h, g * KG + u] = s[:, u * TQ:(u + 1) * TQ].astype(t_ref.dtype)
            mx_ref[h] = jnp.maximum(mx_ref[h], ref)
        return carry

    n_far = lax.shift_right_logical(jnp.maximum(i - 1, 0) + (KG - 1), KG.bit_length() - 1)
    lax.fori_loop(0, n_far, far_group, 0)

    off_prev = pl.multiple_of(jnp.maximum(i - 1, 0) * TQ, TQ)
    off_own = pl.multiple_of(i * TQ, TQ)
    p_near = []
    for h in heads:
        k2 = jnp.concatenate([k_aug(h, off_prev, TQ), k_aug(h, off_own, TQ)], axis=0)
        bias = jnp.concatenate([jnp.where(i >= 1, d1s[h], NEG), d0s[h]], axis=1)
        s = lax.dot_general(q_near[h], k2, _NT, preferred_element_type=F32) + bias
        m = jnp.max(jnp.maximum(mx_ref[h], _lane_tile_max(s)), axis=1, keepdims=True)
        mx_ref[h] = jnp.broadcast_to(m, (TQ, HEAD_DIM))
        p_near.append(jnp.exp2(s - m).astype(BF16))
    for h in heads:
        v2 = jnp.concatenate([v_aug(h, off_prev, TQ), v_aug(h, off_own, TQ)], axis=0)
        acc_ref[h] = jnp.dot(p_near[h], v2, preferred_element_type=F32)

    def pv_group(g, carry):
        off = pl.multiple_of(g * GROUP, GROUP)
        for h in heads:
            d = -mx_ref[h] if r_ref is None else r_ref[h, g] - mx_ref[h]
            d2 = jnp.concatenate([d, d], axis=1)
            p = jnp.concatenate([jnp.exp2(t_ref[h, g * KG + u].astype(F32) + d2).astype(BF16)
                                 for u in range(KG)], axis=1)
            acc_ref[h] += jnp.dot(p, v_aug(h, off, GROUP), preferred_element_type=F32)
        return carry

    lax.fori_loop(0, n_far, pv_group, 0)
    outs = []
    for h in heads:
        acc = acc_ref[h]
        outs.append(acc[:, :HEAD_DIM] / acc[:, HEAD_DIM:])
    return outs


def _flash_scratch(hp, relative_bf16):
    logits = [pltpu.VMEM((hp, N_QT, TQ, TQ), BF16),
              pltpu.VMEM((hp, SEQ // GROUP, TQ, HEAD_DIM), F32)]
    if not relative_bf16:
        logits = [pltpu.VMEM((hp, N_QT, TQ, TQ), F32)]
    return logits + [pltpu.VMEM((hp, TQ, HEAD_DIM), F32),
                     pltpu.VMEM((hp, TQ, 2 * HEAD_DIM), F32)]


SEL_CHUNK = 4096


def _moba_select_kernel(q_ref, k_ref, sel_ref):
    blk = lax.broadcasted_iota(jnp.int32, (HEAD_DIM, SEQ), 0)
    pos = lax.broadcasted_iota(jnp.int32, (HEAD_DIM, SEQ), 1)
    ind = jnp.where(lax.shift_right_logical(pos, 8) == blk, 1.0 / MOBA_BLOCK, 0.0).astype(BF16)
    n_blk = SEQ // MOBA_BLOCK
    kmean = _split3(jnp.dot(ind, k_ref[...], preferred_element_type=F32)[:n_blk])
    blk_t = lax.broadcasted_iota(jnp.int32, (n_blk, SEL_CHUNK), 0)
    col = lax.broadcasted_iota(jnp.int32, (n_blk, SEL_CHUNK), 1)
    no_block = jnp.full((HEAD_DIM - n_blk, SEL_CHUNK), NEG, F32)

    def chunk(c, carry):
        off = pl.multiple_of(c * SEL_CHUNK, SEL_CHUNK)
        q = q_ref[pl.ds(off, SEL_CHUNK), :]
        gate = jnp.zeros((n_blk, SEL_CHUNK), F32)
        for part in kmean:
            gate = gate + lax.dot_general(part, q, _NT, preferred_element_type=F32)
        own = lax.shift_right_logical(off + col, 8)
        past = blk_t < own
        selb = _topk_bias_t(jnp.where(past, gate, NEG), MOBA_TOPK)
        selb = jnp.where(past, selb, jnp.where(blk_t == own, 0.0, NEG))
        sel_ref[0, pl.ds(off, SEL_CHUNK), :] = jnp.concatenate([selb, no_block], axis=0).T.astype(sel_ref.dtype)
        return carry

    lax.fori_loop(0, SEQ // SEL_CHUNK, chunk, 0)


def _moba_select(zb):
    return pl.pallas_call(
        _moba_select_kernel,
        grid=(MOBA_HEADS,),
        in_specs=[pl.BlockSpec((SEQ, HEAD_DIM), lambda h: (0, COL_MQ + h)),
                  pl.BlockSpec((SEQ, HEAD_DIM), lambda h: (0, COL_MK + h))],
        out_specs=pl.BlockSpec((1, SEQ, HEAD_DIM), lambda h: (h, 0, 0)),
        out_shape=jax.ShapeDtypeStruct((MOBA_HEADS, SEQ, HEAD_DIM), BF16),
        compiler_params=_cparams("parallel"),
        name="moba_select",
    )(zb, zb)


def _moba_kernel(q_ref, sel_ref, k_ref, v_ref, pat_ref, d0_ref, d1_ref, o_ref, *flash_scratch):
    i = pl.program_id(1)
    hp = HP_MOBA
    qs = [q_ref[:, h * HEAD_DIM:(h + 1) * HEAD_DIM] for h in range(hp)]
    selbs = [sel_ref[h].astype(F32) for h in range(hp)]
    cols = [h * HEAD_DIM for h in range(hp)]
    outs = _sparse_flash(qs, selbs, k_ref, v_ref, cols, pat_ref, [d0_ref[0, h] for h in range(hp)],
                         [d1_ref[0, h] for h in range(hp)], i, 1, flash_scratch)
    for h in range(hp):
        o_ref[:, h * HEAD_DIM:(h + 1) * HEAD_DIM] = outs[h].astype(o_ref.dtype)


def _moba(zb, selb, near, pat):
    hp = HP_MOBA
    w = hp * HEAD_DIM
    return pl.pallas_call(
        _moba_kernel,
        grid=(MOBA_HEADS // hp, N_QT),
        in_specs=[pl.BlockSpec((TQ, w), lambda h, i: (i, COL_MQ // hp + h)),
                  pl.BlockSpec((hp, TQ, HEAD_DIM), lambda h, i: (h, i, 0)),
                  pl.BlockSpec((SEQ, w), lambda h, i: (0, COL_MK // hp + h), pipeline_mode=pl.Buffered(1)),
                  pl.BlockSpec((SEQ, w), lambda h, i: (0, COL_MV // hp + h), pipeline_mode=pl.Buffered(1)),
                  pl.BlockSpec((SEQ, HEAD_DIM), lambda h, i: (0, 0), pipeline_mode=pl.Buffered(1)),
                  pl.BlockSpec((1, hp, TQ, TQ), lambda h, i: (0, h, 0, 0)),
                  pl.BlockSpec((1, hp, TQ, TQ), lambda h, i: (1, h, 0, 0))],
        out_specs=pl.BlockSpec((TQ, w), lambda h, i: (i, h)),
        out_shape=jax.ShapeDtypeStruct((SEQ, MOBA_HEADS * HEAD_DIM), BF16),
        scratch_shapes=_flash_scratch(hp, True),
        compiler_params=_cparams("parallel", "arbitrary", vmem=BIG_VMEM_LIMIT),
        name="moba",
    )(zb, selb, zb, zb, pat, near, near)


def _cmp_kernel(q_ref, kc_ref, vct_ref, ovt_ref, cb_ref, oc_ref, sel_ref, s_ref):
    i = pl.program_id(0)
    win0 = pl.multiple_of(jnp.maximum(i * CMP_STRIDE - CMP_STRIDE, 0), CMP_STRIDE)

    def body(nk):
        nb = nk * CMP_STRIDE // SLC_LEN
        blk = lax.broadcasted_iota(jnp.int32, (nb, TQ), 0)
        cur = i * SLC_PER_TILE + lax.shift_right_logical(lax.broadcasted_iota(jnp.int32, (nb, TQ), 1), 6)
        allowed = blk <= cur
        forced = (blk == 0) | (blk == cur) | (blk == cur - 1)
        reachable = lax.broadcasted_iota(jnp.int32, (nk, TQ), 0) < win0 + CMP_WIN
        for g in range(NSA_KV_HEADS):
            p_sum = jnp.zeros((nk, TQ), F32)
            for r in range(NSA_GROUP):
                hd = g * NSA_GROUP + r
                q = q_ref[:, hd * HEAD_DIM:(hd + 1) * HEAD_DIM]
                s_ref[:nk, :] = lax.dot_general(kc_ref[g, :nk, :], q, _NT, preferred_element_type=F32)
                s_ref[pl.ds(win0, CMP_WIN), :] += cb_ref[0, hd]
                s = jnp.where(reachable, s_ref[:nk, :], NEG)
                valid = s > 0.5 * NEG
                e = jnp.where(valid, jnp.exp2(s - jnp.max(s, axis=0, keepdims=True)), 0.0)
                p = e * (1.0 / jnp.maximum(jnp.sum(e, axis=0, keepdims=True), TINY))
                o_t = jnp.dot(vct_ref[g, :, :nk], p.astype(BF16), preferred_element_type=F32)
                oc_ref[:, hd * HEAD_DIM:(hd + 1) * HEAD_DIM] = o_t.T
                p_sum = p_sum + p
            imp = jnp.zeros((nb, TQ), F32)
            for part in _split3(p_sum):
                imp = imp + jnp.dot(ovt_ref[:nb, :nk], part, preferred_element_type=F32)
            picked = _topk_bias_t(jnp.where(allowed & ~forced, imp, -1.0), SLC_TOPK - 3)
            selb = jnp.where(allowed, jnp.where(forced, 0.0, picked), NEG)
            if nb < N_SLC:
                selb = jnp.concatenate([selb, jnp.full((N_SLC - nb, TQ), NEG, F32)], axis=0)
            sel_ref[g] = selb.T.astype(sel_ref.dtype)

    for c in range(1, 512 // HEAD_DIM + 1):
        @pl.when(lax.shift_right_logical(i, 3) == c - 1)
        def _():
            body(c * HEAD_DIM)


def _cmp_select(zb, kc, vct, ovt, cmpb):
    w = NSA_HEADS * HEAD_DIM
    return pl.pallas_call(
        _cmp_kernel,
        grid=(N_QT,),
        in_specs=[pl.BlockSpec((TQ, w), lambda i: (i, COL_NQ // NSA_HEADS)),
                  pl.BlockSpec((NSA_KV_HEADS, 512, HEAD_DIM), lambda i: (0, 0, 0)),
                  pl.BlockSpec((NSA_KV_HEADS, HEAD_DIM, 512), lambda i: (0, 0, 0)),
                  pl.BlockSpec((N_SLC, 512), lambda i: (0, 0)),
                  pl.BlockSpec((1, NSA_HEADS, CMP_WIN, TQ), lambda i: (jnp.minimum(i, 1), 1, 0, 0))],
        out_specs=[pl.BlockSpec((TQ, w), lambda i: (i, 0)),
                   pl.BlockSpec((NSA_KV_HEADS, TQ, N_SLC), lambda i: (0, i, 0))],
        out_shape=[jax.ShapeDtypeStruct((SEQ, w), F32),
                   jax.ShapeDtypeStruct((NSA_KV_HEADS, SEQ, N_SLC), BF16)],
        scratch_shapes=[pltpu.VMEM((512, TQ), F32)],
        compiler_params=_cparams("arbitrary"),
        name="cmp_select",
    )(zb, kc, vct, ovt, cmpb)


def _slc_kernel(q_ref, sel_ref, k_ref, v_ref, pat_ref, d0_ref, d1_ref, o_ref, *flash_scratch):
    i = pl.program_id(1)
    hp = HP_SLC
    selb = sel_ref[0].astype(F32)
    qs = [q_ref[:, h * HEAD_DIM:(h + 1) * HEAD_DIM] for h in range(hp)]
    outs = _sparse_flash(qs, [selb] * hp, k_ref, v_ref, [0] * hp, pat_ref,
                         [d0_ref[0, h] for h in range(hp)], [d1_ref[0, h] for h in range(hp)],
                         i, SLC_PER_TILE, flash_scratch)
    for h in range(hp):
        o_ref[:, h * HEAD_DIM:(h + 1) * HEAD_DIM] = outs[h].astype(o_ref.dtype)


def _slc(zb, selb, near, pat):
    hp = HP_SLC
    w = hp * HEAD_DIM
    per_group = NSA_GROUP // hp
    return pl.pallas_call(
        _slc_kernel,
        grid=(NSA_HEADS // hp, N_QT),
        in_specs=[pl.BlockSpec((TQ, w), lambda h, i: (i, COL_NQ // hp + h)),
                  pl.BlockSpec((1, TQ, N_SLC), lambda h, i: (h // per_group, i, 0)),
                  pl.BlockSpec((SEQ, HEAD_DIM), lambda h, i: (0, COL_SK + h // per_group),
                               pipeline_mode=pl.Buffered(1)),
                  pl.BlockSpec((SEQ, HEAD_DIM), lambda h, i: (0, COL_SV + h // per_group),
                               pipeline_mode=pl.Buffered(1)),
                  pl.BlockSpec((SEQ, HEAD_DIM), lambda h, i: (0, 0), pipeline_mode=pl.Buffered(1)),
                  pl.BlockSpec((1, hp, TQ, TQ), lambda h, i: (0, MOBA_HEADS // hp + h, 0, 0)),
                  pl.BlockSpec((1, hp, TQ, TQ), lambda h, i: (1, MOBA_HEADS // hp + h, 0, 0))],
        out_specs=pl.BlockSpec((TQ, w), lambda h, i: (i, h)),
        out_shape=jax.ShapeDtypeStruct((SEQ, NSA_HEADS * HEAD_DIM), F32),
        scratch_shapes=_flash_scratch(hp, False),
        compiler_params=_cparams("parallel", "arbitrary", vmem=BIG_VMEM_LIMIT),
        name="slc",
    )(zb, selb, zb, zb, pat, near, near)


def _win_kernel(q_ref, k_ref, v_ref, d0_ref, d1_ref, w2_ref, o_ref):
    i = pl.program_id(0)
    offs = [pl.multiple_of(jnp.maximum(i - b, 0) * TQ, TQ) for b in range(3)]
    mask2 = jnp.where(i >= 2, w2_ref[...], NEG)
    ones = jnp.ones((3 * TQ, HEAD_DIM), BF16)
    for g in range(NSA_KV_HEADS):
        cols = slice(g * HEAD_DIM, (g + 1) * HEAD_DIM)
        k_all = jnp.concatenate([k_ref[pl.ds(off, TQ), cols] for off in offs], axis=0)
        v_aug = jnp.concatenate([jnp.concatenate([v_ref[pl.ds(off, TQ), cols] for off in offs], axis=0), ones],
                                axis=1)
        for r in range(NSA_GROUP):
            hd = g * NSA_GROUP + r
            q = q_ref[:, hd * HEAD_DIM:(hd + 1) * HEAD_DIM]
            bias = jnp.concatenate([d0_ref[0, hd], jnp.where(i >= 1, d1_ref[0, hd], NEG), mask2], axis=1)
            s = lax.dot_general(q, k_all, _NT, preferred_element_type=F32) + bias
            m = jnp.max(s, axis=1, keepdims=True)
            pv = jnp.dot(jnp.exp2(s - m).astype(BF16), v_aug, preferred_element_type=F32)
            o_ref[:, hd * HEAD_DIM:(hd + 1) * HEAD_DIM] = pv[:, :HEAD_DIM] / pv[:, HEAD_DIM:]


def _win(zb, near, win2):
    w = NSA_HEADS * HEAD_DIM
    kv_w = NSA_KV_HEADS * HEAD_DIM
    return pl.pallas_call(
        _win_kernel,
        grid=(N_QT,),
        in_specs=[pl.BlockSpec((TQ, w), lambda i: (i, COL_NQ // NSA_HEADS)),
                  pl.BlockSpec((SEQ, kv_w), lambda i: (0, COL_WK // NSA_KV_HEADS), pipeline_mode=pl.Buffered(1)),
                  pl.BlockSpec((SEQ, kv_w), lambda i: (0, COL_WV // NSA_KV_HEADS), pipeline_mode=pl.Buffered(1)),
                  pl.BlockSpec((1, NSA_HEADS, TQ, TQ), lambda i: (0, 1, 0, 0), pipeline_mode=pl.Buffered(1)),
                  pl.BlockSpec((1, NSA_HEADS, TQ, TQ), lambda i: (1, 1, 0, 0), pipeline_mode=pl.Buffered(1)),
                  pl.BlockSpec((TQ, TQ), lambda i: (0, 0), pipeline_mode=pl.Buffered(1))],
        out_specs=pl.BlockSpec((TQ, w), lambda i: (i, 0)),
        out_shape=jax.ShapeDtypeStruct((SEQ, w), F32),
        compiler_params=_cparams("arbitrary"),
        name="win",
    )(zb, zb, zb, near, near, win2)


def _layer_norm(y, g, b):
    mu = jnp.mean(y, axis=-1, keepdims=True)
    yc = y - mu
    var = jnp.mean(yc * yc, axis=-1, keepdims=True)
    return yc * lax.rsqrt(var + LN_EPS) * g + b


ROW_CHUNK = 256


def _outproj_kernel(x_ref, om_ref, oc_ref, os_ref, ow_ref, g_ref, e_ref, w_ref, lg_ref, lb_ref,
                    h_ref, hb_ref):
    for c0 in range(0, x_ref.shape[0], ROW_CHUNK):
        rows = slice(c0, c0 + ROW_CHUNK)
        gate = g_ref[rows, :]
        hi = gate.astype(BF16)
        lo = (gate - hi.astype(F32)).astype(BF16)
        g2 = jnp.concatenate([hi, lo], axis=1)
        o_n = jnp.zeros((ROW_CHUNK, oc_ref.shape[1]), F32)
        for c, branch in enumerate((oc_ref, os_ref, ow_ref)):
            o_n = o_n + jnp.dot(g2, e_ref[c], preferred_element_type=F32) * branch[rows, :]
        o = jnp.concatenate([om_ref[rows, :], o_n.astype(BF16)], axis=1)
        a = jnp.dot(o, w_ref[...], preferred_element_type=F32)
        h = _layer_norm(DN_ALPHA * x_ref[rows, :] + a, lg_ref[...], lb_ref[...])
        h_ref[rows, :] = h
        hb_ref[rows, :] = h.astype(BF16)


def _resident(shape):
    return pl.BlockSpec(shape, lambda i: (0,) * len(shape), pipeline_mode=pl.Buffered(1))


def _outproj(x, o_m, o_c, o_s, o_w, gates, expand, w_out, ln_g, ln_b, tm=512):
    half = NSA_HEADS * HEAD_DIM
    row = lambda i: (i, 0)
    return pl.pallas_call(
        _outproj_kernel,
        grid=(SEQ // tm,),
        in_specs=[pl.BlockSpec((tm, D_MODEL), row),
                  pl.BlockSpec((tm, half), row), pl.BlockSpec((tm, half), row),
                  pl.BlockSpec((tm, half), row), pl.BlockSpec((tm, half), row),
                  pl.BlockSpec((tm, HEAD_DIM), row),
                  _resident((3, 2 * HEAD_DIM, half)),
                  _resident((D_MODEL, D_MODEL)),
                  _resident((1, D_MODEL)), _resident((1, D_MODEL))],
        out_specs=[pl.BlockSpec((tm, D_MODEL), row), pl.BlockSpec((tm, D_MODEL), row)],
        out_shape=[jax.ShapeDtypeStruct((SEQ, D_MODEL), F32),
                   jax.ShapeDtypeStruct((SEQ, D_MODEL), BF16)],
        compiler_params=_cparams("parallel", vmem=BIG_VMEM_LIMIT),
        name="outproj_ln",
    )(x, o_m, o_c, o_s, o_w, gates, expand, w_out, ln_g, ln_b)


def _ffn_kernel(hb_ref, w1_ref, w2_ref, f_ref):
    @pl.when(pl.program_id(1) == 0)
    def _():
        f_ref[...] = jnp.zeros(f_ref.shape, F32)

    u = jnp.maximum(jnp.dot(hb_ref[...], w1_ref[...].astype(BF16), preferred_element_type=F32), 0.0)
    f_ref[...] += jnp.dot((u * u).astype(BF16), w2_ref[...].astype(BF16), preferred_element_type=F32)


def _ffn(hb, w1, w2, tm=1024, tf=512):
    return pl.pallas_call(
        _ffn_kernel,
        grid=(SEQ // tm, D_FF // tf),
        in_specs=[pl.BlockSpec((tm, D_MODEL), lambda i, c: (i, 0)),
                  pl.BlockSpec((D_MODEL, tf), lambda i, c: (0, c)),
                  pl.BlockSpec((tf, D_MODEL), lambda i, c: (c, 0))],
        out_specs=pl.BlockSpec((tm, D_MODEL), lambda i, c: (i, 0)),
        out_shape=jax.ShapeDtypeStruct((SEQ, D_MODEL), F32),
        compiler_params=_cparams("parallel", "arbitrary", vmem=BIG_VMEM_LIMIT),
        name="ffn",
    )(hb, w1, w2)


def _final_kernel(h_ref, hb_ref, f_ref, p_ref, wg_ref, wp_ref, lg_ref, lb_ref, o_ref):
    for c0 in range(0, h_ref.shape[0], ROW_CHUNK):
        rows = slice(c0, c0 + ROW_CHUNK)
        gate = jax.nn.sigmoid(jnp.dot(hb_ref[rows, :], wg_ref[...], preferred_element_type=F32))
        emb = jnp.dot(p_ref[rows, :].astype(BF16), wp_ref[...], preferred_element_type=F32)
        y = DN_ALPHA * h_ref[rows, :] + f_ref[rows, :] + gate * emb
        o_ref[rows, :] = _layer_norm(y, lg_ref[...], lb_ref[...])


def _final(h, hb, f, p, wg, wp, ln_g, ln_b, tm=512):
    row = lambda i: (i, 0)
    return pl.pallas_call(
        _final_kernel,
        grid=(SEQ // tm,),
        in_specs=[pl.BlockSpec((tm, D_MODEL), row), pl.BlockSpec((tm, D_MODEL), row),
                  pl.BlockSpec((tm, D_MODEL), row), pl.BlockSpec((tm, D_PLE), row),
                  _resident((D_MODEL, D_MODEL)), _resident((D_PLE, D_MODEL)),
                  _resident((1, D_MODEL)), _resident((1, D_MODEL))],
        out_specs=pl.BlockSpec((tm, D_MODEL), row),
        out_shape=jax.ShapeDtypeStruct((SEQ, D_MODEL), F32),
        compiler_params=_cparams("parallel"),
        name="final_ln",
    )(h, hb, f, p, wg, wp, ln_g, ln_b)


def _overlap_matrix_t():
    j = np.arange(N_SLC)[:, None]
    n = np.arange(512)[None, :]
    ov = ((n * CMP_STRIDE < j * SLC_LEN + SLC_LEN) & (n * CMP_STRIDE + CMP_LEN > j * SLC_LEN) & (n < N_CMP))
    return ov.astype(np.float32)


def _gate_expand():
    e = np.zeros((3, 2 * HEAD_DIM, NSA_HEADS * HEAD_DIM), np.float32)
    for c in range(3):
        for h in range(NSA_HEADS):
            e[c, 3 * h + c, h * HEAD_DIM:(h + 1) * HEAD_DIM] = 1.0
            e[c, HEAD_DIM + 3 * h + c, h * HEAD_DIM:(h + 1) * HEAD_DIM] = 1.0
    return e


def _layer(h, p, w_in, cmp_pe_k, cmp_w1_k, cmp_w2_k, cmp_pe_v, cmp_w1_v, cmp_w2_v, tiles, w_out,
           ln1_g, ln1_b, w_ff1, w_ff2, w_ple, w_ple_gate, ln2_g, ln2_b):
    near, win2, cmpb = tiles
    scale = HEAD_DIM ** -0.5 * LOG2E
    colscale = np.ones((1, GATE_COL0), np.float32)
    colscale[0, COL_MQ * HEAD_DIM:(COL_MQ + MOBA_HEADS) * HEAD_DIM] = scale
    colscale[0, COL_NQ * HEAD_DIM:(COL_NQ + NSA_HEADS) * HEAD_DIM] = scale
    w_in_t = w_in.T
    w_gl_t = jnp.pad(w_in_t[GATE_COL0:], ((0, HEAD_DIM - NSA_HEADS * 3), (0, 0))).astype(BF16)
    zb, gates, ckv = _inproj(h, w_in_t, jnp.asarray(colscale), w_gl_t)

    half = CMP_STRIDE * HEAD_DIM
    xr = ckv.reshape(4, SEQ // CMP_STRIDE, half)
    kc, vct = _compress(xr, cmp_pe_k.reshape(2, 1, half), cmp_w1_k.reshape(2, half, HEAD_DIM), cmp_w2_k,
                        cmp_pe_v.reshape(2, 1, half), cmp_w1_v.reshape(2, half, HEAD_DIM), cmp_w2_v)

    key = np.arange(SEQ)[:, None]
    lane = np.arange(HEAD_DIM)[None, :]
    pat_moba = jnp.asarray(lane == key // MOBA_BLOCK, BF16)
    pat_slc = jnp.asarray(lane == key // SLC_LEN, BF16)
    o_m = _moba(zb, _moba_select(zb), near, pat_moba)
    o_c, selb = _cmp_select(zb, kc, vct, jnp.asarray(_overlap_matrix_t(), BF16), cmpb)
    o_s = _slc(zb, selb, near, pat_slc)
    o_w = _win(zb, near, win2)

    h1, h1b = _outproj(h, o_m, o_c, o_s, o_w, gates, jnp.asarray(_gate_expand(), BF16),
                       w_out.astype(BF16), ln1_g.reshape(1, -1), ln1_b.reshape(1, -1))
    f = _ffn(h1b, w_ff1, w_ff2)
    return _final(h1, h1b, f, p, w_ple_gate.astype(BF16), w_ple.astype(BF16),
                  ln2_g.reshape(1, -1), ln2_b.reshape(1, -1))


def kernel(x, p, w_in, cmp_pe_k, cmp_w1_k, cmp_w2_k, cmp_pe_v, cmp_w1_v, cmp_w2_v, rel_bias, w_out,
           ln1_g, ln1_b, w_ff1, w_ff2, w_ple, w_ple_gate, ln2_g, ln2_b):
    tiles = _bias_tiles(rel_bias)
    h = x[0]
    for i in range(w_in.shape[0]):
        h = _layer(h, p[i, 0], w_in[i], cmp_pe_k[i], cmp_w1_k[i], cmp_w2_k[i], cmp_pe_v[i], cmp_w1_v[i],
                   cmp_w2_v[i], tiles, w_out[i], ln1_g[i], ln1_b[i], w_ff1[i], w_ff2[i], w_ple[i],
                   w_ple_gate[i], ln2_g[i], ln2_b[i])
    return h[None]
```

```python
import functools
import math

import numpy as np
import jax
import jax.numpy as jnp
from jax import lax
from jax.experimental import pallas as pl
from jax.experimental.pallas import tpu as pltpu

D_MODEL = 2048
SEQ = 8192
HEAD_DIM = 128
N_HEADS = 16
MOBA_HEADS = 8
NSA_HEADS = 8
NSA_KV_HEADS = 2
NSA_GROUP = 4
MOBA_BLOCK = 256
MOBA_TOPK = 3
CMP_LEN = 32
CMP_STRIDE = 16
SLC_LEN = 64
SLC_TOPK = 16
WINDOW = 512
N_BUCKETS = 32
MAX_DISTANCE = 128
D_FF = 4 * D_MODEL
D_PLE = 256
LN_EPS = 1e-5
DN_ALPHA = 2.0 ** 0.25
NEG = -1e30
LOG2E = math.log2(math.e)
TINY = 1e-30
FORCE_SCORE = 1e4

N_CMP = (SEQ - CMP_LEN) // CMP_STRIDE + 1
N_SLC = SEQ // SLC_LEN
TQ = 256
N_QT = SEQ // TQ
CMP_WIN = 32
SLC_PER_TILE = TQ // SLC_LEN
HP_MOBA = 4
HP_SLC = 4
KG = 4
GROUP = KG * TQ
COL_MQ, COL_MK, COL_MV, COL_NQ = 0, 8, 16, 24
COL_CK, COL_CV, COL_SK, COL_SV, COL_WK, COL_WV = 32, 34, 36, 38, 40, 42
N_COLBLK = 44
GATE_COL0 = N_COLBLK * HEAD_DIM

VMEM_LIMIT = 48 * 1024 * 1024
BIG_VMEM_LIMIT = 56 * 1024 * 1024

BF16 = jnp.bfloat16
F32 = jnp.float32
_NT = (((1,), (1,)), ((), ()))


def _cparams(*sem, vmem=VMEM_LIMIT):
    return pltpu.CompilerParams(dimension_semantics=sem, vmem_limit_bytes=vmem)


def _bucket_table():
    n = np.arange(1024, dtype=np.int32)
    max_exact = N_BUCKETS // 2
    ratio = np.maximum(n, 1).astype(np.float32) / np.float32(max_exact)
    large = max_exact + (np.log(ratio).astype(np.float32) / np.float32(math.log(MAX_DISTANCE / max_exact))
                         * np.float32(N_BUCKETS - max_exact)).astype(np.int32)
    large = np.minimum(large, N_BUCKETS - 1)
    return np.where(n < max_exact, n, large).astype(np.int32)


def _bias_kernel(tab_ref, near_ref, cmpb_ref, *, steps):
    h = pl.program_id(0)
    far = tab_ref[h, N_BUCKETS - 1]

    def lookup(d):
        val = jnp.full(d.shape, tab_ref[h, 0], F32)
        for start, bucket in steps:
            val = jnp.where(d >= start, tab_ref[h, bucket], val)
        return jnp.where(d >= 0, (val - far) * LOG2E, NEG)

    a = lax.broadcasted_iota(jnp.int32, (TQ, TQ), 0)
    b = lax.broadcasted_iota(jnp.int32, (TQ, TQ), 1)
    near_ref[0, 0] = lookup(a - b)
    near_ref[1, 0] = lookup(a - b + TQ)
    w = lax.broadcasted_iota(jnp.int32, (CMP_WIN, TQ), 0)
    aw = lax.broadcasted_iota(jnp.int32, (CMP_WIN, TQ), 1)
    cmpb_ref[0, 0] = lookup(aw - CMP_STRIDE * w - (CMP_LEN - 1))
    cmpb_ref[1, 0] = lookup(aw - CMP_STRIDE * w + (TQ - CMP_LEN + 1))


def _bias_tiles(rel_bias):
    bucket = _bucket_table()
    steps = tuple((int(n), int(bucket[n])) for n in range(1, bucket.size) if bucket[n] != bucket[n - 1])
    near, cmpb = pl.pallas_call(
        functools.partial(_bias_kernel, steps=steps),
        grid=(N_HEADS,),
        in_specs=[pl.BlockSpec(memory_space=pltpu.SMEM)],
        out_specs=[pl.BlockSpec((2, 1, TQ, TQ), lambda h: (0, h, 0, 0)),
                   pl.BlockSpec((2, 1, CMP_WIN, TQ), lambda h: (0, h, 0, 0))],
        out_shape=[jax.ShapeDtypeStruct((2, N_HEADS, TQ, TQ), F32),
                   jax.ShapeDtypeStruct((2, N_HEADS, CMP_WIN, TQ), F32)],
        compiler_params=_cparams("parallel"),
        name="bias_tiles",
    )(rel_bias.T.astype(F32))
    a = np.arange(TQ)[:, None]
    b = np.arange(TQ)[None, :]
    win2 = jnp.asarray(np.where(a < b, 0.0, NEG).astype(np.float32))
    return near, win2, cmpb


INPROJ_TN = 4 * HEAD_DIM
CKV_TILE = COL_CK * HEAD_DIM // INPROJ_TN


def _inproj_kernel(x_ref, w_ref, cs_ref, wg_ref, o_ref, g_ref, ckv_ref, xb_ref):
    j = pl.program_id(1)

    @pl.when(j == 0)
    def _():
        xb = x_ref[...].astype(BF16)
        xb_ref[...] = xb
        g_ref[...] = jax.nn.sigmoid(lax.dot_general(xb, wg_ref[...], _NT, preferred_element_type=F32))

    acc = lax.dot_general(xb_ref[...], w_ref[...].astype(BF16), _NT, preferred_element_type=F32)
    z = (acc * cs_ref[...]).astype(o_ref.dtype)
    o_ref[...] = z

    @pl.when(j == CKV_TILE)
    def _():
        for c in range(INPROJ_TN // HEAD_DIM):
            ckv_ref[c] = z[:, c * HEAD_DIM:(c + 1) * HEAD_DIM]


def _inproj(x, w_in_t, colscale, w_gate_t, tm=1024):
    m, k = x.shape
    tn = INPROJ_TN
    return pl.pallas_call(
        _inproj_kernel,
        grid=(m // tm, GATE_COL0 // tn),
        in_specs=[pl.BlockSpec((tm, k), lambda i, j: (i, 0)),
                  pl.BlockSpec((tn, k), lambda i, j: (j, 0)),
                  pl.BlockSpec((1, tn), lambda i, j: (0, j)),
                  pl.BlockSpec((HEAD_DIM, k), lambda i, j: (0, 0))],
        out_specs=[pl.BlockSpec((tm, tn), lambda i, j: (i, j)),
                   pl.BlockSpec((tm, HEAD_DIM), lambda i, j: (i, 0)),
                   pl.BlockSpec((tn // HEAD_DIM, tm, HEAD_DIM), lambda i, j: (0, i, 0))],
        out_shape=[jax.ShapeDtypeStruct((m, GATE_COL0), BF16),
                   jax.ShapeDtypeStruct((m, HEAD_DIM), F32),
                   jax.ShapeDtypeStruct((tn // HEAD_DIM, m, HEAD_DIM), BF16)],
        scratch_shapes=[pltpu.VMEM((tm, k), BF16)],
        compiler_params=_cparams("parallel", "arbitrary"),
        name="inproj",
    )(x, w_in_t, colscale, w_gate_t)


def _gelu_tanh(x):
    return 0.5 * x * (1.0 + jnp.tanh(math.sqrt(2.0 / math.pi) * (x + 0.044715 * (x * x * x))))


def _compress_kernel(xk_ref, xv_ref, pek_ref, w1k_ref, w2k_ref, pev_ref, w1v_ref, w2v_ref, kc_ref, vct_ref):
    def compress(x_ref, pe_ref, w1_ref, w2_ref):
        x = x_ref[0].astype(F32)
        lo = jnp.dot((x + pe_ref[0]).astype(BF16), w1_ref[0].astype(BF16), preferred_element_type=F32)
        hi = jnp.dot((x + pe_ref[1]).astype(BF16), w1_ref[1].astype(BF16), preferred_element_type=F32)
        y = lo + pltpu.roll(hi, N_CMP, 0)
        out = jnp.dot(_gelu_tanh(y).astype(BF16), w2_ref[...].astype(BF16), preferred_element_type=F32)
        row = lax.broadcasted_iota(jnp.int32, out.shape, 0)
        return jnp.where(row < N_CMP, out, 0.0)

    kc_ref[0] = compress(xk_ref, pek_ref, w1k_ref, w2k_ref).astype(kc_ref.dtype)
    vct_ref[0] = compress(xv_ref, pev_ref, w1v_ref, w2v_ref).T.astype(vct_ref.dtype)


def _compress(xr, pe_k, w1_k, w2_k, pe_v, w1_v, w2_v):
    half = CMP_STRIDE * HEAD_DIM
    weights = [pl.BlockSpec((2, 1, half), lambda g: (0, 0, 0)),
               pl.BlockSpec((2, half, HEAD_DIM), lambda g: (0, 0, 0)),
               pl.BlockSpec((HEAD_DIM, HEAD_DIM), lambda g: (0, 0))]
    return pl.pallas_call(
        _compress_kernel,
        grid=(NSA_KV_HEADS,),
        in_specs=[pl.BlockSpec((1, 512, half), lambda g: (g, 0, 0)),
                  pl.BlockSpec((1, 512, half), lambda g: (NSA_KV_HEADS + g, 0, 0))] + weights + weights,
        out_specs=[pl.BlockSpec((1, 512, HEAD_DIM), lambda g: (g, 0, 0)),
                   pl.BlockSpec((1, HEAD_DIM, 512), lambda g: (g, 0, 0))],
        out_shape=[jax.ShapeDtypeStruct((NSA_KV_HEADS, 512, HEAD_DIM), BF16),
                   jax.ShapeDtypeStruct((NSA_KV_HEADS, HEAD_DIM, 512), BF16)],
        compiler_params=_cparams("parallel"),
        name="compress",
    )(xr, xr, pe_k, w1_k, w2_k, pe_v, w1_v, w2_v)


def _split3(x):
    hi = x.astype(BF16)
    r1 = x - hi.astype(F32)
    mid = r1.astype(BF16)
    lo = (r1 - mid.astype(F32)).astype(BF16)
    return hi, mid, lo


def _topk_bias_t(score, k):
    row = lax.broadcasted_iota(jnp.int32, score.shape, 0).astype(F32)
    selb = jnp.full(score.shape, NEG, F32)
    s = score
    for _ in range(k):
        mx = jnp.max(s, axis=0, keepdims=True)
        idx = jnp.min(jnp.where(s == mx, row, 1e9), axis=0, keepdims=True)
        hit = row == idx
        selb = jnp.where(hit, 0.0, selb)
        s = jnp.where(hit, -3e38, s)
    return selb


def _lane_tile_max(s):
    m = s[:, :HEAD_DIM]
    for c in range(1, s.shape[1] // HEAD_DIM):
        m = jnp.maximum(m, s[:, c * HEAD_DIM:(c + 1) * HEAD_DIM])
    return m


def _sparse_flash(qs, selbs, k_ref, v_ref, kcols, pat_ref, d0s, d1s, i, sub, scratch):
    t_ref, *maybe_r_ref, mx_ref, acc_ref = scratch
    r_ref = maybe_r_ref[0] if maybe_r_ref else None
    heads = range(len(qs))
    lane = lax.broadcasted_iota(jnp.int32, (TQ, HEAD_DIM), 1)
    far = lane < (i - 1) * sub
    q_near = [jnp.concatenate([qs[h], selbs[h].astype(BF16)], axis=1) for h in heads]
    q_far = [jnp.concatenate([qs[h], jnp.where(far, selbs[h], NEG).astype(BF16)], axis=1) for h in heads]

    def k_aug(h, off, n):
        c = kcols[h]
        return jnp.concatenate([k_ref[pl.ds(off, n), c:c + HEAD_DIM], pat_ref[pl.ds(off, n), :]], axis=1)

    def v_aug(h, off, n):
        c = kcols[h]
        return jnp.concatenate([v_ref[pl.ds(off, n), c:c + HEAD_DIM], jnp.ones((n, HEAD_DIM), BF16)], axis=1)

    for h in heads:
        mx_ref[h] = jnp.full((TQ, HEAD_DIM), NEG, F32)

    def far_group(g, carry):
        off = pl.multiple_of(g * GROUP, GROUP)
        for h in heads:
            s = lax.dot_general(q_far[h], k_aug(h, off, GROUP), _NT, preferred_element_type=F32)
            ref = _lane_tile_max(s)
            if r_ref is not None:
                s = s - jnp.concatenate([ref] * (GROUP // HEAD_DIM), axis=1)
                r_ref[h, g] = ref
            for u in range(KG):
                t_ref[h, g * KG + u] = s[:, u * TQ:(u + 1) * TQ].astype(t_ref.dtype)
            mx_ref[h] = jnp.maximum(mx_ref[h], ref)
        return carry

    n_far = lax.shift_right_logical(jnp.maximum(i - 1, 0) + (KG - 1), KG.bit_length() - 1)
    lax.fori_loop(0, n_far, far_group, 0)

    off_prev = pl.multiple_of(jnp.maximum(i - 1, 0) * TQ, TQ)
    off_own = pl.multiple_of(i * TQ, TQ)
    p_near = []
    for h in heads:
        k2 = jnp.concatenate([k_aug(h, off_prev, TQ), k_aug(h, off_own, TQ)], axis=0)
        bias = jnp.concatenate([jnp.where(i >= 1, d1s[h], NEG), d0s[h]], axis=1)
        s = lax.dot_general(q_near[h], k2, _NT, preferred_element_type=F32) + bias
        m = jnp.max(jnp.maximum(mx_ref[h], _lane_tile_max(s)), axis=1, keepdims=True)
        mx_ref[h] = jnp.broadcast_to(m, (TQ, HEAD_DIM))
        p_near.append(jnp.exp2(s - m).astype(BF16))
    for h in heads:
        v2 = jnp.concatenate([v_aug(h, off_prev, TQ), v_aug(h, off_own, TQ)], axis=0)
        acc_ref[h] = jnp.dot(p_near[h], v2, preferred_element_type=F32)

    def accumulate(g0, ng):
        off = pl.multiple_of(g0 * GROUP, GROUP)
        for h in heads:
            probs = []
            for g in [g0 + k for k in range(ng)]:
                d = -mx_ref[h] if r_ref is None else r_ref[h, g] - mx_ref[h]
                d2 = jnp.concatenate([d, d], axis=1)
                probs += [jnp.exp2(t_ref[h, g * KG + u].astype(F32) + d2).astype(BF16) for u in range(KG)]
            acc_ref[h] += jnp.dot(jnp.concatenate(probs, axis=1), v_aug(h, off, ng * GROUP),
                                  preferred_element_type=F32)

    def pv_pair(m, carry):
        accumulate(2 * m, 2)
        return carry

    lax.fori_loop(0, lax.shift_right_logical(n_far, 1), pv_pair, 0)

    @pl.when((n_far & 1) == 1)
    def _():
        accumulate(n_far - 1, 1)

    outs = []
    for h in heads:
        acc = acc_ref[h]
        outs.append(acc[:, :HEAD_DIM] / acc[:, HEAD_DIM:])
    return outs


def _flash_scratch(hp, relative_bf16):
    logits = [pltpu.VMEM((hp, N_QT, TQ, TQ), BF16),
              pltpu.VMEM((hp, SEQ // GROUP, TQ, HEAD_DIM), F32)]
    if not relative_bf16:
        logits = [pltpu.VMEM((hp, N_QT, TQ, TQ), F32)]
    return logits + [pltpu.VMEM((hp, TQ, HEAD_DIM), F32),
                     pltpu.VMEM((hp, TQ, 2 * HEAD_DIM), F32)]


SEL_CHUNK = 4096


def _moba_select_kernel(q_ref, k_ref, sel_ref):
    blk = lax.broadcasted_iota(jnp.int32, (HEAD_DIM, SEQ), 0)
    pos = lax.broadcasted_iota(jnp.int32, (HEAD_DIM, SEQ), 1)
    ind = jnp.where(lax.shift_right_logical(pos, 8) == blk, 1.0 / MOBA_BLOCK, 0.0).astype(BF16)
    n_blk = SEQ // MOBA_BLOCK
    kmean = _split3(jnp.dot(ind, k_ref[...], preferred_element_type=F32)[:n_blk])
    blk_t = lax.broadcasted_iota(jnp.int32, (n_blk, SEL_CHUNK), 0)
    col = lax.broadcasted_iota(jnp.int32, (n_blk, SEL_CHUNK), 1)
    no_block = jnp.full((HEAD_DIM - n_blk, SEL_CHUNK), NEG, F32)

    def chunk(c, carry):
        off = pl.multiple_of(c * SEL_CHUNK, SEL_CHUNK)
        q = q_ref[pl.ds(off, SEL_CHUNK), :]
        gate = jnp.zeros((n_blk, SEL_CHUNK), F32)
        for part in kmean:
            gate = gate + lax.dot_general(part, q, _NT, preferred_element_type=F32)
        own = lax.shift_right_logical(off + col, 8)
        past = blk_t < own
        selb = _topk_bias_t(jnp.where(past, gate, NEG), MOBA_TOPK)
        selb = jnp.where(past, selb, jnp.where(blk_t == own, 0.0, NEG))
        sel_ref[0, pl.ds(off, SEL_CHUNK), :] = jnp.concatenate([selb, no_block], axis=0).T.astype(sel_ref.dtype)
        return carry

    lax.fori_loop(0, SEQ // SEL_CHUNK, chunk, 0)


def _moba_select(zb):
    return pl.pallas_call(
        _moba_select_kernel,
        grid=(MOBA_HEADS,),
        in_specs=[pl.BlockSpec((SEQ, HEAD_DIM), lambda h: (0, COL_MQ + h)),
                  pl.BlockSpec((SEQ, HEAD_DIM), lambda h: (0, COL_MK + h))],
        out_specs=pl.BlockSpec((1, SEQ, HEAD_DIM), lambda h: (h, 0, 0)),
        out_shape=jax.ShapeDtypeStruct((MOBA_HEADS, SEQ, HEAD_DIM), BF16),
        compiler_params=_cparams("parallel"),
        name="moba_select",
    )(zb, zb)


def _moba_kernel(q_ref, sel_ref, k_ref, v_ref, pat_ref, d0_ref, d1_ref, o_ref, *flash_scratch):
    i = pl.program_id(1)
    hp = HP_MOBA
    qs = [q_ref[:, h * HEAD_DIM:(h + 1) * HEAD_DIM] for h in range(hp)]
    selbs = [sel_ref[h].astype(F32) for h in range(hp)]
    cols = [h * HEAD_DIM for h in range(hp)]
    outs = _sparse_flash(qs, selbs, k_ref, v_ref, cols, pat_ref, [d0_ref[0, h] for h in range(hp)],
                         [d1_ref[0, h] for h in range(hp)], i, 1, flash_scratch)
    for h in range(hp):
        o_ref[:, h * HEAD_DIM:(h + 1) * HEAD_DIM] = outs[h].astype(o_ref.dtype)


def _moba(zb, selb, near, pat):
    hp = HP_MOBA
    w = hp * HEAD_DIM
    return pl.pallas_call(
        _moba_kernel,
        grid=(MOBA_HEADS // hp, N_QT),
        in_specs=[pl.BlockSpec((TQ, w), lambda h, i: (i, COL_MQ // hp + h)),
                  pl.BlockSpec((hp, TQ, HEAD_DIM), lambda h, i: (h, i, 0)),
                  pl.BlockSpec((SEQ, w), lambda h, i: (0, COL_MK // hp + h), pipeline_mode=pl.Buffered(1)),
                  pl.BlockSpec((SEQ, w), lambda h, i: (0, COL_MV // hp + h), pipeline_mode=pl.Buffered(1)),
                  pl.BlockSpec((SEQ, HEAD_DIM), lambda h, i: (0, 0), pipeline_mode=pl.Buffered(1)),
                  pl.BlockSpec((1, hp, TQ, TQ), lambda h, i: (0, h, 0, 0)),
                  pl.BlockSpec((1, hp, TQ, TQ), lambda h, i: (1, h, 0, 0))],
        out_specs=pl.BlockSpec((TQ, w), lambda h, i: (i, h)),
        out_shape=jax.ShapeDtypeStruct((SEQ, MOBA_HEADS * HEAD_DIM), BF16),
        scratch_shapes=_flash_scratch(hp, True),
        compiler_params=_cparams("parallel", "arbitrary", vmem=BIG_VMEM_LIMIT),
        name="moba",
    )(zb, selb, zb, zb, pat, near, near)


def _cmp_kernel(q_ref, kc_ref, vct_ref, ovt_ref, cb_ref, oc_ref, sel_ref, s_ref):
    i = pl.program_id(0)
    win0 = pl.multiple_of(jnp.maximum(i * CMP_STRIDE - CMP_STRIDE, 0), CMP_STRIDE)

    def body(nk):
        nb = nk * CMP_STRIDE // SLC_LEN
        blk = lax.broadcasted_iota(jnp.int32, (nb, TQ), 0)
        cur = i * SLC_PER_TILE + lax.shift_right_logical(lax.broadcasted_iota(jnp.int32, (nb, TQ), 1), 6)
        allowed = blk <= cur
        forced = (blk == 0) | (blk == cur) | (blk == cur - 1)
        reachable = lax.broadcasted_iota(jnp.int32, (nk, TQ), 0) < win0 + CMP_WIN
        for g in range(NSA_KV_HEADS):
            p_sum = jnp.zeros((nk, TQ), F32)
            for r in range(NSA_GROUP):
                hd = g * NSA_GROUP + r
                q = q_ref[:, hd * HEAD_DIM:(hd + 1) * HEAD_DIM]
                s_ref[:nk, :] = lax.dot_general(kc_ref[g, :nk, :], q, _NT, preferred_element_type=F32)
                s_ref[pl.ds(win0, CMP_WIN), :] += cb_ref[0, hd]
                s = jnp.where(reachable, s_ref[:nk, :], NEG)
                valid = s > 0.5 * NEG
                e = jnp.where(valid, jnp.exp2(s - jnp.max(s, axis=0, keepdims=True)), 0.0)
                p = e * (1.0 / jnp.maximum(jnp.sum(e, axis=0, keepdims=True), TINY))
                o_t = jnp.dot(vct_ref[g, :, :nk], p.astype(BF16), preferred_element_type=F32)
                oc_ref[:, hd * HEAD_DIM:(hd + 1) * HEAD_DIM] = o_t.T
                p_sum = p_sum + p
            imp = jnp.zeros((nb, TQ), F32)
            for part in _split3(p_sum):
                imp = imp + jnp.dot(ovt_ref[:nb, :nk], part, preferred_element_type=F32)
            picked = _topk_bias_t(jnp.where(allowed & ~forced, imp, -1.0), SLC_TOPK - 3)
            selb = jnp.where(allowed, jnp.where(forced, 0.0, picked), NEG)
            if nb < N_SLC:
                selb = jnp.concatenate([selb, jnp.full((N_SLC - nb, TQ), NEG, F32)], axis=0)
            sel_ref[g] = selb.T.astype(sel_ref.dtype)

    for c in range(1, 512 // HEAD_DIM + 1):
        @pl.when(lax.shift_right_logical(i, 3) == c - 1)
        def _():
            body(c * HEAD_DIM)


def _cmp_select(zb, kc, vct, ovt, cmpb):
    w = NSA_HEADS * HEAD_DIM
    return pl.pallas_call(
        _cmp_kernel,
        grid=(N_QT,),
        in_specs=[pl.BlockSpec((TQ, w), lambda i: (i, COL_NQ // NSA_HEADS)),
                  pl.BlockSpec((NSA_KV_HEADS, 512, HEAD_DIM), lambda i: (0, 0, 0)),
                  pl.BlockSpec((NSA_KV_HEADS, HEAD_DIM, 512), lambda i: (0, 0, 0)),
                  pl.BlockSpec((N_SLC, 512), lambda i: (0, 0)),
                  pl.BlockSpec((1, NSA_HEADS, CMP_WIN, TQ), lambda i: (jnp.minimum(i, 1), 1, 0, 0))],
        out_specs=[pl.BlockSpec((TQ, w), lambda i: (i, 0)),
                   pl.BlockSpec((NSA_KV_HEADS, TQ, N_SLC), lambda i: (0, i, 0))],
        out_shape=[jax.ShapeDtypeStruct((SEQ, w), F32),
                   jax.ShapeDtypeStruct((NSA_KV_HEADS, SEQ, N_SLC), BF16)],
        scratch_shapes=[pltpu.VMEM((512, TQ), F32)],
        compiler_params=_cparams("arbitrary"),
        name="cmp_select",
    )(zb, kc, vct, ovt, cmpb)


def _slc_kernel(q_ref, sel_ref, k_ref, v_ref, pat_ref, d0_ref, d1_ref, o_ref, *flash_scratch):
    i = pl.program_id(1)
    hp = HP_SLC
    selb = sel_ref[0].astype(F32)
    qs = [q_ref[:, h * HEAD_DIM:(h + 1) * HEAD_DIM] for h in range(hp)]
    outs = _sparse_flash(qs, [selb] * hp, k_ref, v_ref, [0] * hp, pat_ref,
                         [d0_ref[0, h] for h in range(hp)], [d1_ref[0, h] for h in range(hp)],
                         i, SLC_PER_TILE, flash_scratch)
    for h in range(hp):
        o_ref[:, h * HEAD_DIM:(h + 1) * HEAD_DIM] = outs[h].astype(o_ref.dtype)


def _slc(zb, selb, near, pat):
    hp = HP_SLC
    w = hp * HEAD_DIM
    per_group = NSA_GROUP // hp
    return pl.pallas_call(
        _slc_kernel,
        grid=(NSA_HEADS // hp, N_QT),
        in_specs=[pl.BlockSpec((TQ, w), lambda h, i: (i, COL_NQ // hp + h)),
                  pl.BlockSpec((1, TQ, N_SLC), lambda h, i: (h // per_group, i, 0)),
                  pl.BlockSpec((SEQ, HEAD_DIM), lambda h, i: (0, COL_SK + h // per_group),
                               pipeline_mode=pl.Buffered(1)),
                  pl.BlockSpec((SEQ, HEAD_DIM), lambda h, i: (0, COL_SV + h // per_group),
                               pipeline_mode=pl.Buffered(1)),
                  pl.BlockSpec((SEQ, HEAD_DIM), lambda h, i: (0, 0), pipeline_mode=pl.Buffered(1)),
                  pl.BlockSpec((1, hp, TQ, TQ), lambda h, i: (0, MOBA_HEADS // hp + h, 0, 0)),
                  pl.BlockSpec((1, hp, TQ, TQ), lambda h, i: (1, MOBA_HEADS // hp + h, 0, 0))],
        out_specs=pl.BlockSpec((TQ, w), lambda h, i: (i, h)),
        out_shape=jax.ShapeDtypeStruct((SEQ, NSA_HEADS * HEAD_DIM), F32),
        scratch_shapes=_flash_scratch(hp, False),
        compiler_params=_cparams("parallel", "arbitrary", vmem=BIG_VMEM_LIMIT),
        name="slc",
    )(zb, selb, zb, zb, pat, near, near)


def _win_kernel(q_ref, k_ref, v_ref, d0_ref, d1_ref, w2_ref, o_ref):
    i = pl.program_id(0)
    offs = [pl.multiple_of(jnp.maximum(i - b, 0) * TQ, TQ) for b in range(3)]
    mask2 = jnp.where(i >= 2, w2_ref[...], NEG)
    ones = jnp.ones((3 * TQ, HEAD_DIM), BF16)
    for g in range(NSA_KV_HEADS):
        cols = slice(g * HEAD_DIM, (g + 1) * HEAD_DIM)
        k_all = jnp.concatenate([k_ref[pl.ds(off, TQ), cols] for off in offs], axis=0)
        v_aug = jnp.concatenate([jnp.concatenate([v_ref[pl.ds(off, TQ), cols] for off in offs], axis=0), ones],
                                axis=1)
        for r in range(NSA_GROUP):
            hd = g * NSA_GROUP + r
            q = q_ref[:, hd * HEAD_DIM:(hd + 1) * HEAD_DIM]
            bias = jnp.concatenate([d0_ref[0, hd], jnp.where(i >= 1, d1_ref[0, hd], NEG), mask2], axis=1)
            s = lax.dot_general(q, k_all, _NT, preferred_element_type=F32) + bias
            m = jnp.max(s, axis=1, keepdims=True)
            pv = jnp.dot(jnp.exp2(s - m).astype(BF16), v_aug, preferred_element_type=F32)
            o_ref[:, hd * HEAD_DIM:(hd + 1) * HEAD_DIM] = pv[:, :HEAD_DIM] / pv[:, HEAD_DIM:]


def _win(zb, near, win2):
    w = NSA_HEADS * HEAD_DIM
    kv_w = NSA_KV_HEADS * HEAD_DIM
    return pl.pallas_call(
        _win_kernel,
        grid=(N_QT,),
        in_specs=[pl.BlockSpec((TQ, w), lambda i: (i, COL_NQ // NSA_HEADS)),
                  pl.BlockSpec((SEQ, kv_w), lambda i: (0, COL_WK // NSA_KV_HEADS), pipeline_mode=pl.Buffered(1)),
                  pl.BlockSpec((SEQ, kv_w), lambda i: (0, COL_WV // NSA_KV_HEADS), pipeline_mode=pl.Buffered(1)),
                  pl.BlockSpec((1, NSA_HEADS, TQ, TQ), lambda i: (0, 1, 0, 0), pipeline_mode=pl.Buffered(1)),
                  pl.BlockSpec((1, NSA_HEADS, TQ, TQ), lambda i: (1, 1, 0, 0), pipeline_mode=pl.Buffered(1)),
                  pl.BlockSpec((TQ, TQ), lambda i: (0, 0), pipeline_mode=pl.Buffered(1))],
        out_specs=pl.BlockSpec((TQ, w), lambda i: (i, 0)),
        out_shape=jax.ShapeDtypeStruct((SEQ, w), F32),
        compiler_params=_cparams("arbitrary"),
        name="win",
    )(zb, zb, zb, near, near, win2)


def _layer_norm(y, g, b):
    mu = jnp.mean(y, axis=-1, keepdims=True)
    yc = y - mu
    var = jnp.mean(yc * yc, axis=-1, keepdims=True)
    return yc * lax.rsqrt(var + LN_EPS) * g + b


ROW_CHUNK = 256


def _outproj_kernel(x_ref, om_ref, oc_ref, os_ref, ow_ref, g_ref, e_ref, w_ref, lg_ref, lb_ref,
                    h_ref, hb_ref):
    for c0 in range(0, x_ref.shape[0], ROW_CHUNK):
        rows = slice(c0, c0 + ROW_CHUNK)
        gate = g_ref[rows, :]
        hi = gate.astype(BF16)
        lo = (gate - hi.astype(F32)).astype(BF16)
        g2 = jnp.concatenate([hi, lo], axis=1)
        o_n = jnp.zeros((ROW_CHUNK, oc_ref.shape[1]), F32)
        for c, branch in enumerate((oc_ref, os_ref, ow_ref)):
            o_n = o_n + jnp.dot(g2, e_ref[c], preferred_element_type=F32) * branch[rows, :]
        o = jnp.concatenate([om_ref[rows, :], o_n.astype(BF16)], axis=1)
        a = jnp.dot(o, w_ref[...], preferred_element_type=F32)
        h = _layer_norm(DN_ALPHA * x_ref[rows, :] + a, lg_ref[...], lb_ref[...])
        h_ref[rows, :] = h
        hb_ref[rows, :] = h.astype(BF16)


def _resident(shape):
    return pl.BlockSpec(shape, lambda i: (0,) * len(shape), pipeline_mode=pl.Buffered(1))


def _outproj(x, o_m, o_c, o_s, o_w, gates, expand, w_out, ln_g, ln_b, tm=512):
    half = NSA_HEADS * HEAD_DIM
    row = lambda i: (i, 0)
    return pl.pallas_call(
        _outproj_kernel,
        grid=(SEQ // tm,),
        in_specs=[pl.BlockSpec((tm, D_MODEL), row),
                  pl.BlockSpec((tm, half), row), pl.BlockSpec((tm, half), row),
                  pl.BlockSpec((tm, half), row), pl.BlockSpec((tm, half), row),
                  pl.BlockSpec((tm, HEAD_DIM), row),
                  _resident((3, 2 * HEAD_DIM, half)),
                  _resident((D_MODEL, D_MODEL)),
                  _resident((1, D_MODEL)), _resident((1, D_MODEL))],
        out_specs=[pl.BlockSpec((tm, D_MODEL), row), pl.BlockSpec((tm, D_MODEL), row)],
        out_shape=[jax.ShapeDtypeStruct((SEQ, D_MODEL), F32),
                   jax.ShapeDtypeStruct((SEQ, D_MODEL), BF16)],
        compiler_params=_cparams("parallel", vmem=BIG_VMEM_LIMIT),
        name="outproj_ln",
    )(x, o_m, o_c, o_s, o_w, gates, expand, w_out, ln_g, ln_b)


def _ffn_kernel(hb_ref, w1_ref, w2_ref, f_ref):
    @pl.when(pl.program_id(1) == 0)
    def _():
        f_ref[...] = jnp.zeros(f_ref.shape, F32)

    u = jnp.maximum(jnp.dot(hb_ref[...], w1_ref[...].astype(BF16), preferred_element_type=F32), 0.0)
    f_ref[...] += jnp.dot((u * u).astype(BF16), w2_ref[...].astype(BF16), preferred_element_type=F32)


def _ffn(hb, w1, w2, tm=1024, tf=512):
    return pl.pallas_call(
        _ffn_kernel,
        grid=(SEQ // tm, D_FF // tf),
        in_specs=[pl.BlockSpec((tm, D_MODEL), lambda i, c: (i, 0)),
                  pl.BlockSpec((D_MODEL, tf), lambda i, c: (0, c)),
                  pl.BlockSpec((tf, D_MODEL), lambda i, c: (c, 0))],
        out_specs=pl.BlockSpec((tm, D_MODEL), lambda i, c: (i, 0)),
        out_shape=jax.ShapeDtypeStruct((SEQ, D_MODEL), F32),
        compiler_params=_cparams("parallel", "arbitrary", vmem=BIG_VMEM_LIMIT),
        name="ffn",
    )(hb, w1, w2)


def _final_kernel(h_ref, hb_ref, f_ref, p_ref, wg_ref, wp_ref, lg_ref, lb_ref, o_ref):
    for c0 in range(0, h_ref.shape[0], ROW_CHUNK):
        rows = slice(c0, c0 + ROW_CHUNK)
        gate = jax.nn.sigmoid(jnp.dot(hb_ref[rows, :], wg_ref[...], preferred_element_type=F32))
        emb = jnp.dot(p_ref[rows, :].astype(BF16), wp_ref[...], preferred_element_type=F32)
        y = DN_ALPHA * h_ref[rows, :] + f_ref[rows, :] + gate * emb
        o_ref[rows, :] = _layer_norm(y, lg_ref[...], lb_ref[...])


def _final(h, hb, f, p, wg, wp, ln_g, ln_b, tm=512):
    row = lambda i: (i, 0)
    return pl.pallas_call(
        _final_kernel,
        grid=(SEQ // tm,),
        in_specs=[pl.BlockSpec((tm, D_MODEL), row), pl.BlockSpec((tm, D_MODEL), row),
                  pl.BlockSpec((tm, D_MODEL), row), pl.BlockSpec((tm, D_PLE), row),
                  _resident((D_MODEL, D_MODEL)), _resident((D_PLE, D_MODEL)),
                  _resident((1, D_MODEL)), _resident((1, D_MODEL))],
        out_specs=pl.BlockSpec((tm, D_MODEL), row),
        out_shape=jax.ShapeDtypeStruct((SEQ, D_MODEL), F32),
        compiler_params=_cparams("parallel"),
        name="final_ln",
    )(h, hb, f, p, wg, wp, ln_g, ln_b)


def _overlap_matrix_t():
    j = np.arange(N_SLC)[:, None]
    n = np.arange(512)[None, :]
    ov = ((n * CMP_STRIDE < j * SLC_LEN + SLC_LEN) & (n * CMP_STRIDE + CMP_LEN > j * SLC_LEN) & (n < N_CMP))
    return ov.astype(np.float32)


def _gate_expand():
    e = np.zeros((3, 2 * HEAD_DIM, NSA_HEADS * HEAD_DIM), np.float32)
    for c in range(3):
        for h in range(NSA_HEADS):
            e[c, 3 * h + c, h * HEAD_DIM:(h + 1) * HEAD_DIM] = 1.0
            e[c, HEAD_DIM + 3 * h + c, h * HEAD_DIM:(h + 1) * HEAD_DIM] = 1.0
    return e


def _layer(h, p, w_in, cmp_pe_k, cmp_w1_k, cmp_w2_k, cmp_pe_v, cmp_w1_v, cmp_w2_v, tiles, w_out,
           ln1_g, ln1_b, w_ff1, w_ff2, w_ple, w_ple_gate, ln2_g, ln2_b):
    near, win2, cmpb = tiles
    scale = HEAD_DIM ** -0.5 * LOG2E
    colscale = np.ones((1, GATE_COL0), np.float32)
    colscale[0, COL_MQ * HEAD_DIM:(COL_MQ + MOBA_HEADS) * HEAD_DIM] = scale
    colscale[0, COL_NQ * HEAD_DIM:(COL_NQ + NSA_HEADS) * HEAD_DIM] = scale
    w_in_t = w_in.T
    w_gl_t = jnp.pad(w_in_t[GATE_COL0:], ((0, HEAD_DIM - NSA_HEADS * 3), (0, 0))).astype(BF16)
    zb, gates, ckv = _inproj(h, w_in_t, jnp.asarray(colscale), w_gl_t)

    half = CMP_STRIDE * HEAD_DIM
    xr = ckv.reshape(4, SEQ // CMP_STRIDE, half)
    kc, vct = _compress(xr, cmp_pe_k.reshape(2, 1, half), cmp_w1_k.reshape(2, half, HEAD_DIM), cmp_w2_k,
                        cmp_pe_v.reshape(2, 1, half), cmp_w1_v.reshape(2, half, HEAD_DIM), cmp_w2_v)

    key = np.arange(SEQ)[:, None]
    lane = np.arange(HEAD_DIM)[None, :]
    pat_moba = jnp.asarray(lane == key // MOBA_BLOCK, BF16)
    pat_slc = jnp.asarray(lane == key // SLC_LEN, BF16)
    o_m = _moba(zb, _moba_select(zb), near, pat_moba)
    o_c, selb = _cmp_select(zb, kc, vct, jnp.asarray(_overlap_matrix_t(), BF16), cmpb)
    o_s = _slc(zb, selb, near, pat_slc)
    o_w = _win(zb, near, win2)

    h1, h1b = _outproj(h, o_m, o_c, o_s, o_w, gates, jnp.asarray(_gate_expand(), BF16),
                       w_out.astype(BF16), ln1_g.reshape(1, -1), ln1_b.reshape(1, -1))
    f = _ffn(h1b, w_ff1, w_ff2)
    return _final(h1, h1b, f, p, w_ple_gate.astype(BF16), w_ple.astype(BF16),
                  ln2_g.reshape(1, -1), ln2_b.reshape(1, -1))


def kernel(x, p, w_in, cmp_pe_k, cmp_w1_k, cmp_w2_k, cmp_pe_v, cmp_w1_v, cmp_w2_v, rel_bias, w_out,
           ln1_g, ln1_b, w_ff1, w_ff2, w_ple, w_ple_gate, ln2_g, ln2_b):
    tiles = _bias_tiles(rel_bias)
    h = x[0]
    for i in range(w_in.shape[0]):
        h = _layer(h, p[i, 0], w_in[i], cmp_pe_k[i], cmp_w1_k[i], cmp_w2_k[i], cmp_pe_v[i], cmp_w1_v[i],
                   cmp_w2_v[i], tiles, w_out[i], ln1_g[i], ln1_b[i], w_ff1[i], w_ff2[i], w_ple[i],
                   w_ple_gate[i], ln2_g[i], ln2_b[i])
    return h[None]
```

```python
import functools
import math

import numpy as np
import jax
import jax.numpy as jnp
from jax import lax
from jax.experimental import pallas as pl
from jax.experimental.pallas import tpu as pltpu

D_MODEL = 2048
SEQ = 8192
HEAD_DIM = 128
N_HEADS = 16
MOBA_HEADS = 8
NSA_HEADS = 8
NSA_KV_HEADS = 2
NSA_GROUP = 4
MOBA_BLOCK = 256
MOBA_TOPK = 3
CMP_LEN = 32
CMP_STRIDE = 16
SLC_LEN = 64
SLC_TOPK = 16
WINDOW = 512
N_BUCKETS = 32
MAX_DISTANCE = 128
D_FF = 4 * D_MODEL
D_PLE = 256
LN_EPS = 1e-5
DN_ALPHA = 2.0 ** 0.25
NEG = -1e30
LOG2E = math.log2(math.e)
TINY = 1e-30
FORCE_SCORE = 1e4

N_CMP = (SEQ - CMP_LEN) // CMP_STRIDE + 1
N_SLC = SEQ // SLC_LEN
TQ = 256
N_QT = SEQ // TQ
CMP_WIN = 32
SLC_PER_TILE = TQ // SLC_LEN
HP_MOBA = 4
HP_SLC = 4
KG = 4
GROUP = KG * TQ
COL_MQ, COL_MK, COL_MV, COL_NQ = 0, 8, 16, 24
COL_CK, COL_CV, COL_SK, COL_SV, COL_WK, COL_WV = 32, 34, 36, 38, 40, 42
N_COLBLK = 44
GATE_COL0 = N_COLBLK * HEAD_DIM

VMEM_LIMIT = 48 * 1024 * 1024
BIG_VMEM_LIMIT = 56 * 1024 * 1024

BF16 = jnp.bfloat16
F32 = jnp.float32
_NT = (((1,), (1,)), ((), ()))


def _cparams(*sem, vmem=VMEM_LIMIT):
    return pltpu.CompilerParams(dimension_semantics=sem, vmem_limit_bytes=vmem)


def _bucket_table():
    n = np.arange(1024, dtype=np.int32)
    max_exact = N_BUCKETS // 2
    ratio = np.maximum(n, 1).astype(np.float32) / np.float32(max_exact)
    large = max_exact + (np.log(ratio).astype(np.float32) / np.float32(math.log(MAX_DISTANCE / max_exact))
                         * np.float32(N_BUCKETS - max_exact)).astype(np.int32)
    large = np.minimum(large, N_BUCKETS - 1)
    return np.where(n < max_exact, n, large).astype(np.int32)


def _bias_kernel(tab_ref, near_ref, cmpb_ref, *, steps):
    h = pl.program_id(0)
    far = tab_ref[h, N_BUCKETS - 1]

    def lookup(d):
        val = jnp.full(d.shape, tab_ref[h, 0], F32)
        for start, bucket in steps:
            val = jnp.where(d >= start, tab_ref[h, bucket], val)
        return jnp.where(d >= 0, (val - far) * LOG2E, NEG)

    a = lax.broadcasted_iota(jnp.int32, (TQ, TQ), 0)
    b = lax.broadcasted_iota(jnp.int32, (TQ, TQ), 1)
    near_ref[0, 0] = lookup(a - b)
    near_ref[1, 0] = lookup(a - b + TQ)
    w = lax.broadcasted_iota(jnp.int32, (CMP_WIN, TQ), 0)
    aw = lax.broadcasted_iota(jnp.int32, (CMP_WIN, TQ), 1)
    cmpb_ref[0, 0] = lookup(aw - CMP_STRIDE * w - (CMP_LEN - 1))
    cmpb_ref[1, 0] = lookup(aw - CMP_STRIDE * w + (TQ - CMP_LEN + 1))


def _bias_tiles(rel_bias):
    bucket = _bucket_table()
    steps = tuple((int(n), int(bucket[n])) for n in range(1, bucket.size) if bucket[n] != bucket[n - 1])
    near, cmpb = pl.pallas_call(
        functools.partial(_bias_kernel, steps=steps),
        grid=(N_HEADS,),
        in_specs=[pl.BlockSpec(memory_space=pltpu.SMEM)],
        out_specs=[pl.BlockSpec((2, 1, TQ, TQ), lambda h: (0, h, 0, 0)),
                   pl.BlockSpec((2, 1, CMP_WIN, TQ), lambda h: (0, h, 0, 0))],
        out_shape=[jax.ShapeDtypeStruct((2, N_HEADS, TQ, TQ), F32),
                   jax.ShapeDtypeStruct((2, N_HEADS, CMP_WIN, TQ), F32)],
        compiler_params=_cparams("parallel"),
        name="bias_tiles",
    )(rel_bias.T.astype(F32))
    a = np.arange(TQ)[:, None]
    b = np.arange(TQ)[None, :]
    win2 = jnp.asarray(np.where(a < b, 0.0, NEG).astype(np.float32))
    return near, win2, cmpb


INPROJ_TN = 4 * HEAD_DIM
CKV_TILE = COL_CK * HEAD_DIM // INPROJ_TN


def _inproj_kernel(x_ref, w_ref, cs_ref, wg_ref, o_ref, g_ref, ckv_ref, xb_ref):
    j = pl.program_id(1)

    @pl.when(j == 0)
    def _():
        xb = x_ref[...].astype(BF16)
        xb_ref[...] = xb
        g_ref[...] = jax.nn.sigmoid(lax.dot_general(xb, wg_ref[...], _NT, preferred_element_type=F32))

    acc = lax.dot_general(xb_ref[...], w_ref[...].astype(BF16), _NT, preferred_element_type=F32)
    z = (acc * cs_ref[...]).astype(o_ref.dtype)
    o_ref[...] = z

    @pl.when(j == CKV_TILE)
    def _():
        for c in range(INPROJ_TN // HEAD_DIM):
            ckv_ref[c] = z[:, c * HEAD_DIM:(c + 1) * HEAD_DIM]


def _inproj(x, w_in_t, colscale, w_gate_t, tm=1024):
    m, k = x.shape
    tn = INPROJ_TN
    return pl.pallas_call(
        _inproj_kernel,
        grid=(m // tm, GATE_COL0 // tn),
        in_specs=[pl.BlockSpec((tm, k), lambda i, j: (i, 0)),
                  pl.BlockSpec((tn, k), lambda i, j: (j, 0)),
                  pl.BlockSpec((1, tn), lambda i, j: (0, j)),
                  pl.BlockSpec((HEAD_DIM, k), lambda i, j: (0, 0))],
        out_specs=[pl.BlockSpec((tm, tn), lambda i, j: (i, j)),
                   pl.BlockSpec((tm, HEAD_DIM), lambda i, j: (i, 0)),
                   pl.BlockSpec((tn // HEAD_DIM, tm, HEAD_DIM), lambda i, j: (0, i, 0))],
        out_shape=[jax.ShapeDtypeStruct((m, GATE_COL0), BF16),
                   jax.ShapeDtypeStruct((m, HEAD_DIM), F32),
                   jax.ShapeDtypeStruct((tn // HEAD_DIM, m, HEAD_DIM), BF16)],
        scratch_shapes=[pltpu.VMEM((tm, k), BF16)],
        compiler_params=_cparams("parallel", "arbitrary"),
        name="inproj",
    )(x, w_in_t, colscale, w_gate_t)


def _gelu_tanh(x):
    return 0.5 * x * (1.0 + jnp.tanh(math.sqrt(2.0 / math.pi) * (x + 0.044715 * (x * x * x))))


def _compress_kernel(xk_ref, xv_ref, pek_ref, w1k_ref, w2k_ref, pev_ref, w1v_ref, w2v_ref, kc_ref, vct_ref):
    def compress(x_ref, pe_ref, w1_ref, w2_ref):
        x = x_ref[0].astype(F32)
        lo = jnp.dot((x + pe_ref[0]).astype(BF16), w1_ref[0].astype(BF16), preferred_element_type=F32)
        hi = jnp.dot((x + pe_ref[1]).astype(BF16), w1_ref[1].astype(BF16), preferred_element_type=F32)
        y = lo + pltpu.roll(hi, N_CMP, 0)
        out = jnp.dot(_gelu_tanh(y).astype(BF16), w2_ref[...].astype(BF16), preferred_element_type=F32)
        row = lax.broadcasted_iota(jnp.int32, out.shape, 0)
        return jnp.where(row < N_CMP, out, 0.0)

    kc_ref[0] = compress(xk_ref, pek_ref, w1k_ref, w2k_ref).astype(kc_ref.dtype)
    vct_ref[0] = compress(xv_ref, pev_ref, w1v_ref, w2v_ref).T.astype(vct_ref.dtype)


def _compress(xr, pe_k, w1_k, w2_k, pe_v, w1_v, w2_v):
    half = CMP_STRIDE * HEAD_DIM
    weights = [pl.BlockSpec((2, 1, half), lambda g: (0, 0, 0)),
               pl.BlockSpec((2, half, HEAD_DIM), lambda g: (0, 0, 0)),
               pl.BlockSpec((HEAD_DIM, HEAD_DIM), lambda g: (0, 0))]
    return pl.pallas_call(
        _compress_kernel,
        grid=(NSA_KV_HEADS,),
        in_specs=[pl.BlockSpec((1, 512, half), lambda g: (g, 0, 0)),
                  pl.BlockSpec((1, 512, half), lambda g: (NSA_KV_HEADS + g, 0, 0))] + weights + weights,
        out_specs=[pl.BlockSpec((1, 512, HEAD_DIM), lambda g: (g, 0, 0)),
                   pl.BlockSpec((1, HEAD_DIM, 512), lambda g: (g, 0, 0))],
        out_shape=[jax.ShapeDtypeStruct((NSA_KV_HEADS, 512, HEAD_DIM), BF16),
                   jax.ShapeDtypeStruct((NSA_KV_HEADS, HEAD_DIM, 512), BF16)],
        compiler_params=_cparams("parallel"),
        name="compress",
    )(xr, xr, pe_k, w1_k, w2_k, pe_v, w1_v, w2_v)


def _split3(x):
    hi = x.astype(BF16)
    r1 = x - hi.astype(F32)
    mid = r1.astype(BF16)
    lo = (r1 - mid.astype(F32)).astype(BF16)
    return hi, mid, lo


def _topk_bias_t(score, k):
    row = lax.broadcasted_iota(jnp.int32, score.shape, 0).astype(F32)
    selb = jnp.full(score.shape, NEG, F32)
    s = score
    for _ in range(k):
        mx = jnp.max(s, axis=0, keepdims=True)
        idx = jnp.min(jnp.where(s == mx, row, 1e9), axis=0, keepdims=True)
        hit = row == idx
        selb = jnp.where(hit, 0.0, selb)
        s = jnp.where(hit, -3e38, s)
    return selb


def _lane_tile_max(s):
    m = s[:, :HEAD_DIM]
    for c in range(1, s.shape[1] // HEAD_DIM):
        m = jnp.maximum(m, s[:, c * HEAD_DIM:(c + 1) * HEAD_DIM])
    return m


def _sparse_flash(qs, selbs, k_ref, v_ref, kcols, pat_ref, d0s, d1s, i, sub, scratch):
    t_ref, *maybe_r_ref, mx_ref, acc_ref = scratch
    r_ref = maybe_r_ref[0] if maybe_r_ref else None
    heads = range(len(qs))
    lane = lax.broadcasted_iota(jnp.int32, (TQ, HEAD_DIM), 1)
    far = lane < (i - 1) * sub
    q_near = [jnp.concatenate([qs[h], selbs[h].astype(BF16)], axis=1) for h in heads]
    q_far = [jnp.concatenate([qs[h], jnp.where(far, selbs[h], NEG).astype(BF16)], axis=1) for h in heads]

    def k_aug(h, off, n):
        c = kcols[h]
        return jnp.concatenate([k_ref[pl.ds(off, n), c:c + HEAD_DIM], pat_ref[pl.ds(off, n), :]], axis=1)

    def v_aug(h, off, n):
        c = kcols[h]
        return jnp.concatenate([v_ref[pl.ds(off, n), c:c + HEAD_DIM], jnp.ones((n, HEAD_DIM), BF16)], axis=1)

    for h in heads:
        mx_ref[h] = jnp.full((TQ, HEAD_DIM), NEG, F32)

    def score(g):
        off = pl.multiple_of(g * GROUP, GROUP)
        for h in heads:
            s = lax.dot_general(q_far[h], k_aug(h, off, GROUP), _NT, preferred_element_type=F32)
            ref = _lane_tile_max(s)
            if r_ref is not None:
                s = s - jnp.concatenate([ref] * (GROUP // HEAD_DIM), axis=1)
                r_ref[h, g] = ref
            for u in range(KG):
                t_ref[h, g * KG + u] = s[:, u * TQ:(u + 1) * TQ].astype(t_ref.dtype)
            mx_ref[h] = jnp.maximum(mx_ref[h], ref)

    def score_pair(m, carry):
        score(2 * m)
        score(2 * m + 1)
        return carry

    n_far = lax.shift_right_logical(jnp.maximum(i - 1, 0) + (KG - 1), KG.bit_length() - 1)
    lax.fori_loop(0, lax.shift_right_logical(n_far, 1), score_pair, 0)

    @pl.when((n_far & 1) == 1)
    def _():
        score(n_far - 1)

    off_prev = pl.multiple_of(jnp.maximum(i - 1, 0) * TQ, TQ)
    off_own = pl.multiple_of(i * TQ, TQ)
    p_near = []
    for h in heads:
        k2 = jnp.concatenate([k_aug(h, off_prev, TQ), k_aug(h, off_own, TQ)], axis=0)
        bias = jnp.concatenate([jnp.where(i >= 1, d1s[h], NEG), d0s[h]], axis=1)
        s = lax.dot_general(q_near[h], k2, _NT, preferred_element_type=F32) + bias
        m = jnp.max(jnp.maximum(mx_ref[h], _lane_tile_max(s)), axis=1, keepdims=True)
        mx_ref[h] = jnp.broadcast_to(m, (TQ, HEAD_DIM))
        p_near.append(jnp.exp2(s - m).astype(BF16))
    for h in heads:
        v2 = jnp.concatenate([v_aug(h, off_prev, TQ), v_aug(h, off_own, TQ)], axis=0)
        acc_ref[h] = jnp.dot(p_near[h], v2, preferred_element_type=F32)

    def accumulate(g0, ng):
        off = pl.multiple_of(g0 * GROUP, GROUP)
        for h in heads:
            probs = []
            for g in [g0 + k for k in range(ng)]:
                d = -mx_ref[h] if r_ref is None else r_ref[h, g] - mx_ref[h]
                d2 = jnp.concatenate([d, d], axis=1)
                probs += [jnp.exp2(t_ref[h, g * KG + u].astype(F32) + d2).astype(BF16) for u in range(KG)]
            acc_ref[h] += jnp.dot(jnp.concatenate(probs, axis=1), v_aug(h, off, ng * GROUP),
                                  preferred_element_type=F32)

    def pv_pair(m, carry):
        accumulate(2 * m, 2)
        return carry

    lax.fori_loop(0, lax.shift_right_logical(n_far, 1), pv_pair, 0)

    @pl.when((n_far & 1) == 1)
    def _():
        accumulate(n_far - 1, 1)

    outs = []
    for h in heads:
        acc = acc_ref[h]
        outs.append(acc[:, :HEAD_DIM] / acc[:, HEAD_DIM:])
    return outs


def _flash_scratch(hp, relative_bf16):
    logits = [pltpu.VMEM((hp, N_QT, TQ, TQ), BF16),
              pltpu.VMEM((hp, SEQ // GROUP, TQ, HEAD_DIM), F32)]
    if not relative_bf16:
        logits = [pltpu.VMEM((hp, N_QT, TQ, TQ), F32)]
    return logits + [pltpu.VMEM((hp, TQ, HEAD_DIM), F32),
                     pltpu.VMEM((hp, TQ, 2 * HEAD_DIM), F32)]


SEL_CHUNK = 4096


def _moba_select_kernel(q_ref, k_ref, sel_ref):
    blk = lax.broadcasted_iota(jnp.int32, (HEAD_DIM, SEQ), 0)
    pos = lax.broadcasted_iota(jnp.int32, (HEAD_DIM, SEQ), 1)
    ind = jnp.where(lax.shift_right_logical(pos, 8) == blk, 1.0 / MOBA_BLOCK, 0.0).astype(BF16)
    n_blk = SEQ // MOBA_BLOCK
    kmean = _split3(jnp.dot(ind, k_ref[...], preferred_element_type=F32)[:n_blk])
    blk_t = lax.broadcasted_iota(jnp.int32, (n_blk, SEL_CHUNK), 0)
    col = lax.broadcasted_iota(jnp.int32, (n_blk, SEL_CHUNK), 1)
    no_block = jnp.full((HEAD_DIM - n_blk, SEL_CHUNK), NEG, F32)

    def chunk(c, carry):
        off = pl.multiple_of(c * SEL_CHUNK, SEL_CHUNK)
        q = q_ref[pl.ds(off, SEL_CHUNK), :]
        gate = jnp.zeros((n_blk, SEL_CHUNK), F32)
        for part in kmean:
            gate = gate + lax.dot_general(part, q, _NT, preferred_element_type=F32)
        own = lax.shift_right_logical(off + col, 8)
        past = blk_t < own
        selb = _topk_bias_t(jnp.where(past, gate, NEG), MOBA_TOPK)
        selb = jnp.where(past, selb, jnp.where(blk_t == own, 0.0, NEG))
        sel_ref[0, pl.ds(off, SEL_CHUNK), :] = jnp.concatenate([selb, no_block], axis=0).T.astype(sel_ref.dtype)
        return carry

    lax.fori_loop(0, SEQ // SEL_CHUNK, chunk, 0)


def _moba_select(zb):
    return pl.pallas_call(
        _moba_select_kernel,
        grid=(MOBA_HEADS,),
        in_specs=[pl.BlockSpec((SEQ, HEAD_DIM), lambda h: (0, COL_MQ + h)),
                  pl.BlockSpec((SEQ, HEAD_DIM), lambda h: (0, COL_MK + h))],
        out_specs=pl.BlockSpec((1, SEQ, HEAD_DIM), lambda h: (h, 0, 0)),
        out_shape=jax.ShapeDtypeStruct((MOBA_HEADS, SEQ, HEAD_DIM), BF16),
        compiler_params=_cparams("parallel"),
        name="moba_select",
    )(zb, zb)


def _moba_kernel(q_ref, sel_ref, k_ref, v_ref, pat_ref, d0_ref, d1_ref, o_ref, *flash_scratch):
    i = pl.program_id(1)
    hp = HP_MOBA
    qs = [q_ref[:, h * HEAD_DIM:(h + 1) * HEAD_DIM] for h in range(hp)]
    selbs = [sel_ref[h].astype(F32) for h in range(hp)]
    cols = [h * HEAD_DIM for h in range(hp)]
    outs = _sparse_flash(qs, selbs, k_ref, v_ref, cols, pat_ref, [d0_ref[0, h] for h in range(hp)],
                         [d1_ref[0, h] for h in range(hp)], i, 1, flash_scratch)
    for h in range(hp):
        o_ref[:, h * HEAD_DIM:(h + 1) * HEAD_DIM] = outs[h].astype(o_ref.dtype)


def _moba(zb, selb, near, pat):
    hp = HP_MOBA
    w = hp * HEAD_DIM
    return pl.pallas_call(
        _moba_kernel,
        grid=(MOBA_HEADS // hp, N_QT),
        in_specs=[pl.BlockSpec((TQ, w), lambda h, i: (i, COL_MQ // hp + h)),
                  pl.BlockSpec((hp, TQ, HEAD_DIM), lambda h, i: (h, i, 0)),
                  pl.BlockSpec((SEQ, w), lambda h, i: (0, COL_MK // hp + h), pipeline_mode=pl.Buffered(1)),
                  pl.BlockSpec((SEQ, w), lambda h, i: (0, COL_MV // hp + h), pipeline_mode=pl.Buffered(1)),
                  pl.BlockSpec((SEQ, HEAD_DIM), lambda h, i: (0, 0), pipeline_mode=pl.Buffered(1)),
                  pl.BlockSpec((1, hp, TQ, TQ), lambda h, i: (0, h, 0, 0)),
                  pl.BlockSpec((1, hp, TQ, TQ), lambda h, i: (1, h, 0, 0))],
        out_specs=pl.BlockSpec((TQ, w), lambda h, i: (i, h)),
        out_shape=jax.ShapeDtypeStruct((SEQ, MOBA_HEADS * HEAD_DIM), BF16),
        scratch_shapes=_flash_scratch(hp, True),
        compiler_params=_cparams("parallel", "arbitrary", vmem=BIG_VMEM_LIMIT),
        name="moba",
    )(zb, selb, zb, zb, pat, near, near)


def _cmp_kernel(q_ref, kc_ref, vct_ref, ovt_ref, cb_ref, oc_ref, sel_ref, s_ref):
    i = pl.program_id(0)
    win0 = pl.multiple_of(jnp.maximum(i * CMP_STRIDE - CMP_STRIDE, 0), CMP_STRIDE)

    def body(nk):
        nb = nk * CMP_STRIDE // SLC_LEN
        blk = lax.broadcasted_iota(jnp.int32, (nb, TQ), 0)
        cur = i * SLC_PER_TILE + lax.shift_right_logical(lax.broadcasted_iota(jnp.int32, (nb, TQ), 1), 6)
        allowed = blk <= cur
        forced = (blk == 0) | (blk == cur) | (blk == cur - 1)
        reachable = lax.broadcasted_iota(jnp.int32, (nk, TQ), 0) < win0 + CMP_WIN
        for g in range(NSA_KV_HEADS):
            p_sum = jnp.zeros((nk, TQ), F32)
            for r in range(NSA_GROUP):
                hd = g * NSA_GROUP + r
                q = q_ref[:, hd * HEAD_DIM:(hd + 1) * HEAD_DIM]
                s_ref[:nk, :] = lax.dot_general(kc_ref[g, :nk, :], q, _NT, preferred_element_type=F32)
                s_ref[pl.ds(win0, CMP_WIN), :] += cb_ref[0, hd]
                s = jnp.where(reachable, s_ref[:nk, :], NEG)
                valid = s > 0.5 * NEG
                e = jnp.where(valid, jnp.exp2(s - jnp.max(s, axis=0, keepdims=True)), 0.0)
                p = e * (1.0 / jnp.maximum(jnp.sum(e, axis=0, keepdims=True), TINY))
                o_t = jnp.dot(vct_ref[g, :, :nk], p.astype(BF16), preferred_element_type=F32)
                oc_ref[:, hd * HEAD_DIM:(hd + 1) * HEAD_DIM] = o_t.T
                p_sum = p_sum + p
            imp = jnp.zeros((nb, TQ), F32)
            for part in _split3(p_sum):
                imp = imp + jnp.dot(ovt_ref[:nb, :nk], part, preferred_element_type=F32)
            picked = _topk_bias_t(jnp.where(allowed & ~forced, imp, -1.0), SLC_TOPK - 3)
            selb = jnp.where(allowed, jnp.where(forced, 0.0, picked), NEG)
            if nb < N_SLC:
                selb = jnp.concatenate([selb, jnp.full((N_SLC - nb, TQ), NEG, F32)], axis=0)
            sel_ref[g] = selb.T.astype(sel_ref.dtype)

    for c in range(1, 512 // HEAD_DIM + 1):
        @pl.when(lax.shift_right_logical(i, 3) == c - 1)
        def _():
            body(c * HEAD_DIM)


def _cmp_select(zb, kc, vct, ovt, cmpb):
    w = NSA_HEADS * HEAD_DIM
    return pl.pallas_call(
        _cmp_kernel,
        grid=(N_QT,),
        in_specs=[pl.BlockSpec((TQ, w), lambda i: (i, COL_NQ // NSA_HEADS)),
                  pl.BlockSpec((NSA_KV_HEADS, 512, HEAD_DIM), lambda i: (0, 0, 0)),
                  pl.BlockSpec((NSA_KV_HEADS, HEAD_DIM, 512), lambda i: (0, 0, 0)),
                  pl.BlockSpec((N_SLC, 512), lambda i: (0, 0)),
                  pl.BlockSpec((1, NSA_HEADS, CMP_WIN, TQ), lambda i: (jnp.minimum(i, 1), 1, 0, 0))],
        out_specs=[pl.BlockSpec((TQ, w), lambda i: (i, 0)),
                   pl.BlockSpec((NSA_KV_HEADS, TQ, N_SLC), lambda i: (0, i, 0))],
        out_shape=[jax.ShapeDtypeStruct((SEQ, w), F32),
                   jax.ShapeDtypeStruct((NSA_KV_HEADS, SEQ, N_SLC), BF16)],
        scratch_shapes=[pltpu.VMEM((512, TQ), F32)],
        compiler_params=_cparams("arbitrary"),
        name="cmp_select",
    )(zb, kc, vct, ovt, cmpb)


def _slc_kernel(q_ref, sel_ref, k_ref, v_ref, pat_ref, d0_ref, d1_ref, o_ref, *flash_scratch):
    i = pl.program_id(1)
    hp = HP_SLC
    selb = sel_ref[0].astype(F32)
    qs = [q_ref[:, h * HEAD_DIM:(h + 1) * HEAD_DIM] for h in range(hp)]
    outs = _sparse_flash(qs, [selb] * hp, k_ref, v_ref, [0] * hp, pat_ref,
                         [d0_ref[0, h] for h in range(hp)], [d1_ref[0, h] for h in range(hp)],
                         i, SLC_PER_TILE, flash_scratch)
    for h in range(hp):
        o_ref[:, h * HEAD_DIM:(h + 1) * HEAD_DIM] = outs[h].astype(o_ref.dtype)


def _slc(zb, selb, near, pat):
    hp = HP_SLC
    w = hp * HEAD_DIM
    per_group = NSA_GROUP // hp
    return pl.pallas_call(
        _slc_kernel,
        grid=(NSA_HEADS // hp, N_QT),
        in_specs=[pl.BlockSpec((TQ, w), lambda h, i: (i, COL_NQ // hp + h)),
                  pl.BlockSpec((1, TQ, N_SLC), lambda h, i: (h // per_group, i, 0)),
                  pl.BlockSpec((SEQ, HEAD_DIM), lambda h, i: (0, COL_SK + h // per_group),
                               pipeline_mode=pl.Buffered(1)),
                  pl.BlockSpec((SEQ, HEAD_DIM), lambda h, i: (0, COL_SV + h // per_group),
                               pipeline_mode=pl.Buffered(1)),
                  pl.BlockSpec((SEQ, HEAD_DIM), lambda h, i: (0, 0), pipeline_mode=pl.Buffered(1)),
                  pl.BlockSpec((1, hp, TQ, TQ), lambda h, i: (0, MOBA_HEADS // hp + h, 0, 0)),
                  pl.BlockSpec((1, hp, TQ, TQ), lambda h, i: (1, MOBA_HEADS // hp + h, 0, 0))],
        out_specs=pl.BlockSpec((TQ, w), lambda h, i: (i, h)),
        out_shape=jax.ShapeDtypeStruct((SEQ, NSA_HEADS * HEAD_DIM), F32),
        scratch_shapes=_flash_scratch(hp, False),
        compiler_params=_cparams("parallel", "arbitrary", vmem=BIG_VMEM_LIMIT),
        name="slc",
    )(zb, selb, zb, zb, pat, near, near)


def _win_kernel(q_ref, k_ref, v_ref, d0_ref, d1_ref, w2_ref, o_ref):
    i = pl.program_id(0)
    offs = [pl.multiple_of(jnp.maximum(i - b, 0) * TQ, TQ) for b in range(3)]
    mask2 = jnp.where(i >= 2, w2_ref[...], NEG)
    ones = jnp.ones((3 * TQ, HEAD_DIM), BF16)
    for g in range(NSA_KV_HEADS):
        cols = slice(g * HEAD_DIM, (g + 1) * HEAD_DIM)
        k_all = jnp.concatenate([k_ref[pl.ds(off, TQ), cols] for off in offs], axis=0)
        v_aug = jnp.concatenate([jnp.concatenate([v_ref[pl.ds(off, TQ), cols] for off in offs], axis=0), ones],
                                axis=1)
        for r in range(NSA_GROUP):
            hd = g * NSA_GROUP + r
            q = q_ref[:, hd * HEAD_DIM:(hd + 1) * HEAD_DIM]
            bias = jnp.concatenate([d0_ref[0, hd], jnp.where(i >= 1, d1_ref[0, hd], NEG), mask2], axis=1)
            s = lax.dot_general(q, k_all, _NT, preferred_element_type=F32) + bias
            m = jnp.max(s, axis=1, keepdims=True)
            pv = jnp.dot(jnp.exp2(s - m).astype(BF16), v_aug, preferred_element_type=F32)
            o_ref[:, hd * HEAD_DIM:(hd + 1) * HEAD_DIM] = pv[:, :HEAD_DIM] / pv[:, HEAD_DIM:]


def _win(zb, near, win2):
    w = NSA_HEADS * HEAD_DIM
    kv_w = NSA_KV_HEADS * HEAD_DIM
    return pl.pallas_call(
        _win_kernel,
        grid=(N_QT,),
        in_specs=[pl.BlockSpec((TQ, w), lambda i: (i, COL_NQ // NSA_HEADS)),
                  pl.BlockSpec((SEQ, kv_w), lambda i: (0, COL_WK // NSA_KV_HEADS), pipeline_mode=pl.Buffered(1)),
                  pl.BlockSpec((SEQ, kv_w), lambda i: (0, COL_WV // NSA_KV_HEADS), pipeline_mode=pl.Buffered(1)),
                  pl.BlockSpec((1, NSA_HEADS, TQ, TQ), lambda i: (0, 1, 0, 0), pipeline_mode=pl.Buffered(1)),
                  pl.BlockSpec((1, NSA_HEADS, TQ, TQ), lambda i: (1, 1, 0, 0), pipeline_mode=pl.Buffered(1)),
                  pl.BlockSpec((TQ, TQ), lambda i: (0, 0), pipeline_mode=pl.Buffered(1))],
        out_specs=pl.BlockSpec((TQ, w), lambda i: (i, 0)),
        out_shape=jax.ShapeDtypeStruct((SEQ, w), F32),
        compiler_params=_cparams("arbitrary"),
        name="win",
    )(zb, zb, zb, near, near, win2)


def _layer_norm(y, g, b):
    mu = jnp.mean(y, axis=-1, keepdims=True)
    yc = y - mu
    var = jnp.mean(yc * yc, axis=-1, keepdims=True)
    return yc * lax.rsqrt(var + LN_EPS) * g + b


ROW_CHUNK = 256


def _outproj_kernel(x_ref, om_ref, oc_ref, os_ref, ow_ref, g_ref, e_ref, w_ref, lg_ref, lb_ref,
                    h_ref, hb_ref):
    for c0 in range(0, x_ref.shape[0], ROW_CHUNK):
        rows = slice(c0, c0 + ROW_CHUNK)
        gate = g_ref[rows, :]
        hi = gate.astype(BF16)
        lo = (gate - hi.astype(F32)).astype(BF16)
        g2 = jnp.concatenate([hi, lo], axis=1)
        o_n = jnp.zeros((ROW_CHUNK, oc_ref.shape[1]), F32)
        for c, branch in enumerate((oc_ref, os_ref, ow_ref)):
            o_n = o_n + jnp.dot(g2, e_ref[c], preferred_element_type=F32) * branch[rows, :]
        o = jnp.concatenate([om_ref[rows, :], o_n.astype(BF16)], axis=1)
        a = jnp.dot(o, w_ref[...], preferred_element_type=F32)
        h = _layer_norm(DN_ALPHA * x_ref[rows, :] + a, lg_ref[...], lb_ref[...])
        h_ref[rows, :] = h
        hb_ref[rows, :] = h.astype(BF16)


def _resident(shape):
    return pl.BlockSpec(shape, lambda i: (0,) * len(shape), pipeline_mode=pl.Buffered(1))


def _outproj(x, o_m, o_c, o_s, o_w, gates, expand, w_out, ln_g, ln_b, tm=512):
    half = NSA_HEADS * HEAD_DIM
    row = lambda i: (i, 0)
    return pl.pallas_call(
        _outproj_kernel,
        grid=(SEQ // tm,),
        in_specs=[pl.BlockSpec((tm, D_MODEL), row),
                  pl.BlockSpec((tm, half), row), pl.BlockSpec((tm, half), row),
                  pl.BlockSpec((tm, half), row), pl.BlockSpec((tm, half), row),
                  pl.BlockSpec((tm, HEAD_DIM), row),
                  _resident((3, 2 * HEAD_DIM, half)),
                  _resident((D_MODEL, D_MODEL)),
                  _resident((1, D_MODEL)), _resident((1, D_MODEL))],
        out_specs=[pl.BlockSpec((tm, D_MODEL), row), pl.BlockSpec((tm, D_MODEL), row)],
        out_shape=[jax.ShapeDtypeStruct((SEQ, D_MODEL), F32),
                   jax.ShapeDtypeStruct((SEQ, D_MODEL), BF16)],
        compiler_params=_cparams("parallel", vmem=BIG_VMEM_LIMIT),
        name="outproj_ln",
    )(x, o_m, o_c, o_s, o_w, gates, expand, w_out, ln_g, ln_b)


def _ffn_kernel(hb_ref, w1_ref, w2_ref, f_ref):
    @pl.when(pl.program_id(1) == 0)
    def _():
        f_ref[...] = jnp.zeros(f_ref.shape, F32)

    u = jnp.maximum(jnp.dot(hb_ref[...], w1_ref[...].astype(BF16), preferred_element_type=F32), 0.0)
    f_ref[...] += jnp.dot((u * u).astype(BF16), w2_ref[...].astype(BF16), preferred_element_type=F32)


def _ffn(hb, w1, w2, tm=1024, tf=512):
    return pl.pallas_call(
        _ffn_kernel,
        grid=(SEQ // tm, D_FF // tf),
        in_specs=[pl.BlockSpec((tm, D_MODEL), lambda i, c: (i, 0)),
                  pl.BlockSpec((D_MODEL, tf), lambda i, c: (0, c)),
                  pl.BlockSpec((tf, D_MODEL), lambda i, c: (c, 0))],
        out_specs=pl.BlockSpec((tm, D_MODEL), lambda i, c: (i, 0)),
        out_shape=jax.ShapeDtypeStruct((SEQ, D_MODEL), F32),
        compiler_params=_cparams("parallel", "arbitrary", vmem=BIG_VMEM_LIMIT),
        name="ffn",
    )(hb, w1, w2)


def _final_kernel(h_ref, hb_ref, f_ref, p_ref, wg_ref, wp_ref, lg_ref, lb_ref, o_ref):
    for c0 in range(0, h_ref.shape[0], ROW_CHUNK):
        rows = slice(c0, c0 + ROW_CHUNK)
        gate = jax.nn.sigmoid(jnp.dot(hb_ref[rows, :], wg_ref[...], preferred_element_type=F32))
        emb = jnp.dot(p_ref[rows, :].astype(BF16), wp_ref[...], preferred_element_type=F32)
        y = DN_ALPHA * h_ref[rows, :] + f_ref[rows, :] + gate * emb
        o_ref[rows, :] = _layer_norm(y, lg_ref[...], lb_ref[...])


def _final(h, hb, f, p, wg, wp, ln_g, ln_b, tm=512):
    row = lambda i: (i, 0)
    return pl.pallas_call(
        _final_kernel,
        grid=(SEQ // tm,),
        in_specs=[pl.BlockSpec((tm, D_MODEL), row), pl.BlockSpec((tm, D_MODEL), row),
                  pl.BlockSpec((tm, D_MODEL), row), pl.BlockSpec((tm, D_PLE), row),
                  _resident((D_MODEL, D_MODEL)), _resident((D_PLE, D_MODEL)),
                  _resident((1, D_MODEL)), _resident((1, D_MODEL))],
        out_specs=pl.BlockSpec((tm, D_MODEL), row),
        out_shape=jax.ShapeDtypeStruct((SEQ, D_MODEL), F32),
        compiler_params=_cparams("parallel"),
        name="final_ln",
    )(h, hb, f, p, wg, wp, ln_g, ln_b)


def _overlap_matrix_t():
    j = np.arange(N_SLC)[:, None]
    n = np.arange(512)[None, :]
    ov = ((n * CMP_STRIDE < j * SLC_LEN + SLC_LEN) & (n * CMP_STRIDE + CMP_LEN > j * SLC_LEN) & (n < N_CMP))
    return ov.astype(np.float32)


def _gate_expand():
    e = np.zeros((3, 2 * HEAD_DIM, NSA_HEADS * HEAD_DIM), np.float32)
    for c in range(3):
        for h in range(NSA_HEADS):
            e[c, 3 * h + c, h * HEAD_DIM:(h + 1) * HEAD_DIM] = 1.0
            e[c, HEAD_DIM + 3 * h + c, h * HEAD_DIM:(h + 1) * HEAD_DIM] = 1.0
    return e


def _layer(h, p, w_in, cmp_pe_k, cmp_w1_k, cmp_w2_k, cmp_pe_v, cmp_w1_v, cmp_w2_v, tiles, w_out,
           ln1_g, ln1_b, w_ff1, w_ff2, w_ple, w_ple_gate, ln2_g, ln2_b):
    near, win2, cmpb = tiles
    scale = HEAD_DIM ** -0.5 * LOG2E
    colscale = np.ones((1, GATE_COL0), np.float32)
    colscale[0, COL_MQ * HEAD_DIM:(COL_MQ + MOBA_HEADS) * HEAD_DIM] = scale
    colscale[0, COL_NQ * HEAD_DIM:(COL_NQ + NSA_HEADS) * HEAD_DIM] = scale
    w_in_t = w_in.T
    w_gl_t = jnp.pad(w_in_t[GATE_COL0:], ((0, HEAD_DIM - NSA_HEADS * 3), (0, 0))).astype(BF16)
    zb, gates, ckv = _inproj(h, w_in_t, jnp.asarray(colscale), w_gl_t)

    half = CMP_STRIDE * HEAD_DIM
    xr = ckv.reshape(4, SEQ // CMP_STRIDE, half)
    kc, vct = _compress(xr, cmp_pe_k.reshape(2, 1, half), cmp_w1_k.reshape(2, half, HEAD_DIM), cmp_w2_k,
                        cmp_pe_v.reshape(2, 1, half), cmp_w1_v.reshape(2, half, HEAD_DIM), cmp_w2_v)

    key = np.arange(SEQ)[:, None]
    lane = np.arange(HEAD_DIM)[None, :]
    pat_moba = jnp.asarray(lane == key // MOBA_BLOCK, BF16)
    pat_slc = jnp.asarray(lane == key // SLC_LEN, BF16)
    o_m = _moba(zb, _moba_select(zb), near, pat_moba)
    o_c, selb = _cmp_select(zb, kc, vct, jnp.asarray(_overlap_matrix_t(), BF16), cmpb)
    o_s = _slc(zb, selb, near, pat_slc)
    o_w = _win(zb, near, win2)

    h1, h1b = _outproj(h, o_m, o_c, o_s, o_w, gates, jnp.asarray(_gate_expand(), BF16),
                       w_out.astype(BF16), ln1_g.reshape(1, -1), ln1_b.reshape(1, -1))
    f = _ffn(h1b, w_ff1, w_ff2)
    return _final(h1, h1b, f, p, w_ple_gate.astype(BF16), w_ple.astype(BF16),
                  ln2_g.reshape(1, -1), ln2_b.reshape(1, -1))


def kernel(x, p, w_in, cmp_pe_k, cmp_w1_k, cmp_w2_k, cmp_pe_v, cmp_w1_v, cmp_w2_v, rel_bias, w_out,
           ln1_g, ln1_b, w_ff1, w_ff2, w_ple, w_ple_gate, ln2_g, ln2_b):
    tiles = _bias_tiles(rel_bias)
    h = x[0]
    for i in range(w_in.shape[0]):
        h = _layer(h, p[i, 0], w_in[i], cmp_pe_k[i], cmp_w1_k[i], cmp_w2_k[i], cmp_pe_v[i], cmp_w1_v[i],
                   cmp_w2_v[i], tiles, w_out[i], ln1_g[i], ln1_b[i], w_ff1[i], w_ff2[i], w_ple[i],
                   w_ple_gate[i], ln2_g[i], ln2_b[i])
    return h[None]
```

```python
import functools
import math

import numpy as np
import jax
import jax.numpy as jnp
from jax import lax
from jax.experimental import pallas as pl
from jax.experimental.pallas import tpu as pltpu

D_MODEL = 2048
SEQ = 8192
HEAD_DIM = 128
N_HEADS = 16
MOBA_HEADS = 8
NSA_HEADS = 8
NSA_KV_HEADS = 2
NSA_GROUP = 4
MOBA_BLOCK = 256
MOBA_TOPK = 3
CMP_LEN = 32
CMP_STRIDE = 16
SLC_LEN = 64
SLC_TOPK = 16
WINDOW = 512
N_BUCKETS = 32
MAX_DISTANCE = 128
D_FF = 4 * D_MODEL
D_PLE = 256
LN_EPS = 1e-5
DN_ALPHA = 2.0 ** 0.25
NEG = -1e30
LOG2E = math.log2(math.e)
TINY = 1e-30
FORCE_SCORE = 1e4

N_CMP = (SEQ - CMP_LEN) // CMP_STRIDE + 1
N_SLC = SEQ // SLC_LEN
TQ = 256
N_QT = SEQ // TQ
CMP_WIN = 32
SLC_PER_TILE = TQ // SLC_LEN
HP_MOBA = 4
HP_SLC = 4
KG = 4
GROUP = KG * TQ
COL_MQ, COL_MK, COL_MV, COL_NQ = 0, 8, 16, 24
COL_CK, COL_CV, COL_SK, COL_SV, COL_WK, COL_WV = 32, 34, 36, 38, 40, 42
N_COLBLK = 44
GATE_COL0 = N_COLBLK * HEAD_DIM

VMEM_LIMIT = 48 * 1024 * 1024
BIG_VMEM_LIMIT = 56 * 1024 * 1024

BF16 = jnp.bfloat16
F32 = jnp.float32
_NT = (((1,), (1,)), ((), ()))


def _cparams(*sem, vmem=VMEM_LIMIT):
    return pltpu.CompilerParams(dimension_semantics=sem, vmem_limit_bytes=vmem)


def _bucket_table():
    n = np.arange(1024, dtype=np.int32)
    max_exact = N_BUCKETS // 2
    ratio = np.maximum(n, 1).astype(np.float32) / np.float32(max_exact)
    large = max_exact + (np.log(ratio).astype(np.float32) / np.float32(math.log(MAX_DISTANCE / max_exact))
                         * np.float32(N_BUCKETS - max_exact)).astype(np.int32)
    large = np.minimum(large, N_BUCKETS - 1)
    return np.where(n < max_exact, n, large).astype(np.int32)


def _bias_kernel(tab_ref, near_ref, cmpb_ref, *, steps):
    h = pl.program_id(0)
    far = tab_ref[h, N_BUCKETS - 1]

    def lookup(d):
        val = jnp.full(d.shape, tab_ref[h, 0], F32)
        for start, bucket in steps:
            val = jnp.where(d >= start, tab_ref[h, bucket], val)
        return jnp.where(d >= 0, (val - far) * LOG2E, NEG)

    a = lax.broadcasted_iota(jnp.int32, (TQ, TQ), 0)
    b = lax.broadcasted_iota(jnp.int32, (TQ, TQ), 1)
    near_ref[0, 0] = lookup(a - b)
    near_ref[1, 0] = lookup(a - b + TQ)
    w = lax.broadcasted_iota(jnp.int32, (CMP_WIN, TQ), 0)
    aw = lax.broadcasted_iota(jnp.int32, (CMP_WIN, TQ), 1)
    cmpb_ref[0, 0] = lookup(aw - CMP_STRIDE * w - (CMP_LEN - 1))
    cmpb_ref[1, 0] = lookup(aw - CMP_STRIDE * w + (TQ - CMP_LEN + 1))


def _bias_tiles(rel_bias):
    bucket = _bucket_table()
    steps = tuple((int(n), int(bucket[n])) for n in range(1, bucket.size) if bucket[n] != bucket[n - 1])
    near, cmpb = pl.pallas_call(
        functools.partial(_bias_kernel, steps=steps),
        grid=(N_HEADS,),
        in_specs=[pl.BlockSpec(memory_space=pltpu.SMEM)],
        out_specs=[pl.BlockSpec((2, 1, TQ, TQ), lambda h: (0, h, 0, 0)),
                   pl.BlockSpec((2, 1, CMP_WIN, TQ), lambda h: (0, h, 0, 0))],
        out_shape=[jax.ShapeDtypeStruct((2, N_HEADS, TQ, TQ), F32),
                   jax.ShapeDtypeStruct((2, N_HEADS, CMP_WIN, TQ), F32)],
        compiler_params=_cparams("parallel"),
        name="bias_tiles",
    )(rel_bias.T.astype(F32))
    a = np.arange(TQ)[:, None]
    b = np.arange(TQ)[None, :]
    win2 = jnp.asarray(np.where(a < b, 0.0, NEG).astype(np.float32))
    return near, win2, cmpb


INPROJ_TN = 4 * HEAD_DIM
CKV_TILE = COL_CK * HEAD_DIM // INPROJ_TN


def _inproj_kernel(x_ref, w_ref, cs_ref, wg_ref, o_ref, g_ref, ckv_ref, xb_ref):
    j = pl.program_id(1)

    @pl.when(j == 0)
    def _():
        xb = x_ref[...].astype(BF16)
        xb_ref[...] = xb
        g_ref[...] = jax.nn.sigmoid(lax.dot_general(xb, wg_ref[...], _NT, preferred_element_type=F32))

    acc = lax.dot_general(xb_ref[...], w_ref[...].astype(BF16), _NT, preferred_element_type=F32)
    z = (acc * cs_ref[...]).astype(o_ref.dtype)
    o_ref[...] = z

    @pl.when(j == CKV_TILE)
    def _():
        for c in range(INPROJ_TN // HEAD_DIM):
            ckv_ref[c] = z[:, c * HEAD_DIM:(c + 1) * HEAD_DIM]


def _inproj(x, w_in_t, colscale, w_gate_t, tm=1024):
    m, k = x.shape
    tn = INPROJ_TN
    return pl.pallas_call(
        _inproj_kernel,
        grid=(m // tm, GATE_COL0 // tn),
        in_specs=[pl.BlockSpec((tm, k), lambda i, j: (i, 0)),
                  pl.BlockSpec((tn, k), lambda i, j: (j, 0)),
                  pl.BlockSpec((1, tn), lambda i, j: (0, j)),
                  pl.BlockSpec((HEAD_DIM, k), lambda i, j: (0, 0))],
        out_specs=[pl.BlockSpec((tm, tn), lambda i, j: (i, j)),
                   pl.BlockSpec((tm, HEAD_DIM), lambda i, j: (i, 0)),
                   pl.BlockSpec((tn // HEAD_DIM, tm, HEAD_DIM), lambda i, j: (0, i, 0))],
        out_shape=[jax.ShapeDtypeStruct((m, GATE_COL0), BF16),
                   jax.ShapeDtypeStruct((m, HEAD_DIM), F32),
                   jax.ShapeDtypeStruct((tn // HEAD_DIM, m, HEAD_DIM), BF16)],
        scratch_shapes=[pltpu.VMEM((tm, k), BF16)],
        compiler_params=_cparams("parallel", "arbitrary"),
        name="inproj",
    )(x, w_in_t, colscale, w_gate_t)


def _gelu_tanh(x):
    return 0.5 * x * (1.0 + jnp.tanh(math.sqrt(2.0 / math.pi) * (x + 0.044715 * (x * x * x))))


def _compress_kernel(xk_ref, xv_ref, pek_ref, w1k_ref, w2k_ref, pev_ref, w1v_ref, w2v_ref, kc_ref, vct_ref):
    def compress(x_ref, pe_ref, w1_ref, w2_ref):
        x = x_ref[0].astype(F32)
        lo = jnp.dot((x + pe_ref[0]).astype(BF16), w1_ref[0].astype(BF16), preferred_element_type=F32)
        hi = jnp.dot((x + pe_ref[1]).astype(BF16), w1_ref[1].astype(BF16), preferred_element_type=F32)
        y = lo + pltpu.roll(hi, N_CMP, 0)
        out = jnp.dot(_gelu_tanh(y).astype(BF16), w2_ref[...].astype(BF16), preferred_element_type=F32)
        row = lax.broadcasted_iota(jnp.int32, out.shape, 0)
        return jnp.where(row < N_CMP, out, 0.0)

    kc_ref[0] = compress(xk_ref, pek_ref, w1k_ref, w2k_ref).astype(kc_ref.dtype)
    vct_ref[0] = compress(xv_ref, pev_ref, w1v_ref, w2v_ref).T.astype(vct_ref.dtype)


def _compress(xr, pe_k, w1_k, w2_k, pe_v, w1_v, w2_v):
    half = CMP_STRIDE * HEAD_DIM
    weights = [pl.BlockSpec((2, 1, half), lambda g: (0, 0, 0)),
               pl.BlockSpec((2, half, HEAD_DIM), lambda g: (0, 0, 0)),
               pl.BlockSpec((HEAD_DIM, HEAD_DIM), lambda g: (0, 0))]
    return pl.pallas_call(
        _compress_kernel,
        grid=(NSA_KV_HEADS,),
        in_specs=[pl.BlockSpec((1, 512, half), lambda g: (g, 0, 0)),
                  pl.BlockSpec((1, 512, half), lambda g: (NSA_KV_HEADS + g, 0, 0))] + weights + weights,
        out_specs=[pl.BlockSpec((1, 512, HEAD_DIM), lambda g: (g, 0, 0)),
                   pl.BlockSpec((1, HEAD_DIM, 512), lambda g: (g, 0, 0))],
        out_shape=[jax.ShapeDtypeStruct((NSA_KV_HEADS, 512, HEAD_DIM), BF16),
                   jax.ShapeDtypeStruct((NSA_KV_HEADS, HEAD_DIM, 512), BF16)],
        compiler_params=_cparams("parallel"),
        name="compress",
    )(xr, xr, pe_k, w1_k, w2_k, pe_v, w1_v, w2_v)


def _split3(x):
    hi = x.astype(BF16)
    r1 = x - hi.astype(F32)
    mid = r1.astype(BF16)
    lo = (r1 - mid.astype(F32)).astype(BF16)
    return hi, mid, lo


def _topk_bias_t(score, k):
    row = lax.broadcasted_iota(jnp.int32, score.shape, 0).astype(F32)
    selb = jnp.full(score.shape, NEG, F32)
    s = score
    for _ in range(k):
        mx = jnp.max(s, axis=0, keepdims=True)
        idx = jnp.min(jnp.where(s == mx, row, 1e9), axis=0, keepdims=True)
        hit = row == idx
        selb = jnp.where(hit, 0.0, selb)
        s = jnp.where(hit, -3e38, s)
    return selb


def _lane_tile_max(s):
    m = s[:, :HEAD_DIM]
    for c in range(1, s.shape[1] // HEAD_DIM):
        m = jnp.maximum(m, s[:, c * HEAD_DIM:(c + 1) * HEAD_DIM])
    return m


def _sparse_flash(qs, selbs, k_ref, v_ref, kcols, pat_ref, d0s, d1s, i, sub, scratch):
    t_ref, *maybe_r_ref, mx_ref, acc_ref = scratch
    r_ref = maybe_r_ref[0] if maybe_r_ref else None
    heads = range(len(qs))
    lane = lax.broadcasted_iota(jnp.int32, (TQ, HEAD_DIM), 1)
    far = lane < (i - 1) * sub
    q_near = [jnp.concatenate([qs[h], selbs[h].astype(BF16)], axis=1) for h in heads]
    q_far = [jnp.concatenate([qs[h], jnp.where(far, selbs[h], NEG).astype(BF16)], axis=1) for h in heads]

    def k_aug(h, off, n):
        c = kcols[h]
        return jnp.concatenate([k_ref[pl.ds(off, n), c:c + HEAD_DIM], pat_ref[pl.ds(off, n), :]], axis=1)

    def v_aug(h, off, n):
        c = kcols[h]
        return jnp.concatenate([v_ref[pl.ds(off, n), c:c + HEAD_DIM], jnp.ones((n, HEAD_DIM), BF16)], axis=1)

    for h in heads:
        mx_ref[h] = jnp.full((TQ, HEAD_DIM), NEG, F32)

    def score(g):
        off = pl.multiple_of(g * GROUP, GROUP)
        for h in heads:
            s = lax.dot_general(q_far[h], k_aug(h, off, GROUP), _NT, preferred_element_type=F32)
            ref = _lane_tile_max(s)
            if r_ref is not None:
                s = s - jnp.concatenate([ref] * (GROUP // HEAD_DIM), axis=1)
                r_ref[h, g] = ref
            for u in range(KG):
                t_ref[h, g * KG + u] = s[:, u * TQ:(u + 1) * TQ].astype(t_ref.dtype)
            mx_ref[h] = jnp.maximum(mx_ref[h], ref)

    def run_unrolled(n, body):
        def quad(m, carry):
            body(4 * m, 4)
            return carry

        lax.fori_loop(0, lax.shift_right_logical(n, 2), quad, 0)
        done = n - (n & 3)

        @pl.when((n & 2) != 0)
        def _():
            body(done, 2)

        @pl.when((n & 1) != 0)
        def _():
            body(done + (n & 2), 1)

    n_far = lax.shift_right_logical(jnp.maximum(i - 1, 0) + (KG - 1), KG.bit_length() - 1)

    def score_groups(g0, count):
        for k in range(count):
            score(g0 + k)

    run_unrolled(n_far, score_groups)

    off_prev = pl.multiple_of(jnp.maximum(i - 1, 0) * TQ, TQ)
    off_own = pl.multiple_of(i * TQ, TQ)
    p_near = []
    for h in heads:
        k2 = jnp.concatenate([k_aug(h, off_prev, TQ), k_aug(h, off_own, TQ)], axis=0)
        bias = jnp.concatenate([jnp.where(i >= 1, d1s[h], NEG), d0s[h]], axis=1)
        s = lax.dot_general(q_near[h], k2, _NT, preferred_element_type=F32) + bias
        m = jnp.max(jnp.maximum(mx_ref[h], _lane_tile_max(s)), axis=1, keepdims=True)
        mx_ref[h] = jnp.broadcast_to(m, (TQ, HEAD_DIM))
        p_near.append(jnp.exp2(s - m).astype(BF16))
    for h in heads:
        v2 = jnp.concatenate([v_aug(h, off_prev, TQ), v_aug(h, off_own, TQ)], axis=0)
        acc_ref[h] = jnp.dot(p_near[h], v2, preferred_element_type=F32)

    def accumulate(g0, ng):
        off = pl.multiple_of(g0 * GROUP, GROUP)
        for h in heads:
            probs = []
            for g in [g0 + k for k in range(ng)]:
                d = -mx_ref[h] if r_ref is None else r_ref[h, g] - mx_ref[h]
                d2 = jnp.concatenate([d, d], axis=1)
                probs += [jnp.exp2(t_ref[h, g * KG + u].astype(F32) + d2).astype(BF16) for u in range(KG)]
            acc_ref[h] += jnp.dot(jnp.concatenate(probs, axis=1), v_aug(h, off, ng * GROUP),
                                  preferred_element_type=F32)

    def accumulate_groups(g0, count):
        for k in range(0, count, 2):
            accumulate(g0 + k, min(2, count - k))

    run_unrolled(n_far, accumulate_groups)

    outs = []
    for h in heads:
        acc = acc_ref[h]
        outs.append(acc[:, :HEAD_DIM] / acc[:, HEAD_DIM:])
    return outs


def _flash_scratch(hp, relative_bf16):
    logits = [pltpu.VMEM((hp, N_QT, TQ, TQ), BF16),
              pltpu.VMEM((hp, SEQ // GROUP, TQ, HEAD_DIM), F32)]
    if not relative_bf16:
        logits = [pltpu.VMEM((hp, N_QT, TQ, TQ), F32)]
    return logits + [pltpu.VMEM((hp, TQ, HEAD_DIM), F32),
                     pltpu.VMEM((hp, TQ, 2 * HEAD_DIM), F32)]


SEL_CHUNK = 4096


def _moba_select_kernel(q_ref, k_ref, sel_ref):
    blk = lax.broadcasted_iota(jnp.int32, (HEAD_DIM, SEQ), 0)
    pos = lax.broadcasted_iota(jnp.int32, (HEAD_DIM, SEQ), 1)
    ind = jnp.where(lax.shift_right_logical(pos, 8) == blk, 1.0 / MOBA_BLOCK, 0.0).astype(BF16)
    n_blk = SEQ // MOBA_BLOCK
    kmean = _split3(jnp.dot(ind, k_ref[...], preferred_element_type=F32)[:n_blk])
    blk_t = lax.broadcasted_iota(jnp.int32, (n_blk, SEL_CHUNK), 0)
    col = lax.broadcasted_iota(jnp.int32, (n_blk, SEL_CHUNK), 1)
    no_block = jnp.full((HEAD_DIM - n_blk, SEL_CHUNK), NEG, F32)

    def chunk(c, carry):
        off = pl.multiple_of(c * SEL_CHUNK, SEL_CHUNK)
        q = q_ref[pl.ds(off, SEL_CHUNK), :]
        gate = jnp.zeros((n_blk, SEL_CHUNK), F32)
        for part in kmean:
            gate = gate + lax.dot_general(part, q, _NT, preferred_element_type=F32)
        own = lax.shift_right_logical(off + col, 8)
        past = blk_t < own
        selb = _topk_bias_t(jnp.where(past, gate, NEG), MOBA_TOPK)
        selb = jnp.where(past, selb, jnp.where(blk_t == own, 0.0, NEG))
        sel_ref[0, pl.ds(off, SEL_CHUNK), :] = jnp.concatenate([selb, no_block], axis=0).T.astype(sel_ref.dtype)
        return carry

    lax.fori_loop(0, SEQ // SEL_CHUNK, chunk, 0)


def _moba_select(zb):
    return pl.pallas_call(
        _moba_select_kernel,
        grid=(MOBA_HEADS,),
        in_specs=[pl.BlockSpec((SEQ, HEAD_DIM), lambda h: (0, COL_MQ + h)),
                  pl.BlockSpec((SEQ, HEAD_DIM), lambda h: (0, COL_MK + h))],
        out_specs=pl.BlockSpec((1, SEQ, HEAD_DIM), lambda h: (h, 0, 0)),
        out_shape=jax.ShapeDtypeStruct((MOBA_HEADS, SEQ, HEAD_DIM), BF16),
        compiler_params=_cparams("parallel"),
        name="moba_select",
    )(zb, zb)


def _moba_kernel(q_ref, sel_ref, k_ref, v_ref, pat_ref, d0_ref, d1_ref, o_ref, *flash_scratch):
    i = pl.program_id(1)
    hp = HP_MOBA
    qs = [q_ref[:, h * HEAD_DIM:(h + 1) * HEAD_DIM] for h in range(hp)]
    selbs = [sel_ref[h].astype(F32) for h in range(hp)]
    cols = [h * HEAD_DIM for h in range(hp)]
    outs = _sparse_flash(qs, selbs, k_ref, v_ref, cols, pat_ref, [d0_ref[0, h] for h in range(hp)],
                         [d1_ref[0, h] for h in range(hp)], i, 1, flash_scratch)
    for h in range(hp):
        o_ref[:, h * HEAD_DIM:(h + 1) * HEAD_DIM] = outs[h].astype(o_ref.dtype)


def _moba(zb, selb, near, pat):
    hp = HP_MOBA
    w = hp * HEAD_DIM
    return pl.pallas_call(
        _moba_kernel,
        grid=(MOBA_HEADS // hp, N_QT),
        in_specs=[pl.BlockSpec((TQ, w), lambda h, i: (i, COL_MQ // hp + h)),
                  pl.BlockSpec((hp, TQ, HEAD_DIM), lambda h, i: (h, i, 0)),
                  pl.BlockSpec((SEQ, w), lambda h, i: (0, COL_MK // hp + h), pipeline_mode=pl.Buffered(1)),
                  pl.BlockSpec((SEQ, w), lambda h, i: (0, COL_MV // hp + h), pipeline_mode=pl.Buffered(1)),
                  pl.BlockSpec((SEQ, HEAD_DIM), lambda h, i: (0, 0), pipeline_mode=pl.Buffered(1)),
                  pl.BlockSpec((1, hp, TQ, TQ), lambda h, i: (0, h, 0, 0)),
                  pl.BlockSpec((1, hp, TQ, TQ), lambda h, i: (1, h, 0, 0))],
        out_specs=pl.BlockSpec((TQ, w), lambda h, i: (i, h)),
        out_shape=jax.ShapeDtypeStruct((SEQ, MOBA_HEADS * HEAD_DIM), BF16),
        scratch_shapes=_flash_scratch(hp, True),
        compiler_params=_cparams("parallel", "arbitrary", vmem=BIG_VMEM_LIMIT),
        name="moba",
    )(zb, selb, zb, zb, pat, near, near)


def _cmp_kernel(q_ref, kc_ref, vct_ref, ovt_ref, cb_ref, oc_ref, sel_ref, s_ref):
    i = pl.program_id(0)
    win0 = pl.multiple_of(jnp.maximum(i * CMP_STRIDE - CMP_STRIDE, 0), CMP_STRIDE)

    def body(nk):
        nb = nk * CMP_STRIDE // SLC_LEN
        blk = lax.broadcasted_iota(jnp.int32, (nb, TQ), 0)
        cur = i * SLC_PER_TILE + lax.shift_right_logical(lax.broadcasted_iota(jnp.int32, (nb, TQ), 1), 6)
        allowed = blk <= cur
        forced = (blk == 0) | (blk == cur) | (blk == cur - 1)
        reachable = lax.broadcasted_iota(jnp.int32, (nk, TQ), 0) < win0 + CMP_WIN
        for g in range(NSA_KV_HEADS):
            p_sum = jnp.zeros((nk, TQ), F32)
            for r in range(NSA_GROUP):
                hd = g * NSA_GROUP + r
                q = q_ref[:, hd * HEAD_DIM:(hd + 1) * HEAD_DIM]
                s_ref[:nk, :] = lax.dot_general(kc_ref[g, :nk, :], q, _NT, preferred_element_type=F32)
                s_ref[pl.ds(win0, CMP_WIN), :] += cb_ref[0, hd]
                s = jnp.where(reachable, s_ref[:nk, :], NEG)
                valid = s > 0.5 * NEG
                e = jnp.where(valid, jnp.exp2(s - jnp.max(s, axis=0, keepdims=True)), 0.0)
                p = e * (1.0 / jnp.maximum(jnp.sum(e, axis=0, keepdims=True), TINY))
                o_t = jnp.dot(vct_ref[g, :, :nk], p.astype(BF16), preferred_element_type=F32)
                oc_ref[:, hd * HEAD_DIM:(hd + 1) * HEAD_DIM] = o_t.T
                p_sum = p_sum + p
            imp = jnp.zeros((nb, TQ), F32)
            for part in _split3(p_sum):
                imp = imp + jnp.dot(ovt_ref[:nb, :nk], part, preferred_element_type=F32)
            picked = _topk_bias_t(jnp.where(allowed & ~forced, imp, -1.0), SLC_TOPK - 3)
            selb = jnp.where(allowed, jnp.where(forced, 0.0, picked), NEG)
            if nb < N_SLC:
                selb = jnp.concatenate([selb, jnp.full((N_SLC - nb, TQ), NEG, F32)], axis=0)
            sel_ref[g] = selb.T.astype(sel_ref.dtype)

    for c in range(1, 512 // HEAD_DIM + 1):
        @pl.when(lax.shift_right_logical(i, 3) == c - 1)
        def _():
            body(c * HEAD_DIM)


def _cmp_select(zb, kc, vct, ovt, cmpb):
    w = NSA_HEADS * HEAD_DIM
    return pl.pallas_call(
        _cmp_kernel,
        grid=(N_QT,),
        in_specs=[pl.BlockSpec((TQ, w), lambda i: (i, COL_NQ // NSA_HEADS)),
                  pl.BlockSpec((NSA_KV_HEADS, 512, HEAD_DIM), lambda i: (0, 0, 0)),
                  pl.BlockSpec((NSA_KV_HEADS, HEAD_DIM, 512), lambda i: (0, 0, 0)),
                  pl.BlockSpec((N_SLC, 512), lambda i: (0, 0)),
                  pl.BlockSpec((1, NSA_HEADS, CMP_WIN, TQ), lambda i: (jnp.minimum(i, 1), 1, 0, 0))],
        out_specs=[pl.BlockSpec((TQ, w), lambda i: (i, 0)),
                   pl.BlockSpec((NSA_KV_HEADS, TQ, N_SLC), lambda i: (0, i, 0))],
        out_shape=[jax.ShapeDtypeStruct((SEQ, w), F32),
                   jax.ShapeDtypeStruct((NSA_KV_HEADS, SEQ, N_SLC), BF16)],
        scratch_shapes=[pltpu.VMEM((512, TQ), F32)],
        compiler_params=_cparams("arbitrary"),
        name="cmp_select",
    )(zb, kc, vct, ovt, cmpb)


def _slc_kernel(q_ref, sel_ref, k_ref, v_ref, pat_ref, d0_ref, d1_ref, o_ref, *flash_scratch):
    i = pl.program_id(1)
    hp = HP_SLC
    selb = sel_ref[0].astype(F32)
    qs = [q_ref[:, h * HEAD_DIM:(h + 1) * HEAD_DIM] for h in range(hp)]
    outs = _sparse_flash(qs, [selb] * hp, k_ref, v_ref, [0] * hp, pat_ref,
                         [d0_ref[0, h] for h in range(hp)], [d1_ref[0, h] for h in range(hp)],
                         i, SLC_PER_TILE, flash_scratch)
    for h in range(hp):
        o_ref[:, h * HEAD_DIM:(h + 1) * HEAD_DIM] = outs[h].astype(o_ref.dtype)


def _slc(zb, selb, near, pat):
    hp = HP_SLC
    w = hp * HEAD_DIM
    per_group = NSA_GROUP // hp
    return pl.pallas_call(
        _slc_kernel,
        grid=(NSA_HEADS // hp, N_QT),
        in_specs=[pl.BlockSpec((TQ, w), lambda h, i: (i, COL_NQ // hp + h)),
                  pl.BlockSpec((1, TQ, N_SLC), lambda h, i: (h // per_group, i, 0)),
                  pl.BlockSpec((SEQ, HEAD_DIM), lambda h, i: (0, COL_SK + h // per_group),
                               pipeline_mode=pl.Buffered(1)),
                  pl.BlockSpec((SEQ, HEAD_DIM), lambda h, i: (0, COL_SV + h // per_group),
                               pipeline_mode=pl.Buffered(1)),
                  pl.BlockSpec((SEQ, HEAD_DIM), lambda h, i: (0, 0), pipeline_mode=pl.Buffered(1)),
                  pl.BlockSpec((1, hp, TQ, TQ), lambda h, i: (0, MOBA_HEADS // hp + h, 0, 0)),
                  pl.BlockSpec((1, hp, TQ, TQ), lambda h, i: (1, MOBA_HEADS // hp + h, 0, 0))],
        out_specs=pl.BlockSpec((TQ, w), lambda h, i: (i, h)),
        out_shape=jax.ShapeDtypeStruct((SEQ, NSA_HEADS * HEAD_DIM), F32),
        scratch_shapes=_flash_scratch(hp, False),
        compiler_params=_cparams("parallel", "arbitrary", vmem=BIG_VMEM_LIMIT),
        name="slc",
    )(zb, selb, zb, zb, pat, near, near)


def _win_kernel(q_ref, k_ref, v_ref, d0_ref, d1_ref, w2_ref, o_ref):
    i = pl.program_id(0)
    offs = [pl.multiple_of(jnp.maximum(i - b, 0) * TQ, TQ) for b in range(3)]
    mask2 = jnp.where(i >= 2, w2_ref[...], NEG)
    ones = jnp.ones((3 * TQ, HEAD_DIM), BF16)
    for g in range(NSA_KV_HEADS):
        cols = slice(g * HEAD_DIM, (g + 1) * HEAD_DIM)
        k_all = jnp.concatenate([k_ref[pl.ds(off, TQ), cols] for off in offs], axis=0)
        v_aug = jnp.concatenate([jnp.concatenate([v_ref[pl.ds(off, TQ), cols] for off in offs], axis=0), ones],
                                axis=1)
        for r in range(NSA_GROUP):
            hd = g * NSA_GROUP + r
            q = q_ref[:, hd * HEAD_DIM:(hd + 1) * HEAD_DIM]
            bias = jnp.concatenate([d0_ref[0, hd], jnp.where(i >= 1, d1_ref[0, hd], NEG), mask2], axis=1)
            s = lax.dot_general(q, k_all, _NT, preferred_element_type=F32) + bias
            m = jnp.max(s, axis=1, keepdims=True)
            pv = jnp.dot(jnp.exp2(s - m).astype(BF16), v_aug, preferred_element_type=F32)
            o_ref[:, hd * HEAD_DIM:(hd + 1) * HEAD_DIM] = pv[:, :HEAD_DIM] / pv[:, HEAD_DIM:]


def _win(zb, near, win2):
    w = NSA_HEADS * HEAD_DIM
    kv_w = NSA_KV_HEADS * HEAD_DIM
    return pl.pallas_call(
        _win_kernel,
        grid=(N_QT,),
        in_specs=[pl.BlockSpec((TQ, w), lambda i: (i, COL_NQ // NSA_HEADS)),
                  pl.BlockSpec((SEQ, kv_w), lambda i: (0, COL_WK // NSA_KV_HEADS), pipeline_mode=pl.Buffered(1)),
                  pl.BlockSpec((SEQ, kv_w), lambda i: (0, COL_WV // NSA_KV_HEADS), pipeline_mode=pl.Buffered(1)),
                  pl.BlockSpec((1, NSA_HEADS, TQ, TQ), lambda i: (0, 1, 0, 0), pipeline_mode=pl.Buffered(1)),
                  pl.BlockSpec((1, NSA_HEADS, TQ, TQ), lambda i: (1, 1, 0, 0), pipeline_mode=pl.Buffered(1)),
                  pl.BlockSpec((TQ, TQ), lambda i: (0, 0), pipeline_mode=pl.Buffered(1))],
        out_specs=pl.BlockSpec((TQ, w), lambda i: (i, 0)),
        out_shape=jax.ShapeDtypeStruct((SEQ, w), F32),
        compiler_params=_cparams("arbitrary"),
        name="win",
    )(zb, zb, zb, near, near, win2)


def _layer_norm(y, g, b):
    mu = jnp.mean(y, axis=-1, keepdims=True)
    yc = y - mu
    var = jnp.mean(yc * yc, axis=-1, keepdims=True)
    return yc * lax.rsqrt(var + LN_EPS) * g + b


ROW_CHUNK = 256


def _outproj_kernel(x_ref, om_ref, oc_ref, os_ref, ow_ref, g_ref, e_ref, w_ref, lg_ref, lb_ref,
                    h_ref, hb_ref):
    for c0 in range(0, x_ref.shape[0], ROW_CHUNK):
        rows = slice(c0, c0 + ROW_CHUNK)
        gate = g_ref[rows, :]
        hi = gate.astype(BF16)
        lo = (gate - hi.astype(F32)).astype(BF16)
        g2 = jnp.concatenate([hi, lo], axis=1)
        o_n = jnp.zeros((ROW_CHUNK, oc_ref.shape[1]), F32)
        for c, branch in enumerate((oc_ref, os_ref, ow_ref)):
            o_n = o_n + jnp.dot(g2, e_ref[c], preferred_element_type=F32) * branch[rows, :]
        o = jnp.concatenate([om_ref[rows, :], o_n.astype(BF16)], axis=1)
        a = jnp.dot(o, w_ref[...], preferred_element_type=F32)
        h = _layer_norm(DN_ALPHA * x_ref[rows, :] + a, lg_ref[...], lb_ref[...])
        h_ref[rows, :] = h
        hb_ref[rows, :] = h.astype(BF16)


def _resident(shape):
    return pl.BlockSpec(shape, lambda i: (0,) * len(shape), pipeline_mode=pl.Buffered(1))


def _outproj(x, o_m, o_c, o_s, o_w, gates, expand, w_out, ln_g, ln_b, tm=512):
    half = NSA_HEADS * HEAD_DIM
    row = lambda i: (i, 0)
    return pl.pallas_call(
        _outproj_kernel,
        grid=(SEQ // tm,),
        in_specs=[pl.BlockSpec((tm, D_MODEL), row),
                  pl.BlockSpec((tm, half), row), pl.BlockSpec((tm, half), row),
                  pl.BlockSpec((tm, half), row), pl.BlockSpec((tm, half), row),
                  pl.BlockSpec((tm, HEAD_DIM), row),
                  _resident((3, 2 * HEAD_DIM, half)),
                  _resident((D_MODEL, D_MODEL)),
                  _resident((1, D_MODEL)), _resident((1, D_MODEL))],
        out_specs=[pl.BlockSpec((tm, D_MODEL), row), pl.BlockSpec((tm, D_MODEL), row)],
        out_shape=[jax.ShapeDtypeStruct((SEQ, D_MODEL), F32),
                   jax.ShapeDtypeStruct((SEQ, D_MODEL), BF16)],
        compiler_params=_cparams("parallel", vmem=BIG_VMEM_LIMIT),
        name="outproj_ln",
    )(x, o_m, o_c, o_s, o_w, gates, expand, w_out, ln_g, ln_b)


def _ffn_kernel(hb_ref, w1_ref, w2_ref, f_ref):
    @pl.when(pl.program_id(1) == 0)
    def _():
        f_ref[...] = jnp.zeros(f_ref.shape, F32)

    u = jnp.maximum(jnp.dot(hb_ref[...], w1_ref[...].astype(BF16), preferred_element_type=F32), 0.0)
    f_ref[...] += jnp.dot((u * u).astype(BF16), w2_ref[...].astype(BF16), preferred_element_type=F32)


def _ffn(hb, w1, w2, tm=1024, tf=512):
    return pl.pallas_call(
        _ffn_kernel,
        grid=(SEQ // tm, D_FF // tf),
        in_specs=[pl.BlockSpec((tm, D_MODEL), lambda i, c: (i, 0)),
                  pl.BlockSpec((D_MODEL, tf), lambda i, c: (0, c)),
                  pl.BlockSpec((tf, D_MODEL), lambda i, c: (c, 0))],
        out_specs=pl.BlockSpec((tm, D_MODEL), lambda i, c: (i, 0)),
        out_shape=jax.ShapeDtypeStruct((SEQ, D_MODEL), F32),
        compiler_params=_cparams("parallel", "arbitrary", vmem=BIG_VMEM_LIMIT),
        name="ffn",
    )(hb, w1, w2)


def _final_kernel(h_ref, hb_ref, f_ref, p_ref, wg_ref, wp_ref, lg_ref, lb_ref, o_ref):
    for c0 in range(0, h_ref.shape[0], ROW_CHUNK):
        rows = slice(c0, c0 + ROW_CHUNK)
        gate = jax.nn.sigmoid(jnp.dot(hb_ref[rows, :], wg_ref[...], preferred_element_type=F32))
        emb = jnp.dot(p_ref[rows, :].astype(BF16), wp_ref[...], preferred_element_type=F32)
        y = DN_ALPHA * h_ref[rows, :] + f_ref[rows, :] + gate * emb
        o_ref[rows, :] = _layer_norm(y, lg_ref[...], lb_ref[...])


def _final(h, hb, f, p, wg, wp, ln_g, ln_b, tm=512):
    row = lambda i: (i, 0)
    return pl.pallas_call(
        _final_kernel,
        grid=(SEQ // tm,),
        in_specs=[pl.BlockSpec((tm, D_MODEL), row), pl.BlockSpec((tm, D_MODEL), row),
                  pl.BlockSpec((tm, D_MODEL), row), pl.BlockSpec((tm, D_PLE), row),
                  _resident((D_MODEL, D_MODEL)), _resident((D_PLE, D_MODEL)),
                  _resident((1, D_MODEL)), _resident((1, D_MODEL))],
        out_specs=pl.BlockSpec((tm, D_MODEL), row),
        out_shape=jax.ShapeDtypeStruct((SEQ, D_MODEL), F32),
        compiler_params=_cparams("parallel"),
        name="final_ln",
    )(h, hb, f, p, wg, wp, ln_g, ln_b)


def _overlap_matrix_t():
    j = np.arange(N_SLC)[:, None]
    n = np.arange(512)[None, :]
    ov = ((n * CMP_STRIDE < j * SLC_LEN + SLC_LEN) & (n * CMP_STRIDE + CMP_LEN > j * SLC_LEN) & (n < N_CMP))
    return ov.astype(np.float32)


def _gate_expand():
    e = np.zeros((3, 2 * HEAD_DIM, NSA_HEADS * HEAD_DIM), np.float32)
    for c in range(3):
        for h in range(NSA_HEADS):
            e[c, 3 * h + c, h * HEAD_DIM:(h + 1) * HEAD_DIM] = 1.0
            e[c, HEAD_DIM + 3 * h + c, h * HEAD_DIM:(h + 1) * HEAD_DIM] = 1.0
    return e


def _layer(h, p, w_in, cmp_pe_k, cmp_w1_k, cmp_w2_k, cmp_pe_v, cmp_w1_v, cmp_w2_v, tiles, w_out,
           ln1_g, ln1_b, w_ff1, w_ff2, w_ple, w_ple_gate, ln2_g, ln2_b):
    near, win2, cmpb = tiles
    scale = HEAD_DIM ** -0.5 * LOG2E
    colscale = np.ones((1, GATE_COL0), np.float32)
    colscale[0, COL_MQ * HEAD_DIM:(COL_MQ + MOBA_HEADS) * HEAD_DIM] = scale
    colscale[0, COL_NQ * HEAD_DIM:(COL_NQ + NSA_HEADS) * HEAD_DIM] = scale
    w_in_t = w_in.T
    w_gl_t = jnp.pad(w_in_t[GATE_COL0:], ((0, HEAD_DIM - NSA_HEADS * 3), (0, 0))).astype(BF16)
    zb, gates, ckv = _inproj(h, w_in_t, jnp.asarray(colscale), w_gl_t)

    half = CMP_STRIDE * HEAD_DIM
    xr = ckv.reshape(4, SEQ // CMP_STRIDE, half)
    kc, vct = _compress(xr, cmp_pe_k.reshape(2, 1, half), cmp_w1_k.reshape(2, half, HEAD_DIM), cmp_w2_k,
                        cmp_pe_v.reshape(2, 1, half), cmp_w1_v.reshape(2, half, HEAD_DIM), cmp_w2_v)

    key = np.arange(SEQ)[:, None]
    lane = np.arange(HEAD_DIM)[None, :]
    pat_moba = jnp.asarray(lane == key // MOBA_BLOCK, BF16)
    pat_slc = jnp.asarray(lane == key // SLC_LEN, BF16)
    o_m = _moba(zb, _moba_select(zb), near, pat_moba)
    o_c, selb = _cmp_select(zb, kc, vct, jnp.asarray(_overlap_matrix_t(), BF16), cmpb)
    o_s = _slc(zb, selb, near, pat_slc)
    o_w = _win(zb, near, win2)

    h1, h1b = _outproj(h, o_m, o_c, o_s, o_w, gates, jnp.asarray(_gate_expand(), BF16),
                       w_out.astype(BF16), ln1_g.reshape(1, -1), ln1_b.reshape(1, -1))
    f = _ffn(h1b, w_ff1, w_ff2)
    return _final(h1, h1b, f, p, w_ple_gate.astype(BF16), w_ple.astype(BF16),
                  ln2_g.reshape(1, -1), ln2_b.reshape(1, -1))


def kernel(x, p, w_in, cmp_pe_k, cmp_w1_k, cmp_w2_k, cmp_pe_v, cmp_w1_v, cmp_w2_v, rel_bias, w_out,
           ln1_g, ln1_b, w_ff1, w_ff2, w_ple, w_ple_gate, ln2_g, ln2_b):
    tiles = _bias_tiles(rel_bias)
    h = x[0]
    for i in range(w_in.shape[0]):
        h = _layer(h, p[i, 0], w_in[i], cmp_pe_k[i], cmp_w1_k[i], cmp_w2_k[i], cmp_pe_v[i], cmp_w1_v[i],
                   cmp_w2_v[i], tiles, w_out[i], ln1_g[i], ln1_b[i], w_ff1[i], w_ff2[i], w_ple[i],
                   w_ple_gate[i], ln2_g[i], ln2_b[i])
    return h[None]
```

```python
import functools
import math

import numpy as np
import jax
import jax.numpy as jnp
from jax import lax
from jax.experimental import pallas as pl
from jax.experimental.pallas import tpu as pltpu

D_MODEL = 2048
SEQ = 8192
HEAD_DIM = 128
N_HEADS = 16
MOBA_HEADS = 8
NSA_HEADS = 8
NSA_KV_HEADS = 2
NSA_GROUP = 4
MOBA_BLOCK = 256
MOBA_TOPK = 3
CMP_LEN = 32
CMP_STRIDE = 16
SLC_LEN = 64
SLC_TOPK = 16
WINDOW = 512
N_BUCKETS = 32
MAX_DISTANCE = 128
D_FF = 4 * D_MODEL
D_PLE = 256
LN_EPS = 1e-5
DN_ALPHA = 2.0 ** 0.25
NEG = -1e30
LOG2E = math.log2(math.e)
TINY = 1e-30
FORCE_SCORE = 1e4

N_CMP = (SEQ - CMP_LEN) // CMP_STRIDE + 1
N_SLC = SEQ // SLC_LEN
TQ = 256
N_QT = SEQ // TQ
CMP_WIN = 32
SLC_PER_TILE = TQ // SLC_LEN
HP_MOBA = 4
HP_SLC = 4
KG = 4
GROUP = KG * TQ
COL_MQ, COL_MK, COL_MV, COL_NQ = 0, 8, 16, 24
COL_CK, COL_CV, COL_SK, COL_SV, COL_WK, COL_WV = 32, 34, 36, 38, 40, 42
N_COLBLK = 44
GATE_COL0 = N_COLBLK * HEAD_DIM

VMEM_LIMIT = 48 * 1024 * 1024
BIG_VMEM_LIMIT = 56 * 1024 * 1024

BF16 = jnp.bfloat16
F32 = jnp.float32
_NT = (((1,), (1,)), ((), ()))


def _cparams(*sem, vmem=VMEM_LIMIT):
    return pltpu.CompilerParams(dimension_semantics=sem, vmem_limit_bytes=vmem)


def _bucket_table():
    n = np.arange(1024, dtype=np.int32)
    max_exact = N_BUCKETS // 2
    ratio = np.maximum(n, 1).astype(np.float32) / np.float32(max_exact)
    large = max_exact + (np.log(ratio).astype(np.float32) / np.float32(math.log(MAX_DISTANCE / max_exact))
                         * np.float32(N_BUCKETS - max_exact)).astype(np.int32)
    large = np.minimum(large, N_BUCKETS - 1)
    return np.where(n < max_exact, n, large).astype(np.int32)


def _bias_kernel(tab_ref, near_ref, cmpb_ref, *, steps):
    h = pl.program_id(0)
    far = tab_ref[h, N_BUCKETS - 1]

    def lookup(d):
        val = jnp.full(d.shape, tab_ref[h, 0], F32)
        for start, bucket in steps:
            val = jnp.where(d >= start, tab_ref[h, bucket], val)
        return jnp.where(d >= 0, (val - far) * LOG2E, NEG)

    a = lax.broadcasted_iota(jnp.int32, (TQ, TQ), 0)
    b = lax.broadcasted_iota(jnp.int32, (TQ, TQ), 1)
    near_ref[0, 0] = lookup(a - b)
    near_ref[1, 0] = lookup(a - b + TQ)
    w = lax.broadcasted_iota(jnp.int32, (CMP_WIN, TQ), 0)
    aw = lax.broadcasted_iota(jnp.int32, (CMP_WIN, TQ), 1)
    cmpb_ref[0, 0] = lookup(aw - CMP_STRIDE * w - (CMP_LEN - 1))
    cmpb_ref[1, 0] = lookup(aw - CMP_STRIDE * w + (TQ - CMP_LEN + 1))


def _bias_tiles(rel_bias):
    bucket = _bucket_table()
    steps = tuple((int(n), int(bucket[n])) for n in range(1, bucket.size) if bucket[n] != bucket[n - 1])
    near, cmpb = pl.pallas_call(
        functools.partial(_bias_kernel, steps=steps),
        grid=(N_HEADS,),
        in_specs=[pl.BlockSpec(memory_space=pltpu.SMEM)],
        out_specs=[pl.BlockSpec((2, 1, TQ, TQ), lambda h: (0, h, 0, 0)),
                   pl.BlockSpec((2, 1, CMP_WIN, TQ), lambda h: (0, h, 0, 0))],
        out_shape=[jax.ShapeDtypeStruct((2, N_HEADS, TQ, TQ), F32),
                   jax.ShapeDtypeStruct((2, N_HEADS, CMP_WIN, TQ), F32)],
        compiler_params=_cparams("parallel"),
        name="bias_tiles",
    )(rel_bias.T.astype(F32))
    a = np.arange(TQ)[:, None]
    b = np.arange(TQ)[None, :]
    win2 = jnp.asarray(np.where(a < b, 0.0, NEG).astype(np.float32))
    return near, win2, cmpb


INPROJ_TN = 4 * HEAD_DIM
CKV_TILE = COL_CK * HEAD_DIM // INPROJ_TN


def _inproj_kernel(x_ref, w_ref, cs_ref, wg_ref, o_ref, g_ref, ckv_ref, xb_ref):
    j = pl.program_id(1)

    @pl.when(j == 0)
    def _():
        xb = x_ref[...].astype(BF16)
        xb_ref[...] = xb
        g_ref[...] = jax.nn.sigmoid(lax.dot_general(xb, wg_ref[...], _NT, preferred_element_type=F32))

    acc = lax.dot_general(xb_ref[...], w_ref[...].astype(BF16), _NT, preferred_element_type=F32)
    z = (acc * cs_ref[...]).astype(o_ref.dtype)
    o_ref[...] = z

    @pl.when(j == CKV_TILE)
    def _():
        for c in range(INPROJ_TN // HEAD_DIM):
            ckv_ref[c] = z[:, c * HEAD_DIM:(c + 1) * HEAD_DIM]


def _inproj(x, w_in_t, colscale, w_gate_t, tm=1024):
    m, k = x.shape
    tn = INPROJ_TN
    return pl.pallas_call(
        _inproj_kernel,
        grid=(m // tm, GATE_COL0 // tn),
        in_specs=[pl.BlockSpec((tm, k), lambda i, j: (i, 0)),
                  pl.BlockSpec((tn, k), lambda i, j: (j, 0)),
                  pl.BlockSpec((1, tn), lambda i, j: (0, j)),
                  pl.BlockSpec((HEAD_DIM, k), lambda i, j: (0, 0))],
        out_specs=[pl.BlockSpec((tm, tn), lambda i, j: (i, j)),
                   pl.BlockSpec((tm, HEAD_DIM), lambda i, j: (i, 0)),
                   pl.BlockSpec((tn // HEAD_DIM, tm, HEAD_DIM), lambda i, j: (0, i, 0))],
        out_shape=[jax.ShapeDtypeStruct((m, GATE_COL0), BF16),
                   jax.ShapeDtypeStruct((m, HEAD_DIM), F32),
                   jax.ShapeDtypeStruct((tn // HEAD_DIM, m, HEAD_DIM), BF16)],
        scratch_shapes=[pltpu.VMEM((tm, k), BF16)],
        compiler_params=_cparams("parallel", "arbitrary"),
        name="inproj",
    )(x, w_in_t, colscale, w_gate_t)


def _gelu_tanh(x):
    return 0.5 * x * (1.0 + jnp.tanh(math.sqrt(2.0 / math.pi) * (x + 0.044715 * (x * x * x))))


def _compress_kernel(xk_ref, xv_ref, pek_ref, w1k_ref, w2k_ref, pev_ref, w1v_ref, w2v_ref, kc_ref, vct_ref):
    def compress(x_ref, pe_ref, w1_ref, w2_ref):
        x = x_ref[0].astype(F32)
        lo = jnp.dot((x + pe_ref[0]).astype(BF16), w1_ref[0].astype(BF16), preferred_element_type=F32)
        hi = jnp.dot((x + pe_ref[1]).astype(BF16), w1_ref[1].astype(BF16), preferred_element_type=F32)
        y = lo + pltpu.roll(hi, N_CMP, 0)
        out = jnp.dot(_gelu_tanh(y).astype(BF16), w2_ref[...].astype(BF16), preferred_element_type=F32)
        row = lax.broadcasted_iota(jnp.int32, out.shape, 0)
        return jnp.where(row < N_CMP, out, 0.0)

    kc_ref[0] = compress(xk_ref, pek_ref, w1k_ref, w2k_ref).astype(kc_ref.dtype)
    vct_ref[0] = compress(xv_ref, pev_ref, w1v_ref, w2v_ref).T.astype(vct_ref.dtype)


def _compress(xr, pe_k, w1_k, w2_k, pe_v, w1_v, w2_v):
    half = CMP_STRIDE * HEAD_DIM
    weights = [pl.BlockSpec((2, 1, half), lambda g: (0, 0, 0)),
               pl.BlockSpec((2, half, HEAD_DIM), lambda g: (0, 0, 0)),
               pl.BlockSpec((HEAD_DIM, HEAD_DIM), lambda g: (0, 0))]
    return pl.pallas_call(
        _compress_kernel,
        grid=(NSA_KV_HEADS,),
        in_specs=[pl.BlockSpec((1, 512, half), lambda g: (g, 0, 0)),
                  pl.BlockSpec((1, 512, half), lambda g: (NSA_KV_HEADS + g, 0, 0))] + weights + weights,
        out_specs=[pl.BlockSpec((1, 512, HEAD_DIM), lambda g: (g, 0, 0)),
                   pl.BlockSpec((1, HEAD_DIM, 512), lambda g: (g, 0, 0))],
        out_shape=[jax.ShapeDtypeStruct((NSA_KV_HEADS, 512, HEAD_DIM), BF16),
                   jax.ShapeDtypeStruct((NSA_KV_HEADS, HEAD_DIM, 512), BF16)],
        compiler_params=_cparams("parallel"),
        name="compress",
    )(xr, xr, pe_k, w1_k, w2_k, pe_v, w1_v, w2_v)


def _split3(x):
    hi = x.astype(BF16)
    r1 = x - hi.astype(F32)
    mid = r1.astype(BF16)
    lo = (r1 - mid.astype(F32)).astype(BF16)
    return hi, mid, lo


def _topk_bias_t(score, k):
    row = lax.broadcasted_iota(jnp.int32, score.shape, 0).astype(F32)
    selb = jnp.full(score.shape, NEG, F32)
    s = score
    for _ in range(k):
        mx = jnp.max(s, axis=0, keepdims=True)
        idx = jnp.min(jnp.where(s == mx, row, 1e9), axis=0, keepdims=True)
        hit = row == idx
        selb = jnp.where(hit, 0.0, selb)
        s = jnp.where(hit, -3e38, s)
    return selb


def _lane_tile_max(s):
    m = s[:, :HEAD_DIM]
    for c in range(1, s.shape[1] // HEAD_DIM):
        m = jnp.maximum(m, s[:, c * HEAD_DIM:(c + 1) * HEAD_DIM])
    return m


def _sparse_flash(qs, selbs, k_ref, v_ref, kcols, pat_ref, d0s, d1s, i, sub, scratch):
    t_ref, *maybe_r_ref, mx_ref, acc_ref = scratch
    r_ref = maybe_r_ref[0] if maybe_r_ref else None
    heads = range(len(qs))
    lane = lax.broadcasted_iota(jnp.int32, (TQ, HEAD_DIM), 1)
    far = lane < (i - 1) * sub
    q_near = [jnp.concatenate([qs[h], selbs[h].astype(BF16)], axis=1) for h in heads]
    q_far = [jnp.concatenate([qs[h], jnp.where(far, selbs[h], NEG).astype(BF16)], axis=1) for h in heads]

    def k_aug(h, off, n):
        c = kcols[h]
        return jnp.concatenate([k_ref[pl.ds(off, n), c:c + HEAD_DIM], pat_ref[pl.ds(off, n), :]], axis=1)

    def v_aug(h, off, n):
        c = kcols[h]
        return jnp.concatenate([v_ref[pl.ds(off, n), c:c + HEAD_DIM], jnp.ones((n, HEAD_DIM), BF16)], axis=1)

    for h in heads:
        mx_ref[h] = jnp.full((TQ, HEAD_DIM), NEG, F32)

    def score(g):
        off = pl.multiple_of(g * GROUP, GROUP)
        for h in heads:
            s = lax.dot_general(q_far[h], k_aug(h, off, GROUP), _NT, preferred_element_type=F32)
            ref = _lane_tile_max(s)
            if r_ref is not None:
                s = s - jnp.concatenate([ref] * (GROUP // HEAD_DIM), axis=1)
                r_ref[h, g] = ref
            for u in range(KG):
                t_ref[h, g * KG + u] = s[:, u * TQ:(u + 1) * TQ].astype(t_ref.dtype)
            mx_ref[h] = jnp.maximum(mx_ref[h], ref)

    def run_unrolled(n, body):
        def quad(m, carry):
            body(4 * m, 4)
            return carry

        lax.fori_loop(0, lax.shift_right_logical(n, 2), quad, 0)
        for left in (1, 2, 3):
            @pl.when((n & 3) == left)
            def _(left=left):
                body(n - left, left)

    n_far = lax.shift_right_logical(jnp.maximum(i - 1, 0) + (KG - 1), KG.bit_length() - 1)

    def score_groups(g0, count):
        for k in range(count):
            score(g0 + k)

    run_unrolled(n_far, score_groups)

    off_prev = pl.multiple_of(jnp.maximum(i - 1, 0) * TQ, TQ)
    off_own = pl.multiple_of(i * TQ, TQ)
    p_near = []
    for h in heads:
        k2 = jnp.concatenate([k_aug(h, off_prev, TQ), k_aug(h, off_own, TQ)], axis=0)
        bias = jnp.concatenate([jnp.where(i >= 1, d1s[h], NEG), d0s[h]], axis=1)
        s = lax.dot_general(q_near[h], k2, _NT, preferred_element_type=F32) + bias
        m = jnp.max(jnp.maximum(mx_ref[h], _lane_tile_max(s)), axis=1, keepdims=True)
        mx_ref[h] = jnp.broadcast_to(m, (TQ, HEAD_DIM))
        p_near.append(jnp.exp2(s - m).astype(BF16))
    for h in heads:
        v2 = jnp.concatenate([v_aug(h, off_prev, TQ), v_aug(h, off_own, TQ)], axis=0)
        acc_ref[h] = jnp.dot(p_near[h], v2, preferred_element_type=F32)

    def accumulate(g0, ng):
        off = pl.multiple_of(g0 * GROUP, GROUP)
        for h in heads:
            probs = []
            for g in [g0 + k for k in range(ng)]:
                d = -mx_ref[h] if r_ref is None else r_ref[h, g] - mx_ref[h]
                d2 = jnp.concatenate([d, d], axis=1)
                probs += [jnp.exp2(t_ref[h, g * KG + u].astype(F32) + d2).astype(BF16) for u in range(KG)]
            acc_ref[h] += jnp.dot(jnp.concatenate(probs, axis=1), v_aug(h, off, ng * GROUP),
                                  preferred_element_type=F32)

    def accumulate_groups(g0, count):
        for k in range(0, count, 2):
            accumulate(g0 + k, min(2, count - k))

    run_unrolled(n_far, accumulate_groups)

    outs = []
    for h in heads:
        acc = acc_ref[h]
        outs.append(acc[:, :HEAD_DIM] / acc[:, HEAD_DIM:])
    return outs


def _flash_scratch(hp, relative_bf16):
    logits = [pltpu.VMEM((hp, N_QT, TQ, TQ), BF16),
              pltpu.VMEM((hp, SEQ // GROUP, TQ, HEAD_DIM), F32)]
    if not relative_bf16:
        logits = [pltpu.VMEM((hp, N_QT, TQ, TQ), F32)]
    return logits + [pltpu.VMEM((hp, TQ, HEAD_DIM), F32),
                     pltpu.VMEM((hp, TQ, 2 * HEAD_DIM), F32)]


SEL_CHUNK = 4096


def _moba_select_kernel(q_ref, k_ref, sel_ref):
    blk = lax.broadcasted_iota(jnp.int32, (HEAD_DIM, SEQ), 0)
    pos = lax.broadcasted_iota(jnp.int32, (HEAD_DIM, SEQ), 1)
    ind = jnp.where(lax.shift_right_logical(pos, 8) == blk, 1.0 / MOBA_BLOCK, 0.0).astype(BF16)
    n_blk = SEQ // MOBA_BLOCK
    kmean = _split3(jnp.dot(ind, k_ref[...], preferred_element_type=F32)[:n_blk])
    blk_t = lax.broadcasted_iota(jnp.int32, (n_blk, SEL_CHUNK), 0)
    col = lax.broadcasted_iota(jnp.int32, (n_blk, SEL_CHUNK), 1)
    no_block = jnp.full((HEAD_DIM - n_blk, SEL_CHUNK), NEG, F32)

    def chunk(c, carry):
        off = pl.multiple_of(c * SEL_CHUNK, SEL_CHUNK)
        q = q_ref[pl.ds(off, SEL_CHUNK), :]
        gate = jnp.zeros((n_blk, SEL_CHUNK), F32)
        for part in kmean:
            gate = gate + lax.dot_general(part, q, _NT, preferred_element_type=F32)
        own = lax.shift_right_logical(off + col, 8)
        past = blk_t < own
        selb = _topk_bias_t(jnp.where(past, gate, NEG), MOBA_TOPK)
        selb = jnp.where(past, selb, jnp.where(blk_t == own, 0.0, NEG))
        sel_ref[0, pl.ds(off, SEL_CHUNK), :] = jnp.concatenate([selb, no_block], axis=0).T.astype(sel_ref.dtype)
        return carry

    lax.fori_loop(0, SEQ // SEL_CHUNK, chunk, 0)


def _moba_select(zb):
    return pl.pallas_call(
        _moba_select_kernel,
        grid=(MOBA_HEADS,),
        in_specs=[pl.BlockSpec((SEQ, HEAD_DIM), lambda h: (0, COL_MQ + h)),
                  pl.BlockSpec((SEQ, HEAD_DIM), lambda h: (0, COL_MK + h))],
        out_specs=pl.BlockSpec((1, SEQ, HEAD_DIM), lambda h: (h, 0, 0)),
        out_shape=jax.ShapeDtypeStruct((MOBA_HEADS, SEQ, HEAD_DIM), BF16),
        compiler_params=_cparams("parallel"),
        name="moba_select",
    )(zb, zb)


def _moba_kernel(q_ref, sel_ref, k_ref, v_ref, pat_ref, d0_ref, d1_ref, o_ref, *flash_scratch):
    i = pl.program_id(1)
    hp = HP_MOBA
    qs = [q_ref[:, h * HEAD_DIM:(h + 1) * HEAD_DIM] for h in range(hp)]
    selbs = [sel_ref[h].astype(F32) for h in range(hp)]
    cols = [h * HEAD_DIM for h in range(hp)]
    outs = _sparse_flash(qs, selbs, k_ref, v_ref, cols, pat_ref, [d0_ref[0, h] for h in range(hp)],
                         [d1_ref[0, h] for h in range(hp)], i, 1, flash_scratch)
    for h in range(hp):
        o_ref[:, h * HEAD_DIM:(h + 1) * HEAD_DIM] = outs[h].astype(o_ref.dtype)


def _moba(zb, selb, near, pat):
    hp = HP_MOBA
    w = hp * HEAD_DIM
    return pl.pallas_call(
        _moba_kernel,
        grid=(MOBA_HEADS // hp, N_QT),
        in_specs=[pl.BlockSpec((TQ, w), lambda h, i: (i, COL_MQ // hp + h)),
                  pl.BlockSpec((hp, TQ, HEAD_DIM), lambda h, i: (h, i, 0)),
                  pl.BlockSpec((SEQ, w), lambda h, i: (0, COL_MK // hp + h), pipeline_mode=pl.Buffered(1)),
                  pl.BlockSpec((SEQ, w), lambda h, i: (0, COL_MV // hp + h), pipeline_mode=pl.Buffered(1)),
                  pl.BlockSpec((SEQ, HEAD_DIM), lambda h, i: (0, 0), pipeline_mode=pl.Buffered(1)),
                  pl.BlockSpec((1, hp, TQ, TQ), lambda h, i: (0, h, 0, 0)),
                  pl.BlockSpec((1, hp, TQ, TQ), lambda h, i: (1, h, 0, 0))],
        out_specs=pl.BlockSpec((TQ, w), lambda h, i: (i, h)),
        out_shape=jax.ShapeDtypeStruct((SEQ, MOBA_HEADS * HEAD_DIM), BF16),
        scratch_shapes=_flash_scratch(hp, True),
        compiler_params=_cparams("parallel", "arbitrary", vmem=BIG_VMEM_LIMIT),
        name="moba",
    )(zb, selb, zb, zb, pat, near, near)


def _cmp_kernel(q_ref, kc_ref, vct_ref, ovt_ref, cb_ref, oc_ref, sel_ref, s_ref):
    i = pl.program_id(0)
    win0 = pl.multiple_of(jnp.maximum(i * CMP_STRIDE - CMP_STRIDE, 0), CMP_STRIDE)

    def body(nk):
        nb = nk * CMP_STRIDE // SLC_LEN
        blk = lax.broadcasted_iota(jnp.int32, (nb, TQ), 0)
        cur = i * SLC_PER_TILE + lax.shift_right_logical(lax.broadcasted_iota(jnp.int32, (nb, TQ), 1), 6)
        allowed = blk <= cur
        forced = (blk == 0) | (blk == cur) | (blk == cur - 1)
        reachable = lax.broadcasted_iota(jnp.int32, (nk, TQ), 0) < win0 + CMP_WIN
        for g in range(NSA_KV_HEADS):
            p_sum = jnp.zeros((nk, TQ), F32)
            for r in range(NSA_GROUP):
                hd = g * NSA_GROUP + r
                q = q_ref[:, hd * HEAD_DIM:(hd + 1) * HEAD_DIM]
                s_ref[:nk, :] = lax.dot_general(kc_ref[g, :nk, :], q, _NT, preferred_element_type=F32)
                s_ref[pl.ds(win0, CMP_WIN), :] += cb_ref[0, hd]
                s = jnp.where(reachable, s_ref[:nk, :], NEG)
                valid = s > 0.5 * NEG
                e = jnp.where(valid, jnp.exp2(s - jnp.max(s, axis=0, keepdims=True)), 0.0)
                p = e * (1.0 / jnp.maximum(jnp.sum(e, axis=0, keepdims=True), TINY))
                o_t = jnp.dot(vct_ref[g, :, :nk], p.astype(BF16), preferred_element_type=F32)
                oc_ref[:, hd * HEAD_DIM:(hd + 1) * HEAD_DIM] = o_t.T
                p_sum = p_sum + p
            imp = jnp.zeros((nb, TQ), F32)
            for part in _split3(p_sum):
                imp = imp + jnp.dot(ovt_ref[:nb, :nk], part, preferred_element_type=F32)
            picked = _topk_bias_t(jnp.where(allowed & ~forced, imp, -1.0), SLC_TOPK - 3)
            selb = jnp.where(allowed, jnp.where(forced, 0.0, picked), NEG)
            if nb < N_SLC:
                selb = jnp.concatenate([selb, jnp.full((N_SLC - nb, TQ), NEG, F32)], axis=0)
            sel_ref[g] = selb.T.astype(sel_ref.dtype)

    for c in range(1, 512 // HEAD_DIM + 1):
        @pl.when(lax.shift_right_logical(i, 3) == c - 1)
        def _():
            body(c * HEAD_DIM)


def _cmp_select(zb, kc, vct, ovt, cmpb):
    w = NSA_HEADS * HEAD_DIM
    return pl.pallas_call(
        _cmp_kernel,
        grid=(N_QT,),
        in_specs=[pl.BlockSpec((TQ, w), lambda i: (i, COL_NQ // NSA_HEADS)),
                  pl.BlockSpec((NSA_KV_HEADS, 512, HEAD_DIM), lambda i: (0, 0, 0)),
                  pl.BlockSpec((NSA_KV_HEADS, HEAD_DIM, 512), lambda i: (0, 0, 0)),
                  pl.BlockSpec((N_SLC, 512), lambda i: (0, 0)),
                  pl.BlockSpec((1, NSA_HEADS, CMP_WIN, TQ), lambda i: (jnp.minimum(i, 1), 1, 0, 0))],
        out_specs=[pl.BlockSpec((TQ, w), lambda i: (i, 0)),
                   pl.BlockSpec((NSA_KV_HEADS, TQ, N_SLC), lambda i: (0, i, 0))],
        out_shape=[jax.ShapeDtypeStruct((SEQ, w), F32),
                   jax.ShapeDtypeStruct((NSA_KV_HEADS, SEQ, N_SLC), BF16)],
        scratch_shapes=[pltpu.VMEM((512, TQ), F32)],
        compiler_params=_cparams("arbitrary"),
        name="cmp_select",
    )(zb, kc, vct, ovt, cmpb)


def _slc_kernel(q_ref, sel_ref, k_ref, v_ref, pat_ref, d0_ref, d1_ref, o_ref, *flash_scratch):
    i = pl.program_id(1)
    hp = HP_SLC
    selb = sel_ref[0].astype(F32)
    qs = [q_ref[:, h * HEAD_DIM:(h + 1) * HEAD_DIM] for h in range(hp)]
    outs = _sparse_flash(qs, [selb] * hp, k_ref, v_ref, [0] * hp, pat_ref,
                         [d0_ref[0, h] for h in range(hp)], [d1_ref[0, h] for h in range(hp)],
                         i, SLC_PER_TILE, flash_scratch)
    for h in range(hp):
        o_ref[:, h * HEAD_DIM:(h + 1) * HEAD_DIM] = outs[h].astype(o_ref.dtype)


def _slc(zb, selb, near, pat):
    hp = HP_SLC
    w = hp * HEAD_DIM
    per_group = NSA_GROUP // hp
    return pl.pallas_call(
        _slc_kernel,
        grid=(NSA_HEADS // hp, N_QT),
        in_specs=[pl.BlockSpec((TQ, w), lambda h, i: (i, COL_NQ // hp + h)),
                  pl.BlockSpec((1, TQ, N_SLC), lambda h, i: (h // per_group, i, 0)),
                  pl.BlockSpec((SEQ, HEAD_DIM), lambda h, i: (0, COL_SK + h // per_group),
                               pipeline_mode=pl.Buffered(1)),
                  pl.BlockSpec((SEQ, HEAD_DIM), lambda h, i: (0, COL_SV + h // per_group),
                               pipeline_mode=pl.Buffered(1)),
                  pl.BlockSpec((SEQ, HEAD_DIM), lambda h, i: (0, 0), pipeline_mode=pl.Buffered(1)),
                  pl.BlockSpec((1, hp, TQ, TQ), lambda h, i: (0, MOBA_HEADS // hp + h, 0, 0)),
                  pl.BlockSpec((1, hp, TQ, TQ), lambda h, i: (1, MOBA_HEADS // hp + h, 0, 0))],
        out_specs=pl.BlockSpec((TQ, w), lambda h, i: (i, h)),
        out_shape=jax.ShapeDtypeStruct((SEQ, NSA_HEADS * HEAD_DIM), F32),
        scratch_shapes=_flash_scratch(hp, False),
        compiler_params=_cparams("parallel", "arbitrary", vmem=BIG_VMEM_LIMIT),
        name="slc",
    )(zb, selb, zb, zb, pat, near, near)


def _win_kernel(q_ref, k_ref, v_ref, d0_ref, d1_ref, w2_ref, o_ref):
    i = pl.program_id(0)
    offs = [pl.multiple_of(jnp.maximum(i - b, 0) * TQ, TQ) for b in range(3)]
    mask2 = jnp.where(i >= 2, w2_ref[...], NEG)
    ones = jnp.ones((3 * TQ, HEAD_DIM), BF16)
    for g in range(NSA_KV_HEADS):
        cols = slice(g * HEAD_DIM, (g + 1) * HEAD_DIM)
        k_all = jnp.concatenate([k_ref[pl.ds(off, TQ), cols] for off in offs], axis=0)
        v_aug = jnp.concatenate([jnp.concatenate([v_ref[pl.ds(off, TQ), cols] for off in offs], axis=0), ones],
                                axis=1)
        for r in range(NSA_GROUP):
            hd = g * NSA_GROUP + r
            q = q_ref[:, hd * HEAD_DIM:(hd + 1) * HEAD_DIM]
            bias = jnp.concatenate([d0_ref[0, hd], jnp.where(i >= 1, d1_ref[0, hd], NEG), mask2], axis=1)
            s = lax.dot_general(q, k_all, _NT, preferred_element_type=F32) + bias
            m = jnp.max(s, axis=1, keepdims=True)
            pv = jnp.dot(jnp.exp2(s - m).astype(BF16), v_aug, preferred_element_type=F32)
            o_ref[:, hd * HEAD_DIM:(hd + 1) * HEAD_DIM] = pv[:, :HEAD_DIM] / pv[:, HEAD_DIM:]


def _win(zb, near, win2):
    w = NSA_HEADS * HEAD_DIM
    kv_w = NSA_KV_HEADS * HEAD_DIM
    return pl.pallas_call(
        _win_kernel,
        grid=(N_QT,),
        in_specs=[pl.BlockSpec((TQ, w), lambda i: (i, COL_NQ // NSA_HEADS)),
                  pl.BlockSpec((SEQ, kv_w), lambda i: (0, COL_WK // NSA_KV_HEADS), pipeline_mode=pl.Buffered(1)),
                  pl.BlockSpec((SEQ, kv_w), lambda i: (0, COL_WV // NSA_KV_HEADS), pipeline_mode=pl.Buffered(1)),
                  pl.BlockSpec((1, NSA_HEADS, TQ, TQ), lambda i: (0, 1, 0, 0), pipeline_mode=pl.Buffered(1)),
                  pl.BlockSpec((1, NSA_HEADS, TQ, TQ), lambda i: (1, 1, 0, 0), pipeline_mode=pl.Buffered(1)),
                  pl.BlockSpec((TQ, TQ), lambda i: (0, 0), pipeline_mode=pl.Buffered(1))],
        out_specs=pl.BlockSpec((TQ, w), lambda i: (i, 0)),
        out_shape=jax.ShapeDtypeStruct((SEQ, w), F32),
        compiler_params=_cparams("arbitrary"),
        name="win",
    )(zb, zb, zb, near, near, win2)


def _layer_norm(y, g, b):
    mu = jnp.mean(y, axis=-1, keepdims=True)
    yc = y - mu
    var = jnp.mean(yc * yc, axis=-1, keepdims=True)
    return yc * lax.rsqrt(var + LN_EPS) * g + b


ROW_CHUNK = 256


def _outproj_kernel(x_ref, om_ref, oc_ref, os_ref, ow_ref, g_ref, e_ref, w_ref, lg_ref, lb_ref,
                    h_ref, hb_ref):
    for c0 in range(0, x_ref.shape[0], ROW_CHUNK):
        rows = slice(c0, c0 + ROW_CHUNK)
        gate = g_ref[rows, :]
        hi = gate.astype(BF16)
        lo = (gate - hi.astype(F32)).astype(BF16)
        g2 = jnp.concatenate([hi, lo], axis=1)
        o_n = jnp.zeros((ROW_CHUNK, oc_ref.shape[1]), F32)
        for c, branch in enumerate((oc_ref, os_ref, ow_ref)):
            o_n = o_n + jnp.dot(g2, e_ref[c], preferred_element_type=F32) * branch[rows, :]
        o = jnp.concatenate([om_ref[rows, :], o_n.astype(BF16)], axis=1)
        a = jnp.dot(o, w_ref[...], preferred_element_type=F32)
        h = _layer_norm(DN_ALPHA * x_ref[rows, :] + a, lg_ref[...], lb_ref[...])
        h_ref[rows, :] = h
        hb_ref[rows, :] = h.astype(BF16)


def _resident(shape):
    return pl.BlockSpec(shape, lambda i: (0,) * len(shape), pipeline_mode=pl.Buffered(1))


def _outproj(x, o_m, o_c, o_s, o_w, gates, expand, w_out, ln_g, ln_b, tm=512):
    half = NSA_HEADS * HEAD_DIM
    row = lambda i: (i, 0)
    return pl.pallas_call(
        _outproj_kernel,
        grid=(SEQ // tm,),
        in_specs=[pl.BlockSpec((tm, D_MODEL), row),
                  pl.BlockSpec((tm, half), row), pl.BlockSpec((tm, half), row),
                  pl.BlockSpec((tm, half), row), pl.BlockSpec((tm, half), row),
                  pl.BlockSpec((tm, HEAD_DIM), row),
                  _resident((3, 2 * HEAD_DIM, half)),
                  _resident((D_MODEL, D_MODEL)),
                  _resident((1, D_MODEL)), _resident((1, D_MODEL))],
        out_specs=[pl.BlockSpec((tm, D_MODEL), row), pl.BlockSpec((tm, D_MODEL), row)],
        out_shape=[jax.ShapeDtypeStruct((SEQ, D_MODEL), F32),
                   jax.ShapeDtypeStruct((SEQ, D_MODEL), BF16)],
        compiler_params=_cparams("parallel", vmem=BIG_VMEM_LIMIT),
        name="outproj_ln",
    )(x, o_m, o_c, o_s, o_w, gates, expand, w_out, ln_g, ln_b)


def _ffn_kernel(hb_ref, w1_ref, w2_ref, f_ref):
    @pl.when(pl.program_id(1) == 0)
    def _():
        f_ref[...] = jnp.zeros(f_ref.shape, F32)

    u = jnp.maximum(jnp.dot(hb_ref[...], w1_ref[...].astype(BF16), preferred_element_type=F32), 0.0)
    f_ref[...] += jnp.dot((u * u).astype(BF16), w2_ref[...].astype(BF16), preferred_element_type=F32)


def _ffn(hb, w1, w2, tm=1024, tf=512):
    return pl.pallas_call(
        _ffn_kernel,
        grid=(SEQ // tm, D_FF // tf),
        in_specs=[pl.BlockSpec((tm, D_MODEL), lambda i, c: (i, 0)),
                  pl.BlockSpec((D_MODEL, tf), lambda i, c: (0, c)),
                  pl.BlockSpec((tf, D_MODEL), lambda i, c: (c, 0))],
        out_specs=pl.BlockSpec((tm, D_MODEL), lambda i, c: (i, 0)),
        out_shape=jax.ShapeDtypeStruct((SEQ, D_MODEL), F32),
        compiler_params=_cparams("parallel", "arbitrary", vmem=BIG_VMEM_LIMIT),
        name="ffn",
    )(hb, w1, w2)


def _final_kernel(h_ref, hb_ref, f_ref, p_ref, wg_ref, wp_ref, lg_ref, lb_ref, o_ref):
    for c0 in range(0, h_ref.shape[0], ROW_CHUNK):
        rows = slice(c0, c0 + ROW_CHUNK)
        gate = jax.nn.sigmoid(jnp.dot(hb_ref[rows, :], wg_ref[...], preferred_element_type=F32))
        emb = jnp.dot(p_ref[rows, :].astype(BF16), wp_ref[...], preferred_element_type=F32)
        y = DN_ALPHA * h_ref[rows, :] + f_ref[rows, :] + gate * emb
        o_ref[rows, :] = _layer_norm(y, lg_ref[...], lb_ref[...])


def _final(h, hb, f, p, wg, wp, ln_g, ln_b, tm=512):
    row = lambda i: (i, 0)
    return pl.pallas_call(
        _final_kernel,
        grid=(SEQ // tm,),
        in_specs=[pl.BlockSpec((tm, D_MODEL), row), pl.BlockSpec((tm, D_MODEL), row),
                  pl.BlockSpec((tm, D_MODEL), row), pl.BlockSpec((tm, D_PLE), row),
                  _resident((D_MODEL, D_MODEL)), _resident((D_PLE, D_MODEL)),
                  _resident((1, D_MODEL)), _resident((1, D_MODEL))],
        out_specs=pl.BlockSpec((tm, D_MODEL), row),
        out_shape=jax.ShapeDtypeStruct((SEQ, D_MODEL), F32),
        compiler_params=_cparams("parallel"),
        name="final_ln",
    )(h, hb, f, p, wg, wp, ln_g, ln_b)


def _overlap_matrix_t():
    j = np.arange(N_SLC)[:, None]
    n = np.arange(512)[None, :]
    ov = ((n * CMP_STRIDE < j * SLC_LEN + SLC_LEN) & (n * CMP_STRIDE + CMP_LEN > j * SLC_LEN) & (n < N_CMP))
    return ov.astype(np.float32)


def _gate_expand():
    e = np.zeros((3, 2 * HEAD_DIM, NSA_HEADS * HEAD_DIM), np.float32)
    for c in range(3):
        for h in range(NSA_HEADS):
            e[c, 3 * h + c, h * HEAD_DIM:(h + 1) * HEAD_DIM] = 1.0
            e[c, HEAD_DIM + 3 * h + c, h * HEAD_DIM:(h + 1) * HEAD_DIM] = 1.0
    return e


def _layer(h, p, w_in, cmp_pe_k, cmp_w1_k, cmp_w2_k, cmp_pe_v, cmp_w1_v, cmp_w2_v, tiles, w_out,
           ln1_g, ln1_b, w_ff1, w_ff2, w_ple, w_ple_gate, ln2_g, ln2_b):
    near, win2, cmpb = tiles
    scale = HEAD_DIM ** -0.5 * LOG2E
    colscale = np.ones((1, GATE_COL0), np.float32)
    colscale[0, COL_MQ * HEAD_DIM:(COL_MQ + MOBA_HEADS) * HEAD_DIM] = scale
    colscale[0, COL_NQ * HEAD_DIM:(COL_NQ + NSA_HEADS) * HEAD_DIM] = scale
    w_in_t = w_in.T
    w_gl_t = jnp.pad(w_in_t[GATE_COL0:], ((0, HEAD_DIM - NSA_HEADS * 3), (0, 0))).astype(BF16)
    zb, gates, ckv = _inproj(h, w_in_t, jnp.asarray(colscale), w_gl_t)

    half = CMP_STRIDE * HEAD_DIM
    xr = ckv.reshape(4, SEQ // CMP_STRIDE, half)
    kc, vct = _compress(xr, cmp_pe_k.reshape(2, 1, half), cmp_w1_k.reshape(2, half, HEAD_DIM), cmp_w2_k,
                        cmp_pe_v.reshape(2, 1, half), cmp_w1_v.reshape(2, half, HEAD_DIM), cmp_w2_v)

    key = np.arange(SEQ)[:, None]
    lane = np.arange(HEAD_DIM)[None, :]
    pat_moba = jnp.asarray(lane == key // MOBA_BLOCK, BF16)
    pat_slc = jnp.asarray(lane == key // SLC_LEN, BF16)
    o_m = _moba(zb, _moba_select(zb), near, pat_moba)
    o_c, selb = _cmp_select(zb, kc, vct, jnp.asarray(_overlap_matrix_t(), BF16), cmpb)
    o_s = _slc(zb, selb, near, pat_slc)
    o_w = _win(zb, near, win2)

    h1, h1b = _outproj(h, o_m, o_c, o_s, o_w, gates, jnp.asarray(_gate_expand(), BF16),
                       w_out.astype(BF16), ln1_g.reshape(1, -1), ln1_b.reshape(1, -1))
    f = _ffn(h1b, w_ff1, w_ff2)
    return _final(h1, h1b, f, p, w_ple_gate.astype(BF16), w_ple.astype(BF16),
                  ln2_g.reshape(1, -1), ln2_b.reshape(1, -1))


def kernel(x, p, w_in, cmp_pe_k, cmp_w1_k, cmp_w2_k, cmp_pe_v, cmp_w1_v, cmp_w2_v, rel_bias, w_out,
           ln1_g, ln1_b, w_ff1, w_ff2, w_ple, w_ple_gate, ln2_g, ln2_b):
    tiles = _bias_tiles(rel_bias)
    h = x[0]
    for i in range(w_in.shape[0]):
        h = _layer(h, p[i, 0], w_in[i], cmp_pe_k[i], cmp_w1_k[i], cmp_w2_k[i], cmp_pe_v[i], cmp_w1_v[i],
                   cmp_w2_v[i], tiles, w_out[i], ln1_g[i], ln1_b[i], w_ff1[i], w_ff2[i], w_ple[i],
                   w_ple_gate[i], ln2_g[i], ln2_b[i])
    return h[None]
```

```python
import functools
import math

import numpy as np
import jax
import jax.numpy as jnp
from jax import lax
from jax.experimental import pallas as pl
from jax.experimental.pallas import tpu as pltpu

D_MODEL = 2048
SEQ = 8192
HEAD_DIM = 128
N_HEADS = 16
MOBA_HEADS = 8
NSA_HEADS = 8
NSA_KV_HEADS = 2
NSA_GROUP = 4
MOBA_BLOCK = 256
MOBA_TOPK = 3
CMP_LEN = 32
CMP_STRIDE = 16
SLC_LEN = 64
SLC_TOPK = 16
WINDOW = 512
N_BUCKETS = 32
MAX_DISTANCE = 128
D_FF = 4 * D_MODEL
D_PLE = 256
LN_EPS = 1e-5
DN_ALPHA = 2.0 ** 0.25
NEG = -1e30
LOG2E = math.log2(math.e)
TINY = 1e-30
FORCE_SCORE = 1e4

N_CMP = (SEQ - CMP_LEN) // CMP_STRIDE + 1
N_SLC = SEQ // SLC_LEN
TQ = 256
N_QT = SEQ // TQ
CMP_WIN = 32
SLC_PER_TILE = TQ // SLC_LEN
HP_MOBA = 4
HP_SLC = 4
KG = 4
GROUP = KG * TQ
COL_MQ, COL_MK, COL_MV, COL_NQ = 0, 8, 16, 24
COL_CK, COL_CV, COL_SK, COL_SV, COL_WK, COL_WV = 32, 34, 36, 38, 40, 42
N_COLBLK = 44
GATE_COL0 = N_COLBLK * HEAD_DIM

VMEM_LIMIT = 48 * 1024 * 1024
BIG_VMEM_LIMIT = 56 * 1024 * 1024

BF16 = jnp.bfloat16
F32 = jnp.float32
_NT = (((1,), (1,)), ((), ()))


def _cparams(*sem, vmem=VMEM_LIMIT):
    return pltpu.CompilerParams(dimension_semantics=sem, vmem_limit_bytes=vmem)


def _bucket_table():
    n = np.arange(1024, dtype=np.int32)
    max_exact = N_BUCKETS // 2
    ratio = np.maximum(n, 1).astype(np.float32) / np.float32(max_exact)
    large = max_exact + (np.log(ratio).astype(np.float32) / np.float32(math.log(MAX_DISTANCE / max_exact))
                         * np.float32(N_BUCKETS - max_exact)).astype(np.int32)
    large = np.minimum(large, N_BUCKETS - 1)
    return np.where(n < max_exact, n, large).astype(np.int32)


def _bias_kernel(tab_ref, near_ref, cmpb_ref, *, steps):
    h = pl.program_id(0)
    far = tab_ref[h, N_BUCKETS - 1]

    def lookup(d):
        val = jnp.full(d.shape, tab_ref[h, 0], F32)
        for start, bucket in steps:
            val = jnp.where(d >= start, tab_ref[h, bucket], val)
        return jnp.where(d >= 0, (val - far) * LOG2E, NEG)

    a = lax.broadcasted_iota(jnp.int32, (TQ, TQ), 0)
    b = lax.broadcasted_iota(jnp.int32, (TQ, TQ), 1)
    near_ref[0, 0] = lookup(a - b)
    near_ref[1, 0] = lookup(a - b + TQ)
    w = lax.broadcasted_iota(jnp.int32, (CMP_WIN, TQ), 0)
    aw = lax.broadcasted_iota(jnp.int32, (CMP_WIN, TQ), 1)
    cmpb_ref[0, 0] = lookup(aw - CMP_STRIDE * w - (CMP_LEN - 1))
    cmpb_ref[1, 0] = lookup(aw - CMP_STRIDE * w + (TQ - CMP_LEN + 1))


def _bias_tiles(rel_bias):
    bucket = _bucket_table()
    steps = tuple((int(n), int(bucket[n])) for n in range(1, bucket.size) if bucket[n] != bucket[n - 1])
    near, cmpb = pl.pallas_call(
        functools.partial(_bias_kernel, steps=steps),
        grid=(N_HEADS,),
        in_specs=[pl.BlockSpec(memory_space=pltpu.SMEM)],
        out_specs=[pl.BlockSpec((2, 1, TQ, TQ), lambda h: (0, h, 0, 0)),
                   pl.BlockSpec((2, 1, CMP_WIN, TQ), lambda h: (0, h, 0, 0))],
        out_shape=[jax.ShapeDtypeStruct((2, N_HEADS, TQ, TQ), F32),
                   jax.ShapeDtypeStruct((2, N_HEADS, CMP_WIN, TQ), F32)],
        compiler_params=_cparams("parallel"),
        name="bias_tiles",
    )(rel_bias.T.astype(F32))
    a = np.arange(TQ)[:, None]
    b = np.arange(TQ)[None, :]
    win2 = jnp.asarray(np.where(a < b, 0.0, NEG).astype(np.float32))
    return near, win2, cmpb


INPROJ_TN = 4 * HEAD_DIM
CKV_TILE = COL_CK * HEAD_DIM // INPROJ_TN


def _inproj_kernel(x_ref, w_ref, cs_ref, wg_ref, o_ref, g_ref, ckv_ref, xb_ref):
    j = pl.program_id(1)

    @pl.when(j == 0)
    def _():
        xb = x_ref[...].astype(BF16)
        xb_ref[...] = xb
        g_ref[...] = jax.nn.sigmoid(lax.dot_general(xb, wg_ref[...], _NT, preferred_element_type=F32))

    acc = lax.dot_general(xb_ref[...], w_ref[...].astype(BF16), _NT, preferred_element_type=F32)
    z = (acc * cs_ref[...]).astype(o_ref.dtype)
    o_ref[...] = z

    @pl.when(j == CKV_TILE)
    def _():
        for c in range(INPROJ_TN // HEAD_DIM):
            ckv_ref[c] = z[:, c * HEAD_DIM:(c + 1) * HEAD_DIM]


def _inproj(x, w_in_t, colscale, w_gate_t, tm=1024):
    m, k = x.shape
    tn = INPROJ_TN
    return pl.pallas_call(
        _inproj_kernel,
        grid=(m // tm, GATE_COL0 // tn),
        in_specs=[pl.BlockSpec((tm, k), lambda i, j: (i, 0)),
                  pl.BlockSpec((tn, k), lambda i, j: (j, 0)),
                  pl.BlockSpec((1, tn), lambda i, j: (0, j)),
                  pl.BlockSpec((HEAD_DIM, k), lambda i, j: (0, 0))],
        out_specs=[pl.BlockSpec((tm, tn), lambda i, j: (i, j)),
                   pl.BlockSpec((tm, HEAD_DIM), lambda i, j: (i, 0)),
                   pl.BlockSpec((tn // HEAD_DIM, tm, HEAD_DIM), lambda i, j: (0, i, 0))],
        out_shape=[jax.ShapeDtypeStruct((m, GATE_COL0), BF16),
                   jax.ShapeDtypeStruct((m, HEAD_DIM), F32),
                   jax.ShapeDtypeStruct((tn // HEAD_DIM, m, HEAD_DIM), BF16)],
        scratch_shapes=[pltpu.VMEM((tm, k), BF16)],
        compiler_params=_cparams("parallel", "arbitrary"),
        name="inproj",
    )(x, w_in_t, colscale, w_gate_t)


def _gelu_tanh(x):
    return 0.5 * x * (1.0 + jnp.tanh(math.sqrt(2.0 / math.pi) * (x + 0.044715 * (x * x * x))))


def _compress_kernel(xk_ref, xv_ref, pek_ref, w1k_ref, w2k_ref, pev_ref, w1v_ref, w2v_ref, kc_ref, vct_ref):
    def compress(x_ref, pe_ref, w1_ref, w2_ref):
        x = x_ref[0].astype(F32)
        lo = jnp.dot((x + pe_ref[0]).astype(BF16), w1_ref[0].astype(BF16), preferred_element_type=F32)
        hi = jnp.dot((x + pe_ref[1]).astype(BF16), w1_ref[1].astype(BF16), preferred_element_type=F32)
        y = lo + pltpu.roll(hi, N_CMP, 0)
        out = jnp.dot(_gelu_tanh(y).astype(BF16), w2_ref[...].astype(BF16), preferred_element_type=F32)
        row = lax.broadcasted_iota(jnp.int32, out.shape, 0)
        return jnp.where(row < N_CMP, out, 0.0)

    kc_ref[0] = compress(xk_ref, pek_ref, w1k_ref, w2k_ref).astype(kc_ref.dtype)
    vct_ref[0] = compress(xv_ref, pev_ref, w1v_ref, w2v_ref).T.astype(vct_ref.dtype)


def _compress(xr, pe_k, w1_k, w2_k, pe_v, w1_v, w2_v):
    half = CMP_STRIDE * HEAD_DIM
    weights = [pl.BlockSpec((2, 1, half), lambda g: (0, 0, 0)),
               pl.BlockSpec((2, half, HEAD_DIM), lambda g: (0, 0, 0)),
               pl.BlockSpec((HEAD_DIM, HEAD_DIM), lambda g: (0, 0))]
    return pl.pallas_call(
        _compress_kernel,
        grid=(NSA_KV_HEADS,),
        in_specs=[pl.BlockSpec((1, 512, half), lambda g: (g, 0, 0)),
                  pl.BlockSpec((1, 512, half), lambda g: (NSA_KV_HEADS + g, 0, 0))] + weights + weights,
        out_specs=[pl.BlockSpec((1, 512, HEAD_DIM), lambda g: (g, 0, 0)),
                   pl.BlockSpec((1, HEAD_DIM, 512), lambda g: (g, 0, 0))],
        out_shape=[jax.ShapeDtypeStruct((NSA_KV_HEADS, 512, HEAD_DIM), BF16),
                   jax.ShapeDtypeStruct((NSA_KV_HEADS, HEAD_DIM, 512), BF16)],
        compiler_params=_cparams("parallel"),
        name="compress",
    )(xr, xr, pe_k, w1_k, w2_k, pe_v, w1_v, w2_v)


def _split3(x):
    hi = x.astype(BF16)
    r1 = x - hi.astype(F32)
    mid = r1.astype(BF16)
    lo = (r1 - mid.astype(F32)).astype(BF16)
    return hi, mid, lo


def _topk_bias_t(score, k):
    row = lax.broadcasted_iota(jnp.int32, score.shape, 0).astype(F32)
    selb = jnp.full(score.shape, NEG, F32)
    s = score
    for _ in range(k):
        mx = jnp.max(s, axis=0, keepdims=True)
        idx = jnp.min(jnp.where(s == mx, row, 1e9), axis=0, keepdims=True)
        hit = row == idx
        selb = jnp.where(hit, 0.0, selb)
        s = jnp.where(hit, -3e38, s)
    return selb


def _lane_tile_max(s):
    m = s[:, :HEAD_DIM]
    for c in range(1, s.shape[1] // HEAD_DIM):
        m = jnp.maximum(m, s[:, c * HEAD_DIM:(c + 1) * HEAD_DIM])
    return m


def _sparse_flash(qs, selbs, k_ref, v_ref, kcols, pat_ref, d0s, d1s, i, sub, scratch):
    t_ref, *maybe_r_ref, mx_ref, acc_ref = scratch
    r_ref = maybe_r_ref[0] if maybe_r_ref else None
    heads = range(len(qs))
    lane = lax.broadcasted_iota(jnp.int32, (TQ, HEAD_DIM), 1)
    far = lane < (i - 1) * sub
    q_near = [jnp.concatenate([qs[h], selbs[h].astype(BF16)], axis=1) for h in heads]
    q_far = [jnp.concatenate([qs[h], jnp.where(far, selbs[h], NEG).astype(BF16)], axis=1) for h in heads]

    def k_aug(h, off, n):
        c = kcols[h]
        return jnp.concatenate([k_ref[pl.ds(off, n), c:c + HEAD_DIM], pat_ref[pl.ds(off, n), :]], axis=1)

    def v_aug(h, off, n):
        c = kcols[h]
        return jnp.concatenate([v_ref[pl.ds(off, n), c:c + HEAD_DIM], jnp.ones((n, HEAD_DIM), BF16)], axis=1)

    for h in heads:
        mx_ref[h] = jnp.full((TQ, HEAD_DIM), NEG, F32)

    def score(g):
        off = pl.multiple_of(g * GROUP, GROUP)
        for h in heads:
            s = lax.dot_general(q_far[h], k_aug(h, off, GROUP), _NT, preferred_element_type=F32)
            ref = _lane_tile_max(s)
            if r_ref is not None:
                s = s - jnp.concatenate([ref] * (GROUP // HEAD_DIM), axis=1)
                r_ref[h, g] = ref
            for u in range(KG):
                t_ref[h, g * KG + u] = s[:, u * TQ:(u + 1) * TQ].astype(t_ref.dtype)
            mx_ref[h] = jnp.maximum(mx_ref[h], ref)

    def run_unrolled(n, body):
        def quad(m, carry):
            body(4 * m, 4)
            return carry

        whole = (5, 6, 7)
        one_piece = (n >= whole[0]) & (n <= whole[-1])
        lax.fori_loop(0, jnp.where(one_piece, 0, lax.shift_right_logical(n, 2)), quad, 0)
        for left in (1, 2, 3):
            @pl.when(((n & 3) == left) & ~one_piece)
            def _(left=left):
                body(n - left, left)
        for count in whole:
            @pl.when(n == count)
            def _(count=count):
                body(n - count, count)

    n_far = lax.shift_right_logical(jnp.maximum(i - 1, 0) + (KG - 1), KG.bit_length() - 1)

    def score_groups(g0, count):
        for k in range(count):
            score(g0 + k)

    run_unrolled(n_far, score_groups)

    off_prev = pl.multiple_of(jnp.maximum(i - 1, 0) * TQ, TQ)
    off_own = pl.multiple_of(i * TQ, TQ)
    p_near = []
    for h in heads:
        k2 = jnp.concatenate([k_aug(h, off_prev, TQ), k_aug(h, off_own, TQ)], axis=0)
        bias = jnp.concatenate([jnp.where(i >= 1, d1s[h], NEG), d0s[h]], axis=1)
        s = lax.dot_general(q_near[h], k2, _NT, preferred_element_type=F32) + bias
        m = jnp.max(jnp.maximum(mx_ref[h], _lane_tile_max(s)), axis=1, keepdims=True)
        mx_ref[h] = jnp.broadcast_to(m, (TQ, HEAD_DIM))
        p_near.append(jnp.exp2(s - m).astype(BF16))
    for h in heads:
        v2 = jnp.concatenate([v_aug(h, off_prev, TQ), v_aug(h, off_own, TQ)], axis=0)
        acc_ref[h] = jnp.dot(p_near[h], v2, preferred_element_type=F32)

    def accumulate(g0, ng):
        off = pl.multiple_of(g0 * GROUP, GROUP)
        for h in heads:
            probs = []
            for g in [g0 + k for k in range(ng)]:
                d = -mx_ref[h] if r_ref is None else r_ref[h, g] - mx_ref[h]
                d2 = jnp.concatenate([d, d], axis=1)
                probs += [jnp.exp2(t_ref[h, g * KG + u].astype(F32) + d2).astype(BF16) for u in range(KG)]
            acc_ref[h] += jnp.dot(jnp.concatenate(probs, axis=1), v_aug(h, off, ng * GROUP),
                                  preferred_element_type=F32)

    def accumulate_groups(g0, count):
        for k in range(0, count, 2):
            accumulate(g0 + k, min(2, count - k))

    run_unrolled(n_far, accumulate_groups)

    outs = []
    for h in heads:
        acc = acc_ref[h]
        outs.append(acc[:, :HEAD_DIM] / acc[:, HEAD_DIM:])
    return outs


def _flash_scratch(hp, relative_bf16):
    logits = [pltpu.VMEM((hp, N_QT, TQ, TQ), BF16),
              pltpu.VMEM((hp, SEQ // GROUP, TQ, HEAD_DIM), F32)]
    if not relative_bf16:
        logits = [pltpu.VMEM((hp, N_QT, TQ, TQ), F32)]
    return logits + [pltpu.VMEM((hp, TQ, HEAD_DIM), F32),
                     pltpu.VMEM((hp, TQ, 2 * HEAD_DIM), F32)]


SEL_CHUNK = 4096


def _moba_select_kernel(q_ref, k_ref, sel_ref):
    blk = lax.broadcasted_iota(jnp.int32, (HEAD_DIM, SEQ), 0)
    pos = lax.broadcasted_iota(jnp.int32, (HEAD_DIM, SEQ), 1)
    ind = jnp.where(lax.shift_right_logical(pos, 8) == blk, 1.0 / MOBA_BLOCK, 0.0).astype(BF16)
    n_blk = SEQ // MOBA_BLOCK
    kmean = _split3(jnp.dot(ind, k_ref[...], preferred_element_type=F32)[:n_blk])
    blk_t = lax.broadcasted_iota(jnp.int32, (n_blk, SEL_CHUNK), 0)
    col = lax.broadcasted_iota(jnp.int32, (n_blk, SEL_CHUNK), 1)
    no_block = jnp.full((HEAD_DIM - n_blk, SEL_CHUNK), NEG, F32)

    def chunk(c, carry):
        off = pl.multiple_of(c * SEL_CHUNK, SEL_CHUNK)
        q = q_ref[pl.ds(off, SEL_CHUNK), :]
        gate = jnp.zeros((n_blk, SEL_CHUNK), F32)
        for part in kmean:
            gate = gate + lax.dot_general(part, q, _NT, preferred_element_type=F32)
        own = lax.shift_right_logical(off + col, 8)
        past = blk_t < own
        selb = _topk_bias_t(jnp.where(past, gate, NEG), MOBA_TOPK)
        selb = jnp.where(past, selb, jnp.where(blk_t == own, 0.0, NEG))
        sel_ref[0, pl.ds(off, SEL_CHUNK), :] = jnp.concatenate([selb, no_block], axis=0).T.astype(sel_ref.dtype)
        return carry

    lax.fori_loop(0, SEQ // SEL_CHUNK, chunk, 0)


def _moba_select(zb):
    return pl.pallas_call(
        _moba_select_kernel,
        grid=(MOBA_HEADS,),
        in_specs=[pl.BlockSpec((SEQ, HEAD_DIM), lambda h: (0, COL_MQ + h)),
                  pl.BlockSpec((SEQ, HEAD_DIM), lambda h: (0, COL_MK + h))],
        out_specs=pl.BlockSpec((1, SEQ, HEAD_DIM), lambda h: (h, 0, 0)),
        out_shape=jax.ShapeDtypeStruct((MOBA_HEADS, SEQ, HEAD_DIM), BF16),
        compiler_params=_cparams("parallel"),
        name="moba_select",
    )(zb, zb)


def _moba_kernel(q_ref, sel_ref, k_ref, v_ref, pat_ref, d0_ref, d1_ref, o_ref, *flash_scratch):
    i = pl.program_id(1)
    hp = HP_MOBA
    qs = [q_ref[:, h * HEAD_DIM:(h + 1) * HEAD_DIM] for h in range(hp)]
    selbs = [sel_ref[h].astype(F32) for h in range(hp)]
    cols = [h * HEAD_DIM for h in range(hp)]
    outs = _sparse_flash(qs, selbs, k_ref, v_ref, cols, pat_ref, [d0_ref[0, h] for h in range(hp)],
                         [d1_ref[0, h] for h in range(hp)], i, 1, flash_scratch)
    for h in range(hp):
        o_ref[:, h * HEAD_DIM:(h + 1) * HEAD_DIM] = outs[h].astype(o_ref.dtype)


def _moba(zb, selb, near, pat):
    hp = HP_MOBA
    w = hp * HEAD_DIM
    return pl.pallas_call(
        _moba_kernel,
        grid=(MOBA_HEADS // hp, N_QT),
        in_specs=[pl.BlockSpec((TQ, w), lambda h, i: (i, COL_MQ // hp + h)),
                  pl.BlockSpec((hp, TQ, HEAD_DIM), lambda h, i: (h, i, 0)),
                  pl.BlockSpec((SEQ, w), lambda h, i: (0, COL_MK // hp + h), pipeline_mode=pl.Buffered(1)),
                  pl.BlockSpec((SEQ, w), lambda h, i: (0, COL_MV // hp + h), pipeline_mode=pl.Buffered(1)),
                  pl.BlockSpec((SEQ, HEAD_DIM), lambda h, i: (0, 0), pipeline_mode=pl.Buffered(1)),
                  pl.BlockSpec((1, hp, TQ, TQ), lambda h, i: (0, h, 0, 0)),
                  pl.BlockSpec((1, hp, TQ, TQ), lambda h, i: (1, h, 0, 0))],
        out_specs=pl.BlockSpec((TQ, w), lambda h, i: (i, h)),
        out_shape=jax.ShapeDtypeStruct((SEQ, MOBA_HEADS * HEAD_DIM), BF16),
        scratch_shapes=_flash_scratch(hp, True),
        compiler_params=_cparams("parallel", "arbitrary", vmem=BIG_VMEM_LIMIT),
        name="moba",
    )(zb, selb, zb, zb, pat, near, near)


def _cmp_kernel(q_ref, kc_ref, vct_ref, ovt_ref, cb_ref, oc_ref, sel_ref, s_ref):
    i = pl.program_id(0)
    win0 = pl.multiple_of(jnp.maximum(i * CMP_STRIDE - CMP_STRIDE, 0), CMP_STRIDE)

    def body(nk):
        nb = nk * CMP_STRIDE // SLC_LEN
        blk = lax.broadcasted_iota(jnp.int32, (nb, TQ), 0)
        cur = i * SLC_PER_TILE + lax.shift_right_logical(lax.broadcasted_iota(jnp.int32, (nb, TQ), 1), 6)
        allowed = blk <= cur
        forced = (blk == 0) | (blk == cur) | (blk == cur - 1)
        reachable = lax.broadcasted_iota(jnp.int32, (nk, TQ), 0) < win0 + CMP_WIN
        for g in range(NSA_KV_HEADS):
            p_sum = jnp.zeros((nk, TQ), F32)
            for r in range(NSA_GROUP):
                hd = g * NSA_GROUP + r
                q = q_ref[:, hd * HEAD_DIM:(hd + 1) * HEAD_DIM]
                s_ref[:nk, :] = lax.dot_general(kc_ref[g, :nk, :], q, _NT, preferred_element_type=F32)
                s_ref[pl.ds(win0, CMP_WIN), :] += cb_ref[0, hd]
                s = jnp.where(reachable, s_ref[:nk, :], NEG)
                valid = s > 0.5 * NEG
                e = jnp.where(valid, jnp.exp2(s - jnp.max(s, axis=0, keepdims=True)), 0.0)
                p = e * (1.0 / jnp.maximum(jnp.sum(e, axis=0, keepdims=True), TINY))
                o_t = jnp.dot(vct_ref[g, :, :nk], p.astype(BF16), preferred_element_type=F32)
                oc_ref[:, hd * HEAD_DIM:(hd + 1) * HEAD_DIM] = o_t.T
                p_sum = p_sum + p
            imp = jnp.zeros((nb, TQ), F32)
            for part in _split3(p_sum):
                imp = imp + jnp.dot(ovt_ref[:nb, :nk], part, preferred_element_type=F32)
            picked = _topk_bias_t(jnp.where(allowed & ~forced, imp, -1.0), SLC_TOPK - 3)
            selb = jnp.where(allowed, jnp.where(forced, 0.0, picked), NEG)
            if nb < N_SLC:
                selb = jnp.concatenate([selb, jnp.full((N_SLC - nb, TQ), NEG, F32)], axis=0)
            sel_ref[g] = selb.T.astype(sel_ref.dtype)

    for c in range(1, 512 // HEAD_DIM + 1):
        @pl.when(lax.shift_right_logical(i, 3) == c - 1)
        def _():
            body(c * HEAD_DIM)


def _cmp_select(zb, kc, vct, ovt, cmpb):
    w = NSA_HEADS * HEAD_DIM
    return pl.pallas_call(
        _cmp_kernel,
        grid=(N_QT,),
        in_specs=[pl.BlockSpec((TQ, w), lambda i: (i, COL_NQ // NSA_HEADS)),
                  pl.BlockSpec((NSA_KV_HEADS, 512, HEAD_DIM), lambda i: (0, 0, 0)),
                  pl.BlockSpec((NSA_KV_HEADS, HEAD_DIM, 512), lambda i: (0, 0, 0)),
                  pl.BlockSpec((N_SLC, 512), lambda i: (0, 0)),
                  pl.BlockSpec((1, NSA_HEADS, CMP_WIN, TQ), lambda i: (jnp.minimum(i, 1), 1, 0, 0))],
        out_specs=[pl.BlockSpec((TQ, w), lambda i: (i, 0)),
                   pl.BlockSpec((NSA_KV_HEADS, TQ, N_SLC), lambda i: (0, i, 0))],
        out_shape=[jax.ShapeDtypeStruct((SEQ, w), F32),
                   jax.ShapeDtypeStruct((NSA_KV_HEADS, SEQ, N_SLC), BF16)],
        scratch_shapes=[pltpu.VMEM((512, TQ), F32)],
        compiler_params=_cparams("arbitrary"),
        name="cmp_select",
    )(zb, kc, vct, ovt, cmpb)


def _slc_kernel(q_ref, sel_ref, k_ref, v_ref, pat_ref, d0_ref, d1_ref, o_ref, *flash_scratch):
    i = pl.program_id(1)
    hp = HP_SLC
    selb = sel_ref[0].astype(F32)
    qs = [q_ref[:, h * HEAD_DIM:(h + 1) * HEAD_DIM] for h in range(hp)]
    outs = _sparse_flash(qs, [selb] * hp, k_ref, v_ref, [0] * hp, pat_ref,
                         [d0_ref[0, h] for h in range(hp)], [d1_ref[0, h] for h in range(hp)],
                         i, SLC_PER_TILE, flash_scratch)
    for h in range(hp):
        o_ref[:, h * HEAD_DIM:(h + 1) * HEAD_DIM] = outs[h].astype(o_ref.dtype)


def _slc(zb, selb, near, pat):
    hp = HP_SLC
    w = hp * HEAD_DIM
    per_group = NSA_GROUP // hp
    return pl.pallas_call(
        _slc_kernel,
        grid=(NSA_HEADS // hp, N_QT),
        in_specs=[pl.BlockSpec((TQ, w), lambda h, i: (i, COL_NQ // hp + h)),
                  pl.BlockSpec((1, TQ, N_SLC), lambda h, i: (h // per_group, i, 0)),
                  pl.BlockSpec((SEQ, HEAD_DIM), lambda h, i: (0, COL_SK + h // per_group),
                               pipeline_mode=pl.Buffered(1)),
                  pl.BlockSpec((SEQ, HEAD_DIM), lambda h, i: (0, COL_SV + h // per_group),
                               pipeline_mode=pl.Buffered(1)),
                  pl.BlockSpec((SEQ, HEAD_DIM), lambda h, i: (0, 0), pipeline_mode=pl.Buffered(1)),
                  pl.BlockSpec((1, hp, TQ, TQ), lambda h, i: (0, MOBA_HEADS // hp + h, 0, 0)),
                  pl.BlockSpec((1, hp, TQ, TQ), lambda h, i: (1, MOBA_HEADS // hp + h, 0, 0))],
        out_specs=pl.BlockSpec((TQ, w), lambda h, i: (i, h)),
        out_shape=jax.ShapeDtypeStruct((SEQ, NSA_HEADS * HEAD_DIM), F32),
        scratch_shapes=_flash_scratch(hp, False),
        compiler_params=_cparams("parallel", "arbitrary", vmem=BIG_VMEM_LIMIT),
        name="slc",
    )(zb, selb, zb, zb, pat, near, near)


def _win_kernel(q_ref, k_ref, v_ref, d0_ref, d1_ref, w2_ref, o_ref):
    i = pl.program_id(0)
    offs = [pl.multiple_of(jnp.maximum(i - b, 0) * TQ, TQ) for b in range(3)]
    mask2 = jnp.where(i >= 2, w2_ref[...], NEG)
    ones = jnp.ones((3 * TQ, HEAD_DIM), BF16)
    for g in range(NSA_KV_HEADS):
        cols = slice(g * HEAD_DIM, (g + 1) * HEAD_DIM)
        k_all = jnp.concatenate([k_ref[pl.ds(off, TQ), cols] for off in offs], axis=0)
        v_aug = jnp.concatenate([jnp.concatenate([v_ref[pl.ds(off, TQ), cols] for off in offs], axis=0), ones],
                                axis=1)
        for r in range(NSA_GROUP):
            hd = g * NSA_GROUP + r
            q = q_ref[:, hd * HEAD_DIM:(hd + 1) * HEAD_DIM]
            bias = jnp.concatenate([d0_ref[0, hd], jnp.where(i >= 1, d1_ref[0, hd], NEG), mask2], axis=1)
            s = lax.dot_general(q, k_all, _NT, preferred_element_type=F32) + bias
            m = jnp.max(s, axis=1, keepdims=True)
            pv = jnp.dot(jnp.exp2(s - m).astype(BF16), v_aug, preferred_element_type=F32)
            o_ref[:, hd * HEAD_DIM:(hd + 1) * HEAD_DIM] = pv[:, :HEAD_DIM] / pv[:, HEAD_DIM:]


def _win(zb, near, win2):
    w = NSA_HEADS * HEAD_DIM
    kv_w = NSA_KV_HEADS * HEAD_DIM
    return pl.pallas_call(
        _win_kernel,
        grid=(N_QT,),
        in_specs=[pl.BlockSpec((TQ, w), lambda i: (i, COL_NQ // NSA_HEADS)),
                  pl.BlockSpec((SEQ, kv_w), lambda i: (0, COL_WK // NSA_KV_HEADS), pipeline_mode=pl.Buffered(1)),
                  pl.BlockSpec((SEQ, kv_w), lambda i: (0, COL_WV // NSA_KV_HEADS), pipeline_mode=pl.Buffered(1)),
                  pl.BlockSpec((1, NSA_HEADS, TQ, TQ), lambda i: (0, 1, 0, 0), pipeline_mode=pl.Buffered(1)),
                  pl.BlockSpec((1, NSA_HEADS, TQ, TQ), lambda i: (1, 1, 0, 0), pipeline_mode=pl.Buffered(1)),
                  pl.BlockSpec((TQ, TQ), lambda i: (0, 0), pipeline_mode=pl.Buffered(1))],
        out_specs=pl.BlockSpec((TQ, w), lambda i: (i, 0)),
        out_shape=jax.ShapeDtypeStruct((SEQ, w), F32),
        compiler_params=_cparams("arbitrary"),
        name="win",
    )(zb, zb, zb, near, near, win2)


def _layer_norm(y, g, b):
    mu = jnp.mean(y, axis=-1, keepdims=True)
    yc = y - mu
    var = jnp.mean(yc * yc, axis=-1, keepdims=True)
    return yc * lax.rsqrt(var + LN_EPS) * g + b


ROW_CHUNK = 256


def _outproj_kernel(x_ref, om_ref, oc_ref, os_ref, ow_ref, g_ref, e_ref, w_ref, lg_ref, lb_ref,
                    h_ref, hb_ref):
    for c0 in range(0, x_ref.shape[0], ROW_CHUNK):
        rows = slice(c0, c0 + ROW_CHUNK)
        gate = g_ref[rows, :]
        hi = gate.astype(BF16)
        lo = (gate - hi.astype(F32)).astype(BF16)
        g2 = jnp.concatenate([hi, lo], axis=1)
        o_n = jnp.zeros((ROW_CHUNK, oc_ref.shape[1]), F32)
        for c, branch in enumerate((oc_ref, os_ref, ow_ref)):
            o_n = o_n + jnp.dot(g2, e_ref[c], preferred_element_type=F32) * branch[rows, :]
        o = jnp.concatenate([om_ref[rows, :], o_n.astype(BF16)], axis=1)
        a = jnp.dot(o, w_ref[...], preferred_element_type=F32)
        h = _layer_norm(DN_ALPHA * x_ref[rows, :] + a, lg_ref[...], lb_ref[...])
        h_ref[rows, :] = h
        hb_ref[rows, :] = h.astype(BF16)


def _resident(shape):
    return pl.BlockSpec(shape, lambda i: (0,) * len(shape), pipeline_mode=pl.Buffered(1))


def _outproj(x, o_m, o_c, o_s, o_w, gates, expand, w_out, ln_g, ln_b, tm=512):
    half = NSA_HEADS * HEAD_DIM
    row = lambda i: (i, 0)
    return pl.pallas_call(
        _outproj_kernel,
        grid=(SEQ // tm,),
        in_specs=[pl.BlockSpec((tm, D_MODEL), row),
                  pl.BlockSpec((tm, half), row), pl.BlockSpec((tm, half), row),
                  pl.BlockSpec((tm, half), row), pl.BlockSpec((tm, half), row),
                  pl.BlockSpec((tm, HEAD_DIM), row),
                  _resident((3, 2 * HEAD_DIM, half)),
                  _resident((D_MODEL, D_MODEL)),
                  _resident((1, D_MODEL)), _resident((1, D_MODEL))],
        out_specs=[pl.BlockSpec((tm, D_MODEL), row), pl.BlockSpec((tm, D_MODEL), row)],
        out_shape=[jax.ShapeDtypeStruct((SEQ, D_MODEL), F32),
                   jax.ShapeDtypeStruct((SEQ, D_MODEL), BF16)],
        compiler_params=_cparams("parallel", vmem=BIG_VMEM_LIMIT),
        name="outproj_ln",
    )(x, o_m, o_c, o_s, o_w, gates, expand, w_out, ln_g, ln_b)


def _ffn_kernel(hb_ref, w1_ref, w2_ref, f_ref):
    @pl.when(pl.program_id(1) == 0)
    def _():
        f_ref[...] = jnp.zeros(f_ref.shape, F32)

    u = jnp.maximum(jnp.dot(hb_ref[...], w1_ref[...].astype(BF16), preferred_element_type=F32), 0.0)
    f_ref[...] += jnp.dot((u * u).astype(BF16), w2_ref[...].astype(BF16), preferred_element_type=F32)


def _ffn(hb, w1, w2, tm=1024, tf=512):
    return pl.pallas_call(
        _ffn_kernel,
        grid=(SEQ // tm, D_FF // tf),
        in_specs=[pl.BlockSpec((tm, D_MODEL), lambda i, c: (i, 0)),
                  pl.BlockSpec((D_MODEL, tf), lambda i, c: (0, c)),
                  pl.BlockSpec((tf, D_MODEL), lambda i, c: (c, 0))],
        out_specs=pl.BlockSpec((tm, D_MODEL), lambda i, c: (i, 0)),
        out_shape=jax.ShapeDtypeStruct((SEQ, D_MODEL), F32),
        compiler_params=_cparams("parallel", "arbitrary", vmem=BIG_VMEM_LIMIT),
        name="ffn",
    )(hb, w1, w2)


def _final_kernel(h_ref, hb_ref, f_ref, p_ref, wg_ref, wp_ref, lg_ref, lb_ref, o_ref):
    for c0 in range(0, h_ref.shape[0], ROW_CHUNK):
        rows = slice(c0, c0 + ROW_CHUNK)
        gate = jax.nn.sigmoid(jnp.dot(hb_ref[rows, :], wg_ref[...], preferred_element_type=F32))
        emb = jnp.dot(p_ref[rows, :].astype(BF16), wp_ref[...], preferred_element_type=F32)
        y = DN_ALPHA * h_ref[rows, :] + f_ref[rows, :] + gate * emb
        o_ref[rows, :] = _layer_norm(y, lg_ref[...], lb_ref[...])


def _final(h, hb, f, p, wg, wp, ln_g, ln_b, tm=512):
    row = lambda i: (i, 0)
    return pl.pallas_call(
        _final_kernel,
        grid=(SEQ // tm,),
        in_specs=[pl.BlockSpec((tm, D_MODEL), row), pl.BlockSpec((tm, D_MODEL), row),
                  pl.BlockSpec((tm, D_MODEL), row), pl.BlockSpec((tm, D_PLE), row),
                  _resident((D_MODEL, D_MODEL)), _resident((D_PLE, D_MODEL)),
                  _resident((1, D_MODEL)), _resident((1, D_MODEL))],
        out_specs=pl.BlockSpec((tm, D_MODEL), row),
        out_shape=jax.ShapeDtypeStruct((SEQ, D_MODEL), F32),
        compiler_params=_cparams("parallel"),
        name="final_ln",
    )(h, hb, f, p, wg, wp, ln_g, ln_b)


def _overlap_matrix_t():
    j = np.arange(N_SLC)[:, None]
    n = np.arange(512)[None, :]
    ov = ((n * CMP_STRIDE < j * SLC_LEN + SLC_LEN) & (n * CMP_STRIDE + CMP_LEN > j * SLC_LEN) & (n < N_CMP))
    return ov.astype(np.float32)


def _gate_expand():
    e = np.zeros((3, 2 * HEAD_DIM, NSA_HEADS * HEAD_DIM), np.float32)
    for c in range(3):
        for h in range(NSA_HEADS):
            e[c, 3 * h + c, h * HEAD_DIM:(h + 1) * HEAD_DIM] = 1.0
            e[c, HEAD_DIM + 3 * h + c, h * HEAD_DIM:(h + 1) * HEAD_DIM] = 1.0
    return e


def _layer(h, p, w_in, cmp_pe_k, cmp_w1_k, cmp_w2_k, cmp_pe_v, cmp_w1_v, cmp_w2_v, tiles, w_out,
           ln1_g, ln1_b, w_ff1, w_ff2, w_ple, w_ple_gate, ln2_g, ln2_b):
    near, win2, cmpb = tiles
    scale = HEAD_DIM ** -0.5 * LOG2E
    colscale = np.ones((1, GATE_COL0), np.float32)
    colscale[0, COL_MQ * HEAD_DIM:(COL_MQ + MOBA_HEADS) * HEAD_DIM] = scale
    colscale[0, COL_NQ * HEAD_DIM:(COL_NQ + NSA_HEADS) * HEAD_DIM] = scale
    w_in_t = w_in.T
    w_gl_t = jnp.pad(w_in_t[GATE_COL0:], ((0, HEAD_DIM - NSA_HEADS * 3), (0, 0))).astype(BF16)
    zb, gates, ckv = _inproj(h, w_in_t, jnp.asarray(colscale), w_gl_t)

    half = CMP_STRIDE * HEAD_DIM
    xr = ckv.reshape(4, SEQ // CMP_STRIDE, half)
    kc, vct = _compress(xr, cmp_pe_k.reshape(2, 1, half), cmp_w1_k.reshape(2, half, HEAD_DIM), cmp_w2_k,
                        cmp_pe_v.reshape(2, 1, half), cmp_w1_v.reshape(2, half, HEAD_DIM), cmp_w2_v)

    key = np.arange(SEQ)[:, None]
    lane = np.arange(HEAD_DIM)[None, :]
    pat_moba = jnp.asarray(lane == key // MOBA_BLOCK, BF16)
    pat_slc = jnp.asarray(lane == key // SLC_LEN, BF16)
    o_m = _moba(zb, _moba_select(zb), near, pat_moba)
    o_c, selb = _cmp_select(zb, kc, vct, jnp.asarray(_overlap_matrix_t(), BF16), cmpb)
    o_s = _slc(zb, selb, near, pat_slc)
    o_w = _win(zb, near, win2)

    h1, h1b = _outproj(h, o_m, o_c, o_s, o_w, gates, jnp.asarray(_gate_expand(), BF16),
                       w_out.astype(BF16), ln1_g.reshape(1, -1), ln1_b.reshape(1, -1))
    f = _ffn(h1b, w_ff1, w_ff2)
    return _final(h1, h1b, f, p, w_ple_gate.astype(BF16), w_ple.astype(BF16),
                  ln2_g.reshape(1, -1), ln2_b.reshape(1, -1))


def kernel(x, p, w_in, cmp_pe_k, cmp_w1_k, cmp_w2_k, cmp_pe_v, cmp_w1_v, cmp_w2_v, rel_bias, w_out,
           ln1_g, ln1_b, w_ff1, w_ff2, w_ple, w_ple_gate, ln2_g, ln2_b):
    tiles = _bias_tiles(rel_bias)
    h = x[0]
    for i in range(w_in.shape[0]):
        h = _layer(h, p[i, 0], w_in[i], cmp_pe_k[i], cmp_w1_k[i], cmp_w2_k[i], cmp_pe_v[i], cmp_w1_v[i],
                   cmp_w2_v[i], tiles, w_out[i], ln1_g[i], ln1_b[i], w_ff1[i], w_ff2[i], w_ple[i],
                   w_ple_gate[i], ln2_g[i], ln2_b[i])
    return h[None]
```

```python
import functools
import math

import numpy as np
import jax
import jax.numpy as jnp
from jax import lax
from jax.experimental import pallas as pl
from jax.experimental.pallas import tpu as pltpu

D_MODEL = 2048
SEQ = 8192
HEAD_DIM = 128
N_HEADS = 16
MOBA_HEADS = 8
NSA_HEADS = 8
NSA_KV_HEADS = 2
NSA_GROUP = 4
MOBA_BLOCK = 256
MOBA_TOPK = 3
CMP_LEN = 32
CMP_STRIDE = 16
SLC_LEN = 64
SLC_TOPK = 16
WINDOW = 512
N_BUCKETS = 32
MAX_DISTANCE = 128
D_FF = 4 * D_MODEL
D_PLE = 256
LN_EPS = 1e-5
DN_ALPHA = 2.0 ** 0.25
NEG = -1e30
LOG2E = math.log2(math.e)
TINY = 1e-30
FORCE_SCORE = 1e4

N_CMP = (SEQ - CMP_LEN) // CMP_STRIDE + 1
N_SLC = SEQ // SLC_LEN
TQ = 256
N_QT = SEQ // TQ
CMP_WIN = 32
SLC_PER_TILE = TQ // SLC_LEN
HP_MOBA = 4
HP_SLC = 4
KG = 4
GROUP = KG * TQ
COL_MQ, COL_MK, COL_MV, COL_NQ = 0, 8, 16, 24
COL_CK, COL_CV, COL_SK, COL_SV, COL_WK, COL_WV = 32, 34, 36, 38, 40, 42
N_COLBLK = 44
GATE_COL0 = N_COLBLK * HEAD_DIM

VMEM_LIMIT = 48 * 1024 * 1024
BIG_VMEM_LIMIT = 56 * 1024 * 1024

BF16 = jnp.bfloat16
F32 = jnp.float32
_NT = (((1,), (1,)), ((), ()))


def _cparams(*sem, vmem=VMEM_LIMIT):
    return pltpu.CompilerParams(dimension_semantics=sem, vmem_limit_bytes=vmem)


def _bucket_table():
    n = np.arange(1024, dtype=np.int32)
    max_exact = N_BUCKETS // 2
    ratio = np.maximum(n, 1).astype(np.float32) / np.float32(max_exact)
    large = max_exact + (np.log(ratio).astype(np.float32) / np.float32(math.log(MAX_DISTANCE / max_exact))
                         * np.float32(N_BUCKETS - max_exact)).astype(np.int32)
    large = np.minimum(large, N_BUCKETS - 1)
    return np.where(n < max_exact, n, large).astype(np.int32)


def _bias_kernel(tab_ref, near_ref, cmpb_ref, *, steps):
    h = pl.program_id(0)
    far = tab_ref[h, N_BUCKETS - 1]

    def lookup(d):
        val = jnp.full(d.shape, tab_ref[h, 0], F32)
        for start, bucket in steps:
            val = jnp.where(d >= start, tab_ref[h, bucket], val)
        return jnp.where(d >= 0, (val - far) * LOG2E, NEG)

    a = lax.broadcasted_iota(jnp.int32, (TQ, TQ), 0)
    b = lax.broadcasted_iota(jnp.int32, (TQ, TQ), 1)
    near_ref[0, 0] = lookup(a - b)
    near_ref[1, 0] = lookup(a - b + TQ)
    w = lax.broadcasted_iota(jnp.int32, (CMP_WIN, TQ), 0)
    aw = lax.broadcasted_iota(jnp.int32, (CMP_WIN, TQ), 1)
    cmpb_ref[0, 0] = lookup(aw - CMP_STRIDE * w - (CMP_LEN - 1))
    cmpb_ref[1, 0] = lookup(aw - CMP_STRIDE * w + (TQ - CMP_LEN + 1))


def _bias_tiles(rel_bias):
    bucket = _bucket_table()
    steps = tuple((int(n), int(bucket[n])) for n in range(1, bucket.size) if bucket[n] != bucket[n - 1])
    near, cmpb = pl.pallas_call(
        functools.partial(_bias_kernel, steps=steps),
        grid=(N_HEADS,),
        in_specs=[pl.BlockSpec(memory_space=pltpu.SMEM)],
        out_specs=[pl.BlockSpec((2, 1, TQ, TQ), lambda h: (0, h, 0, 0)),
                   pl.BlockSpec((2, 1, CMP_WIN, TQ), lambda h: (0, h, 0, 0))],
        out_shape=[jax.ShapeDtypeStruct((2, N_HEADS, TQ, TQ), F32),
                   jax.ShapeDtypeStruct((2, N_HEADS, CMP_WIN, TQ), F32)],
        compiler_params=_cparams("parallel"),
        name="bias_tiles",
    )(rel_bias.T.astype(F32))
    a = np.arange(TQ)[:, None]
    b = np.arange(TQ)[None, :]
    win2 = jnp.asarray(np.where(a < b, 0.0, NEG).astype(np.float32))
    return near, win2, cmpb


INPROJ_TN = 4 * HEAD_DIM
CKV_TILE = COL_CK * HEAD_DIM // INPROJ_TN


def _inproj_kernel(x_ref, w_ref, cs_ref, wg_ref, o_ref, g_ref, ckv_ref, xb_ref):
    j = pl.program_id(1)

    @pl.when(j == 0)
    def _():
        xb = x_ref[...].astype(BF16)
        xb_ref[...] = xb
        g_ref[...] = jax.nn.sigmoid(lax.dot_general(xb, wg_ref[...], _NT, preferred_element_type=F32))

    acc = lax.dot_general(xb_ref[...], w_ref[...].astype(BF16), _NT, preferred_element_type=F32)
    z = (acc * cs_ref[...]).astype(o_ref.dtype)
    o_ref[...] = z

    @pl.when(j == CKV_TILE)
    def _():
        for c in range(INPROJ_TN // HEAD_DIM):
            ckv_ref[c] = z[:, c * HEAD_DIM:(c + 1) * HEAD_DIM]


def _inproj(x, w_in_t, colscale, w_gate_t, tm=1024):
    m, k = x.shape
    tn = INPROJ_TN
    return pl.pallas_call(
        _inproj_kernel,
        grid=(m // tm, GATE_COL0 // tn),
        in_specs=[pl.BlockSpec((tm, k), lambda i, j: (i, 0)),
                  pl.BlockSpec((tn, k), lambda i, j: (j, 0)),
                  pl.BlockSpec((1, tn), lambda i, j: (0, j)),
                  pl.BlockSpec((HEAD_DIM, k), lambda i, j: (0, 0))],
        out_specs=[pl.BlockSpec((tm, tn), lambda i, j: (i, j)),
                   pl.BlockSpec((tm, HEAD_DIM), lambda i, j: (i, 0)),
                   pl.BlockSpec((tn // HEAD_DIM, tm, HEAD_DIM), lambda i, j: (0, i, 0))],
        out_shape=[jax.ShapeDtypeStruct((m, GATE_COL0), BF16),
                   jax.ShapeDtypeStruct((m, HEAD_DIM), F32),
                   jax.ShapeDtypeStruct((tn // HEAD_DIM, m, HEAD_DIM), BF16)],
        scratch_shapes=[pltpu.VMEM((tm, k), BF16)],
        compiler_params=_cparams("parallel", "arbitrary"),
        name="inproj",
    )(x, w_in_t, colscale, w_gate_t)


def _gelu_tanh(x):
    return 0.5 * x * (1.0 + jnp.tanh(math.sqrt(2.0 / math.pi) * (x + 0.044715 * (x * x * x))))


def _compress_kernel(xk_ref, xv_ref, pek_ref, w1k_ref, w2k_ref, pev_ref, w1v_ref, w2v_ref, kc_ref, vct_ref):
    def compress(x_ref, pe_ref, w1_ref, w2_ref):
        x = x_ref[0].astype(F32)
        lo = jnp.dot((x + pe_ref[0]).astype(BF16), w1_ref[0].astype(BF16), preferred_element_type=F32)
        hi = jnp.dot((x + pe_ref[1]).astype(BF16), w1_ref[1].astype(BF16), preferred_element_type=F32)
        y = lo + pltpu.roll(hi, N_CMP, 0)
        out = jnp.dot(_gelu_tanh(y).astype(BF16), w2_ref[...].astype(BF16), preferred_element_type=F32)
        row = lax.broadcasted_iota(jnp.int32, out.shape, 0)
        return jnp.where(row < N_CMP, out, 0.0)

    kc_ref[0] = compress(xk_ref, pek_ref, w1k_ref, w2k_ref).astype(kc_ref.dtype)
    vct_ref[0] = compress(xv_ref, pev_ref, w1v_ref, w2v_ref).T.astype(vct_ref.dtype)


def _compress(xr, pe_k, w1_k, w2_k, pe_v, w1_v, w2_v):
    half = CMP_STRIDE * HEAD_DIM
    weights = [pl.BlockSpec((2, 1, half), lambda g: (0, 0, 0)),
               pl.BlockSpec((2, half, HEAD_DIM), lambda g: (0, 0, 0)),
               pl.BlockSpec((HEAD_DIM, HEAD_DIM), lambda g: (0, 0))]
    return pl.pallas_call(
        _compress_kernel,
        grid=(NSA_KV_HEADS,),
        in_specs=[pl.BlockSpec((1, 512, half), lambda g: (g, 0, 0)),
                  pl.BlockSpec((1, 512, half), lambda g: (NSA_KV_HEADS + g, 0, 0))] + weights + weights,
        out_specs=[pl.BlockSpec((1, 512, HEAD_DIM), lambda g: (g, 0, 0)),
                   pl.BlockSpec((1, HEAD_DIM, 512), lambda g: (g, 0, 0))],
        out_shape=[jax.ShapeDtypeStruct((NSA_KV_HEADS, 512, HEAD_DIM), BF16),
                   jax.ShapeDtypeStruct((NSA_KV_HEADS, HEAD_DIM, 512), BF16)],
        compiler_params=_cparams("parallel"),
        name="compress",
    )(xr, xr, pe_k, w1_k, w2_k, pe_v, w1_v, w2_v)


def _split3(x):
    hi = x.astype(BF16)
    r1 = x - hi.astype(F32)
    mid = r1.astype(BF16)
    lo = (r1 - mid.astype(F32)).astype(BF16)
    return hi, mid, lo


def _topk_bias_t(score, k):
    row = lax.broadcasted_iota(jnp.int32, score.shape, 0).astype(F32)
    selb = jnp.full(score.shape, NEG, F32)
    s = score
    for _ in range(k):
        mx = jnp.max(s, axis=0, keepdims=True)
        idx = jnp.min(jnp.where(s == mx, row, 1e9), axis=0, keepdims=True)
        hit = row == idx
        selb = jnp.where(hit, 0.0, selb)
        s = jnp.where(hit, -3e38, s)
    return selb


def _lane_tile_max(s):
    m = s[:, :HEAD_DIM]
    for c in range(1, s.shape[1] // HEAD_DIM):
        m = jnp.maximum(m, s[:, c * HEAD_DIM:(c + 1) * HEAD_DIM])
    return m


def _sparse_flash(qs, selbs, k_ref, v_ref, kcols, pat_ref, d0s, d1s, i, sub, scratch):
    t_ref, *maybe_r_ref, mx_ref, acc_ref = scratch
    r_ref = maybe_r_ref[0] if maybe_r_ref else None
    heads = range(len(qs))
    lane = lax.broadcasted_iota(jnp.int32, (TQ, HEAD_DIM), 1)
    far = lane < (i - 1) * sub
    q_near = [jnp.concatenate([qs[h], selbs[h].astype(BF16)], axis=1) for h in heads]
    q_far = [jnp.concatenate([qs[h], jnp.where(far, selbs[h], NEG).astype(BF16)], axis=1) for h in heads]

    def k_aug(h, off, n):
        c = kcols[h]
        return jnp.concatenate([k_ref[pl.ds(off, n), c:c + HEAD_DIM], pat_ref[pl.ds(off, n), :]], axis=1)

    def v_aug(h, off, n):
        c = kcols[h]
        return jnp.concatenate([v_ref[pl.ds(off, n), c:c + HEAD_DIM], jnp.ones((n, HEAD_DIM), BF16)], axis=1)

    for h in heads:
        mx_ref[h] = jnp.full((TQ, HEAD_DIM), NEG, F32)

    def score(g):
        off = pl.multiple_of(g * GROUP, GROUP)
        for h in heads:
            s = lax.dot_general(q_far[h], k_aug(h, off, GROUP), _NT, preferred_element_type=F32)
            ref = _lane_tile_max(s)
            if r_ref is not None:
                s = s - jnp.concatenate([ref] * (GROUP // HEAD_DIM), axis=1)
                r_ref[h, g] = ref
            for u in range(KG):
                t_ref[h, g * KG + u] = s[:, u * TQ:(u + 1) * TQ].astype(t_ref.dtype)
            mx_ref[h] = jnp.maximum(mx_ref[h], ref)

    def run_unrolled(n, body):
        def quad(m, carry):
            body(4 * m, 4)
            return carry

        lax.fori_loop(0, lax.shift_right_logical(n, 2), quad, 0)
        for left in (1, 2, 3):
            @pl.when((n & 3) == left)
            def _(left=left):
                body(n - left, left)

    n_far = lax.shift_right_logical(jnp.maximum(i - 1, 0) + (KG - 1), KG.bit_length() - 1)

    def score_groups(g0, count):
        for k in range(count):
            score(g0 + k)

    run_unrolled(n_far, score_groups)

    off_prev = pl.multiple_of(jnp.maximum(i - 1, 0) * TQ, TQ)
    off_own = pl.multiple_of(i * TQ, TQ)
    p_near = []
    for h in heads:
        k2 = jnp.concatenate([k_aug(h, off_prev, TQ), k_aug(h, off_own, TQ)], axis=0)
        bias = jnp.concatenate([jnp.where(i >= 1, d1s[h], NEG), d0s[h]], axis=1)
        s = lax.dot_general(q_near[h], k2, _NT, preferred_element_type=F32) + bias
        m = jnp.max(jnp.maximum(mx_ref[h], _lane_tile_max(s)), axis=1, keepdims=True)
        mx_ref[h] = jnp.broadcast_to(m, (TQ, HEAD_DIM))
        p_near.append(jnp.exp2(s - m).astype(BF16))
    for h in heads:
        v2 = jnp.concatenate([v_aug(h, off_prev, TQ), v_aug(h, off_own, TQ)], axis=0)
        acc_ref[h] = jnp.dot(p_near[h], v2, preferred_element_type=F32)

    def accumulate(g0, ng):
        off = pl.multiple_of(g0 * GROUP, GROUP)
        for h in heads:
            probs = []
            for g in [g0 + k for k in range(ng)]:
                d = -mx_ref[h] if r_ref is None else r_ref[h, g] - mx_ref[h]
                d2 = jnp.concatenate([d, d], axis=1)
                probs += [jnp.exp2(t_ref[h, g * KG + u].astype(F32) + d2).astype(BF16) for u in range(KG)]
            acc_ref[h] += jnp.dot(jnp.concatenate(probs, axis=1), v_aug(h, off, ng * GROUP),
                                  preferred_element_type=F32)

    def accumulate_groups(g0, count):
        for k in range(0, count, 2):
            accumulate(g0 + k, min(2, count - k))

    run_unrolled(n_far, accumulate_groups)

    outs = []
    for h in heads:
        acc = acc_ref[h]
        outs.append(acc[:, :HEAD_DIM] / acc[:, HEAD_DIM:])
    return outs


def _flash_scratch(hp, relative_bf16):
    logits = [pltpu.VMEM((hp, N_QT, TQ, TQ), BF16),
              pltpu.VMEM((hp, SEQ // GROUP, TQ, HEAD_DIM), F32)]
    if not relative_bf16:
        logits = [pltpu.VMEM((hp, N_QT, TQ, TQ), F32)]
    return logits + [pltpu.VMEM((hp, TQ, HEAD_DIM), F32),
                     pltpu.VMEM((hp, TQ, 2 * HEAD_DIM), F32)]


SEL_CHUNK = 4096


def _moba_select_kernel(q_ref, k_ref, sel_ref):
    n_blk = SEQ // MOBA_BLOCK
    blk = lax.broadcasted_iota(jnp.int32, (n_blk, SEQ), 0)
    pos = lax.broadcasted_iota(jnp.int32, (n_blk, SEQ), 1)
    ind = jnp.where(lax.shift_right_logical(pos, 8) == blk, 1.0 / MOBA_BLOCK, 0.0).astype(BF16)
    kmean = _split3(jnp.dot(ind, k_ref[...], preferred_element_type=F32))
    blk_t = lax.broadcasted_iota(jnp.int32, (n_blk, SEL_CHUNK), 0)
    col = lax.broadcasted_iota(jnp.int32, (n_blk, SEL_CHUNK), 1)
    no_block = jnp.full((HEAD_DIM - n_blk, SEL_CHUNK), NEG, F32)

    def chunk(c, carry):
        off = pl.multiple_of(c * SEL_CHUNK, SEL_CHUNK)
        q = q_ref[pl.ds(off, SEL_CHUNK), :]
        gate = jnp.zeros((n_blk, SEL_CHUNK), F32)
        for part in kmean:
            gate = gate + lax.dot_general(part, q, _NT, preferred_element_type=F32)
        own = lax.shift_right_logical(off + col, 8)
        past = blk_t < own
        selb = _topk_bias_t(jnp.where(past, gate, NEG), MOBA_TOPK)
        selb = jnp.where(past, selb, jnp.where(blk_t == own, 0.0, NEG))
        sel_ref[0, pl.ds(off, SEL_CHUNK), :] = jnp.concatenate([selb, no_block], axis=0).T.astype(sel_ref.dtype)
        return carry

    lax.fori_loop(0, SEQ // SEL_CHUNK, chunk, 0)


def _moba_select(zb):
    return pl.pallas_call(
        _moba_select_kernel,
        grid=(MOBA_HEADS,),
        in_specs=[pl.BlockSpec((SEQ, HEAD_DIM), lambda h: (0, COL_MQ + h)),
                  pl.BlockSpec((SEQ, HEAD_DIM), lambda h: (0, COL_MK + h))],
        out_specs=pl.BlockSpec((1, SEQ, HEAD_DIM), lambda h: (h, 0, 0)),
        out_shape=jax.ShapeDtypeStruct((MOBA_HEADS, SEQ, HEAD_DIM), BF16),
        compiler_params=_cparams("parallel"),
        name="moba_select",
    )(zb, zb)


def _moba_kernel(q_ref, sel_ref, k_ref, v_ref, pat_ref, d0_ref, d1_ref, o_ref, *flash_scratch):
    i = pl.program_id(1)
    hp = HP_MOBA
    qs = [q_ref[:, h * HEAD_DIM:(h + 1) * HEAD_DIM] for h in range(hp)]
    selbs = [sel_ref[h].astype(F32) for h in range(hp)]
    cols = [h * HEAD_DIM for h in range(hp)]
    outs = _sparse_flash(qs, selbs, k_ref, v_ref, cols, pat_ref, [d0_ref[0, h] for h in range(hp)],
                         [d1_ref[0, h] for h in range(hp)], i, 1, flash_scratch)
    for h in range(hp):
        o_ref[:, h * HEAD_DIM:(h + 1) * HEAD_DIM] = outs[h].astype(o_ref.dtype)


def _moba(zb, selb, near, pat):
    hp = HP_MOBA
    w = hp * HEAD_DIM
    return pl.pallas_call(
        _moba_kernel,
        grid=(MOBA_HEADS // hp, N_QT),
        in_specs=[pl.BlockSpec((TQ, w), lambda h, i: (i, COL_MQ // hp + h)),
                  pl.BlockSpec((hp, TQ, HEAD_DIM), lambda h, i: (h, i, 0)),
                  pl.BlockSpec((SEQ, w), lambda h, i: (0, COL_MK // hp + h), pipeline_mode=pl.Buffered(1)),
                  pl.BlockSpec((SEQ, w), lambda h, i: (0, COL_MV // hp + h), pipeline_mode=pl.Buffered(1)),
                  pl.BlockSpec((SEQ, HEAD_DIM), lambda h, i: (0, 0), pipeline_mode=pl.Buffered(1)),
                  pl.BlockSpec((1, hp, TQ, TQ), lambda h, i: (0, h, 0, 0)),
                  pl.BlockSpec((1, hp, TQ, TQ), lambda h, i: (1, h, 0, 0))],
        out_specs=pl.BlockSpec((TQ, w), lambda h, i: (i, h)),
        out_shape=jax.ShapeDtypeStruct((SEQ, MOBA_HEADS * HEAD_DIM), BF16),
        scratch_shapes=_flash_scratch(hp, True),
        compiler_params=_cparams("parallel", "arbitrary", vmem=BIG_VMEM_LIMIT),
        name="moba",
    )(zb, selb, zb, zb, pat, near, near)


def _cmp_kernel(q_ref, kc_ref, vct_ref, ovt_ref, cb_ref, oc_ref, sel_ref, s_ref):
    i = pl.program_id(0)
    win0 = pl.multiple_of(jnp.maximum(i * CMP_STRIDE - CMP_STRIDE, 0), CMP_STRIDE)

    def body(nk):
        nb = nk * CMP_STRIDE // SLC_LEN
        blk = lax.broadcasted_iota(jnp.int32, (nb, TQ), 0)
        cur = i * SLC_PER_TILE + lax.shift_right_logical(lax.broadcasted_iota(jnp.int32, (nb, TQ), 1), 6)
        allowed = blk <= cur
        forced = (blk == 0) | (blk == cur) | (blk == cur - 1)
        reachable = lax.broadcasted_iota(jnp.int32, (nk, TQ), 0) < win0 + CMP_WIN
        for g in range(NSA_KV_HEADS):
            p_sum = jnp.zeros((nk, TQ), F32)
            for r in range(NSA_GROUP):
                hd = g * NSA_GROUP + r
                q = q_ref[:, hd * HEAD_DIM:(hd + 1) * HEAD_DIM]
                s_ref[:nk, :] = lax.dot_general(kc_ref[g, :nk, :], q, _NT, preferred_element_type=F32)
                s_ref[pl.ds(win0, CMP_WIN), :] += cb_ref[0, hd]
                s = jnp.where(reachable, s_ref[:nk, :], NEG)
                mx = jnp.max(s, axis=0, keepdims=True)
                e = jnp.exp2(s - mx)
                seen = mx > 0.5 * NEG
                p = e * jnp.where(seen, 1.0 / jnp.maximum(jnp.sum(e, axis=0, keepdims=True), TINY), 0.0)
                o_t = jnp.dot(vct_ref[g, :, :nk], p.astype(BF16), preferred_element_type=F32)
                oc_ref[:, hd * HEAD_DIM:(hd + 1) * HEAD_DIM] = o_t.T
                p_sum = p_sum + p
            imp = jnp.zeros((nb, TQ), F32)
            for part in _split3(p_sum):
                imp = imp + jnp.dot(ovt_ref[:nb, :nk], part, preferred_element_type=F32)
            picked = _topk_bias_t(jnp.where(allowed & ~forced, imp, -1.0), SLC_TOPK - 3)
            selb = jnp.where(allowed, jnp.where(forced, 0.0, picked), NEG)
            if nb < N_SLC:
                selb = jnp.concatenate([selb, jnp.full((N_SLC - nb, TQ), NEG, F32)], axis=0)
            sel_ref[g] = selb.T.astype(sel_ref.dtype)

    for c in range(1, 512 // HEAD_DIM + 1):
        @pl.when(lax.shift_right_logical(i, 3) == c - 1)
        def _():
            body(c * HEAD_DIM)


def _cmp_select(zb, kc, vct, ovt, cmpb):
    w = NSA_HEADS * HEAD_DIM
    return pl.pallas_call(
        _cmp_kernel,
        grid=(N_QT,),
        in_specs=[pl.BlockSpec((TQ, w), lambda i: (i, COL_NQ // NSA_HEADS)),
                  pl.BlockSpec((NSA_KV_HEADS, 512, HEAD_DIM), lambda i: (0, 0, 0)),
                  pl.BlockSpec((NSA_KV_HEADS, HEAD_DIM, 512), lambda i: (0, 0, 0)),
                  pl.BlockSpec((N_SLC, 512), lambda i: (0, 0)),
                  pl.BlockSpec((1, NSA_HEADS, CMP_WIN, TQ), lambda i: (jnp.minimum(i, 1), 1, 0, 0))],
        out_specs=[pl.BlockSpec((TQ, w), lambda i: (i, 0)),
                   pl.BlockSpec((NSA_KV_HEADS, TQ, N_SLC), lambda i: (0, i, 0))],
        out_shape=[jax.ShapeDtypeStruct((SEQ, w), F32),
                   jax.ShapeDtypeStruct((NSA_KV_HEADS, SEQ, N_SLC), BF16)],
        scratch_shapes=[pltpu.VMEM((512, TQ), F32)],
        compiler_params=_cparams("arbitrary"),
        name="cmp_select",
    )(zb, kc, vct, ovt, cmpb)


def _slc_kernel(q_ref, sel_ref, k_ref, v_ref, pat_ref, d0_ref, d1_ref, o_ref, *flash_scratch):
    i = pl.program_id(1)
    hp = HP_SLC
    selb = sel_ref[0].astype(F32)
    qs = [q_ref[:, h * HEAD_DIM:(h + 1) * HEAD_DIM] for h in range(hp)]
    outs = _sparse_flash(qs, [selb] * hp, k_ref, v_ref, [0] * hp, pat_ref,
                         [d0_ref[0, h] for h in range(hp)], [d1_ref[0, h] for h in range(hp)],
                         i, SLC_PER_TILE, flash_scratch)
    for h in range(hp):
        o_ref[:, h * HEAD_DIM:(h + 1) * HEAD_DIM] = outs[h].astype(o_ref.dtype)


def _slc(zb, selb, near, pat):
    hp = HP_SLC
    w = hp * HEAD_DIM
    per_group = NSA_GROUP // hp
    return pl.pallas_call(
        _slc_kernel,
        grid=(NSA_HEADS // hp, N_QT),
        in_specs=[pl.BlockSpec((TQ, w), lambda h, i: (i, COL_NQ // hp + h)),
                  pl.BlockSpec((1, TQ, N_SLC), lambda h, i: (h // per_group, i, 0)),
                  pl.BlockSpec((SEQ, HEAD_DIM), lambda h, i: (0, COL_SK + h // per_group),
                               pipeline_mode=pl.Buffered(1)),
                  pl.BlockSpec((SEQ, HEAD_DIM), lambda h, i: (0, COL_SV + h // per_group),
                               pipeline_mode=pl.Buffered(1)),
                  pl.BlockSpec((SEQ, HEAD_DIM), lambda h, i: (0, 0), pipeline_mode=pl.Buffered(1)),
                  pl.BlockSpec((1, hp, TQ, TQ), lambda h, i: (0, MOBA_HEADS // hp + h, 0, 0)),
                  pl.BlockSpec((1, hp, TQ, TQ), lambda h, i: (1, MOBA_HEADS // hp + h, 0, 0))],
        out_specs=pl.BlockSpec((TQ, w), lambda h, i: (i, h)),
        out_shape=jax.ShapeDtypeStruct((SEQ, NSA_HEADS * HEAD_DIM), F32),
        scratch_shapes=_flash_scratch(hp, False),
        compiler_params=_cparams("parallel", "arbitrary", vmem=BIG_VMEM_LIMIT),
        name="slc",
    )(zb, selb, zb, zb, pat, near, near)


def _win_kernel(q_ref, k_ref, v_ref, d0_ref, d1_ref, w2_ref, o_ref):
    i = pl.program_id(0)
    offs = [pl.multiple_of(jnp.maximum(i - b, 0) * TQ, TQ) for b in range(3)]
    mask2 = jnp.where(i >= 2, w2_ref[...], NEG)
    ones = jnp.ones((3 * TQ, HEAD_DIM), BF16)
    for g in range(NSA_KV_HEADS):
        cols = slice(g * HEAD_DIM, (g + 1) * HEAD_DIM)
        k_all = jnp.concatenate([k_ref[pl.ds(off, TQ), cols] for off in offs], axis=0)
        v_aug = jnp.concatenate([jnp.concatenate([v_ref[pl.ds(off, TQ), cols] for off in offs], axis=0), ones],
                                axis=1)
        for r in range(NSA_GROUP):
            hd = g * NSA_GROUP + r
            q = q_ref[:, hd * HEAD_DIM:(hd + 1) * HEAD_DIM]
            bias = jnp.concatenate([d0_ref[0, hd], jnp.where(i >= 1, d1_ref[0, hd], NEG), mask2], axis=1)
            s = lax.dot_general(q, k_all, _NT, preferred_element_type=F32) + bias
            m = jnp.max(s, axis=1, keepdims=True)
            pv = jnp.dot(jnp.exp2(s - m).astype(BF16), v_aug, preferred_element_type=F32)
            o_ref[:, hd * HEAD_DIM:(hd + 1) * HEAD_DIM] = pv[:, :HEAD_DIM] / pv[:, HEAD_DIM:]


def _win(zb, near, win2):
    w = NSA_HEADS * HEAD_DIM
    kv_w = NSA_KV_HEADS * HEAD_DIM
    return pl.pallas_call(
        _win_kernel,
        grid=(N_QT,),
        in_specs=[pl.BlockSpec((TQ, w), lambda i: (i, COL_NQ // NSA_HEADS)),
                  pl.BlockSpec((SEQ, kv_w), lambda i: (0, COL_WK // NSA_KV_HEADS), pipeline_mode=pl.Buffered(1)),
                  pl.BlockSpec((SEQ, kv_w), lambda i: (0, COL_WV // NSA_KV_HEADS), pipeline_mode=pl.Buffered(1)),
                  pl.BlockSpec((1, NSA_HEADS, TQ, TQ), lambda i: (0, 1, 0, 0), pipeline_mode=pl.Buffered(1)),
                  pl.BlockSpec((1, NSA_HEADS, TQ, TQ), lambda i: (1, 1, 0, 0), pipeline_mode=pl.Buffered(1)),
                  pl.BlockSpec((TQ, TQ), lambda i: (0, 0), pipeline_mode=pl.Buffered(1))],
        out_specs=pl.BlockSpec((TQ, w), lambda i: (i, 0)),
        out_shape=jax.ShapeDtypeStruct((SEQ, w), F32),
        compiler_params=_cparams("arbitrary"),
        name="win",
    )(zb, zb, zb, near, near, win2)


def _layer_norm(y, g, b):
    mu = jnp.mean(y, axis=-1, keepdims=True)
    yc = y - mu
    var = jnp.mean(yc * yc, axis=-1, keepdims=True)
    return yc * lax.rsqrt(var + LN_EPS) * g + b


ROW_CHUNK = 256


def _outproj_kernel(x_ref, om_ref, oc_ref, os_ref, ow_ref, g_ref, e_ref, w_ref, lg_ref, lb_ref,
                    h_ref, hb_ref):
    for c0 in range(0, x_ref.shape[0], ROW_CHUNK):
        rows = slice(c0, c0 + ROW_CHUNK)
        gate = g_ref[rows, :]
        hi = gate.astype(BF16)
        lo = (gate - hi.astype(F32)).astype(BF16)
        g2 = jnp.concatenate([hi, lo], axis=1)
        o_n = jnp.zeros((ROW_CHUNK, oc_ref.shape[1]), F32)
        for c, branch in enumerate((oc_ref, os_ref, ow_ref)):
            o_n = o_n + jnp.dot(g2, e_ref[c], preferred_element_type=F32) * branch[rows, :]
        o = jnp.concatenate([om_ref[rows, :], o_n.astype(BF16)], axis=1)
        a = jnp.dot(o, w_ref[...], preferred_element_type=F32)
        h = _layer_norm(DN_ALPHA * x_ref[rows, :] + a, lg_ref[...], lb_ref[...])
        h_ref[rows, :] = h
        hb_ref[rows, :] = h.astype(BF16)


def _resident(shape):
    return pl.BlockSpec(shape, lambda i: (0,) * len(shape), pipeline_mode=pl.Buffered(1))


def _outproj(x, o_m, o_c, o_s, o_w, gates, expand, w_out, ln_g, ln_b, tm=512):
    half = NSA_HEADS * HEAD_DIM
    row = lambda i: (i, 0)
    return pl.pallas_call(
        _outproj_kernel,
        grid=(SEQ // tm,),
        in_specs=[pl.BlockSpec((tm, D_MODEL), row),
                  pl.BlockSpec((tm, half), row), pl.BlockSpec((tm, half), row),
                  pl.BlockSpec((tm, half), row), pl.BlockSpec((tm, half), row),
                  pl.BlockSpec((tm, HEAD_DIM), row),
                  _resident((3, 2 * HEAD_DIM, half)),
                  _resident((D_MODEL, D_MODEL)),
                  _resident((1, D_MODEL)), _resident((1, D_MODEL))],
        out_specs=[pl.BlockSpec((tm, D_MODEL), row), pl.BlockSpec((tm, D_MODEL), row)],
        out_shape=[jax.ShapeDtypeStruct((SEQ, D_MODEL), F32),
                   jax.ShapeDtypeStruct((SEQ, D_MODEL), BF16)],
        compiler_params=_cparams("parallel", vmem=BIG_VMEM_LIMIT),
        name="outproj_ln",
    )(x, o_m, o_c, o_s, o_w, gates, expand, w_out, ln_g, ln_b)


def _ffn_kernel(hb_ref, w1_ref, w2_ref, f_ref):
    @pl.when(pl.program_id(1) == 0)
    def _():
        f_ref[...] = jnp.zeros(f_ref.shape, F32)

    u = jnp.maximum(jnp.dot(hb_ref[...], w1_ref[...].astype(BF16), preferred_element_type=F32), 0.0)
    f_ref[...] += jnp.dot((u * u).astype(BF16), w2_ref[...].astype(BF16), preferred_element_type=F32)


def _ffn(hb, w1, w2, tm=1024, tf=512):
    return pl.pallas_call(
        _ffn_kernel,
        grid=(SEQ // tm, D_FF // tf),
        in_specs=[pl.BlockSpec((tm, D_MODEL), lambda i, c: (i, 0)),
                  pl.BlockSpec((D_MODEL, tf), lambda i, c: (0, c)),
                  pl.BlockSpec((tf, D_MODEL), lambda i, c: (c, 0))],
        out_specs=pl.BlockSpec((tm, D_MODEL), lambda i, c: (i, 0)),
        out_shape=jax.ShapeDtypeStruct((SEQ, D_MODEL), F32),
        compiler_params=_cparams("parallel", "arbitrary", vmem=BIG_VMEM_LIMIT),
        name="ffn",
    )(hb, w1, w2)


def _final_kernel(h_ref, hb_ref, f_ref, p_ref, wg_ref, wp_ref, lg_ref, lb_ref, o_ref):
    for c0 in range(0, h_ref.shape[0], ROW_CHUNK):
        rows = slice(c0, c0 + ROW_CHUNK)
        gate = jax.nn.sigmoid(jnp.dot(hb_ref[rows, :], wg_ref[...], preferred_element_type=F32))
        emb = jnp.dot(p_ref[rows, :].astype(BF16), wp_ref[...], preferred_element_type=F32)
        y = DN_ALPHA * h_ref[rows, :] + f_ref[rows, :] + gate * emb
        o_ref[rows, :] = _layer_norm(y, lg_ref[...], lb_ref[...])


def _final(h, hb, f, p, wg, wp, ln_g, ln_b, tm=512):
    row = lambda i: (i, 0)
    return pl.pallas_call(
        _final_kernel,
        grid=(SEQ // tm,),
        in_specs=[pl.BlockSpec((tm, D_MODEL), row), pl.BlockSpec((tm, D_MODEL), row),
                  pl.BlockSpec((tm, D_MODEL), row), pl.BlockSpec((tm, D_PLE), row),
                  _resident((D_MODEL, D_MODEL)), _resident((D_PLE, D_MODEL)),
                  _resident((1, D_MODEL)), _resident((1, D_MODEL))],
        out_specs=pl.BlockSpec((tm, D_MODEL), row),
        out_shape=jax.ShapeDtypeStruct((SEQ, D_MODEL), F32),
        compiler_params=_cparams("parallel"),
        name="final_ln",
    )(h, hb, f, p, wg, wp, ln_g, ln_b)


def _overlap_matrix_t():
    j = np.arange(N_SLC)[:, None]
    n = np.arange(512)[None, :]
    ov = ((n * CMP_STRIDE < j * SLC_LEN + SLC_LEN) & (n * CMP_STRIDE + CMP_LEN > j * SLC_LEN) & (n < N_CMP))
    return ov.astype(np.float32)


def _gate_expand():
    e = np.zeros((3, 2 * HEAD_DIM, NSA_HEADS * HEAD_DIM), np.float32)
    for c in range(3):
        for h in range(NSA_HEADS):
            e[c, 3 * h + c, h * HEAD_DIM:(h + 1) * HEAD_DIM] = 1.0
            e[c, HEAD_DIM + 3 * h + c, h * HEAD_DIM:(h + 1) * HEAD_DIM] = 1.0
    return e


def _layer(h, p, w_in, cmp_pe_k, cmp_w1_k, cmp_w2_k, cmp_pe_v, cmp_w1_v, cmp_w2_v, tiles, w_out,
           ln1_g, ln1_b, w_ff1, w_ff2, w_ple, w_ple_gate, ln2_g, ln2_b):
    near, win2, cmpb = tiles
    scale = HEAD_DIM ** -0.5 * LOG2E
    colscale = np.ones((1, GATE_COL0), np.float32)
    colscale[0, COL_MQ * HEAD_DIM:(COL_MQ + MOBA_HEADS) * HEAD_DIM] = scale
    colscale[0, COL_NQ * HEAD_DIM:(COL_NQ + NSA_HEADS) * HEAD_DIM] = scale
    w_in_t = w_in.T
    w_gl_t = jnp.pad(w_in_t[GATE_COL0:], ((0, HEAD_DIM - NSA_HEADS * 3), (0, 0))).astype(BF16)
    zb, gates, ckv = _inproj(h, w_in_t, jnp.asarray(colscale), w_gl_t)

    half = CMP_STRIDE * HEAD_DIM
    xr = ckv.reshape(4, SEQ // CMP_STRIDE, half)
    kc, vct = _compress(xr, cmp_pe_k.reshape(2, 1, half), cmp_w1_k.reshape(2, half, HEAD_DIM), cmp_w2_k,
                        cmp_pe_v.reshape(2, 1, half), cmp_w1_v.reshape(2, half, HEAD_DIM), cmp_w2_v)

    key = np.arange(SEQ)[:, None]
    lane = np.arange(HEAD_DIM)[None, :]
    pat_moba = jnp.asarray(lane == key // MOBA_BLOCK, BF16)
    pat_slc = jnp.asarray(lane == key // SLC_LEN, BF16)
    o_m = _moba(zb, _moba_select(zb), near, pat_moba)
    o_c, selb = _cmp_select(zb, kc, vct, jnp.asarray(_overlap_matrix_t(), BF16), cmpb)
    o_s = _slc(zb, selb, near, pat_slc)
    o_w = _win(zb, near, win2)

    h1, h1b = _outproj(h, o_m, o_c, o_s, o_w, gates, jnp.asarray(_gate_expand(), BF16),
                       w_out.astype(BF16), ln1_g.reshape(1, -1), ln1_b.reshape(1, -1))
    f = _ffn(h1b, w_ff1, w_ff2)
    return _final(h1, h1b, f, p, w_ple_gate.astype(BF16), w_ple.astype(BF16),
                  ln2_g.reshape(1, -1), ln2_b.reshape(1, -1))


def kernel(x, p, w_in, cmp_pe_k, cmp_w1_k, cmp_w2_k, cmp_pe_v, cmp_w1_v, cmp_w2_v, rel_bias, w_out,
           ln1_g, ln1_b, w_ff1, w_ff2, w_ple, w_ple_gate, ln2_g, ln2_b):
    tiles = _bias_tiles(rel_bias)
    h = x[0]
    for i in range(w_in.shape[0]):
        h = _layer(h, p[i, 0], w_in[i], cmp_pe_k[i], cmp_w1_k[i], cmp_w2_k[i], cmp_pe_v[i], cmp_w1_v[i],
                   cmp_w2_v[i], tiles, w_out[i], ln1_g[i], ln1_b[i], w_ff1[i], w_ff2[i], w_ple[i],
                   w_ple_gate[i], ln2_g[i], ln2_b[i])
    return h[None]
```

```python
import functools
import math

import numpy as np
import jax
import jax.numpy as jnp
from jax import lax
from jax.experimental import pallas as pl
from jax.experimental.pallas import tpu as pltpu

D_MODEL = 2048
SEQ = 8192
HEAD_DIM = 128
N_HEADS = 16
MOBA_HEADS = 8
NSA_HEADS = 8
NSA_KV_HEADS = 2
NSA_GROUP = 4
MOBA_BLOCK = 256
MOBA_TOPK = 3
CMP_LEN = 32
CMP_STRIDE = 16
SLC_LEN = 64
SLC_TOPK = 16
WINDOW = 512
N_BUCKETS = 32
MAX_DISTANCE = 128
D_FF = 4 * D_MODEL
D_PLE = 256
LN_EPS = 1e-5
DN_ALPHA = 2.0 ** 0.25
NEG = -1e30
LOG2E = math.log2(math.e)
TINY = 1e-30
FORCE_SCORE = 1e4

N_CMP = (SEQ - CMP_LEN) // CMP_STRIDE + 1
N_SLC = SEQ // SLC_LEN
TQ = 256
N_QT = SEQ // TQ
CMP_WIN = 32
SLC_PER_TILE = TQ // SLC_LEN
HP_MOBA = 4
HP_SLC = 4
KG = 4
GROUP = KG * TQ
COL_MQ, COL_MK, COL_MV, COL_NQ = 0, 8, 16, 24
COL_CK, COL_CV, COL_SK, COL_SV, COL_WK, COL_WV = 32, 34, 36, 38, 40, 42
N_COLBLK = 44
GATE_COL0 = N_COLBLK * HEAD_DIM

VMEM_LIMIT = 48 * 1024 * 1024
BIG_VMEM_LIMIT = 56 * 1024 * 1024

BF16 = jnp.bfloat16
F32 = jnp.float32
_NT = (((1,), (1,)), ((), ()))


def _cparams(*sem, vmem=VMEM_LIMIT):
    return pltpu.CompilerParams(dimension_semantics=sem, vmem_limit_bytes=vmem)


def _bucket_table():
    n = np.arange(1024, dtype=np.int32)
    max_exact = N_BUCKETS // 2
    ratio = np.maximum(n, 1).astype(np.float32) / np.float32(max_exact)
    large = max_exact + (np.log(ratio).astype(np.float32) / np.float32(math.log(MAX_DISTANCE / max_exact))
                         * np.float32(N_BUCKETS - max_exact)).astype(np.int32)
    large = np.minimum(large, N_BUCKETS - 1)
    return np.where(n < max_exact, n, large).astype(np.int32)


def _bias_kernel(tab_ref, near_ref, cmpb_ref, *, steps):
    h = pl.program_id(0)
    far = tab_ref[h, N_BUCKETS - 1]

    def lookup(d):
        val = jnp.full(d.shape, tab_ref[h, 0], F32)
        for start, bucket in steps:
            val = jnp.where(d >= start, tab_ref[h, bucket], val)
        return jnp.where(d >= 0, (val - far) * LOG2E, NEG)

    a = lax.broadcasted_iota(jnp.int32, (TQ, TQ), 0)
    b = lax.broadcasted_iota(jnp.int32, (TQ, TQ), 1)
    near_ref[0, 0] = lookup(a - b)
    near_ref[1, 0] = lookup(a - b + TQ)
    w = lax.broadcasted_iota(jnp.int32, (CMP_WIN, TQ), 0)
    aw = lax.broadcasted_iota(jnp.int32, (CMP_WIN, TQ), 1)
    cmpb_ref[0, 0] = lookup(aw - CMP_STRIDE * w - (CMP_LEN - 1))
    cmpb_ref[1, 0] = lookup(aw - CMP_STRIDE * w + (TQ - CMP_LEN + 1))


def _bias_tiles(rel_bias):
    bucket = _bucket_table()
    steps = tuple((int(n), int(bucket[n])) for n in range(1, bucket.size) if bucket[n] != bucket[n - 1])
    far_start = steps[-1][0]
    assert far_start <= TQ + 1
    assert far_start <= TQ + CMP_STRIDE - (CMP_LEN - 1)
    assert WINDOW == 2 * TQ
    near, cmpb = pl.pallas_call(
        functools.partial(_bias_kernel, steps=steps),
        grid=(N_HEADS,),
        in_specs=[pl.BlockSpec(memory_space=pltpu.SMEM)],
        out_specs=[pl.BlockSpec((2, 1, TQ, TQ), lambda h: (0, h, 0, 0)),
                   pl.BlockSpec((2, 1, CMP_WIN, TQ), lambda h: (0, h, 0, 0))],
        out_shape=[jax.ShapeDtypeStruct((2, N_HEADS, TQ, TQ), F32),
                   jax.ShapeDtypeStruct((2, N_HEADS, CMP_WIN, TQ), F32)],
        compiler_params=_cparams("parallel"),
        name="bias_tiles",
    )(rel_bias.T.astype(F32))
    a = np.arange(TQ)[:, None]
    b = np.arange(TQ)[None, :]
    win2 = jnp.asarray(np.where(a < b, 0.0, NEG).astype(np.float32))
    return near, win2, cmpb


INPROJ_TN = 4 * HEAD_DIM
CKV_TILE = COL_CK * HEAD_DIM // INPROJ_TN


def _inproj_kernel(x_ref, w_ref, cs_ref, wg_ref, o_ref, g_ref, ckv_ref, xb_ref):
    j = pl.program_id(1)

    @pl.when(j == 0)
    def _():
        xb = x_ref[...].astype(BF16)
        xb_ref[...] = xb
        g_ref[...] = jax.nn.sigmoid(lax.dot_general(xb, wg_ref[...], _NT, preferred_element_type=F32))

    acc = lax.dot_general(xb_ref[...], w_ref[...].astype(BF16), _NT, preferred_element_type=F32)
    z = (acc * cs_ref[...]).astype(o_ref.dtype)
    o_ref[...] = z

    @pl.when(j == CKV_TILE)
    def _():
        for c in range(INPROJ_TN // HEAD_DIM):
            ckv_ref[c] = z[:, c * HEAD_DIM:(c + 1) * HEAD_DIM]


def _inproj(x, w_in_t, colscale, w_gate_t, tm=1024):
    m, k = x.shape
    tn = INPROJ_TN
    return pl.pallas_call(
        _inproj_kernel,
        grid=(m // tm, GATE_COL0 // tn),
        in_specs=[pl.BlockSpec((tm, k), lambda i, j: (i, 0)),
                  pl.BlockSpec((tn, k), lambda i, j: (j, 0)),
                  pl.BlockSpec((1, tn), lambda i, j: (0, j)),
                  pl.BlockSpec((HEAD_DIM, k), lambda i, j: (0, 0))],
        out_specs=[pl.BlockSpec((tm, tn), lambda i, j: (i, j)),
                   pl.BlockSpec((tm, HEAD_DIM), lambda i, j: (i, 0)),
                   pl.BlockSpec((tn // HEAD_DIM, tm, HEAD_DIM), lambda i, j: (0, i, 0))],
        out_shape=[jax.ShapeDtypeStruct((m, GATE_COL0), BF16),
                   jax.ShapeDtypeStruct((m, HEAD_DIM), F32),
                   jax.ShapeDtypeStruct((tn // HEAD_DIM, m, HEAD_DIM), BF16)],
        scratch_shapes=[pltpu.VMEM((tm, k), BF16)],
        compiler_params=_cparams("parallel", "arbitrary"),
        name="inproj",
    )(x, w_in_t, colscale, w_gate_t)


def _gelu_tanh(x):
    return 0.5 * x * (1.0 + jnp.tanh(math.sqrt(2.0 / math.pi) * (x + 0.044715 * (x * x * x))))


def _compress_kernel(xk_ref, xv_ref, pek_ref, w1k_ref, w2k_ref, pev_ref, w1v_ref, w2v_ref, kc_ref, vct_ref):
    def compress(x_ref, pe_ref, w1_ref, w2_ref):
        x = x_ref[0].astype(F32)
        lo = jnp.dot((x + pe_ref[0]).astype(BF16), w1_ref[0].astype(BF16), preferred_element_type=F32)
        hi = jnp.dot((x + pe_ref[1]).astype(BF16), w1_ref[1].astype(BF16), preferred_element_type=F32)
        y = lo + pltpu.roll(hi, N_CMP, 0)
        out = jnp.dot(_gelu_tanh(y).astype(BF16), w2_ref[...].astype(BF16), preferred_element_type=F32)
        row = lax.broadcasted_iota(jnp.int32, out.shape, 0)
        return jnp.where(row < N_CMP, out, 0.0)

    kc_ref[0] = compress(xk_ref, pek_ref, w1k_ref, w2k_ref).astype(kc_ref.dtype)
    vct_ref[0] = compress(xv_ref, pev_ref, w1v_ref, w2v_ref).T.astype(vct_ref.dtype)


def _compress(xr, pe_k, w1_k, w2_k, pe_v, w1_v, w2_v):
    half = CMP_STRIDE * HEAD_DIM
    weights = [pl.BlockSpec((2, 1, half), lambda g: (0, 0, 0)),
               pl.BlockSpec((2, half, HEAD_DIM), lambda g: (0, 0, 0)),
               pl.BlockSpec((HEAD_DIM, HEAD_DIM), lambda g: (0, 0))]
    return pl.pallas_call(
        _compress_kernel,
        grid=(NSA_KV_HEADS,),
        in_specs=[pl.BlockSpec((1, 512, half), lambda g: (g, 0, 0)),
                  pl.BlockSpec((1, 512, half), lambda g: (NSA_KV_HEADS + g, 0, 0))] + weights + weights,
        out_specs=[pl.BlockSpec((1, 512, HEAD_DIM), lambda g: (g, 0, 0)),
                   pl.BlockSpec((1, HEAD_DIM, 512), lambda g: (g, 0, 0))],
        out_shape=[jax.ShapeDtypeStruct((NSA_KV_HEADS, 512, HEAD_DIM), BF16),
                   jax.ShapeDtypeStruct((NSA_KV_HEADS, HEAD_DIM, 512), BF16)],
        compiler_params=_cparams("parallel"),
        name="compress",
    )(xr, xr, pe_k, w1_k, w2_k, pe_v, w1_v, w2_v)


def _split3(x):
    hi = x.astype(BF16)
    r1 = x - hi.astype(F32)
    mid = r1.astype(BF16)
    lo = (r1 - mid.astype(F32)).astype(BF16)
    return hi, mid, lo


def _topk_bias_t(score, k):
    row = lax.broadcasted_iota(jnp.int32, score.shape, 0).astype(F32)
    selb = jnp.full(score.shape, NEG, F32)
    s = score
    for _ in range(k):
        mx = jnp.max(s, axis=0, keepdims=True)
        idx = jnp.min(jnp.where(s == mx, row, 1e9), axis=0, keepdims=True)
        hit = row == idx
        selb = jnp.where(hit, 0.0, selb)
        s = jnp.where(hit, -3e38, s)
    return selb


def _lane_tile_max(s):
    m = s[:, :HEAD_DIM]
    for c in range(1, s.shape[1] // HEAD_DIM):
        m = jnp.maximum(m, s[:, c * HEAD_DIM:(c + 1) * HEAD_DIM])
    return m


def _sparse_flash(qs, selbs, k_ref, v_ref, kcols, pat_ref, d0s, d1s, i, sub, scratch):
    t_ref, *maybe_r_ref, mx_ref, acc_ref = scratch
    r_ref = maybe_r_ref[0] if maybe_r_ref else None
    heads = range(len(qs))
    lane = lax.broadcasted_iota(jnp.int32, (TQ, HEAD_DIM), 1)
    far = lane < (i - 1) * sub
    q_near = [jnp.concatenate([qs[h], selbs[h].astype(BF16)], axis=1) for h in heads]
    q_far = [jnp.concatenate([qs[h], jnp.where(far, selbs[h], NEG).astype(BF16)], axis=1) for h in heads]

    def k_aug(h, off, n):
        c = kcols[h]
        return jnp.concatenate([k_ref[pl.ds(off, n), c:c + HEAD_DIM], pat_ref[pl.ds(off, n), :]], axis=1)

    def v_aug(h, off, n):
        c = kcols[h]
        return jnp.concatenate([v_ref[pl.ds(off, n), c:c + HEAD_DIM], jnp.ones((n, HEAD_DIM), BF16)], axis=1)

    for h in heads:
        mx_ref[h] = jnp.full((TQ, HEAD_DIM), NEG, F32)

    def score(g):
        off = pl.multiple_of(g * GROUP, GROUP)
        for h in heads:
            s = lax.dot_general(q_far[h], k_aug(h, off, GROUP), _NT, preferred_element_type=F32)
            ref = _lane_tile_max(s)
            if r_ref is not None:
                s = s - jnp.concatenate([ref] * (GROUP // HEAD_DIM), axis=1)
                r_ref[h, g] = ref
            for u in range(KG):
                t_ref[h, g * KG + u] = s[:, u * TQ:(u + 1) * TQ].astype(t_ref.dtype)
            mx_ref[h] = jnp.maximum(mx_ref[h], ref)

    def run_unrolled(n, body):
        def quad(m, carry):
            body(4 * m, 4)
            return carry

        lax.fori_loop(0, lax.shift_right_logical(n, 2), quad, 0)
        for left in (1, 2, 3):
            @pl.when((n & 3) == left)
            def _(left=left):
                body(n - left, left)

    n_far = lax.shift_right_logical(jnp.maximum(i - 1, 0) + (KG - 1), KG.bit_length() - 1)

    def score_groups(g0, count):
        for k in range(count):
            score(g0 + k)

    run_unrolled(n_far, score_groups)

    off_prev = pl.multiple_of(jnp.maximum(i - 1, 0) * TQ, TQ)
    off_own = pl.multiple_of(i * TQ, TQ)
    p_near = []
    for h in heads:
        k2 = jnp.concatenate([k_aug(h, off_prev, TQ), k_aug(h, off_own, TQ)], axis=0)
        bias = jnp.concatenate([jnp.where(i >= 1, d1s[h], NEG), d0s[h]], axis=1)
        s = lax.dot_general(q_near[h], k2, _NT, preferred_element_type=F32) + bias
        m = jnp.max(jnp.maximum(mx_ref[h], _lane_tile_max(s)), axis=1, keepdims=True)
        mx_ref[h] = jnp.broadcast_to(m, (TQ, HEAD_DIM))
        p_near.append(jnp.exp2(s - m).astype(BF16))
    for h in heads:
        v2 = jnp.concatenate([v_aug(h, off_prev, TQ), v_aug(h, off_own, TQ)], axis=0)
        acc_ref[h] = jnp.dot(p_near[h], v2, preferred_element_type=F32)

    def accumulate(g0, ng):
        off = pl.multiple_of(g0 * GROUP, GROUP)
        for h in heads:
            probs = []
            for g in [g0 + k for k in range(ng)]:
                d = -mx_ref[h] if r_ref is None else r_ref[h, g] - mx_ref[h]
                d2 = jnp.concatenate([d, d], axis=1)
                probs += [jnp.exp2(t_ref[h, g * KG + u].astype(F32) + d2).astype(BF16) for u in range(KG)]
            acc_ref[h] += jnp.dot(jnp.concatenate(probs, axis=1), v_aug(h, off, ng * GROUP),
                                  preferred_element_type=F32)

    def accumulate_groups(g0, count):
        for k in range(0, count, 2):
            accumulate(g0 + k, min(2, count - k))

    run_unrolled(n_far, accumulate_groups)

    outs = []
    for h in heads:
        acc = acc_ref[h]
        outs.append(acc[:, :HEAD_DIM] / acc[:, HEAD_DIM:])
    return outs


def _flash_scratch(hp, relative_bf16):
    logits = [pltpu.VMEM((hp, N_QT, TQ, TQ), BF16),
              pltpu.VMEM((hp, SEQ // GROUP, TQ, HEAD_DIM), F32)]
    if not relative_bf16:
        logits = [pltpu.VMEM((hp, N_QT, TQ, TQ), F32)]
    return logits + [pltpu.VMEM((hp, TQ, HEAD_DIM), F32),
                     pltpu.VMEM((hp, TQ, 2 * HEAD_DIM), F32)]


SEL_CHUNK = 4096


HP_SELECT = 2


def _moba_select_kernel(q_ref, k_ref, sel_ref):
    n_blk = SEQ // MOBA_BLOCK
    blk = lax.broadcasted_iota(jnp.int32, (n_blk, SEQ), 0)
    pos = lax.broadcasted_iota(jnp.int32, (n_blk, SEQ), 1)
    ind = jnp.where(lax.shift_right_logical(pos, 8) == blk, 1.0 / MOBA_BLOCK, 0.0).astype(BF16)
    heads = range(HP_SELECT)
    kmean = [_split3(jnp.dot(ind, k_ref[:, h * HEAD_DIM:(h + 1) * HEAD_DIM], preferred_element_type=F32))
             for h in heads]
    blk_t = lax.broadcasted_iota(jnp.int32, (n_blk, SEL_CHUNK), 0)
    col = lax.broadcasted_iota(jnp.int32, (n_blk, SEL_CHUNK), 1)
    no_block = jnp.full((HEAD_DIM - n_blk, SEL_CHUNK), NEG, F32)

    def chunk(c, carry):
        off = pl.multiple_of(c * SEL_CHUNK, SEL_CHUNK)
        own = lax.shift_right_logical(off + col, 8)
        past = blk_t < own
        for h in heads:
            q = q_ref[pl.ds(off, SEL_CHUNK), h * HEAD_DIM:(h + 1) * HEAD_DIM]
            gate = jnp.zeros((n_blk, SEL_CHUNK), F32)
            for part in kmean[h]:
                gate = gate + lax.dot_general(part, q, _NT, preferred_element_type=F32)
            selb = _topk_bias_t(jnp.where(past, gate, NEG), MOBA_TOPK)
            selb = jnp.where(past, selb, jnp.where(blk_t == own, 0.0, NEG))
            sel_ref[h, pl.ds(off, SEL_CHUNK), :] = jnp.concatenate([selb, no_block], axis=0).T.astype(sel_ref.dtype)
        return carry

    lax.fori_loop(0, SEQ // SEL_CHUNK, chunk, 0)


def _moba_select(zb):
    w = HP_SELECT * HEAD_DIM
    return pl.pallas_call(
        _moba_select_kernel,
        grid=(MOBA_HEADS // HP_SELECT,),
        in_specs=[pl.BlockSpec((SEQ, w), lambda h: (0, COL_MQ // HP_SELECT + h)),
                  pl.BlockSpec((SEQ, w), lambda h: (0, COL_MK // HP_SELECT + h))],
        out_specs=pl.BlockSpec((HP_SELECT, SEQ, HEAD_DIM), lambda h: (h, 0, 0)),
        out_shape=jax.ShapeDtypeStruct((MOBA_HEADS, SEQ, HEAD_DIM), BF16),
        compiler_params=_cparams("parallel"),
        name="moba_select",
    )(zb, zb)


def _moba_kernel(q_ref, sel_ref, k_ref, v_ref, pat_ref, d0_ref, d1_ref, o_ref, *flash_scratch):
    i = pl.program_id(1)
    hp = HP_MOBA
    qs = [q_ref[:, h * HEAD_DIM:(h + 1) * HEAD_DIM] for h in range(hp)]
    selbs = [sel_ref[h].astype(F32) for h in range(hp)]
    cols = [h * HEAD_DIM for h in range(hp)]
    outs = _sparse_flash(qs, selbs, k_ref, v_ref, cols, pat_ref, [d0_ref[0, h] for h in range(hp)],
                         [d1_ref[0, h] for h in range(hp)], i, 1, flash_scratch)
    for h in range(hp):
        o_ref[:, h * HEAD_DIM:(h + 1) * HEAD_DIM] = outs[h].astype(o_ref.dtype)


def _moba(zb, selb, near, pat):
    hp = HP_MOBA
    w = hp * HEAD_DIM
    return pl.pallas_call(
        _moba_kernel,
        grid=(MOBA_HEADS // hp, N_QT),
        in_specs=[pl.BlockSpec((TQ, w), lambda h, i: (i, COL_MQ // hp + h)),
                  pl.BlockSpec((hp, TQ, HEAD_DIM), lambda h, i: (h, i, 0)),
                  pl.BlockSpec((SEQ, w), lambda h, i: (0, COL_MK // hp + h), pipeline_mode=pl.Buffered(1)),
                  pl.BlockSpec((SEQ, w), lambda h, i: (0, COL_MV // hp + h), pipeline_mode=pl.Buffered(1)),
                  pl.BlockSpec((SEQ, HEAD_DIM), lambda h, i: (0, 0), pipeline_mode=pl.Buffered(1)),
                  pl.BlockSpec((1, hp, TQ, TQ), lambda h, i: (0, h, 0, 0)),
                  pl.BlockSpec((1, hp, TQ, TQ), lambda h, i: (1, h, 0, 0))],
        out_specs=pl.BlockSpec((TQ, w), lambda h, i: (i, h)),
        out_shape=jax.ShapeDtypeStruct((SEQ, MOBA_HEADS * HEAD_DIM), BF16),
        scratch_shapes=_flash_scratch(hp, True),
        compiler_params=_cparams("parallel", "arbitrary", vmem=BIG_VMEM_LIMIT),
        name="moba",
    )(zb, selb, zb, zb, pat, near, near)


def _cmp_kernel(q_ref, kc_ref, vct_ref, ovt_ref, cb_ref, oc_ref, sel_ref, s_ref):
    i = pl.program_id(0)
    win0 = pl.multiple_of(jnp.maximum(i * CMP_STRIDE - CMP_STRIDE, 0), CMP_STRIDE)

    def body(nk):
        nb = nk * CMP_STRIDE // SLC_LEN
        blk = lax.broadcasted_iota(jnp.int32, (nb, TQ), 0)
        cur = i * SLC_PER_TILE + lax.shift_right_logical(lax.broadcasted_iota(jnp.int32, (nb, TQ), 1), 6)
        allowed = blk <= cur
        forced = (blk == 0) | (blk == cur) | (blk == cur - 1)
        reachable = lax.broadcasted_iota(jnp.int32, (nk, TQ), 0) < win0 + CMP_WIN
        for g in range(NSA_KV_HEADS):
            p_sum = jnp.zeros((nk, TQ), F32)
            for r in range(NSA_GROUP):
                hd = g * NSA_GROUP + r
                q = q_ref[:, hd * HEAD_DIM:(hd + 1) * HEAD_DIM]
                s_ref[:nk, :] = lax.dot_general(kc_ref[g, :nk, :], q, _NT, preferred_element_type=F32)
                s_ref[pl.ds(win0, CMP_WIN), :] += cb_ref[0, hd]
                s = jnp.where(reachable, s_ref[:nk, :], NEG)
                mx = jnp.max(s, axis=0, keepdims=True)
                e = jnp.exp2(s - mx)
                seen = mx > 0.5 * NEG
                p = e * jnp.where(seen, 1.0 / jnp.maximum(jnp.sum(e, axis=0, keepdims=True), TINY), 0.0)
                o_t = jnp.dot(vct_ref[g, :, :nk], p.astype(BF16), preferred_element_type=F32)
                oc_ref[:, hd * HEAD_DIM:(hd + 1) * HEAD_DIM] = o_t.T
                p_sum = p_sum + p
            imp = jnp.zeros((nb, TQ), F32)
            for part in _split3(p_sum):
                imp = imp + jnp.dot(ovt_ref[:nb, :nk], part, preferred_element_type=F32)
            picked = _topk_bias_t(jnp.where(allowed & ~forced, imp, -1.0), SLC_TOPK - 3)
            selb = jnp.where(allowed, jnp.where(forced, 0.0, picked), NEG)
            if nb < N_SLC:
                selb = jnp.concatenate([selb, jnp.full((N_SLC - nb, TQ), NEG, F32)], axis=0)
            sel_ref[g] = selb.T.astype(sel_ref.dtype)

    for c in range(1, 512 // HEAD_DIM + 1):
        @pl.when(lax.shift_right_logical(i, 3) == c - 1)
        def _():
            body(c * HEAD_DIM)


def _cmp_select(zb, kc, vct, ovt, cmpb):
    w = NSA_HEADS * HEAD_DIM
    return pl.pallas_call(
        _cmp_kernel,
        grid=(N_QT,),
        in_specs=[pl.BlockSpec((TQ, w), lambda i: (i, COL_NQ // NSA_HEADS)),
                  pl.BlockSpec((NSA_KV_HEADS, 512, HEAD_DIM), lambda i: (0, 0, 0)),
                  pl.BlockSpec((NSA_KV_HEADS, HEAD_DIM, 512), lambda i: (0, 0, 0)),
                  pl.BlockSpec((N_SLC, 512), lambda i: (0, 0)),
                  pl.BlockSpec((1, NSA_HEADS, CMP_WIN, TQ), lambda i: (jnp.minimum(i, 1), 1, 0, 0))],
        out_specs=[pl.BlockSpec((TQ, w), lambda i: (i, 0)),
                   pl.BlockSpec((NSA_KV_HEADS, TQ, N_SLC), lambda i: (0, i, 0))],
        out_shape=[jax.ShapeDtypeStruct((SEQ, w), F32),
                   jax.ShapeDtypeStruct((NSA_KV_HEADS, SEQ, N_SLC), BF16)],
        scratch_shapes=[pltpu.VMEM((512, TQ), F32)],
        compiler_params=_cparams("arbitrary"),
        name="cmp_select",
    )(zb, kc, vct, ovt, cmpb)


def _slc_kernel(q_ref, sel_ref, k_ref, v_ref, pat_ref, d0_ref, d1_ref, o_ref, *flash_scratch):
    i = pl.program_id(1)
    hp = HP_SLC
    selb = sel_ref[0].astype(F32)
    qs = [q_ref[:, h * HEAD_DIM:(h + 1) * HEAD_DIM] for h in range(hp)]
    outs = _sparse_flash(qs, [selb] * hp, k_ref, v_ref, [0] * hp, pat_ref,
                         [d0_ref[0, h] for h in range(hp)], [d1_ref[0, h] for h in range(hp)],
                         i, SLC_PER_TILE, flash_scratch)
    for h in range(hp):
        o_ref[:, h * HEAD_DIM:(h + 1) * HEAD_DIM] = outs[h].astype(o_ref.dtype)


def _slc(zb, selb, near, pat):
    hp = HP_SLC
    w = hp * HEAD_DIM
    per_group = NSA_GROUP // hp
    return pl.pallas_call(
        _slc_kernel,
        grid=(NSA_HEADS // hp, N_QT),
        in_specs=[pl.BlockSpec((TQ, w), lambda h, i: (i, COL_NQ // hp + h)),
                  pl.BlockSpec((1, TQ, N_SLC), lambda h, i: (h // per_group, i, 0)),
                  pl.BlockSpec((SEQ, HEAD_DIM), lambda h, i: (0, COL_SK + h // per_group),
                               pipeline_mode=pl.Buffered(1)),
                  pl.BlockSpec((SEQ, HEAD_DIM), lambda h, i: (0, COL_SV + h // per_group),
                               pipeline_mode=pl.Buffered(1)),
                  pl.BlockSpec((SEQ, HEAD_DIM), lambda h, i: (0, 0), pipeline_mode=pl.Buffered(1)),
                  pl.BlockSpec((1, hp, TQ, TQ), lambda h, i: (0, MOBA_HEADS // hp + h, 0, 0)),
                  pl.BlockSpec((1, hp, TQ, TQ), lambda h, i: (1, MOBA_HEADS // hp + h, 0, 0))],
        out_specs=pl.BlockSpec((TQ, w), lambda h, i: (i, h)),
        out_shape=jax.ShapeDtypeStruct((SEQ, NSA_HEADS * HEAD_DIM), F32),
        scratch_shapes=_flash_scratch(hp, False),
        compiler_params=_cparams("parallel", "arbitrary", vmem=BIG_VMEM_LIMIT),
        name="slc",
    )(zb, selb, zb, zb, pat, near, near)


def _win_kernel(q_ref, k_ref, v_ref, d0_ref, d1_ref, w2_ref, o_ref):
    i = pl.program_id(0)
    offs = [pl.multiple_of(jnp.maximum(i - b, 0) * TQ, TQ) for b in range(3)]
    mask2 = jnp.where(i >= 2, w2_ref[...], NEG)
    ones = jnp.ones((3 * TQ, HEAD_DIM), BF16)
    for g in range(NSA_KV_HEADS):
        cols = slice(g * HEAD_DIM, (g + 1) * HEAD_DIM)
        k_all = jnp.concatenate([k_ref[pl.ds(off, TQ), cols] for off in offs], axis=0)
        v_aug = jnp.concatenate([jnp.concatenate([v_ref[pl.ds(off, TQ), cols] for off in offs], axis=0), ones],
                                axis=1)
        for r in range(NSA_GROUP):
            hd = g * NSA_GROUP + r
            q = q_ref[:, hd * HEAD_DIM:(hd + 1) * HEAD_DIM]
            bias = jnp.concatenate([d0_ref[0, hd], jnp.where(i >= 1, d1_ref[0, hd], NEG), mask2], axis=1)
            s = lax.dot_general(q, k_all, _NT, preferred_element_type=F32) + bias
            m = jnp.max(s, axis=1, keepdims=True)
            pv = jnp.dot(jnp.exp2(s - m).astype(BF16), v_aug, preferred_element_type=F32)
            o_ref[:, hd * HEAD_DIM:(hd + 1) * HEAD_DIM] = pv[:, :HEAD_DIM] / pv[:, HEAD_DIM:]


def _win(zb, near, win2):
    w = NSA_HEADS * HEAD_DIM
    kv_w = NSA_KV_HEADS * HEAD_DIM
    return pl.pallas_call(
        _win_kernel,
        grid=(N_QT,),
        in_specs=[pl.BlockSpec((TQ, w), lambda i: (i, COL_NQ // NSA_HEADS)),
                  pl.BlockSpec((SEQ, kv_w), lambda i: (0, COL_WK // NSA_KV_HEADS), pipeline_mode=pl.Buffered(1)),
                  pl.BlockSpec((SEQ, kv_w), lambda i: (0, COL_WV // NSA_KV_HEADS), pipeline_mode=pl.Buffered(1)),
                  pl.BlockSpec((1, NSA_HEADS, TQ, TQ), lambda i: (0, 1, 0, 0), pipeline_mode=pl.Buffered(1)),
                  pl.BlockSpec((1, NSA_HEADS, TQ, TQ), lambda i: (1, 1, 0, 0), pipeline_mode=pl.Buffered(1)),
                  pl.BlockSpec((TQ, TQ), lambda i: (0, 0), pipeline_mode=pl.Buffered(1))],
        out_specs=pl.BlockSpec((TQ, w), lambda i: (i, 0)),
        out_shape=jax.ShapeDtypeStruct((SEQ, w), F32),
        compiler_params=_cparams("arbitrary"),
        name="win",
    )(zb, zb, zb, near, near, win2)


def _layer_norm(y, g, b):
    mu = jnp.mean(y, axis=-1, keepdims=True)
    yc = y - mu
    var = jnp.mean(yc * yc, axis=-1, keepdims=True)
    return yc * lax.rsqrt(var + LN_EPS) * g + b


ROW_CHUNK = 256


def _outproj_kernel(x_ref, om_ref, oc_ref, os_ref, ow_ref, g_ref, e_ref, w_ref, lg_ref, lb_ref,
                    h_ref, hb_ref):
    for c0 in range(0, x_ref.shape[0], ROW_CHUNK):
        rows = slice(c0, c0 + ROW_CHUNK)
        gate = g_ref[rows, :]
        hi = gate.astype(BF16)
        lo = (gate - hi.astype(F32)).astype(BF16)
        g2 = jnp.concatenate([hi, lo], axis=1)
        o_n = jnp.zeros((ROW_CHUNK, oc_ref.shape[1]), F32)
        for c, branch in enumerate((oc_ref, os_ref, ow_ref)):
            o_n = o_n + jnp.dot(g2, e_ref[c], preferred_element_type=F32) * branch[rows, :]
        o = jnp.concatenate([om_ref[rows, :], o_n.astype(BF16)], axis=1)
        a = jnp.dot(o, w_ref[...], preferred_element_type=F32)
        h = _layer_norm(DN_ALPHA * x_ref[rows, :] + a, lg_ref[...], lb_ref[...])
        h_ref[rows, :] = h
        hb_ref[rows, :] = h.astype(BF16)


def _resident(shape):
    return pl.BlockSpec(shape, lambda i: (0,) * len(shape), pipeline_mode=pl.Buffered(1))


def _outproj(x, o_m, o_c, o_s, o_w, gates, expand, w_out, ln_g, ln_b, tm=512):
    half = NSA_HEADS * HEAD_DIM
    row = lambda i: (i, 0)
    return pl.pallas_call(
        _outproj_kernel,
        grid=(SEQ // tm,),
        in_specs=[pl.BlockSpec((tm, D_MODEL), row),
                  pl.BlockSpec((tm, half), row), pl.BlockSpec((tm, half), row),
                  pl.BlockSpec((tm, half), row), pl.BlockSpec((tm, half), row),
                  pl.BlockSpec((tm, HEAD_DIM), row),
                  _resident((3, 2 * HEAD_DIM, half)),
                  _resident((D_MODEL, D_MODEL)),
                  _resident((1, D_MODEL)), _resident((1, D_MODEL))],
        out_specs=[pl.BlockSpec((tm, D_MODEL), row), pl.BlockSpec((tm, D_MODEL), row)],
        out_shape=[jax.ShapeDtypeStruct((SEQ, D_MODEL), F32),
                   jax.ShapeDtypeStruct((SEQ, D_MODEL), BF16)],
        compiler_params=_cparams("parallel", vmem=BIG_VMEM_LIMIT),
        name="outproj_ln",
    )(x, o_m, o_c, o_s, o_w, gates, expand, w_out, ln_g, ln_b)


def _ffn_kernel(hb_ref, w1_ref, w2_ref, f_ref):
    @pl.when(pl.program_id(1) == 0)
    def _():
        f_ref[...] = jnp.zeros(f_ref.shape, F32)

    u = jnp.maximum(jnp.dot(hb_ref[...], w1_ref[...].astype(BF16), preferred_element_type=F32), 0.0)
    f_ref[...] += jnp.dot((u * u).astype(BF16), w2_ref[...].astype(BF16), preferred_element_type=F32)


def _ffn(hb, w1, w2, tm=1024, tf=512):
    return pl.pallas_call(
        _ffn_kernel,
        grid=(SEQ // tm, D_FF // tf),
        in_specs=[pl.BlockSpec((tm, D_MODEL), lambda i, c: (i, 0)),
                  pl.BlockSpec((D_MODEL, tf), lambda i, c: (0, c)),
                  pl.BlockSpec((tf, D_MODEL), lambda i, c: (c, 0))],
        out_specs=pl.BlockSpec((tm, D_MODEL), lambda i, c: (i, 0)),
        out_shape=jax.ShapeDtypeStruct((SEQ, D_MODEL), F32),
        compiler_params=_cparams("parallel", "arbitrary", vmem=BIG_VMEM_LIMIT),
        name="ffn",
    )(hb, w1, w2)


def _final_kernel(h_ref, hb_ref, f_ref, p_ref, wg_ref, wp_ref, lg_ref, lb_ref, o_ref):
    for c0 in range(0, h_ref.shape[0], ROW_CHUNK):
        rows = slice(c0, c0 + ROW_CHUNK)
        gate = jax.nn.sigmoid(jnp.dot(hb_ref[rows, :], wg_ref[...], preferred_element_type=F32))
        emb = jnp.dot(p_ref[rows, :].astype(BF16), wp_ref[...], preferred_element_type=F32)
        y = DN_ALPHA * h_ref[rows, :] + f_ref[rows, :] + gate * emb
        o_ref[rows, :] = _layer_norm(y, lg_ref[...], lb_ref[...])


def _final(h, hb, f, p, wg, wp, ln_g, ln_b, tm=512):
    row = lambda i: (i, 0)
    return pl.pallas_call(
        _final_kernel,
        grid=(SEQ // tm,),
        in_specs=[pl.BlockSpec((tm, D_MODEL), row), pl.BlockSpec((tm, D_MODEL), row),
                  pl.BlockSpec((tm, D_MODEL), row), pl.BlockSpec((tm, D_PLE), row),
                  _resident((D_MODEL, D_MODEL)), _resident((D_PLE, D_MODEL)),
                  _resident((1, D_MODEL)), _resident((1, D_MODEL))],
        out_specs=pl.BlockSpec((tm, D_MODEL), row),
        out_shape=jax.ShapeDtypeStruct((SEQ, D_MODEL), F32),
        compiler_params=_cparams("parallel"),
        name="final_ln",
    )(h, hb, f, p, wg, wp, ln_g, ln_b)


def _overlap_matrix_t():
    j = np.arange(N_SLC)[:, None]
    n = np.arange(512)[None, :]
    ov = ((n * CMP_STRIDE < j * SLC_LEN + SLC_LEN) & (n * CMP_STRIDE + CMP_LEN > j * SLC_LEN) & (n < N_CMP))
    return ov.astype(np.float32)


def _gate_expand():
    e = np.zeros((3, 2 * HEAD_DIM, NSA_HEADS * HEAD_DIM), np.float32)
    for c in range(3):
        for h in range(NSA_HEADS):
            e[c, 3 * h + c, h * HEAD_DIM:(h + 1) * HEAD_DIM] = 1.0
            e[c, HEAD_DIM + 3 * h + c, h * HEAD_DIM:(h + 1) * HEAD_DIM] = 1.0
    return e


def _layer(h, p, w_in, cmp_pe_k, cmp_w1_k, cmp_w2_k, cmp_pe_v, cmp_w1_v, cmp_w2_v, tiles, w_out,
           ln1_g, ln1_b, w_ff1, w_ff2, w_ple, w_ple_gate, ln2_g, ln2_b):
    near, win2, cmpb = tiles
    scale = HEAD_DIM ** -0.5 * LOG2E
    colscale = np.ones((1, GATE_COL0), np.float32)
    colscale[0, COL_MQ * HEAD_DIM:(COL_MQ + MOBA_HEADS) * HEAD_DIM] = scale
    colscale[0, COL_NQ * HEAD_DIM:(COL_NQ + NSA_HEADS) * HEAD_DIM] = scale
    w_in_t = w_in.T
    w_gl_t = jnp.pad(w_in_t[GATE_COL0:], ((0, HEAD_DIM - NSA_HEADS * 3), (0, 0))).astype(BF16)
    zb, gates, ckv = _inproj(h, w_in_t, jnp.asarray(colscale), w_gl_t)

    half = CMP_STRIDE * HEAD_DIM
    xr = ckv.reshape(4, SEQ // CMP_STRIDE, half)
    kc, vct = _compress(xr, cmp_pe_k.reshape(2, 1, half), cmp_w1_k.reshape(2, half, HEAD_DIM), cmp_w2_k,
                        cmp_pe_v.reshape(2, 1, half), cmp_w1_v.reshape(2, half, HEAD_DIM), cmp_w2_v)

    key = np.arange(SEQ)[:, None]
    lane = np.arange(HEAD_DIM)[None, :]
    pat_moba = jnp.asarray(lane == key // MOBA_BLOCK, BF16)
    pat_slc = jnp.asarray(lane == key // SLC_LEN, BF16)
    o_m = _moba(zb, _moba_select(zb), near, pat_moba)
    o_c, selb = _cmp_select(zb, kc, vct, jnp.asarray(_overlap_matrix_t(), BF16), cmpb)
    o_s = _slc(zb, selb, near, pat_slc)
    o_w = _win(zb, near, win2)

    h1, h1b = _outproj(h, o_m, o_c, o_s, o_w, gates, jnp.asarray(_gate_expand(), BF16),
                       w_out.astype(BF16), ln1_g.reshape(1, -1), ln1_b.reshape(1, -1))
    f = _ffn(h1b, w_ff1, w_ff2)
    return _final(h1, h1b, f, p, w_ple_gate.astype(BF16), w_ple.astype(BF16),
                  ln2_g.reshape(1, -1), ln2_b.reshape(1, -1))


def kernel(x, p, w_in, cmp_pe_k, cmp_w1_k, cmp_w2_k, cmp_pe_v, cmp_w1_v, cmp_w2_v, rel_bias, w_out,
           ln1_g, ln1_b, w_ff1, w_ff2, w_ple, w_ple_gate, ln2_g, ln2_b):
    tiles = _bias_tiles(rel_bias)
    h = x[0]
    for i in range(w_in.shape[0]):
        h = _layer(h, p[i, 0], w_in[i], cmp_pe_k[i], cmp_w1_k[i], cmp_w2_k[i], cmp_pe_v[i], cmp_w1_v[i],
                   cmp_w2_v[i], tiles, w_out[i], ln1_g[i], ln1_b[i], w_ff1[i], w_ff2[i], w_ple[i],
                   w_ple_gate[i], ln2_g[i], ln2_b[i])
    return h[None]
```

```python
import functools
import math

import numpy as np
import jax
import jax.numpy as jnp
from jax import lax
from jax.experimental import pallas as pl
from jax.experimental.pallas import tpu as pltpu

D_MODEL = 2048
SEQ = 8192
HEAD_DIM = 128
N_HEADS = 16
MOBA_HEADS = 8
NSA_HEADS = 8
NSA_KV_HEADS = 2
NSA_GROUP = 4
MOBA_BLOCK = 256
MOBA_TOPK = 3
CMP_LEN = 32
CMP_STRIDE = 16
SLC_LEN = 64
SLC_TOPK = 16
WINDOW = 512
N_BUCKETS = 32
MAX_DISTANCE = 128
D_FF = 4 * D_MODEL
D_PLE = 256
LN_EPS = 1e-5
DN_ALPHA = 2.0 ** 0.25
NEG = -1e30
LOG2E = math.log2(math.e)
TINY = 1e-30
FORCE_SCORE = 1e4

N_CMP = (SEQ - CMP_LEN) // CMP_STRIDE + 1
N_SLC = SEQ // SLC_LEN
TQ = 256
N_QT = SEQ // TQ
CMP_WIN = 32
SLC_PER_TILE = TQ // SLC_LEN
HP_MOBA = 4
HP_SLC = 4
KG = 4
GROUP = KG * TQ
COL_MQ, COL_MK, COL_MV, COL_NQ = 0, 8, 16, 24
COL_CK, COL_CV, COL_SK, COL_SV, COL_WK, COL_WV = 32, 34, 36, 38, 40, 42
N_COLBLK = 44
GATE_COL0 = N_COLBLK * HEAD_DIM

VMEM_LIMIT = 48 * 1024 * 1024
BIG_VMEM_LIMIT = 56 * 1024 * 1024

BF16 = jnp.bfloat16
F32 = jnp.float32
_NT = (((1,), (1,)), ((), ()))


def _cparams(*sem, vmem=VMEM_LIMIT):
    return pltpu.CompilerParams(dimension_semantics=sem, vmem_limit_bytes=vmem)


def _bucket_table():
    n = np.arange(1024, dtype=np.int32)
    max_exact = N_BUCKETS // 2
    ratio = np.maximum(n, 1).astype(np.float32) / np.float32(max_exact)
    large = max_exact + (np.log(ratio).astype(np.float32) / np.float32(math.log(MAX_DISTANCE / max_exact))
                         * np.float32(N_BUCKETS - max_exact)).astype(np.int32)
    large = np.minimum(large, N_BUCKETS - 1)
    return np.where(n < max_exact, n, large).astype(np.int32)


def _bias_kernel(tab_ref, near_ref, cmpb_ref, *, steps):
    h = pl.program_id(0)
    far = tab_ref[h, N_BUCKETS - 1]

    def lookup(d):
        val = jnp.full(d.shape, tab_ref[h, 0], F32)
        for start, bucket in steps:
            val = jnp.where(d >= start, tab_ref[h, bucket], val)
        return jnp.where(d >= 0, (val - far) * LOG2E, NEG)

    a = lax.broadcasted_iota(jnp.int32, (TQ, TQ), 0)
    b = lax.broadcasted_iota(jnp.int32, (TQ, TQ), 1)
    near_ref[0, 0] = lookup(a - b)
    near_ref[1, 0] = lookup(a - b + TQ)
    w = lax.broadcasted_iota(jnp.int32, (CMP_WIN, TQ), 0)
    aw = lax.broadcasted_iota(jnp.int32, (CMP_WIN, TQ), 1)
    cmpb_ref[0, 0] = lookup(aw - CMP_STRIDE * w - (CMP_LEN - 1))
    cmpb_ref[1, 0] = lookup(aw - CMP_STRIDE * w + (TQ - CMP_LEN + 1))


def _bias_tiles(rel_bias):
    bucket = _bucket_table()
    steps = tuple((int(n), int(bucket[n])) for n in range(1, bucket.size) if bucket[n] != bucket[n - 1])
    far_start = steps[-1][0]
    assert far_start <= TQ + 1
    assert far_start <= TQ + CMP_STRIDE - (CMP_LEN - 1)
    assert WINDOW == 2 * TQ
    near, cmpb = pl.pallas_call(
        functools.partial(_bias_kernel, steps=steps),
        grid=(N_HEADS,),
        in_specs=[pl.BlockSpec(memory_space=pltpu.SMEM)],
        out_specs=[pl.BlockSpec((2, 1, TQ, TQ), lambda h: (0, h, 0, 0)),
                   pl.BlockSpec((2, 1, CMP_WIN, TQ), lambda h: (0, h, 0, 0))],
        out_shape=[jax.ShapeDtypeStruct((2, N_HEADS, TQ, TQ), F32),
                   jax.ShapeDtypeStruct((2, N_HEADS, CMP_WIN, TQ), F32)],
        compiler_params=_cparams("parallel"),
        name="bias_tiles",
    )(rel_bias.T.astype(F32))
    a = np.arange(TQ)[:, None]
    b = np.arange(TQ)[None, :]
    win2 = jnp.asarray(np.where(a < b, 0.0, NEG).astype(np.float32))
    return near, win2, cmpb


INPROJ_TN = 4 * HEAD_DIM
CKV_TILE = COL_CK * HEAD_DIM // INPROJ_TN


def _inproj_kernel(x_ref, w_ref, cs_ref, wg_ref, o_ref, g_ref, ckv_ref, xb_ref, zf_ref):
    j = pl.program_id(1)

    @pl.when(j == 0)
    def _():
        xb = x_ref[...].astype(BF16)
        xb_ref[...] = xb
        g_ref[...] = jax.nn.sigmoid(lax.dot_general(xb, wg_ref[...], _NT, preferred_element_type=F32))

    acc = lax.dot_general(xb_ref[...], w_ref[...].astype(BF16), _NT, preferred_element_type=F32)
    z = (acc * cs_ref[...]).astype(o_ref.dtype)
    o_ref[...] = z

    @pl.when(j == CKV_TILE)
    def _():
        rows = zf_ref.shape[1] // CMP_STRIDE
        for c in range(INPROJ_TN // HEAD_DIM):
            cols = slice(c * HEAD_DIM, (c + 1) * HEAD_DIM)
            zf_ref[c] = acc[:, cols] * cs_ref[:, cols]
            for l in range(CMP_STRIDE):
                ckv_ref[c, :, l * HEAD_DIM:(l + 1) * HEAD_DIM] = (
                    zf_ref[c, pl.ds(l, rows, stride=CMP_STRIDE), :].astype(ckv_ref.dtype))


def _inproj(x, w_in_t, colscale, w_gate_t, tm=1024):
    m, k = x.shape
    tn = INPROJ_TN
    return pl.pallas_call(
        _inproj_kernel,
        grid=(m // tm, GATE_COL0 // tn),
        in_specs=[pl.BlockSpec((tm, k), lambda i, j: (i, 0)),
                  pl.BlockSpec((tn, k), lambda i, j: (j, 0)),
                  pl.BlockSpec((1, tn), lambda i, j: (0, j)),
                  pl.BlockSpec((HEAD_DIM, k), lambda i, j: (0, 0))],
        out_specs=[pl.BlockSpec((tm, tn), lambda i, j: (i, j)),
                   pl.BlockSpec((tm, HEAD_DIM), lambda i, j: (i, 0)),
                   pl.BlockSpec((tn // HEAD_DIM, tm // CMP_STRIDE, CMP_STRIDE * HEAD_DIM), lambda i, j: (0, i, 0))],
        out_shape=[jax.ShapeDtypeStruct((m, GATE_COL0), BF16),
                   jax.ShapeDtypeStruct((m, HEAD_DIM), F32),
                   jax.ShapeDtypeStruct((tn // HEAD_DIM, m // CMP_STRIDE, CMP_STRIDE * HEAD_DIM), BF16)],
        scratch_shapes=[pltpu.VMEM((tm, k), BF16), pltpu.VMEM((tn // HEAD_DIM, tm, HEAD_DIM), F32)],
        compiler_params=_cparams("parallel", "arbitrary"),
        name="inproj",
    )(x, w_in_t, colscale, w_gate_t)


def _gelu_tanh(x):
    return 0.5 * x * (1.0 + jnp.tanh(math.sqrt(2.0 / math.pi) * (x + 0.044715 * (x * x * x))))


def _compress_kernel(xk_ref, xv_ref, pek_ref, w1k_ref, w2k_ref, pev_ref, w1v_ref, w2v_ref, kc_ref, vct_ref):
    def compress(x_ref, pe_ref, w1_ref, w2_ref):
        x = x_ref[0].astype(F32)
        lo = jnp.dot((x + pe_ref[0]).astype(BF16), w1_ref[0].astype(BF16), preferred_element_type=F32)
        hi = jnp.dot((x + pe_ref[1]).astype(BF16), w1_ref[1].astype(BF16), preferred_element_type=F32)
        y = lo + pltpu.roll(hi, N_CMP, 0)
        out = jnp.dot(_gelu_tanh(y).astype(BF16), w2_ref[...].astype(BF16), preferred_element_type=F32)
        row = lax.broadcasted_iota(jnp.int32, out.shape, 0)
        return jnp.where(row < N_CMP, out, 0.0)

    kc_ref[0] = compress(xk_ref, pek_ref, w1k_ref, w2k_ref).astype(kc_ref.dtype)
    vct_ref[0] = compress(xv_ref, pev_ref, w1v_ref, w2v_ref).T.astype(vct_ref.dtype)


def _compress(xr, pe_k, w1_k, w2_k, pe_v, w1_v, w2_v):
    half = CMP_STRIDE * HEAD_DIM
    weights = [pl.BlockSpec((2, 1, half), lambda g: (0, 0, 0)),
               pl.BlockSpec((2, half, HEAD_DIM), lambda g: (0, 0, 0)),
               pl.BlockSpec((HEAD_DIM, HEAD_DIM), lambda g: (0, 0))]
    return pl.pallas_call(
        _compress_kernel,
        grid=(NSA_KV_HEADS,),
        in_specs=[pl.BlockSpec((1, 512, half), lambda g: (g, 0, 0)),
                  pl.BlockSpec((1, 512, half), lambda g: (NSA_KV_HEADS + g, 0, 0))] + weights + weights,
        out_specs=[pl.BlockSpec((1, 512, HEAD_DIM), lambda g: (g, 0, 0)),
                   pl.BlockSpec((1, HEAD_DIM, 512), lambda g: (g, 0, 0))],
        out_shape=[jax.ShapeDtypeStruct((NSA_KV_HEADS, 512, HEAD_DIM), BF16),
                   jax.ShapeDtypeStruct((NSA_KV_HEADS, HEAD_DIM, 512), BF16)],
        compiler_params=_cparams("parallel"),
        name="compress",
    )(xr, xr, pe_k, w1_k, w2_k, pe_v, w1_v, w2_v)


def _split3(x):
    hi = x.astype(BF16)
    r1 = x - hi.astype(F32)
    mid = r1.astype(BF16)
    lo = (r1 - mid.astype(F32)).astype(BF16)
    return hi, mid, lo


def _topk_bias_t(score, k):
    row = lax.broadcasted_iota(jnp.int32, score.shape, 0).astype(F32)
    selb = jnp.full(score.shape, NEG, F32)
    s = score
    for _ in range(k):
        mx = jnp.max(s, axis=0, keepdims=True)
        idx = jnp.min(jnp.where(s == mx, row, 1e9), axis=0, keepdims=True)
        hit = row == idx
        selb = jnp.where(hit, 0.0, selb)
        s = jnp.where(hit, -3e38, s)
    return selb


def _lane_tile_max(s):
    m = s[:, :HEAD_DIM]
    for c in range(1, s.shape[1] // HEAD_DIM):
        m = jnp.maximum(m, s[:, c * HEAD_DIM:(c + 1) * HEAD_DIM])
    return m


def _sparse_flash(qs, selbs, k_ref, v_ref, kcols, pat_ref, d0s, d1s, i, sub, scratch):
    t_ref, *maybe_r_ref, mx_ref, acc_ref = scratch
    r_ref = maybe_r_ref[0] if maybe_r_ref else None
    heads = range(len(qs))
    lane = lax.broadcasted_iota(jnp.int32, (TQ, HEAD_DIM), 1)
    far = lane < (i - 1) * sub
    q_near = [jnp.concatenate([qs[h], selbs[h].astype(BF16)], axis=1) for h in heads]
    q_far = [jnp.concatenate([qs[h], jnp.where(far, selbs[h], NEG).astype(BF16)], axis=1) for h in heads]

    def k_aug(h, off, n):
        c = kcols[h]
        return jnp.concatenate([k_ref[pl.ds(off, n), c:c + HEAD_DIM], pat_ref[pl.ds(off, n), :]], axis=1)

    def v_aug(h, off, n):
        c = kcols[h]
        return jnp.concatenate([v_ref[pl.ds(off, n), c:c + HEAD_DIM], jnp.ones((n, HEAD_DIM), BF16)], axis=1)

    for h in heads:
        mx_ref[h] = jnp.full((TQ, HEAD_DIM), NEG, F32)

    def score(g):
        off = pl.multiple_of(g * GROUP, GROUP)
        for h in heads:
            s = lax.dot_general(q_far[h], k_aug(h, off, GROUP), _NT, preferred_element_type=F32)
            ref = _lane_tile_max(s)
            if r_ref is not None:
                s = s - jnp.concatenate([ref] * (GROUP // HEAD_DIM), axis=1)
                r_ref[h, g] = ref
            for u in range(KG):
                t_ref[h, g * KG + u] = s[:, u * TQ:(u + 1) * TQ].astype(t_ref.dtype)
            mx_ref[h] = jnp.maximum(mx_ref[h], ref)

    def run_unrolled(n, body):
        def quad(m, carry):
            body(4 * m, 4)
            return carry

        lax.fori_loop(0, lax.shift_right_logical(n, 2), quad, 0)
        for left in (1, 2, 3):
            @pl.when((n & 3) == left)
            def _(left=left):
                body(n - left, left)

    n_far = lax.shift_right_logical(jnp.maximum(i - 1, 0) + (KG - 1), KG.bit_length() - 1)

    def score_groups(g0, count):
        for k in range(count):
            score(g0 + k)

    run_unrolled(n_far, score_groups)

    off_prev = pl.multiple_of(jnp.maximum(i - 1, 0) * TQ, TQ)
    off_own = pl.multiple_of(i * TQ, TQ)
    p_near = []
    for h in heads:
        k2 = jnp.concatenate([k_aug(h, off_prev, TQ), k_aug(h, off_own, TQ)], axis=0)
        bias = jnp.concatenate([jnp.where(i >= 1, d1s[h], NEG), d0s[h]], axis=1)
        s = lax.dot_general(q_near[h], k2, _NT, preferred_element_type=F32) + bias
        m = jnp.max(jnp.maximum(mx_ref[h], _lane_tile_max(s)), axis=1, keepdims=True)
        mx_ref[h] = jnp.broadcast_to(m, (TQ, HEAD_DIM))
        p_near.append(jnp.exp2(s - m).astype(BF16))
    for h in heads:
        v2 = jnp.concatenate([v_aug(h, off_prev, TQ), v_aug(h, off_own, TQ)], axis=0)
        acc_ref[h] = jnp.dot(p_near[h], v2, preferred_element_type=F32)

    def accumulate(g0, ng):
        off = pl.multiple_of(g0 * GROUP, GROUP)
        for h in heads:
            probs = []
            for g in [g0 + k for k in range(ng)]:
                d = -mx_ref[h] if r_ref is None else r_ref[h, g] - mx_ref[h]
                d2 = jnp.concatenate([d, d], axis=1)
                probs += [jnp.exp2(t_ref[h, g * KG + u].astype(F32) + d2).astype(BF16) for u in range(KG)]
            acc_ref[h] += jnp.dot(jnp.concatenate(probs, axis=1), v_aug(h, off, ng * GROUP),
                                  preferred_element_type=F32)

    def accumulate_groups(g0, count):
        for k in range(0, count, 2):
            accumulate(g0 + k, min(2, count - k))

    run_unrolled(n_far, accumulate_groups)

    outs = []
    for h in heads:
        acc = acc_ref[h]
        outs.append(acc[:, :HEAD_DIM] / acc[:, HEAD_DIM:])
    return outs


def _flash_scratch(hp, relative_bf16):
    logits = [pltpu.VMEM((hp, N_QT, TQ, TQ), BF16),
              pltpu.VMEM((hp, SEQ // GROUP, TQ, HEAD_DIM), F32)]
    if not relative_bf16:
        logits = [pltpu.VMEM((hp, N_QT, TQ, TQ), F32)]
    return logits + [pltpu.VMEM((hp, TQ, HEAD_DIM), F32),
                     pltpu.VMEM((hp, TQ, 2 * HEAD_DIM), F32)]


SEL_CHUNK = 4096


HP_SELECT = 2


def _moba_select_kernel(q_ref, k_ref, sel_ref):
    n_blk = SEQ // MOBA_BLOCK
    blk = lax.broadcasted_iota(jnp.int32, (n_blk, SEQ), 0)
    pos = lax.broadcasted_iota(jnp.int32, (n_blk, SEQ), 1)
    ind = jnp.where(lax.shift_right_logical(pos, 8) == blk, 1.0 / MOBA_BLOCK, 0.0).astype(BF16)
    heads = range(HP_SELECT)
    kmean = [_split3(jnp.dot(ind, k_ref[:, h * HEAD_DIM:(h + 1) * HEAD_DIM], preferred_element_type=F32))
             for h in heads]
    blk_t = lax.broadcasted_iota(jnp.int32, (n_blk, SEL_CHUNK), 0)
    col = lax.broadcasted_iota(jnp.int32, (n_blk, SEL_CHUNK), 1)
    no_block = jnp.full((HEAD_DIM - n_blk, SEL_CHUNK), NEG, F32)

    def chunk(c, carry):
        off = pl.multiple_of(c * SEL_CHUNK, SEL_CHUNK)
        own = lax.shift_right_logical(off + col, 8)
        past = blk_t < own
        for h in heads:
            q = q_ref[pl.ds(off, SEL_CHUNK), h * HEAD_DIM:(h + 1) * HEAD_DIM]
            gate = jnp.zeros((n_blk, SEL_CHUNK), F32)
            for part in kmean[h]:
                gate = gate + lax.dot_general(part, q, _NT, preferred_element_type=F32)
            selb = _topk_bias_t(jnp.where(past, gate, NEG), MOBA_TOPK)
            selb = jnp.where(past, selb, jnp.where(blk_t == own, 0.0, NEG))
            sel_ref[h, pl.ds(off, SEL_CHUNK), :] = jnp.concatenate([selb, no_block], axis=0).T.astype(sel_ref.dtype)
        return carry

    lax.fori_loop(0, SEQ // SEL_CHUNK, chunk, 0)


def _moba_select(zb):
    w = HP_SELECT * HEAD_DIM
    return pl.pallas_call(
        _moba_select_kernel,
        grid=(MOBA_HEADS // HP_SELECT,),
        in_specs=[pl.BlockSpec((SEQ, w), lambda h: (0, COL_MQ // HP_SELECT + h)),
                  pl.BlockSpec((SEQ, w), lambda h: (0, COL_MK // HP_SELECT + h))],
        out_specs=pl.BlockSpec((HP_SELECT, SEQ, HEAD_DIM), lambda h: (h, 0, 0)),
        out_shape=jax.ShapeDtypeStruct((MOBA_HEADS, SEQ, HEAD_DIM), BF16),
        compiler_params=_cparams("parallel"),
        name="moba_select",
    )(zb, zb)


def _moba_kernel(q_ref, sel_ref, k_ref, v_ref, pat_ref, d0_ref, d1_ref, o_ref, *flash_scratch):
    i = pl.program_id(1)
    hp = HP_MOBA
    qs = [q_ref[:, h * HEAD_DIM:(h + 1) * HEAD_DIM] for h in range(hp)]
    selbs = [sel_ref[h].astype(F32) for h in range(hp)]
    cols = [h * HEAD_DIM for h in range(hp)]
    outs = _sparse_flash(qs, selbs, k_ref, v_ref, cols, pat_ref, [d0_ref[0, h] for h in range(hp)],
                         [d1_ref[0, h] for h in range(hp)], i, 1, flash_scratch)
    for h in range(hp):
        o_ref[:, h * HEAD_DIM:(h + 1) * HEAD_DIM] = outs[h].astype(o_ref.dtype)


def _moba(zb, selb, near, pat):
    hp = HP_MOBA
    w = hp * HEAD_DIM
    return pl.pallas_call(
        _moba_kernel,
        grid=(MOBA_HEADS // hp, N_QT),
        in_specs=[pl.BlockSpec((TQ, w), lambda h, i: (i, COL_MQ // hp + h)),
                  pl.BlockSpec((hp, TQ, HEAD_DIM), lambda h, i: (h, i, 0)),
                  pl.BlockSpec((SEQ, w), lambda h, i: (0, COL_MK // hp + h), pipeline_mode=pl.Buffered(1)),
                  pl.BlockSpec((SEQ, w), lambda h, i: (0, COL_MV // hp + h), pipeline_mode=pl.Buffered(1)),
                  pl.BlockSpec((SEQ, HEAD_DIM), lambda h, i: (0, 0), pipeline_mode=pl.Buffered(1)),
                  pl.BlockSpec((1, hp, TQ, TQ), lambda h, i: (0, h, 0, 0)),
                  pl.BlockSpec((1, hp, TQ, TQ), lambda h, i: (1, h, 0, 0))],
        out_specs=pl.BlockSpec((TQ, w), lambda h, i: (i, h)),
        out_shape=jax.ShapeDtypeStruct((SEQ, MOBA_HEADS * HEAD_DIM), BF16),
        scratch_shapes=_flash_scratch(hp, True),
        compiler_params=_cparams("parallel", "arbitrary", vmem=BIG_VMEM_LIMIT),
        name="moba",
    )(zb, selb, zb, zb, pat, near, near)


def _cmp_kernel(q_ref, kc_ref, vct_ref, ovt_ref, cb_ref, oc_ref, sel_ref, s_ref):
    i = pl.program_id(0)
    win0 = pl.multiple_of(jnp.maximum(i * CMP_STRIDE - CMP_STRIDE, 0), CMP_STRIDE)

    def body(nk):
        nb = nk * CMP_STRIDE // SLC_LEN
        blk = lax.broadcasted_iota(jnp.int32, (nb, TQ), 0)
        cur = i * SLC_PER_TILE + lax.shift_right_logical(lax.broadcasted_iota(jnp.int32, (nb, TQ), 1), 6)
        allowed = blk <= cur
        forced = (blk == 0) | (blk == cur) | (blk == cur - 1)
        reachable = lax.broadcasted_iota(jnp.int32, (nk, TQ), 0) < win0 + CMP_WIN
        for g in range(NSA_KV_HEADS):
            p_sum = jnp.zeros((nk, TQ), F32)
            for r in range(NSA_GROUP):
                hd = g * NSA_GROUP + r
                q = q_ref[:, hd * HEAD_DIM:(hd + 1) * HEAD_DIM]
                s_ref[:nk, :] = lax.dot_general(kc_ref[g, :nk, :], q, _NT, preferred_element_type=F32)
                s_ref[pl.ds(win0, CMP_WIN), :] += cb_ref[0, hd]
                s = jnp.where(reachable, s_ref[:nk, :], NEG)
                mx = jnp.max(s, axis=0, keepdims=True)
                e = jnp.exp2(s - mx)
                seen = mx > 0.5 * NEG
                p = e * jnp.where(seen, 1.0 / jnp.maximum(jnp.sum(e, axis=0, keepdims=True), TINY), 0.0)
                o_t = jnp.dot(vct_ref[g, :, :nk], p.astype(BF16), preferred_element_type=F32)
                oc_ref[:, hd * HEAD_DIM:(hd + 1) * HEAD_DIM] = o_t.T
                p_sum = p_sum + p
            imp = jnp.zeros((nb, TQ), F32)
            for part in _split3(p_sum):
                imp = imp + jnp.dot(ovt_ref[:nb, :nk], part, preferred_element_type=F32)
            picked = _topk_bias_t(jnp.where(allowed & ~forced, imp, -1.0), SLC_TOPK - 3)
            selb = jnp.where(allowed, jnp.where(forced, 0.0, picked), NEG)
            if nb < N_SLC:
                selb = jnp.concatenate([selb, jnp.full((N_SLC - nb, TQ), NEG, F32)], axis=0)
            sel_ref[g] = selb.T.astype(sel_ref.dtype)

    for c in range(1, 512 // HEAD_DIM + 1):
        @pl.when(lax.shift_right_logical(i, 3) == c - 1)
        def _():
            body(c * HEAD_DIM)


def _cmp_select(zb, kc, vct, ovt, cmpb):
    w = NSA_HEADS * HEAD_DIM
    return pl.pallas_call(
        _cmp_kernel,
        grid=(N_QT,),
        in_specs=[pl.BlockSpec((TQ, w), lambda i: (i, COL_NQ // NSA_HEADS)),
                  pl.BlockSpec((NSA_KV_HEADS, 512, HEAD_DIM), lambda i: (0, 0, 0)),
                  pl.BlockSpec((NSA_KV_HEADS, HEAD_DIM, 512), lambda i: (0, 0, 0)),
                  pl.BlockSpec((N_SLC, 512), lambda i: (0, 0)),
                  pl.BlockSpec((1, NSA_HEADS, CMP_WIN, TQ), lambda i: (jnp.minimum(i, 1), 1, 0, 0))],
        out_specs=[pl.BlockSpec((TQ, w), lambda i: (i, 0)),
                   pl.BlockSpec((NSA_KV_HEADS, TQ, N_SLC), lambda i: (0, i, 0))],
        out_shape=[jax.ShapeDtypeStruct((SEQ, w), F32),
                   jax.ShapeDtypeStruct((NSA_KV_HEADS, SEQ, N_SLC), BF16)],
        scratch_shapes=[pltpu.VMEM((512, TQ), F32)],
        compiler_params=_cparams("arbitrary"),
        name="cmp_select",
    )(zb, kc, vct, ovt, cmpb)


def _slc_kernel(q_ref, sel_ref, k_ref, v_ref, pat_ref, d0_ref, d1_ref, o_ref, *flash_scratch):
    i = pl.program_id(1)
    hp = HP_SLC
    selb = sel_ref[0].astype(F32)
    qs = [q_ref[:, h * HEAD_DIM:(h + 1) * HEAD_DIM] for h in range(hp)]
    outs = _sparse_flash(qs, [selb] * hp, k_ref, v_ref, [0] * hp, pat_ref,
                         [d0_ref[0, h] for h in range(hp)], [d1_ref[0, h] for h in range(hp)],
                         i, SLC_PER_TILE, flash_scratch)
    for h in range(hp):
        o_ref[:, h * HEAD_DIM:(h + 1) * HEAD_DIM] = outs[h].astype(o_ref.dtype)


def _slc(zb, selb, near, pat):
    hp = HP_SLC
    w = hp * HEAD_DIM
    per_group = NSA_GROUP // hp
    return pl.pallas_call(
        _slc_kernel,
        grid=(NSA_HEADS // hp, N_QT),
        in_specs=[pl.BlockSpec((TQ, w), lambda h, i: (i, COL_NQ // hp + h)),
                  pl.BlockSpec((1, TQ, N_SLC), lambda h, i: (h // per_group, i, 0)),
                  pl.BlockSpec((SEQ, HEAD_DIM), lambda h, i: (0, COL_SK + h // per_group),
                               pipeline_mode=pl.Buffered(1)),
                  pl.BlockSpec((SEQ, HEAD_DIM), lambda h, i: (0, COL_SV + h // per_group),
                               pipeline_mode=pl.Buffered(1)),
                  pl.BlockSpec((SEQ, HEAD_DIM), lambda h, i: (0, 0), pipeline_mode=pl.Buffered(1)),
                  pl.BlockSpec((1, hp, TQ, TQ), lambda h, i: (0, MOBA_HEADS // hp + h, 0, 0)),
                  pl.BlockSpec((1, hp, TQ, TQ), lambda h, i: (1, MOBA_HEADS // hp + h, 0, 0))],
        out_specs=pl.BlockSpec((TQ, w), lambda h, i: (i, h)),
        out_shape=jax.ShapeDtypeStruct((SEQ, NSA_HEADS * HEAD_DIM), F32),
        scratch_shapes=_flash_scratch(hp, False),
        compiler_params=_cparams("parallel", "arbitrary", vmem=BIG_VMEM_LIMIT),
        name="slc",
    )(zb, selb, zb, zb, pat, near, near)


def _win_kernel(q_ref, k_ref, v_ref, d0_ref, d1_ref, w2_ref, o_ref):
    i = pl.program_id(0)
    offs = [pl.multiple_of(jnp.maximum(i - b, 0) * TQ, TQ) for b in range(3)]
    mask2 = jnp.where(i >= 2, w2_ref[...], NEG)
    ones = jnp.ones((3 * TQ, HEAD_DIM), BF16)
    for g in range(NSA_KV_HEADS):
        cols = slice(g * HEAD_DIM, (g + 1) * HEAD_DIM)
        k_all = jnp.concatenate([k_ref[pl.ds(off, TQ), cols] for off in offs], axis=0)
        v_aug = jnp.concatenate([jnp.concatenate([v_ref[pl.ds(off, TQ), cols] for off in offs], axis=0), ones],
                                axis=1)
        for r in range(NSA_GROUP):
            hd = g * NSA_GROUP + r
            q = q_ref[:, hd * HEAD_DIM:(hd + 1) * HEAD_DIM]
            bias = jnp.concatenate([d0_ref[0, hd], jnp.where(i >= 1, d1_ref[0, hd], NEG), mask2], axis=1)
            s = lax.dot_general(q, k_all, _NT, preferred_element_type=F32) + bias
            m = jnp.max(s, axis=1, keepdims=True)
            pv = jnp.dot(jnp.exp2(s - m).astype(BF16), v_aug, preferred_element_type=F32)
            o_ref[:, hd * HEAD_DIM:(hd + 1) * HEAD_DIM] = pv[:, :HEAD_DIM] / pv[:, HEAD_DIM:]


def _win(zb, near, win2):
    w = NSA_HEADS * HEAD_DIM
    kv_w = NSA_KV_HEADS * HEAD_DIM
    return pl.pallas_call(
        _win_kernel,
        grid=(N_QT,),
        in_specs=[pl.BlockSpec((TQ, w), lambda i: (i, COL_NQ // NSA_HEADS)),
                  pl.BlockSpec((SEQ, kv_w), lambda i: (0, COL_WK // NSA_KV_HEADS), pipeline_mode=pl.Buffered(1)),
                  pl.BlockSpec((SEQ, kv_w), lambda i: (0, COL_WV // NSA_KV_HEADS), pipeline_mode=pl.Buffered(1)),
                  pl.BlockSpec((1, NSA_HEADS, TQ, TQ), lambda i: (0, 1, 0, 0), pipeline_mode=pl.Buffered(1)),
                  pl.BlockSpec((1, NSA_HEADS, TQ, TQ), lambda i: (1, 1, 0, 0), pipeline_mode=pl.Buffered(1)),
                  pl.BlockSpec((TQ, TQ), lambda i: (0, 0), pipeline_mode=pl.Buffered(1))],
        out_specs=pl.BlockSpec((TQ, w), lambda i: (i, 0)),
        out_shape=jax.ShapeDtypeStruct((SEQ, w), F32),
        compiler_params=_cparams("arbitrary"),
        name="win",
    )(zb, zb, zb, near, near, win2)


def _layer_norm(y, g, b):
    mu = jnp.mean(y, axis=-1, keepdims=True)
    yc = y - mu
    var = jnp.mean(yc * yc, axis=-1, keepdims=True)
    return yc * lax.rsqrt(var + LN_EPS) * g + b


ROW_CHUNK = 256


def _outproj_kernel(x_ref, om_ref, oc_ref, os_ref, ow_ref, g_ref, e_ref, w_ref, lg_ref, lb_ref,
                    h_ref, hb_ref):
    for c0 in range(0, x_ref.shape[0], ROW_CHUNK):
        rows = slice(c0, c0 + ROW_CHUNK)
        gate = g_ref[rows, :]
        hi = gate.astype(BF16)
        lo = (gate - hi.astype(F32)).astype(BF16)
        g2 = jnp.concatenate([hi, lo], axis=1)
        o_n = jnp.zeros((ROW_CHUNK, oc_ref.shape[1]), F32)
        for c, branch in enumerate((oc_ref, os_ref, ow_ref)):
            o_n = o_n + jnp.dot(g2, e_ref[c], preferred_element_type=F32) * branch[rows, :]
        o = jnp.concatenate([om_ref[rows, :], o_n.astype(BF16)], axis=1)
        a = jnp.dot(o, w_ref[...], preferred_element_type=F32)
        h = _layer_norm(DN_ALPHA * x_ref[rows, :] + a, lg_ref[...], lb_ref[...])
        h_ref[rows, :] = h
        hb_ref[rows, :] = h.astype(BF16)


def _resident(shape):
    return pl.BlockSpec(shape, lambda i: (0,) * len(shape), pipeline_mode=pl.Buffered(1))


def _outproj(x, o_m, o_c, o_s, o_w, gates, expand, w_out, ln_g, ln_b, tm=512):
    half = NSA_HEADS * HEAD_DIM
    row = lambda i: (i, 0)
    return pl.pallas_call(
        _outproj_kernel,
        grid=(SEQ // tm,),
        in_specs=[pl.BlockSpec((tm, D_MODEL), row),
                  pl.BlockSpec((tm, half), row), pl.BlockSpec((tm, half), row),
                  pl.BlockSpec((tm, half), row), pl.BlockSpec((tm, half), row),
                  pl.BlockSpec((tm, HEAD_DIM), row),
                  _resident((3, 2 * HEAD_DIM, half)),
                  _resident((D_MODEL, D_MODEL)),
                  _resident((1, D_MODEL)), _resident((1, D_MODEL))],
        out_specs=[pl.BlockSpec((tm, D_MODEL), row), pl.BlockSpec((tm, D_MODEL), row)],
        out_shape=[jax.ShapeDtypeStruct((SEQ, D_MODEL), F32),
                   jax.ShapeDtypeStruct((SEQ, D_MODEL), BF16)],
        compiler_params=_cparams("parallel", vmem=BIG_VMEM_LIMIT),
        name="outproj_ln",
    )(x, o_m, o_c, o_s, o_w, gates, expand, w_out, ln_g, ln_b)


def _ffn_kernel(hb_ref, w1_ref, w2_ref, f_ref):
    @pl.when(pl.program_id(1) == 0)
    def _():
        f_ref[...] = jnp.zeros(f_ref.shape, F32)

    u = jnp.maximum(jnp.dot(hb_ref[...], w1_ref[...].astype(BF16), preferred_element_type=F32), 0.0)
    f_ref[...] += jnp.dot((u * u).astype(BF16), w2_ref[...].astype(BF16), preferred_element_type=F32)


def _ffn(hb, w1, w2, tm=1024, tf=512):
    return pl.pallas_call(
        _ffn_kernel,
        grid=(SEQ // tm, D_FF // tf),
        in_specs=[pl.BlockSpec((tm, D_MODEL), lambda i, c: (i, 0)),
                  pl.BlockSpec((D_MODEL, tf), lambda i, c: (0, c)),
                  pl.BlockSpec((tf, D_MODEL), lambda i, c: (c, 0))],
        out_specs=pl.BlockSpec((tm, D_MODEL), lambda i, c: (i, 0)),
        out_shape=jax.ShapeDtypeStruct((SEQ, D_MODEL), F32),
        compiler_params=_cparams("parallel", "arbitrary", vmem=BIG_VMEM_LIMIT),
        name="ffn",
    )(hb, w1, w2)


def _final_kernel(h_ref, hb_ref, f_ref, p_ref, wg_ref, wp_ref, lg_ref, lb_ref, o_ref):
    for c0 in range(0, h_ref.shape[0], ROW_CHUNK):
        rows = slice(c0, c0 + ROW_CHUNK)
        gate = jax.nn.sigmoid(jnp.dot(hb_ref[rows, :], wg_ref[...], preferred_element_type=F32))
        emb = jnp.dot(p_ref[rows, :].astype(BF16), wp_ref[...], preferred_element_type=F32)
        y = DN_ALPHA * h_ref[rows, :] + f_ref[rows, :] + gate * emb
        o_ref[rows, :] = _layer_norm(y, lg_ref[...], lb_ref[...])


def _final(h, hb, f, p, wg, wp, ln_g, ln_b, tm=512):
    row = lambda i: (i, 0)
    return pl.pallas_call(
        _final_kernel,
        grid=(SEQ // tm,),
        in_specs=[pl.BlockSpec((tm, D_MODEL), row), pl.BlockSpec((tm, D_MODEL), row),
                  pl.BlockSpec((tm, D_MODEL), row), pl.BlockSpec((tm, D_PLE), row),
                  _resident((D_MODEL, D_MODEL)), _resident((D_PLE, D_MODEL)),
                  _resident((1, D_MODEL)), _resident((1, D_MODEL))],
        out_specs=pl.BlockSpec((tm, D_MODEL), row),
        out_shape=jax.ShapeDtypeStruct((SEQ, D_MODEL), F32),
        compiler_params=_cparams("parallel"),
        name="final_ln",
    )(h, hb, f, p, wg, wp, ln_g, ln_b)


def _overlap_matrix_t():
    j = np.arange(N_SLC)[:, None]
    n = np.arange(512)[None, :]
    ov = ((n * CMP_STRIDE < j * SLC_LEN + SLC_LEN) & (n * CMP_STRIDE + CMP_LEN > j * SLC_LEN) & (n < N_CMP))
    return ov.astype(np.float32)


def _gate_expand():
    e = np.zeros((3, 2 * HEAD_DIM, NSA_HEADS * HEAD_DIM), np.float32)
    for c in range(3):
        for h in range(NSA_HEADS):
            e[c, 3 * h + c, h * HEAD_DIM:(h + 1) * HEAD_DIM] = 1.0
            e[c, HEAD_DIM + 3 * h + c, h * HEAD_DIM:(h + 1) * HEAD_DIM] = 1.0
    return e


def _layer(h, p, w_in, cmp_pe_k, cmp_w1_k, cmp_w2_k, cmp_pe_v, cmp_w1_v, cmp_w2_v, tiles, w_out,
           ln1_g, ln1_b, w_ff1, w_ff2, w_ple, w_ple_gate, ln2_g, ln2_b):
    near, win2, cmpb = tiles
    scale = HEAD_DIM ** -0.5 * LOG2E
    colscale = np.ones((1, GATE_COL0), np.float32)
    colscale[0, COL_MQ * HEAD_DIM:(COL_MQ + MOBA_HEADS) * HEAD_DIM] = scale
    colscale[0, COL_NQ * HEAD_DIM:(COL_NQ + NSA_HEADS) * HEAD_DIM] = scale
    w_in_t = w_in.T
    w_gl_t = jnp.pad(w_in_t[GATE_COL0:], ((0, HEAD_DIM - NSA_HEADS * 3), (0, 0))).astype(BF16)
    zb, gates, ckv = _inproj(h, w_in_t, jnp.asarray(colscale), w_gl_t)

    half = CMP_STRIDE * HEAD_DIM
    kc, vct = _compress(ckv, cmp_pe_k.reshape(2, 1, half), cmp_w1_k.reshape(2, half, HEAD_DIM), cmp_w2_k,
                        cmp_pe_v.reshape(2, 1, half), cmp_w1_v.reshape(2, half, HEAD_DIM), cmp_w2_v)

    key = np.arange(SEQ)[:, None]
    lane = np.arange(HEAD_DIM)[None, :]
    pat_moba = jnp.asarray(lane == key // MOBA_BLOCK, BF16)
    pat_slc = jnp.asarray(lane == key // SLC_LEN, BF16)
    o_m = _moba(zb, _moba_select(zb), near, pat_moba)
    o_c, selb = _cmp_select(zb, kc, vct, jnp.asarray(_overlap_matrix_t(), BF16), cmpb)
    o_s = _slc(zb, selb, near, pat_slc)
    o_w = _win(zb, near, win2)

    h1, h1b = _outproj(h, o_m, o_c, o_s, o_w, gates, jnp.asarray(_gate_expand(), BF16),
                       w_out.astype(BF16), ln1_g.reshape(1, -1), ln1_b.reshape(1, -1))
    f = _ffn(h1b, w_ff1, w_ff2)
    return _final(h1, h1b, f, p, w_ple_gate.astype(BF16), w_ple.astype(BF16),
                  ln2_g.reshape(1, -1), ln2_b.reshape(1, -1))


def kernel(x, p, w_in, cmp_pe_k, cmp_w1_k, cmp_w2_k, cmp_pe_v, cmp_w1_v, cmp_w2_v, rel_bias, w_out,
           ln1_g, ln1_b, w_ff1, w_ff2, w_ple, w_ple_gate, ln2_g, ln2_b):
    tiles = _bias_tiles(rel_bias)
    h = x[0]
    for i in range(w_in.shape[0]):
        h = _layer(h, p[i, 0], w_in[i], cmp_pe_k[i], cmp_w1_k[i], cmp_w2_k[i], cmp_pe_v[i], cmp_w1_v[i],
                   cmp_w2_v[i], tiles, w_out[i], ln1_g[i], ln1_b[i], w_ff1[i], w_ff2[i], w_ple[i],
                   w_ple_gate[i], ln2_g[i], ln2_b[i])
    return h[None]
```

```python
import functools
import math

import numpy as np
import jax
import jax.numpy as jnp
from jax import lax
from jax.experimental import pallas as pl
from jax.experimental.pallas import tpu as pltpu

D_MODEL = 2048
SEQ = 8192
HEAD_DIM = 128
N_HEADS = 16
MOBA_HEADS = 8
NSA_HEADS = 8
NSA_KV_HEADS = 2
NSA_GROUP = 4
MOBA_BLOCK = 256
MOBA_TOPK = 3
CMP_LEN = 32
CMP_STRIDE = 16
SLC_LEN = 64
SLC_TOPK = 16
WINDOW = 512
N_BUCKETS = 32
MAX_DISTANCE = 128
D_FF = 4 * D_MODEL
D_PLE = 256
LN_EPS = 1e-5
DN_ALPHA = 2.0 ** 0.25
NEG = -1e30
LOG2E = math.log2(math.e)
TINY = 1e-30
FORCE_SCORE = 1e4

N_CMP = (SEQ - CMP_LEN) // CMP_STRIDE + 1
N_SLC = SEQ // SLC_LEN
TQ = 256
N_QT = SEQ // TQ
CMP_WIN = 32
SLC_PER_TILE = TQ // SLC_LEN
HP_MOBA = 4
HP_SLC = 4
KG = 4
GROUP = KG * TQ
COL_MQ, COL_MK, COL_MV, COL_NQ = 0, 8, 16, 24
COL_CK, COL_CV, COL_SK, COL_SV, COL_WK, COL_WV = 32, 34, 36, 38, 40, 42
N_COLBLK = 44
GATE_COL0 = N_COLBLK * HEAD_DIM

VMEM_LIMIT = 48 * 1024 * 1024
BIG_VMEM_LIMIT = 56 * 1024 * 1024

BF16 = jnp.bfloat16
F32 = jnp.float32
_NT = (((1,), (1,)), ((), ()))


def _cparams(*sem, vmem=VMEM_LIMIT):
    return pltpu.CompilerParams(dimension_semantics=sem, vmem_limit_bytes=vmem)


def _bucket_table():
    n = np.arange(1024, dtype=np.int32)
    max_exact = N_BUCKETS // 2
    ratio = np.maximum(n, 1).astype(np.float32) / np.float32(max_exact)
    large = max_exact + (np.log(ratio).astype(np.float32) / np.float32(math.log(MAX_DISTANCE / max_exact))
                         * np.float32(N_BUCKETS - max_exact)).astype(np.int32)
    large = np.minimum(large, N_BUCKETS - 1)
    return np.where(n < max_exact, n, large).astype(np.int32)


def _bias_kernel(tab_ref, near_ref, cmpb_ref, *, steps):
    h = pl.program_id(0)
    far = tab_ref[h, N_BUCKETS - 1]

    def lookup(d):
        val = jnp.full(d.shape, tab_ref[h, 0], F32)
        for start, bucket in steps:
            val = jnp.where(d >= start, tab_ref[h, bucket], val)
        return jnp.where(d >= 0, (val - far) * LOG2E, NEG)

    a = lax.broadcasted_iota(jnp.int32, (TQ, TQ), 0)
    b = lax.broadcasted_iota(jnp.int32, (TQ, TQ), 1)
    near_ref[0, 0] = lookup(a - b)
    near_ref[1, 0] = lookup(a - b + TQ)
    w = lax.broadcasted_iota(jnp.int32, (CMP_WIN, TQ), 0)
    aw = lax.broadcasted_iota(jnp.int32, (CMP_WIN, TQ), 1)
    cmpb_ref[0, 0] = lookup(aw - CMP_STRIDE * w - (CMP_LEN - 1))
    cmpb_ref[1, 0] = lookup(aw - CMP_STRIDE * w + (TQ - CMP_LEN + 1))


def _bias_tiles(rel_bias):
    bucket = _bucket_table()
    steps = tuple((int(n), int(bucket[n])) for n in range(1, bucket.size) if bucket[n] != bucket[n - 1])
    far_start = steps[-1][0]
    assert far_start <= TQ + 1
    assert far_start <= TQ + CMP_STRIDE - (CMP_LEN - 1)
    assert WINDOW == 2 * TQ
    near, cmpb = pl.pallas_call(
        functools.partial(_bias_kernel, steps=steps),
        grid=(N_HEADS,),
        in_specs=[pl.BlockSpec(memory_space=pltpu.SMEM)],
        out_specs=[pl.BlockSpec((2, 1, TQ, TQ), lambda h: (0, h, 0, 0)),
                   pl.BlockSpec((2, 1, CMP_WIN, TQ), lambda h: (0, h, 0, 0))],
        out_shape=[jax.ShapeDtypeStruct((2, N_HEADS, TQ, TQ), F32),
                   jax.ShapeDtypeStruct((2, N_HEADS, CMP_WIN, TQ), F32)],
        compiler_params=_cparams("parallel"),
        name="bias_tiles",
    )(rel_bias.T.astype(F32))
    a = np.arange(TQ)[:, None]
    b = np.arange(TQ)[None, :]
    win2 = jnp.asarray(np.where(a < b, 0.0, NEG).astype(np.float32))
    return near, win2, cmpb


INPROJ_TN = 4 * HEAD_DIM
CKV_TILE = COL_CK * HEAD_DIM // INPROJ_TN


def _inproj_kernel(x_ref, w_ref, cs_ref, wg_ref, o_ref, g_ref, ckv_ref, xb_ref):
    j = pl.program_id(1)

    @pl.when(j == 0)
    def _():
        xb = x_ref[...].astype(BF16)
        xb_ref[...] = xb
        g_ref[...] = jax.nn.sigmoid(lax.dot_general(xb, wg_ref[...], _NT, preferred_element_type=F32))

    acc = lax.dot_general(xb_ref[...], w_ref[...].astype(BF16), _NT, preferred_element_type=F32)
    z = (acc * cs_ref[...]).astype(o_ref.dtype)
    o_ref[...] = z

    @pl.when(j == CKV_TILE)
    def _():
        for c in range(INPROJ_TN // HEAD_DIM):
            ckv_ref[c] = z[:, c * HEAD_DIM:(c + 1) * HEAD_DIM]


def _inproj(x, w_in_t, colscale, w_gate_t, tm=1024):
    m, k = x.shape
    tn = INPROJ_TN
    return pl.pallas_call(
        _inproj_kernel,
        grid=(m // tm, GATE_COL0 // tn),
        in_specs=[pl.BlockSpec((tm, k), lambda i, j: (i, 0)),
                  pl.BlockSpec((tn, k), lambda i, j: (j, 0)),
                  pl.BlockSpec((1, tn), lambda i, j: (0, j)),
                  pl.BlockSpec((HEAD_DIM, k), lambda i, j: (0, 0))],
        out_specs=[pl.BlockSpec((tm, tn), lambda i, j: (i, j)),
                   pl.BlockSpec((tm, HEAD_DIM), lambda i, j: (i, 0)),
                   pl.BlockSpec((tn // HEAD_DIM, tm, HEAD_DIM), lambda i, j: (0, i, 0))],
        out_shape=[jax.ShapeDtypeStruct((m, GATE_COL0), BF16),
                   jax.ShapeDtypeStruct((m, HEAD_DIM), F32),
                   jax.ShapeDtypeStruct((tn // HEAD_DIM, m, HEAD_DIM), BF16)],
        scratch_shapes=[pltpu.VMEM((tm, k), BF16)],
        compiler_params=_cparams("parallel", "arbitrary"),
        name="inproj",
    )(x, w_in_t, colscale, w_gate_t)


def _gelu_tanh(x):
    return 0.5 * x * (1.0 + jnp.tanh(math.sqrt(2.0 / math.pi) * (x + 0.044715 * (x * x * x))))


def _compress_kernel(xk_ref, xv_ref, pek_ref, w1k_ref, w2k_ref, pev_ref, w1v_ref, w2v_ref, kc_ref, vct_ref):
    def compress(x_ref, pe_ref, w1_ref, w2_ref):
        x = x_ref[0].astype(F32)
        lo = jnp.dot((x + pe_ref[0]).astype(BF16), w1_ref[0].astype(BF16), preferred_element_type=F32)
        hi = jnp.dot((x + pe_ref[1]).astype(BF16), w1_ref[1].astype(BF16), preferred_element_type=F32)
        y = lo + pltpu.roll(hi, N_CMP, 0)
        out = jnp.dot(_gelu_tanh(y).astype(BF16), w2_ref[...].astype(BF16), preferred_element_type=F32)
        row = lax.broadcasted_iota(jnp.int32, out.shape, 0)
        return jnp.where(row < N_CMP, out, 0.0)

    kc_ref[0] = compress(xk_ref, pek_ref, w1k_ref, w2k_ref).astype(kc_ref.dtype)
    vct_ref[0] = compress(xv_ref, pev_ref, w1v_ref, w2v_ref).T.astype(vct_ref.dtype)


def _compress(xr, pe_k, w1_k, w2_k, pe_v, w1_v, w2_v):
    half = CMP_STRIDE * HEAD_DIM
    weights = [pl.BlockSpec((2, 1, half), lambda g: (0, 0, 0)),
               pl.BlockSpec((2, half, HEAD_DIM), lambda g: (0, 0, 0)),
               pl.BlockSpec((HEAD_DIM, HEAD_DIM), lambda g: (0, 0))]
    return pl.pallas_call(
        _compress_kernel,
        grid=(NSA_KV_HEADS,),
        in_specs=[pl.BlockSpec((1, 512, half), lambda g: (g, 0, 0)),
                  pl.BlockSpec((1, 512, half), lambda g: (NSA_KV_HEADS + g, 0, 0))] + weights + weights,
        out_specs=[pl.BlockSpec((1, 512, HEAD_DIM), lambda g: (g, 0, 0)),
                   pl.BlockSpec((1, HEAD_DIM, 512), lambda g: (g, 0, 0))],
        out_shape=[jax.ShapeDtypeStruct((NSA_KV_HEADS, 512, HEAD_DIM), BF16),
                   jax.ShapeDtypeStruct((NSA_KV_HEADS, HEAD_DIM, 512), BF16)],
        compiler_params=_cparams("parallel"),
        name="compress",
    )(xr, xr, pe_k, w1_k, w2_k, pe_v, w1_v, w2_v)


def _split3(x):
    hi = x.astype(BF16)
    r1 = x - hi.astype(F32)
    mid = r1.astype(BF16)
    lo = (r1 - mid.astype(F32)).astype(BF16)
    return hi, mid, lo


def _topk_bias_t(score, k):
    row = lax.broadcasted_iota(jnp.int32, score.shape, 0).astype(F32)
    selb = jnp.full(score.shape, NEG, F32)
    s = score
    for _ in range(k):
        mx = jnp.max(s, axis=0, keepdims=True)
        idx = jnp.min(jnp.where(s == mx, row, 1e9), axis=0, keepdims=True)
        hit = row == idx
        selb = jnp.where(hit, 0.0, selb)
        s = jnp.where(hit, -3e38, s)
    return selb


def _lane_tile_max(s):
    m = s[:, :HEAD_DIM]
    for c in range(1, s.shape[1] // HEAD_DIM):
        m = jnp.maximum(m, s[:, c * HEAD_DIM:(c + 1) * HEAD_DIM])
    return m


def _sparse_flash(qs, selbs, k_ref, v_ref, kcols, pat_ref, d0s, d1s, i, sub, scratch):
    t_ref, *maybe_r_ref, mx_ref, acc_ref = scratch
    r_ref = maybe_r_ref[0] if maybe_r_ref else None
    heads = range(len(qs))
    lane = lax.broadcasted_iota(jnp.int32, (TQ, HEAD_DIM), 1)
    far = lane < (i - 1) * sub
    q_near = [jnp.concatenate([qs[h], selbs[h].astype(BF16)], axis=1) for h in heads]
    q_far = [jnp.concatenate([qs[h], jnp.where(far, selbs[h], NEG).astype(BF16)], axis=1) for h in heads]

    def k_aug(h, off, n):
        c = kcols[h]
        return jnp.concatenate([k_ref[pl.ds(off, n), c:c + HEAD_DIM], pat_ref[pl.ds(off, n), :]], axis=1)

    def v_aug(h, off, n):
        c = kcols[h]
        return jnp.concatenate([v_ref[pl.ds(off, n), c:c + HEAD_DIM], jnp.ones((n, HEAD_DIM), BF16)], axis=1)

    for h in heads:
        mx_ref[h] = jnp.full((TQ, HEAD_DIM), NEG, F32)

    def score(g):
        off = pl.multiple_of(g * GROUP, GROUP)
        for h in heads:
            s = lax.dot_general(q_far[h], k_aug(h, off, GROUP), _NT, preferred_element_type=F32)
            ref = _lane_tile_max(s)
            if r_ref is not None:
                s = s - jnp.concatenate([ref] * (GROUP // HEAD_DIM), axis=1)
                r_ref[h, g] = ref
            for u in range(KG):
                t_ref[h, g * KG + u] = s[:, u * TQ:(u + 1) * TQ].astype(t_ref.dtype)
            mx_ref[h] = jnp.maximum(mx_ref[h], ref)

    def run_unrolled(n, body):
        def quad(m, carry):
            body(4 * m, 4)
            return carry

        lax.fori_loop(0, lax.shift_right_logical(n, 2), quad, 0)
        for left in (1, 2, 3):
            @pl.when((n & 3) == left)
            def _(left=left):
                body(n - left, left)

    n_far = lax.shift_right_logical(jnp.maximum(i - 1, 0) + (KG - 1), KG.bit_length() - 1)

    def score_groups(g0, count):
        for k in range(count):
            score(g0 + k)

    run_unrolled(n_far, score_groups)

    off_prev = pl.multiple_of(jnp.maximum(i - 1, 0) * TQ, TQ)
    off_own = pl.multiple_of(i * TQ, TQ)
    p_near = []
    for h in heads:
        k2 = jnp.concatenate([k_aug(h, off_prev, TQ), k_aug(h, off_own, TQ)], axis=0)
        bias = jnp.concatenate([jnp.where(i >= 1, d1s[h], NEG), d0s[h]], axis=1)
        s = lax.dot_general(q_near[h], k2, _NT, preferred_element_type=F32) + bias
        m = jnp.max(jnp.maximum(mx_ref[h], _lane_tile_max(s)), axis=1, keepdims=True)
        mx_ref[h] = jnp.broadcast_to(m, (TQ, HEAD_DIM))
        p_near.append(jnp.exp2(s - m).astype(BF16))
    for h in heads:
        v2 = jnp.concatenate([v_aug(h, off_prev, TQ), v_aug(h, off_own, TQ)], axis=0)
        acc_ref[h] = jnp.dot(p_near[h], v2, preferred_element_type=F32)

    def accumulate(g0, ng):
        off = pl.multiple_of(g0 * GROUP, GROUP)
        for h in heads:
            probs = []
            for g in [g0 + k for k in range(ng)]:
                d = -mx_ref[h] if r_ref is None else r_ref[h, g] - mx_ref[h]
                d2 = jnp.concatenate([d, d], axis=1)
                probs += [jnp.exp2(t_ref[h, g * KG + u].astype(F32) + d2).astype(BF16) for u in range(KG)]
            acc_ref[h] += jnp.dot(jnp.concatenate(probs, axis=1), v_aug(h, off, ng * GROUP),
                                  preferred_element_type=F32)

    def accumulate_groups(g0, count):
        for k in range(0, count, 2):
            accumulate(g0 + k, min(2, count - k))

    run_unrolled(n_far, accumulate_groups)

    outs = []
    for h in heads:
        acc = acc_ref[h]
        outs.append(acc[:, :HEAD_DIM] / acc[:, HEAD_DIM:])
    return outs


def _flash_scratch(hp, relative_bf16):
    logits = [pltpu.VMEM((hp, N_QT, TQ, TQ), BF16),
              pltpu.VMEM((hp, SEQ // GROUP, TQ, HEAD_DIM), F32)]
    if not relative_bf16:
        logits = [pltpu.VMEM((hp, N_QT, TQ, TQ), F32)]
    return logits + [pltpu.VMEM((hp, TQ, HEAD_DIM), F32),
                     pltpu.VMEM((hp, TQ, 2 * HEAD_DIM), F32)]


SEL_CHUNK = 4096


HP_SELECT = 2


def _moba_select_kernel(q_ref, k_ref, sel_ref):
    n_blk = SEQ // MOBA_BLOCK
    blk = lax.broadcasted_iota(jnp.int32, (n_blk, SEQ), 0)
    pos = lax.broadcasted_iota(jnp.int32, (n_blk, SEQ), 1)
    ind = jnp.where(lax.shift_right_logical(pos, 8) == blk, 1.0 / MOBA_BLOCK, 0.0).astype(BF16)
    heads = range(HP_SELECT)
    kmean = [_split3(jnp.dot(ind, k_ref[:, h * HEAD_DIM:(h + 1) * HEAD_DIM], preferred_element_type=F32))
             for h in heads]
    blk_t = lax.broadcasted_iota(jnp.int32, (n_blk, SEL_CHUNK), 0)
    col = lax.broadcasted_iota(jnp.int32, (n_blk, SEL_CHUNK), 1)
    no_block = jnp.full((HEAD_DIM - n_blk, SEL_CHUNK), NEG, F32)

    def chunk(c, carry):
        off = pl.multiple_of(c * SEL_CHUNK, SEL_CHUNK)
        own = lax.shift_right_logical(off + col, 8)
        past = blk_t < own
        for h in heads:
            q = q_ref[pl.ds(off, SEL_CHUNK), h * HEAD_DIM:(h + 1) * HEAD_DIM]
            gate = jnp.zeros((n_blk, SEL_CHUNK), F32)
            for part in kmean[h]:
                gate = gate + lax.dot_general(part, q, _NT, preferred_element_type=F32)
            selb = _topk_bias_t(jnp.where(past, gate, NEG), MOBA_TOPK)
            selb = jnp.where(past, selb, jnp.where(blk_t == own, 0.0, NEG))
            sel_ref[h, pl.ds(off, SEL_CHUNK), :] = jnp.concatenate([selb, no_block], axis=0).T.astype(sel_ref.dtype)
        return carry

    lax.fori_loop(0, SEQ // SEL_CHUNK, chunk, 0)


def _moba_select(zb):
    w = HP_SELECT * HEAD_DIM
    return pl.pallas_call(
        _moba_select_kernel,
        grid=(MOBA_HEADS // HP_SELECT,),
        in_specs=[pl.BlockSpec((SEQ, w), lambda h: (0, COL_MQ // HP_SELECT + h)),
                  pl.BlockSpec((SEQ, w), lambda h: (0, COL_MK // HP_SELECT + h))],
        out_specs=pl.BlockSpec((HP_SELECT, SEQ, HEAD_DIM), lambda h: (h, 0, 0)),
        out_shape=jax.ShapeDtypeStruct((MOBA_HEADS, SEQ, HEAD_DIM), BF16),
        compiler_params=_cparams("parallel"),
        name="moba_select",
    )(zb, zb)


def _moba_kernel(q_ref, sel_ref, k_ref, v_ref, pat_ref, d0_ref, d1_ref, o_ref, *flash_scratch):
    i = pl.program_id(1)
    hp = HP_MOBA
    qs = [q_ref[:, h * HEAD_DIM:(h + 1) * HEAD_DIM] for h in range(hp)]
    selbs = [sel_ref[h].astype(F32) for h in range(hp)]
    cols = [h * HEAD_DIM for h in range(hp)]
    outs = _sparse_flash(qs, selbs, k_ref, v_ref, cols, pat_ref, [d0_ref[0, h] for h in range(hp)],
                         [d1_ref[0, h] for h in range(hp)], i, 1, flash_scratch)
    for h in range(hp):
        o_ref[:, h * HEAD_DIM:(h + 1) * HEAD_DIM] = outs[h].astype(o_ref.dtype)


def _moba(zb, selb, near, pat):
    hp = HP_MOBA
    w = hp * HEAD_DIM
    return pl.pallas_call(
        _moba_kernel,
        grid=(MOBA_HEADS // hp, N_QT),
        in_specs=[pl.BlockSpec((TQ, w), lambda h, i: (i, COL_MQ // hp + h)),
                  pl.BlockSpec((hp, TQ, HEAD_DIM), lambda h, i: (h, i, 0)),
                  pl.BlockSpec((SEQ, w), lambda h, i: (0, COL_MK // hp + h), pipeline_mode=pl.Buffered(1)),
                  pl.BlockSpec((SEQ, w), lambda h, i: (0, COL_MV // hp + h), pipeline_mode=pl.Buffered(1)),
                  pl.BlockSpec((SEQ, HEAD_DIM), lambda h, i: (0, 0), pipeline_mode=pl.Buffered(1)),
                  pl.BlockSpec((1, hp, TQ, TQ), lambda h, i: (0, h, 0, 0)),
                  pl.BlockSpec((1, hp, TQ, TQ), lambda h, i: (1, h, 0, 0))],
        out_specs=pl.BlockSpec((TQ, w), lambda h, i: (i, h)),
        out_shape=jax.ShapeDtypeStruct((SEQ, MOBA_HEADS * HEAD_DIM), BF16),
        scratch_shapes=_flash_scratch(hp, True),
        compiler_params=_cparams("parallel", "arbitrary", vmem=BIG_VMEM_LIMIT),
        name="moba",
    )(zb, selb, zb, zb, pat, near, near)


def _cmp_kernel(q_ref, kc_ref, vct_ref, ovt_ref, cb_ref, oc_ref, sel_ref, s_ref):
    i = pl.program_id(0)
    win0 = pl.multiple_of(jnp.maximum(i * CMP_STRIDE - CMP_STRIDE, 0), CMP_STRIDE)

    def body(nk):
        nb = nk * CMP_STRIDE // SLC_LEN
        blk = lax.broadcasted_iota(jnp.int32, (nb, TQ), 0)
        cur = i * SLC_PER_TILE + lax.shift_right_logical(lax.broadcasted_iota(jnp.int32, (nb, TQ), 1), 6)
        allowed = blk <= cur
        forced = (blk == 0) | (blk == cur) | (blk == cur - 1)
        reachable = lax.broadcasted_iota(jnp.int32, (nk, TQ), 0) < win0 + CMP_WIN
        for g in range(NSA_KV_HEADS):
            p_sum = jnp.zeros((nk, TQ), F32)
            for r in range(NSA_GROUP):
                hd = g * NSA_GROUP + r
                q = q_ref[:, hd * HEAD_DIM:(hd + 1) * HEAD_DIM]
                s_ref[:nk, :] = lax.dot_general(kc_ref[g, :nk, :], q, _NT, preferred_element_type=F32)
                s_ref[pl.ds(win0, CMP_WIN), :] += cb_ref[0, hd]
                s = jnp.where(reachable, s_ref[:nk, :], NEG)
                mx = jnp.max(s, axis=0, keepdims=True)
                e = jnp.exp2(s - mx)
                seen = mx > 0.5 * NEG
                p = e * jnp.where(seen, 1.0 / jnp.maximum(jnp.sum(e, axis=0, keepdims=True), TINY), 0.0)
                o_t = jnp.dot(vct_ref[g, :, :nk], p.astype(BF16), preferred_element_type=F32)
                oc_ref[:, hd * HEAD_DIM:(hd + 1) * HEAD_DIM] = o_t.T
                p_sum = p_sum + p
            imp = jnp.zeros((nb, TQ), F32)
            for part in _split3(p_sum):
                imp = imp + jnp.dot(ovt_ref[:nb, :nk], part, preferred_element_type=F32)
            picked = _topk_bias_t(jnp.where(allowed & ~forced, imp, -1.0), SLC_TOPK - 3)
            selb = jnp.where(allowed, jnp.where(forced, 0.0, picked), NEG)
            if nb < N_SLC:
                selb = jnp.concatenate([selb, jnp.full((N_SLC - nb, TQ), NEG, F32)], axis=0)
            sel_ref[g] = selb.T.astype(sel_ref.dtype)

    for c in range(1, 512 // HEAD_DIM + 1):
        @pl.when(lax.shift_right_logical(i, 3) == c - 1)
        def _():
            body(c * HEAD_DIM)


def _cmp_select(zb, kc, vct, ovt, cmpb):
    w = NSA_HEADS * HEAD_DIM
    return pl.pallas_call(
        _cmp_kernel,
        grid=(N_QT,),
        in_specs=[pl.BlockSpec((TQ, w), lambda i: (i, COL_NQ // NSA_HEADS)),
                  pl.BlockSpec((NSA_KV_HEADS, 512, HEAD_DIM), lambda i: (0, 0, 0)),
                  pl.BlockSpec((NSA_KV_HEADS, HEAD_DIM, 512), lambda i: (0, 0, 0)),
                  pl.BlockSpec((N_SLC, 512), lambda i: (0, 0)),
                  pl.BlockSpec((1, NSA_HEADS, CMP_WIN, TQ), lambda i: (jnp.minimum(i, 1), 1, 0, 0))],
        out_specs=[pl.BlockSpec((TQ, w), lambda i: (i, 0)),
                   pl.BlockSpec((NSA_KV_HEADS, TQ, N_SLC), lambda i: (0, i, 0))],
        out_shape=[jax.ShapeDtypeStruct((SEQ, w), F32),
                   jax.ShapeDtypeStruct((NSA_KV_HEADS, SEQ, N_SLC), BF16)],
        scratch_shapes=[pltpu.VMEM((512, TQ), F32)],
        compiler_params=_cparams("arbitrary"),
        name="cmp_select",
    )(zb, kc, vct, ovt, cmpb)


def _slc_kernel(q_ref, sel_ref, k_ref, v_ref, pat_ref, d0_ref, d1_ref, o_ref, *flash_scratch):
    i = pl.program_id(1)
    hp = HP_SLC
    selb = sel_ref[0].astype(F32)
    qs = [q_ref[:, h * HEAD_DIM:(h + 1) * HEAD_DIM] for h in range(hp)]
    outs = _sparse_flash(qs, [selb] * hp, k_ref, v_ref, [0] * hp, pat_ref,
                         [d0_ref[0, h] for h in range(hp)], [d1_ref[0, h] for h in range(hp)],
                         i, SLC_PER_TILE, flash_scratch)
    for h in range(hp):
        o_ref[:, h * HEAD_DIM:(h + 1) * HEAD_DIM] = outs[h].astype(o_ref.dtype)


def _slc(zb, selb, near, pat):
    hp = HP_SLC
    w = hp * HEAD_DIM
    per_group = NSA_GROUP // hp
    return pl.pallas_call(
        _slc_kernel,
        grid=(NSA_HEADS // hp, N_QT),
        in_specs=[pl.BlockSpec((TQ, w), lambda h, i: (i, COL_NQ // hp + h)),
                  pl.BlockSpec((1, TQ, N_SLC), lambda h, i: (h // per_group, i, 0)),
                  pl.BlockSpec((SEQ, HEAD_DIM), lambda h, i: (0, COL_SK + h // per_group),
                               pipeline_mode=pl.Buffered(1)),
                  pl.BlockSpec((SEQ, HEAD_DIM), lambda h, i: (0, COL_SV + h // per_group),
                               pipeline_mode=pl.Buffered(1)),
                  pl.BlockSpec((SEQ, HEAD_DIM), lambda h, i: (0, 0), pipeline_mode=pl.Buffered(1)),
                  pl.BlockSpec((1, hp, TQ, TQ), lambda h, i: (0, MOBA_HEADS // hp + h, 0, 0)),
                  pl.BlockSpec((1, hp, TQ, TQ), lambda h, i: (1, MOBA_HEADS // hp + h, 0, 0))],
        out_specs=pl.BlockSpec((TQ, w), lambda h, i: (i, h)),
        out_shape=jax.ShapeDtypeStruct((SEQ, NSA_HEADS * HEAD_DIM), F32),
        scratch_shapes=_flash_scratch(hp, False),
        compiler_params=_cparams("parallel", "arbitrary", vmem=BIG_VMEM_LIMIT),
        name="slc",
    )(zb, selb, zb, zb, pat, near, near)


def _win_kernel(q_ref, k_ref, v_ref, d0_ref, d1_ref, w2_ref, o_ref):
    ones = jnp.ones((3 * TQ, HEAD_DIM), BF16)
    for t in range(q_ref.shape[0] // TQ):
        i = pl.program_id(0) * (q_ref.shape[0] // TQ) + t
        rows = slice(t * TQ, (t + 1) * TQ)
        offs = [pl.multiple_of(jnp.maximum(i - b, 0) * TQ, TQ) for b in range(3)]
        mask2 = jnp.where(i >= 2, w2_ref[...], NEG)
        for g in range(NSA_KV_HEADS):
            cols = slice(g * HEAD_DIM, (g + 1) * HEAD_DIM)
            k_all = jnp.concatenate([k_ref[pl.ds(off, TQ), cols] for off in offs], axis=0)
            v_aug = jnp.concatenate(
                [jnp.concatenate([v_ref[pl.ds(off, TQ), cols] for off in offs], axis=0), ones], axis=1)
            for r in range(NSA_GROUP):
                hd = g * NSA_GROUP + r
                q = q_ref[rows, hd * HEAD_DIM:(hd + 1) * HEAD_DIM]
                bias = jnp.concatenate([d0_ref[0, hd], jnp.where(i >= 1, d1_ref[0, hd], NEG), mask2], axis=1)
                s = lax.dot_general(q, k_all, _NT, preferred_element_type=F32) + bias
                m = jnp.max(s, axis=1, keepdims=True)
                pv = jnp.dot(jnp.exp2(s - m).astype(BF16), v_aug, preferred_element_type=F32)
                o_ref[rows, hd * HEAD_DIM:(hd + 1) * HEAD_DIM] = pv[:, :HEAD_DIM] / pv[:, HEAD_DIM:]


def _win(zb, near, win2):
    w = NSA_HEADS * HEAD_DIM
    kv_w = NSA_KV_HEADS * HEAD_DIM
    tiles = 2
    return pl.pallas_call(
        _win_kernel,
        grid=(N_QT // tiles,),
        in_specs=[pl.BlockSpec((tiles * TQ, w), lambda i: (i, COL_NQ // NSA_HEADS)),
                  pl.BlockSpec((SEQ, kv_w), lambda i: (0, COL_WK // NSA_KV_HEADS), pipeline_mode=pl.Buffered(1)),
                  pl.BlockSpec((SEQ, kv_w), lambda i: (0, COL_WV // NSA_KV_HEADS), pipeline_mode=pl.Buffered(1)),
                  pl.BlockSpec((1, NSA_HEADS, TQ, TQ), lambda i: (0, 1, 0, 0), pipeline_mode=pl.Buffered(1)),
                  pl.BlockSpec((1, NSA_HEADS, TQ, TQ), lambda i: (1, 1, 0, 0), pipeline_mode=pl.Buffered(1)),
                  pl.BlockSpec((TQ, TQ), lambda i: (0, 0), pipeline_mode=pl.Buffered(1))],
        out_specs=pl.BlockSpec((tiles * TQ, w), lambda i: (i, 0)),
        out_shape=jax.ShapeDtypeStruct((SEQ, w), F32),
        compiler_params=_cparams("arbitrary"),
        name="win",
    )(zb, zb, zb, near, near, win2)


def _layer_norm(y, g, b):
    mu = jnp.mean(y, axis=-1, keepdims=True)
    yc = y - mu
    var = jnp.mean(yc * yc, axis=-1, keepdims=True)
    return yc * lax.rsqrt(var + LN_EPS) * g + b


ROW_CHUNK = 256


def _outproj_kernel(x_ref, om_ref, oc_ref, os_ref, ow_ref, g_ref, e_ref, w_ref, lg_ref, lb_ref,
                    h_ref, hb_ref):
    for c0 in range(0, x_ref.shape[0], ROW_CHUNK):
        rows = slice(c0, c0 + ROW_CHUNK)
        gate = g_ref[rows, :]
        hi = gate.astype(BF16)
        lo = (gate - hi.astype(F32)).astype(BF16)
        g2 = jnp.concatenate([hi, lo], axis=1)
        o_n = jnp.zeros((ROW_CHUNK, oc_ref.shape[1]), F32)
        for c, branch in enumerate((oc_ref, os_ref, ow_ref)):
            o_n = o_n + jnp.dot(g2, e_ref[c], preferred_element_type=F32) * branch[rows, :]
        o = jnp.concatenate([om_ref[rows, :], o_n.astype(BF16)], axis=1)
        a = jnp.dot(o, w_ref[...], preferred_element_type=F32)
        h = _layer_norm(DN_ALPHA * x_ref[rows, :] + a, lg_ref[...], lb_ref[...])
        h_ref[rows, :] = h
        hb_ref[rows, :] = h.astype(BF16)


def _resident(shape):
    return pl.BlockSpec(shape, lambda i: (0,) * len(shape), pipeline_mode=pl.Buffered(1))


def _outproj(x, o_m, o_c, o_s, o_w, gates, expand, w_out, ln_g, ln_b, tm=512):
    half = NSA_HEADS * HEAD_DIM
    row = lambda i: (i, 0)
    return pl.pallas_call(
        _outproj_kernel,
        grid=(SEQ // tm,),
        in_specs=[pl.BlockSpec((tm, D_MODEL), row),
                  pl.BlockSpec((tm, half), row), pl.BlockSpec((tm, half), row),
                  pl.BlockSpec((tm, half), row), pl.BlockSpec((tm, half), row),
                  pl.BlockSpec((tm, HEAD_DIM), row),
                  _resident((3, 2 * HEAD_DIM, half)),
                  _resident((D_MODEL, D_MODEL)),
                  _resident((1, D_MODEL)), _resident((1, D_MODEL))],
        out_specs=[pl.BlockSpec((tm, D_MODEL), row), pl.BlockSpec((tm, D_MODEL), row)],
        out_shape=[jax.ShapeDtypeStruct((SEQ, D_MODEL), F32),
                   jax.ShapeDtypeStruct((SEQ, D_MODEL), BF16)],
        compiler_params=_cparams("parallel", vmem=BIG_VMEM_LIMIT),
        name="outproj_ln",
    )(x, o_m, o_c, o_s, o_w, gates, expand, w_out, ln_g, ln_b)


def _ffn_kernel(hb_ref, w1_ref, w2_ref, f_ref):
    @pl.when(pl.program_id(1) == 0)
    def _():
        f_ref[...] = jnp.zeros(f_ref.shape, F32)

    u = jnp.maximum(jnp.dot(hb_ref[...], w1_ref[...].astype(BF16), preferred_element_type=F32), 0.0)
    f_ref[...] += jnp.dot((u * u).astype(BF16), w2_ref[...].astype(BF16), preferred_element_type=F32)


def _ffn(hb, w1, w2, tm=1024, tf=512):
    return pl.pallas_call(
        _ffn_kernel,
        grid=(SEQ // tm, D_FF // tf),
        in_specs=[pl.BlockSpec((tm, D_MODEL), lambda i, c: (i, 0)),
                  pl.BlockSpec((D_MODEL, tf), lambda i, c: (0, c)),
                  pl.BlockSpec((tf, D_MODEL), lambda i, c: (c, 0))],
        out_specs=pl.BlockSpec((tm, D_MODEL), lambda i, c: (i, 0)),
        out_shape=jax.ShapeDtypeStruct((SEQ, D_MODEL), F32),
        compiler_params=_cparams("parallel", "arbitrary", vmem=BIG_VMEM_LIMIT),
        name="ffn",
    )(hb, w1, w2)


def _final_kernel(h_ref, hb_ref, f_ref, p_ref, wg_ref, wp_ref, lg_ref, lb_ref, o_ref):
    for c0 in range(0, h_ref.shape[0], ROW_CHUNK):
        rows = slice(c0, c0 + ROW_CHUNK)
        gate = jax.nn.sigmoid(jnp.dot(hb_ref[rows, :], wg_ref[...], preferred_element_type=F32))
        emb = jnp.dot(p_ref[rows, :].astype(BF16), wp_ref[...], preferred_element_type=F32)
        y = DN_ALPHA * h_ref[rows, :] + f_ref[rows, :] + gate * emb
        o_ref[rows, :] = _layer_norm(y, lg_ref[...], lb_ref[...])


def _final(h, hb, f, p, wg, wp, ln_g, ln_b, tm=512):
    row = lambda i: (i, 0)
    return pl.pallas_call(
        _final_kernel,
        grid=(SEQ // tm,),
        in_specs=[pl.BlockSpec((tm, D_MODEL), row), pl.BlockSpec((tm, D_MODEL), row),
                  pl.BlockSpec((tm, D_MODEL), row), pl.BlockSpec((tm, D_PLE), row),
                  _resident((D_MODEL, D_MODEL)), _resident((D_PLE, D_MODEL)),
                  _resident((1, D_MODEL)), _resident((1, D_MODEL))],
        out_specs=pl.BlockSpec((tm, D_MODEL), row),
        out_shape=jax.ShapeDtypeStruct((SEQ, D_MODEL), F32),
        compiler_params=_cparams("parallel"),
        name="final_ln",
    )(h, hb, f, p, wg, wp, ln_g, ln_b)


def _overlap_matrix_t():
    j = np.arange(N_SLC)[:, None]
    n = np.arange(512)[None, :]
    ov = ((n * CMP_STRIDE < j * SLC_LEN + SLC_LEN) & (n * CMP_STRIDE + CMP_LEN > j * SLC_LEN) & (n < N_CMP))
    return ov.astype(np.float32)


def _gate_expand():
    e = np.zeros((3, 2 * HEAD_DIM, NSA_HEADS * HEAD_DIM), np.float32)
    for c in range(3):
        for h in range(NSA_HEADS):
            e[c, 3 * h + c, h * HEAD_DIM:(h + 1) * HEAD_DIM] = 1.0
            e[c, HEAD_DIM + 3 * h + c, h * HEAD_DIM:(h + 1) * HEAD_DIM] = 1.0
    return e


def _layer(h, p, w_in, cmp_pe_k, cmp_w1_k, cmp_w2_k, cmp_pe_v, cmp_w1_v, cmp_w2_v, tiles, w_out,
           ln1_g, ln1_b, w_ff1, w_ff2, w_ple, w_ple_gate, ln2_g, ln2_b):
    near, win2, cmpb = tiles
    scale = HEAD_DIM ** -0.5 * LOG2E
    colscale = np.ones((1, GATE_COL0), np.float32)
    colscale[0, COL_MQ * HEAD_DIM:(COL_MQ + MOBA_HEADS) * HEAD_DIM] = scale
    colscale[0, COL_NQ * HEAD_DIM:(COL_NQ + NSA_HEADS) * HEAD_DIM] = scale
    w_in_t = w_in.T
    w_gl_t = jnp.pad(w_in_t[GATE_COL0:], ((0, HEAD_DIM - NSA_HEADS * 3), (0, 0))).astype(BF16)
    zb, gates, ckv = _inproj(h, w_in_t, jnp.asarray(colscale), w_gl_t)

    half = CMP_STRIDE * HEAD_DIM
    xr = ckv.reshape(4, SEQ // CMP_STRIDE, half)
    kc, vct = _compress(xr, cmp_pe_k.reshape(2, 1, half), cmp_w1_k.reshape(2, half, HEAD_DIM), cmp_w2_k,
                        cmp_pe_v.reshape(2, 1, half), cmp_w1_v.reshape(2, half, HEAD_DIM), cmp_w2_v)

    key = np.arange(SEQ)[:, None]
    lane = np.arange(HEAD_DIM)[None, :]
    pat_moba = jnp.asarray(lane == key // MOBA_BLOCK, BF16)
    pat_slc = jnp.asarray(lane == key // SLC_LEN, BF16)
    o_m = _moba(zb, _moba_select(zb), near, pat_moba)
    o_c, selb = _cmp_select(zb, kc, vct, jnp.asarray(_overlap_matrix_t(), BF16), cmpb)
    o_s = _slc(zb, selb, near, pat_slc)
    o_w = _win(zb, near, win2)

    h1, h1b = _outproj(h, o_m, o_c, o_s, o_w, gates, jnp.asarray(_gate_expand(), BF16),
                       w_out.astype(BF16), ln1_g.reshape(1, -1), ln1_b.reshape(1, -1))
    f = _ffn(h1b, w_ff1, w_ff2)
    return _final(h1, h1b, f, p, w_ple_gate.astype(BF16), w_ple.astype(BF16),
                  ln2_g.reshape(1, -1), ln2_b.reshape(1, -1))


def kernel(x, p, w_in, cmp_pe_k, cmp_w1_k, cmp_w2_k, cmp_pe_v, cmp_w1_v, cmp_w2_v, rel_bias, w_out,
           ln1_g, ln1_b, w_ff1, w_ff2, w_ple, w_ple_gate, ln2_g, ln2_b):
    tiles = _bias_tiles(rel_bias)
    h = x[0]
    for i in range(w_in.shape[0]):
        h = _layer(h, p[i, 0], w_in[i], cmp_pe_k[i], cmp_w1_k[i], cmp_w2_k[i], cmp_pe_v[i], cmp_w1_v[i],
                   cmp_w2_v[i], tiles, w_out[i], ln1_g[i], ln1_b[i], w_ff1[i], w_ff2[i], w_ple[i],
                   w_ple_gate[i], ln2_g[i], ln2_b[i])
    return h[None]
```

```python
import functools
import math

import numpy as np
import jax
import jax.numpy as jnp
from jax import lax
from jax.experimental import pallas as pl
from jax.experimental.pallas import tpu as pltpu

D_MODEL = 2048
SEQ = 8192
HEAD_DIM = 128
N_HEADS = 16
MOBA_HEADS = 8
NSA_HEADS = 8
NSA_KV_HEADS = 2
NSA_GROUP = 4
MOBA_BLOCK = 256
MOBA_TOPK = 3
CMP_LEN = 32
CMP_STRIDE = 16
SLC_LEN = 64
SLC_TOPK = 16
WINDOW = 512
N_BUCKETS = 32
MAX_DISTANCE = 128
D_FF = 4 * D_MODEL
D_PLE = 256
LN_EPS = 1e-5
DN_ALPHA = 2.0 ** 0.25
NEG = -1e30
LOG2E = math.log2(math.e)
TINY = 1e-30
FORCE_SCORE = 1e4

N_CMP = (SEQ - CMP_LEN) // CMP_STRIDE + 1
N_SLC = SEQ // SLC_LEN
TQ = 256
N_QT = SEQ // TQ
CMP_WIN = 32
SLC_PER_TILE = TQ // SLC_LEN
HP_MOBA = 4
HP_SLC = 4
KG = 4
GROUP = KG * TQ
COL_MQ, COL_MK, COL_MV, COL_NQ = 0, 8, 16, 24
COL_CK, COL_CV, COL_SK, COL_SV, COL_WK, COL_WV = 32, 34, 36, 38, 40, 42
N_COLBLK = 44
GATE_COL0 = N_COLBLK * HEAD_DIM

VMEM_LIMIT = 48 * 1024 * 1024
BIG_VMEM_LIMIT = 56 * 1024 * 1024

BF16 = jnp.bfloat16
F32 = jnp.float32
_NT = (((1,), (1,)), ((), ()))


def _cparams(*sem, vmem=VMEM_LIMIT):
    return pltpu.CompilerParams(dimension_semantics=sem, vmem_limit_bytes=vmem)


def _bucket_table():
    n = np.arange(1024, dtype=np.int32)
    max_exact = N_BUCKETS // 2
    ratio = np.maximum(n, 1).astype(np.float32) / np.float32(max_exact)
    large = max_exact + (np.log(ratio).astype(np.float32) / np.float32(math.log(MAX_DISTANCE / max_exact))
                         * np.float32(N_BUCKETS - max_exact)).astype(np.int32)
    large = np.minimum(large, N_BUCKETS - 1)
    return np.where(n < max_exact, n, large).astype(np.int32)


def _bias_kernel(tab_ref, near_ref, cmpb_ref, *, steps):
    h = pl.program_id(0)
    far = tab_ref[h, N_BUCKETS - 1]

    def lookup(d):
        val = jnp.full(d.shape, tab_ref[h, 0], F32)
        for start, bucket in steps:
            val = jnp.where(d >= start, tab_ref[h, bucket], val)
        return jnp.where(d >= 0, (val - far) * LOG2E, NEG)

    a = lax.broadcasted_iota(jnp.int32, (TQ, TQ), 0)
    b = lax.broadcasted_iota(jnp.int32, (TQ, TQ), 1)
    near_ref[0, 0] = lookup(a - b)
    near_ref[1, 0] = lookup(a - b + TQ)
    w = lax.broadcasted_iota(jnp.int32, (CMP_WIN, TQ), 0)
    aw = lax.broadcasted_iota(jnp.int32, (CMP_WIN, TQ), 1)
    cmpb_ref[0, 0] = lookup(aw - CMP_STRIDE * w - (CMP_LEN - 1))
    cmpb_ref[1, 0] = lookup(aw - CMP_STRIDE * w + (TQ - CMP_LEN + 1))


def _bias_tiles(rel_bias):
    bucket = _bucket_table()
    steps = tuple((int(n), int(bucket[n])) for n in range(1, bucket.size) if bucket[n] != bucket[n - 1])
    far_start = steps[-1][0]
    assert far_start <= TQ + 1
    assert far_start <= TQ + CMP_STRIDE - (CMP_LEN - 1)
    assert WINDOW == 2 * TQ
    near, cmpb = pl.pallas_call(
        functools.partial(_bias_kernel, steps=steps),
        grid=(N_HEADS,),
        in_specs=[pl.BlockSpec(memory_space=pltpu.SMEM)],
        out_specs=[pl.BlockSpec((2, 1, TQ, TQ), lambda h: (0, h, 0, 0)),
                   pl.BlockSpec((2, 1, CMP_WIN, TQ), lambda h: (0, h, 0, 0))],
        out_shape=[jax.ShapeDtypeStruct((2, N_HEADS, TQ, TQ), F32),
                   jax.ShapeDtypeStruct((2, N_HEADS, CMP_WIN, TQ), F32)],
        compiler_params=_cparams("parallel"),
        name="bias_tiles",
    )(rel_bias.T.astype(F32))
    a = np.arange(TQ)[:, None]
    b = np.arange(TQ)[None, :]
    win2 = jnp.asarray(np.where(a < b, 0.0, NEG).astype(np.float32))
    return near, win2, cmpb


INPROJ_TN = 4 * HEAD_DIM
CKV_TILE = COL_CK * HEAD_DIM // INPROJ_TN


def _inproj_kernel(x_ref, w_ref, cs_ref, wg_ref, o_ref, g_ref, ckv_ref, xb_ref):
    j = pl.program_id(1)

    @pl.when(j == 0)
    def _():
        xb = x_ref[...].astype(BF16)
        xb_ref[...] = xb
        g_ref[...] = jax.nn.sigmoid(lax.dot_general(xb, wg_ref[...], _NT, preferred_element_type=F32))

    acc = lax.dot_general(xb_ref[...], w_ref[...].astype(BF16), _NT, preferred_element_type=F32)
    z = (acc * cs_ref[...]).astype(o_ref.dtype)
    o_ref[...] = z

    @pl.when(j == CKV_TILE)
    def _():
        for c in range(INPROJ_TN // HEAD_DIM):
            ckv_ref[c] = z[:, c * HEAD_DIM:(c + 1) * HEAD_DIM]


def _inproj(x, w_in_t, colscale, w_gate_t, tm=1024):
    m, k = x.shape
    tn = INPROJ_TN
    return pl.pallas_call(
        _inproj_kernel,
        grid=(m // tm, GATE_COL0 // tn),
        in_specs=[pl.BlockSpec((tm, k), lambda i, j: (i, 0)),
                  pl.BlockSpec((tn, k), lambda i, j: (j, 0)),
                  pl.BlockSpec((1, tn), lambda i, j: (0, j)),
                  pl.BlockSpec((HEAD_DIM, k), lambda i, j: (0, 0))],
        out_specs=[pl.BlockSpec((tm, tn), lambda i, j: (i, j)),
                   pl.BlockSpec((tm, HEAD_DIM), lambda i, j: (i, 0)),
                   pl.BlockSpec((tn // HEAD_DIM, tm, HEAD_DIM), lambda i, j: (0, i, 0))],
        out_shape=[jax.ShapeDtypeStruct((m, GATE_COL0), BF16),
                   jax.ShapeDtypeStruct((m, HEAD_DIM), F32),
                   jax.ShapeDtypeStruct((tn // HEAD_DIM, m, HEAD_DIM), BF16)],
        scratch_shapes=[pltpu.VMEM((tm, k), BF16)],
        compiler_params=_cparams("parallel", "arbitrary"),
        name="inproj",
    )(x, w_in_t, colscale, w_gate_t)


def _gelu_tanh(x):
    return 0.5 * x * (1.0 + jnp.tanh(math.sqrt(2.0 / math.pi) * (x + 0.044715 * (x * x * x))))


def _compress_kernel(xk_ref, xv_ref, pek_ref, w1k_ref, w2k_ref, pev_ref, w1v_ref, w2v_ref, kc_ref, vct_ref):
    def compress(x_ref, pe_ref, w1_ref, w2_ref):
        x = x_ref[0].astype(F32)
        lo = jnp.dot((x + pe_ref[0]).astype(BF16), w1_ref[0].astype(BF16), preferred_element_type=F32)
        hi = jnp.dot((x + pe_ref[1]).astype(BF16), w1_ref[1].astype(BF16), preferred_element_type=F32)
        y = lo + pltpu.roll(hi, N_CMP, 0)
        out = jnp.dot(_gelu_tanh(y).astype(BF16), w2_ref[...].astype(BF16), preferred_element_type=F32)
        row = lax.broadcasted_iota(jnp.int32, out.shape, 0)
        return jnp.where(row < N_CMP, out, 0.0)

    kc_ref[0] = compress(xk_ref, pek_ref, w1k_ref, w2k_ref).astype(kc_ref.dtype)
    vct_ref[0] = compress(xv_ref, pev_ref, w1v_ref, w2v_ref).T.astype(vct_ref.dtype)


def _compress(xr, pe_k, w1_k, w2_k, pe_v, w1_v, w2_v):
    half = CMP_STRIDE * HEAD_DIM
    weights = [pl.BlockSpec((2, 1, half), lambda g: (0, 0, 0)),
               pl.BlockSpec((2, half, HEAD_DIM), lambda g: (0, 0, 0)),
               pl.BlockSpec((HEAD_DIM, HEAD_DIM), lambda g: (0, 0))]
    return pl.pallas_call(
        _compress_kernel,
        grid=(NSA_KV_HEADS,),
        in_specs=[pl.BlockSpec((1, 512, half), lambda g: (g, 0, 0)),
                  pl.BlockSpec((1, 512, half), lambda g: (NSA_KV_HEADS + g, 0, 0))] + weights + weights,
        out_specs=[pl.BlockSpec((1, 512, HEAD_DIM), lambda g: (g, 0, 0)),
                   pl.BlockSpec((1, HEAD_DIM, 512), lambda g: (g, 0, 0))],
        out_shape=[jax.ShapeDtypeStruct((NSA_KV_HEADS, 512, HEAD_DIM), BF16),
                   jax.ShapeDtypeStruct((NSA_KV_HEADS, HEAD_DIM, 512), BF16)],
        compiler_params=_cparams("parallel"),
        name="compress",
    )(xr, xr, pe_k, w1_k, w2_k, pe_v, w1_v, w2_v)


def _split3(x):
    hi = x.astype(BF16)
    r1 = x - hi.astype(F32)
    mid = r1.astype(BF16)
    lo = (r1 - mid.astype(F32)).astype(BF16)
    return hi, mid, lo


def _topk_bias_t(score, k):
    row = lax.broadcasted_iota(jnp.int32, score.shape, 0).astype(F32)
    selb = jnp.full(score.shape, NEG, F32)
    s = score
    for _ in range(k):
        mx = jnp.max(s, axis=0, keepdims=True)
        idx = jnp.min(jnp.where(s == mx, row, 1e9), axis=0, keepdims=True)
        hit = row == idx
        selb = jnp.where(hit, 0.0, selb)
        s = jnp.where(hit, -3e38, s)
    return selb


def _lane_tile_max(s):
    m = s[:, :HEAD_DIM]
    for c in range(1, s.shape[1] // HEAD_DIM):
        m = jnp.maximum(m, s[:, c * HEAD_DIM:(c + 1) * HEAD_DIM])
    return m


def _sparse_flash(qs, selbs, k_ref, v_ref, kcols, pat_ref, d0s, d1s, i, sub, scratch):
    t_ref, *maybe_r_ref, mx_ref, acc_ref = scratch
    r_ref = maybe_r_ref[0] if maybe_r_ref else None
    heads = range(len(qs))
    lane = lax.broadcasted_iota(jnp.int32, (TQ, HEAD_DIM), 1)
    far = lane < (i - 1) * sub
    q_near = [jnp.concatenate([qs[h], selbs[h].astype(BF16)], axis=1) for h in heads]
    q_far = [jnp.concatenate([qs[h], jnp.where(far, selbs[h], NEG).astype(BF16)], axis=1) for h in heads]

    def k_aug(h, off, n):
        c = kcols[h]
        return jnp.concatenate([k_ref[pl.ds(off, n), c:c + HEAD_DIM], pat_ref[pl.ds(off, n), :]], axis=1)

    def v_aug(h, off, n):
        c = kcols[h]
        return jnp.concatenate([v_ref[pl.ds(off, n), c:c + HEAD_DIM], jnp.ones((n, HEAD_DIM), BF16)], axis=1)

    for h in heads:
        mx_ref[h] = jnp.full((TQ, HEAD_DIM), NEG, F32)

    def score(g):
        off = pl.multiple_of(g * GROUP, GROUP)
        for h in heads:
            s = lax.dot_general(q_far[h], k_aug(h, off, GROUP), _NT, preferred_element_type=F32)
            ref = _lane_tile_max(s)
            if r_ref is not None:
                s = s - jnp.concatenate([ref] * (GROUP // HEAD_DIM), axis=1)
                r_ref[h, g] = ref
            for u in range(KG):
                t_ref[h, g * KG + u] = s[:, u * TQ:(u + 1) * TQ].astype(t_ref.dtype)
            mx_ref[h] = jnp.maximum(mx_ref[h], ref)

    def run_unrolled(n, body):
        def quad(m, carry):
            body(4 * m, 4)
            return carry

        lax.fori_loop(0, lax.shift_right_logical(n, 2), quad, 0)
        for left in (1, 2, 3):
            @pl.when((n & 3) == left)
            def _(left=left):
                body(n - left, left)

    n_far = lax.shift_right_logical(jnp.maximum(i - 1, 0) + (KG - 1), KG.bit_length() - 1)

    def score_groups(g0, count):
        for k in range(count):
            score(g0 + k)

    run_unrolled(n_far, score_groups)

    off_prev = pl.multiple_of(jnp.maximum(i - 1, 0) * TQ, TQ)
    off_own = pl.multiple_of(i * TQ, TQ)
    p_near = []
    for h in heads:
        k2 = jnp.concatenate([k_aug(h, off_prev, TQ), k_aug(h, off_own, TQ)], axis=0)
        bias = jnp.concatenate([jnp.where(i >= 1, d1s[h], NEG), d0s[h]], axis=1)
        s = lax.dot_general(q_near[h], k2, _NT, preferred_element_type=F32) + bias
        m = jnp.max(jnp.maximum(mx_ref[h], _lane_tile_max(s)), axis=1, keepdims=True)
        mx_ref[h] = jnp.broadcast_to(m, (TQ, HEAD_DIM))
        p_near.append(jnp.exp2(s - m).astype(BF16))
    for h in heads:
        v2 = jnp.concatenate([v_aug(h, off_prev, TQ), v_aug(h, off_own, TQ)], axis=0)
        acc_ref[h] = jnp.dot(p_near[h], v2, preferred_element_type=F32)

    def accumulate(g0, ng):
        off = pl.multiple_of(g0 * GROUP, GROUP)
        for h in heads:
            probs = []
            for g in [g0 + k for k in range(ng)]:
                d = -mx_ref[h] if r_ref is None else r_ref[h, g] - mx_ref[h]
                d2 = jnp.concatenate([d, d], axis=1)
                probs += [jnp.exp2(t_ref[h, g * KG + u].astype(F32) + d2).astype(BF16) for u in range(KG)]
            acc_ref[h] += jnp.dot(jnp.concatenate(probs, axis=1), v_aug(h, off, ng * GROUP),
                                  preferred_element_type=F32)

    def accumulate_groups(g0, count):
        for k in range(0, count, 2):
            accumulate(g0 + k, min(2, count - k))

    run_unrolled(n_far, accumulate_groups)

    outs = []
    for h in heads:
        acc = acc_ref[h]
        outs.append(acc[:, :HEAD_DIM] / acc[:, HEAD_DIM:])
    return outs


def _flash_scratch(hp, relative_bf16):
    logits = [pltpu.VMEM((hp, N_QT, TQ, TQ), BF16),
              pltpu.VMEM((hp, SEQ // GROUP, TQ, HEAD_DIM), F32)]
    if not relative_bf16:
        logits = [pltpu.VMEM((hp, N_QT, TQ, TQ), F32)]
    return logits + [pltpu.VMEM((hp, TQ, HEAD_DIM), F32),
                     pltpu.VMEM((hp, TQ, 2 * HEAD_DIM), F32)]


SEL_CHUNK = 4096


HP_SELECT = 2


def _moba_select_kernel(q_ref, k_ref, sel_ref):
    n_blk = SEQ // MOBA_BLOCK
    blk = lax.broadcasted_iota(jnp.int32, (n_blk, SEQ), 0)
    pos = lax.broadcasted_iota(jnp.int32, (n_blk, SEQ), 1)
    ind = jnp.where(lax.shift_right_logical(pos, 8) == blk, 1.0 / MOBA_BLOCK, 0.0).astype(BF16)
    heads = range(HP_SELECT)
    kmean = [_split3(jnp.dot(ind, k_ref[:, h * HEAD_DIM:(h + 1) * HEAD_DIM], preferred_element_type=F32))
             for h in heads]
    blk_t = lax.broadcasted_iota(jnp.int32, (n_blk, SEL_CHUNK), 0)
    col = lax.broadcasted_iota(jnp.int32, (n_blk, SEL_CHUNK), 1)
    no_block = jnp.full((HEAD_DIM - n_blk, SEL_CHUNK), NEG, F32)

    def chunk(c, carry):
        off = pl.multiple_of(c * SEL_CHUNK, SEL_CHUNK)
        own = lax.shift_right_logical(off + col, 8)
        past = blk_t < own
        for h in heads:
            q = q_ref[pl.ds(off, SEL_CHUNK), h * HEAD_DIM:(h + 1) * HEAD_DIM]
            gate = jnp.zeros((n_blk, SEL_CHUNK), F32)
            for part in kmean[h]:
                gate = gate + lax.dot_general(part, q, _NT, preferred_element_type=F32)
            selb = _topk_bias_t(jnp.where(past, gate, NEG), MOBA_TOPK)
            selb = jnp.where(past, selb, jnp.where(blk_t == own, 0.0, NEG))
            sel_ref[h, pl.ds(off, SEL_CHUNK), :] = jnp.concatenate([selb, no_block], axis=0).T.astype(sel_ref.dtype)
        return carry

    lax.fori_loop(0, SEQ // SEL_CHUNK, chunk, 0)


def _moba_select(zb):
    w = HP_SELECT * HEAD_DIM
    return pl.pallas_call(
        _moba_select_kernel,
        grid=(MOBA_HEADS // HP_SELECT,),
        in_specs=[pl.BlockSpec((SEQ, w), lambda h: (0, COL_MQ // HP_SELECT + h)),
                  pl.BlockSpec((SEQ, w), lambda h: (0, COL_MK // HP_SELECT + h))],
        out_specs=pl.BlockSpec((HP_SELECT, SEQ, HEAD_DIM), lambda h: (h, 0, 0)),
        out_shape=jax.ShapeDtypeStruct((MOBA_HEADS, SEQ, HEAD_DIM), BF16),
        compiler_params=_cparams("parallel"),
        name="moba_select",
    )(zb, zb)


def _moba_kernel(q_ref, sel_ref, k_ref, v_ref, pat_ref, d0_ref, d1_ref, o_ref, *flash_scratch):
    i = pl.program_id(1)
    hp = HP_MOBA
    qs = [q_ref[:, h * HEAD_DIM:(h + 1) * HEAD_DIM] for h in range(hp)]
    selbs = [sel_ref[h].astype(F32) for h in range(hp)]
    cols = [h * HEAD_DIM for h in range(hp)]
    outs = _sparse_flash(qs, selbs, k_ref, v_ref, cols, pat_ref, [d0_ref[0, h] for h in range(hp)],
                         [d1_ref[0, h] for h in range(hp)], i, 1, flash_scratch)
    for h in range(hp):
        o_ref[:, h * HEAD_DIM:(h + 1) * HEAD_DIM] = outs[h].astype(o_ref.dtype)


def _moba(zb, selb, near, pat):
    hp = HP_MOBA
    w = hp * HEAD_DIM
    return pl.pallas_call(
        _moba_kernel,
        grid=(MOBA_HEADS // hp, N_QT),
        in_specs=[pl.BlockSpec((TQ, w), lambda h, i: (i, COL_MQ // hp + h)),
                  pl.BlockSpec((hp, TQ, HEAD_DIM), lambda h, i: (h, i, 0)),
                  pl.BlockSpec((SEQ, w), lambda h, i: (0, COL_MK // hp + h), pipeline_mode=pl.Buffered(1)),
                  pl.BlockSpec((SEQ, w), lambda h, i: (0, COL_MV // hp + h), pipeline_mode=pl.Buffered(1)),
                  pl.BlockSpec((SEQ, HEAD_DIM), lambda h, i: (0, 0), pipeline_mode=pl.Buffered(1)),
                  pl.BlockSpec((1, hp, TQ, TQ), lambda h, i: (0, h, 0, 0)),
                  pl.BlockSpec((1, hp, TQ, TQ), lambda h, i: (1, h, 0, 0))],
        out_specs=pl.BlockSpec((TQ, w), lambda h, i: (i, h)),
        out_shape=jax.ShapeDtypeStruct((SEQ, MOBA_HEADS * HEAD_DIM), BF16),
        scratch_shapes=_flash_scratch(hp, True),
        compiler_params=_cparams("parallel", "arbitrary", vmem=BIG_VMEM_LIMIT),
        name="moba",
    )(zb, selb, zb, zb, pat, near, near)


def _cmp_kernel(q_ref, kc_ref, vct_ref, ovt_ref, cb_ref, oc_ref, sel_ref, s_ref):
    i = pl.program_id(0)
    win0 = pl.multiple_of(jnp.maximum(i * CMP_STRIDE - CMP_STRIDE, 0), CMP_STRIDE)

    def body(nk):
        nb = nk * CMP_STRIDE // SLC_LEN
        blk = lax.broadcasted_iota(jnp.int32, (nb, TQ), 0)
        cur = i * SLC_PER_TILE + lax.shift_right_logical(lax.broadcasted_iota(jnp.int32, (nb, TQ), 1), 6)
        allowed = blk <= cur
        forced = (blk == 0) | (blk == cur) | (blk == cur - 1)
        reachable = lax.broadcasted_iota(jnp.int32, (nk, TQ), 0) < win0 + CMP_WIN
        for g in range(NSA_KV_HEADS):
            p_sum = jnp.zeros((nk, TQ), F32)
            for r in range(NSA_GROUP):
                hd = g * NSA_GROUP + r
                q = q_ref[:, hd * HEAD_DIM:(hd + 1) * HEAD_DIM]
                s_ref[:nk, :] = lax.dot_general(kc_ref[g, :nk, :], q, _NT, preferred_element_type=F32)
                s_ref[pl.ds(win0, CMP_WIN), :] += cb_ref[0, hd]
                s = jnp.where(reachable, s_ref[:nk, :], NEG)
                mx = jnp.max(s, axis=0, keepdims=True)
                e = jnp.exp2(s - mx)
                seen = mx > 0.5 * NEG
                p = e * jnp.where(seen, 1.0 / jnp.maximum(jnp.sum(e, axis=0, keepdims=True), TINY), 0.0)
                o_t = jnp.dot(vct_ref[g, :, :nk], p.astype(BF16), preferred_element_type=F32)
                oc_ref[:, hd * HEAD_DIM:(hd + 1) * HEAD_DIM] = o_t.T
                p_sum = p_sum + p
            imp = jnp.zeros((nb, TQ), F32)
            for part in _split3(p_sum):
                imp = imp + jnp.dot(ovt_ref[:nb, :nk], part, preferred_element_type=F32)
            picked = _topk_bias_t(jnp.where(allowed & ~forced, imp, -1.0), SLC_TOPK - 3)
            selb = jnp.where(allowed, jnp.where(forced, 0.0, picked), NEG)
            if nb < N_SLC:
                selb = jnp.concatenate([selb, jnp.full((N_SLC - nb, TQ), NEG, F32)], axis=0)
            sel_ref[g] = selb.T.astype(sel_ref.dtype)

    for c in range(1, 512 // HEAD_DIM + 1):
        @pl.when(lax.shift_right_logical(i, 3) == c - 1)
        def _():
            body(c * HEAD_DIM)


def _cmp_select(zb, kc, vct, ovt, cmpb):
    w = NSA_HEADS * HEAD_DIM
    return pl.pallas_call(
        _cmp_kernel,
        grid=(N_QT,),
        in_specs=[pl.BlockSpec((TQ, w), lambda i: (i, COL_NQ // NSA_HEADS)),
                  pl.BlockSpec((NSA_KV_HEADS, 512, HEAD_DIM), lambda i: (0, 0, 0)),
                  pl.BlockSpec((NSA_KV_HEADS, HEAD_DIM, 512), lambda i: (0, 0, 0)),
                  pl.BlockSpec((N_SLC, 512), lambda i: (0, 0)),
                  pl.BlockSpec((1, NSA_HEADS, CMP_WIN, TQ), lambda i: (jnp.minimum(i, 1), 1, 0, 0))],
        out_specs=[pl.BlockSpec((TQ, w), lambda i: (i, 0)),
                   pl.BlockSpec((NSA_KV_HEADS, TQ, N_SLC), lambda i: (0, i, 0))],
        out_shape=[jax.ShapeDtypeStruct((SEQ, w), F32),
                   jax.ShapeDtypeStruct((NSA_KV_HEADS, SEQ, N_SLC), BF16)],
        scratch_shapes=[pltpu.VMEM((512, TQ), F32)],
        compiler_params=_cparams("arbitrary"),
        name="cmp_select",
    )(zb, kc, vct, ovt, cmpb)


def _slc_kernel(q_ref, sel_ref, k_ref, v_ref, pat_ref, d0_ref, d1_ref, o_ref, *flash_scratch):
    i = pl.program_id(1)
    hp = HP_SLC
    selb = sel_ref[0].astype(F32)
    qs = [q_ref[:, h * HEAD_DIM:(h + 1) * HEAD_DIM] for h in range(hp)]
    outs = _sparse_flash(qs, [selb] * hp, k_ref, v_ref, [0] * hp, pat_ref,
                         [d0_ref[0, h] for h in range(hp)], [d1_ref[0, h] for h in range(hp)],
                         i, SLC_PER_TILE, flash_scratch)
    for h in range(hp):
        o_ref[:, h * HEAD_DIM:(h + 1) * HEAD_DIM] = outs[h].astype(o_ref.dtype)


def _slc(zb, selb, near, pat):
    hp = HP_SLC
    w = hp * HEAD_DIM
    per_group = NSA_GROUP // hp
    return pl.pallas_call(
        _slc_kernel,
        grid=(NSA_HEADS // hp, N_QT),
        in_specs=[pl.BlockSpec((TQ, w), lambda h, i: (i, COL_NQ // hp + h)),
                  pl.BlockSpec((1, TQ, N_SLC), lambda h, i: (h // per_group, i, 0)),
                  pl.BlockSpec((SEQ, HEAD_DIM), lambda h, i: (0, COL_SK + h // per_group),
                               pipeline_mode=pl.Buffered(1)),
                  pl.BlockSpec((SEQ, HEAD_DIM), lambda h, i: (0, COL_SV + h // per_group),
                               pipeline_mode=pl.Buffered(1)),
                  pl.BlockSpec((SEQ, HEAD_DIM), lambda h, i: (0, 0), pipeline_mode=pl.Buffered(1)),
                  pl.BlockSpec((1, hp, TQ, TQ), lambda h, i: (0, MOBA_HEADS // hp + h, 0, 0)),
                  pl.BlockSpec((1, hp, TQ, TQ), lambda h, i: (1, MOBA_HEADS // hp + h, 0, 0))],
        out_specs=pl.BlockSpec((TQ, w), lambda h, i: (i, h)),
        out_shape=jax.ShapeDtypeStruct((SEQ, NSA_HEADS * HEAD_DIM), F32),
        scratch_shapes=_flash_scratch(hp, False),
        compiler_params=_cparams("parallel", "arbitrary", vmem=BIG_VMEM_LIMIT),
        name="slc",
    )(zb, selb, zb, zb, pat, near, near)


def _win_kernel(q_ref, k_ref, v_ref, d0_ref, d1_ref, w2_ref, o_ref):
    i = pl.program_id(0)
    offs = [pl.multiple_of(jnp.maximum(i - b, 0) * TQ, TQ) for b in range(3)]
    mask2 = jnp.where(i >= 2, w2_ref[...], NEG)
    ones = jnp.ones((3 * TQ, HEAD_DIM), BF16)
    for g in range(NSA_KV_HEADS):
        cols = slice(g * HEAD_DIM, (g + 1) * HEAD_DIM)
        k_all = jnp.concatenate([k_ref[pl.ds(off, TQ), cols] for off in offs], axis=0)
        v_aug = jnp.concatenate([jnp.concatenate([v_ref[pl.ds(off, TQ), cols] for off in offs], axis=0), ones],
                                axis=1)
        for r in range(NSA_GROUP):
            hd = g * NSA_GROUP + r
            q = q_ref[:, hd * HEAD_DIM:(hd + 1) * HEAD_DIM]
            bias = jnp.concatenate([d0_ref[0, hd], jnp.where(i >= 1, d1_ref[0, hd], NEG), mask2], axis=1)
            s = lax.dot_general(q, k_all, _NT, preferred_element_type=F32) + bias
            m = jnp.max(s, axis=1, keepdims=True)
            pv = jnp.dot(jnp.exp2(s - m).astype(BF16), v_aug, preferred_element_type=F32)
            o_ref[:, hd * HEAD_DIM:(hd + 1) * HEAD_DIM] = pv[:, :HEAD_DIM] / pv[:, HEAD_DIM:]


def _win(zb, near, win2):
    w = NSA_HEADS * HEAD_DIM
    kv_w = NSA_KV_HEADS * HEAD_DIM
    return pl.pallas_call(
        _win_kernel,
        grid=(N_QT,),
        in_specs=[pl.BlockSpec((TQ, w), lambda i: (i, COL_NQ // NSA_HEADS)),
                  pl.BlockSpec((SEQ, kv_w), lambda i: (0, COL_WK // NSA_KV_HEADS), pipeline_mode=pl.Buffered(1)),
                  pl.BlockSpec((SEQ, kv_w), lambda i: (0, COL_WV // NSA_KV_HEADS), pipeline_mode=pl.Buffered(1)),
                  pl.BlockSpec((1, NSA_HEADS, TQ, TQ), lambda i: (0, 1, 0, 0), pipeline_mode=pl.Buffered(1)),
                  pl.BlockSpec((1, NSA_HEADS, TQ, TQ), lambda i: (1, 1, 0, 0), pipeline_mode=pl.Buffered(1)),
                  pl.BlockSpec((TQ, TQ), lambda i: (0, 0), pipeline_mode=pl.Buffered(1))],
        out_specs=pl.BlockSpec((TQ, w), lambda i: (i, 0)),
        out_shape=jax.ShapeDtypeStruct((SEQ, w), F32),
        compiler_params=_cparams("arbitrary"),
        name="win",
    )(zb, zb, zb, near, near, win2)


def _layer_norm(y, g, b):
    mu = jnp.mean(y, axis=-1, keepdims=True)
    yc = y - mu
    var = jnp.mean(yc * yc, axis=-1, keepdims=True)
    return yc * lax.rsqrt(var + LN_EPS) * g + b


ROW_CHUNK = 256
OUTPROJ_CHUNK = 128
OUTPROJ_VMEM_LIMIT = 60 * 1024 * 1024


def _outproj_kernel(x_ref, om_ref, oc_ref, os_ref, ow_ref, g_ref, e_ref, w_ref, lg_ref, lb_ref,
                    h_ref, hb_ref, wb_ref):
    @pl.when(pl.program_id(0) == 0)
    def _():
        for r0 in range(0, w_ref.shape[0], ROW_CHUNK):
            wb_ref[r0:r0 + ROW_CHUNK, :] = w_ref[r0:r0 + ROW_CHUNK, :].astype(BF16)

    for c0 in range(0, x_ref.shape[0], OUTPROJ_CHUNK):
        rows = slice(c0, c0 + OUTPROJ_CHUNK)
        gate = g_ref[rows, :]
        hi = gate.astype(BF16)
        lo = (gate - hi.astype(F32)).astype(BF16)
        g2 = jnp.concatenate([hi, lo], axis=1)
        o_n = jnp.zeros((OUTPROJ_CHUNK, oc_ref.shape[1]), F32)
        for c, branch in enumerate((oc_ref, os_ref, ow_ref)):
            o_n = o_n + jnp.dot(g2, e_ref[c], preferred_element_type=F32) * branch[rows, :]
        o = jnp.concatenate([om_ref[rows, :], o_n.astype(BF16)], axis=1)
        a = jnp.dot(o, wb_ref[...], preferred_element_type=F32)
        h = _layer_norm(DN_ALPHA * x_ref[rows, :] + a, lg_ref[...], lb_ref[...])
        h_ref[rows, :] = h
        hb_ref[rows, :] = h.astype(BF16)


def _resident(shape):
    return pl.BlockSpec(shape, lambda i: (0,) * len(shape), pipeline_mode=pl.Buffered(1))


def _outproj(x, o_m, o_c, o_s, o_w, gates, expand, w_out, ln_g, ln_b, tm=256):
    half = NSA_HEADS * HEAD_DIM
    row = lambda i: (i, 0)
    return pl.pallas_call(
        _outproj_kernel,
        grid=(SEQ // tm,),
        in_specs=[pl.BlockSpec((tm, D_MODEL), row),
                  pl.BlockSpec((tm, half), row), pl.BlockSpec((tm, half), row),
                  pl.BlockSpec((tm, half), row), pl.BlockSpec((tm, half), row),
                  pl.BlockSpec((tm, HEAD_DIM), row),
                  _resident((3, 2 * HEAD_DIM, half)),
                  _resident((D_MODEL, D_MODEL)),
                  _resident((1, D_MODEL)), _resident((1, D_MODEL))],
        out_specs=[pl.BlockSpec((tm, D_MODEL), row), pl.BlockSpec((tm, D_MODEL), row)],
        out_shape=[jax.ShapeDtypeStruct((SEQ, D_MODEL), F32),
                   jax.ShapeDtypeStruct((SEQ, D_MODEL), BF16)],
        scratch_shapes=[pltpu.VMEM((D_MODEL, D_MODEL), BF16)],
        compiler_params=_cparams("arbitrary", vmem=OUTPROJ_VMEM_LIMIT),
        name="outproj_ln",
    )(x, o_m, o_c, o_s, o_w, gates, expand, w_out, ln_g, ln_b)


def _ffn_kernel(hb_ref, w1_ref, w2_ref, f_ref):
    @pl.when(pl.program_id(1) == 0)
    def _():
        f_ref[...] = jnp.zeros(f_ref.shape, F32)

    u = jnp.maximum(jnp.dot(hb_ref[...], w1_ref[...].astype(BF16), preferred_element_type=F32), 0.0)
    f_ref[...] += jnp.dot((u * u).astype(BF16), w2_ref[...].astype(BF16), preferred_element_type=F32)


def _ffn(hb, w1, w2, tm=1024, tf=512):
    return pl.pallas_call(
        _ffn_kernel,
        grid=(SEQ // tm, D_FF // tf),
        in_specs=[pl.BlockSpec((tm, D_MODEL), lambda i, c: (i, 0)),
                  pl.BlockSpec((D_MODEL, tf), lambda i, c: (0, c)),
                  pl.BlockSpec((tf, D_MODEL), lambda i, c: (c, 0))],
        out_specs=pl.BlockSpec((tm, D_MODEL), lambda i, c: (i, 0)),
        out_shape=jax.ShapeDtypeStruct((SEQ, D_MODEL), F32),
        compiler_params=_cparams("parallel", "arbitrary", vmem=BIG_VMEM_LIMIT),
        name="ffn",
    )(hb, w1, w2)


def _final_kernel(h_ref, hb_ref, f_ref, p_ref, wg_ref, wp_ref, lg_ref, lb_ref, o_ref):
    for c0 in range(0, h_ref.shape[0], ROW_CHUNK):
        rows = slice(c0, c0 + ROW_CHUNK)
        gate = jax.nn.sigmoid(jnp.dot(hb_ref[rows, :], wg_ref[...], preferred_element_type=F32))
        emb = jnp.dot(p_ref[rows, :].astype(BF16), wp_ref[...], preferred_element_type=F32)
        y = DN_ALPHA * h_ref[rows, :] + f_ref[rows, :] + gate * emb
        o_ref[rows, :] = _layer_norm(y, lg_ref[...], lb_ref[...])


def _final(h, hb, f, p, wg, wp, ln_g, ln_b, tm=512):
    row = lambda i: (i, 0)
    return pl.pallas_call(
        _final_kernel,
        grid=(SEQ // tm,),
        in_specs=[pl.BlockSpec((tm, D_MODEL), row), pl.BlockSpec((tm, D_MODEL), row),
                  pl.BlockSpec((tm, D_MODEL), row), pl.BlockSpec((tm, D_PLE), row),
                  _resident((D_MODEL, D_MODEL)), _resident((D_PLE, D_MODEL)),
                  _resident((1, D_MODEL)), _resident((1, D_MODEL))],
        out_specs=pl.BlockSpec((tm, D_MODEL), row),
        out_shape=jax.ShapeDtypeStruct((SEQ, D_MODEL), F32),
        compiler_params=_cparams("parallel"),
        name="final_ln",
    )(h, hb, f, p, wg, wp, ln_g, ln_b)


def _overlap_matrix_t():
    j = np.arange(N_SLC)[:, None]
    n = np.arange(512)[None, :]
    ov = ((n * CMP_STRIDE < j * SLC_LEN + SLC_LEN) & (n * CMP_STRIDE + CMP_LEN > j * SLC_LEN) & (n < N_CMP))
    return ov.astype(np.float32)


def _gate_expand():
    e = np.zeros((3, 2 * HEAD_DIM, NSA_HEADS * HEAD_DIM), np.float32)
    for c in range(3):
        for h in range(NSA_HEADS):
            e[c, 3 * h + c, h * HEAD_DIM:(h + 1) * HEAD_DIM] = 1.0
            e[c, HEAD_DIM + 3 * h + c, h * HEAD_DIM:(h + 1) * HEAD_DIM] = 1.0
    return e


def _layer(h, p, w_in, cmp_pe_k, cmp_w1_k, cmp_w2_k, cmp_pe_v, cmp_w1_v, cmp_w2_v, tiles, w_out,
           ln1_g, ln1_b, w_ff1, w_ff2, w_ple, w_ple_gate, ln2_g, ln2_b):
    near, win2, cmpb = tiles
    scale = HEAD_DIM ** -0.5 * LOG2E
    colscale = np.ones((1, GATE_COL0), np.float32)
    colscale[0, COL_MQ * HEAD_DIM:(COL_MQ + MOBA_HEADS) * HEAD_DIM] = scale
    colscale[0, COL_NQ * HEAD_DIM:(COL_NQ + NSA_HEADS) * HEAD_DIM] = scale
    w_in_t = w_in.T
    w_gl_t = jnp.pad(w_in_t[GATE_COL0:], ((0, HEAD_DIM - NSA_HEADS * 3), (0, 0))).astype(BF16)
    zb, gates, ckv = _inproj(h, w_in_t, jnp.asarray(colscale), w_gl_t)

    half = CMP_STRIDE * HEAD_DIM
    xr = ckv.reshape(4, SEQ // CMP_STRIDE, half)
    kc, vct = _compress(xr, cmp_pe_k.reshape(2, 1, half), cmp_w1_k.reshape(2, half, HEAD_DIM), cmp_w2_k,
                        cmp_pe_v.reshape(2, 1, half), cmp_w1_v.reshape(2, half, HEAD_DIM), cmp_w2_v)

    key = np.arange(SEQ)[:, None]
    lane = np.arange(HEAD_DIM)[None, :]
    pat_moba = jnp.asarray(lane == key // MOBA_BLOCK, BF16)
    pat_slc = jnp.asarray(lane == key // SLC_LEN, BF16)
    o_m = _moba(zb, _moba_select(zb), near, pat_moba)
    o_c, selb = _cmp_select(zb, kc, vct, jnp.asarray(_overlap_matrix_t(), BF16), cmpb)
    o_s = _slc(zb, selb, near, pat_slc)
    o_w = _win(zb, near, win2)

    h1, h1b = _outproj(h, o_m, o_c, o_s, o_w, gates, jnp.asarray(_gate_expand(), BF16),
                       w_out, ln1_g.reshape(1, -1), ln1_b.reshape(1, -1))
    f = _ffn(h1b, w_ff1, w_ff2)
    return _final(h1, h1b, f, p, w_ple_gate.astype(BF16), w_ple.astype(BF16),
                  ln2_g.reshape(1, -1), ln2_b.reshape(1, -1))


def kernel(x, p, w_in, cmp_pe_k, cmp_w1_k, cmp_w2_k, cmp_pe_v, cmp_w1_v, cmp_w2_v, rel_bias, w_out,
           ln1_g, ln1_b, w_ff1, w_ff2, w_ple, w_ple_gate, ln2_g, ln2_b):
    tiles = _bias_tiles(rel_bias)
    h = x[0]
    for i in range(w_in.shape[0]):
        h = _layer(h, p[i, 0], w_in[i], cmp_pe_k[i], cmp_w1_k[i], cmp_w2_k[i], cmp_pe_v[i], cmp_w1_v[i],
                   cmp_w2_v[i], tiles, w_out[i], ln1_g[i], ln1_b[i], w_ff1[i], w_ff2[i], w_ple[i],
                   w_ple_gate[i], ln2_g[i], ln2_b[i])
    return h[None]
```
